```python
import math
import jax, jax.numpy as jnp
from jax import lax
import numpy as np

D_MODEL = 1024
BATCH = 8
SEQ = 2048
DEPTH = 1
DEC_BATCH = 128
DEC_SEQ = 8
PAST_LEN = 16384
PAGE_SIZE = 128

D_MIX = D_MODEL
D_A = D_MIX // 2
A_HEADS = 4
A_HEAD_DIM = D_A // A_HEADS
CHUNK = 128
D_B = D_MIX - D_A
B_HEADS = 8
B_HEAD_DIM = D_B // B_HEADS
CONV_W = 4
LRU_C = 8.0
D_IN = 2 * D_A + 2 * D_B
N_GROUPS = 4
EXPERTS_PER_GROUP = 8
N_EXPERTS = N_GROUPS * EXPERTS_PER_GROUP
TOP_K = 2
D_EXPERT = D_MODEL // 2
LN_EPS = 1e-5
ALPHA = (2.0 * DEPTH) ** 0.25
BETA = (8.0 * DEPTH) ** -0.25

kernel_name = "hymba_gmlp_rglru_hmoe_step"


def layer_norm(x, g, b):
    xf = x.astype(jnp.float32)
    mu = jnp.mean(xf, axis=-1, keepdims=True)
    var = jnp.mean(jnp.square(xf - mu), axis=-1, keepdims=True)
    y = (xf - mu) * lax.rsqrt(var + LN_EPS)
    return (y * g.astype(jnp.float32) + b.astype(jnp.float32)).astype(x.dtype)


def gmlp_spatial(u, v, w_s, b_s):
    bsz, s, _ = v.shape
    pad = (-s) % CHUNK
    vp = jnp.pad(v, ((0, 0), (0, pad), (0, 0)))
    n_chunks = (s + pad) // CHUNK
    v5 = vp.reshape(bsz, n_chunks, CHUNK, A_HEADS, A_HEAD_DIM)
    mask = jnp.tril(jnp.ones((CHUNK, CHUNK), dtype=bool))
    ws = jnp.where(mask[None], w_s, jnp.zeros_like(w_s))
    mixed = jnp.einsum('hij,bcjhd->bcihd', ws, v5)
    mixed = mixed + jnp.transpose(b_s)[None, None, :, :, None]
    mixed = mixed.reshape(bsz, n_chunks * CHUNK, D_A)[:, :s]
    return u * mixed


def causal_conv(x, buf, w, b):
    s = x.shape[1]
    xp = jnp.concatenate([buf.astype(x.dtype), x], axis=1)
    out = b
    for k in range(CONV_W):
        out = out + xp[:, k:k + s] * w[k]
    return out, xp[:, -(CONV_W - 1):]


def rglru(x, h0, w_a, b_a, w_x, b_x, lam):
    bsz, s, _ = x.shape
    xh = x.reshape(bsz, s, B_HEADS, B_HEAD_DIM)
    gate_a = jax.nn.sigmoid(jnp.einsum('bshi,hij->bshj', xh, w_a) + b_a).reshape(bsz, s, D_B)
    gate_x = jax.nn.sigmoid(jnp.einsum('bshi,hij->bshj', xh, w_x) + b_x).reshape(bsz, s, D_B)
    log_a = -LRU_C * gate_a.astype(jnp.float32) * jax.nn.softplus(-lam.astype(jnp.float32))
    a = jnp.exp(log_a)
    mult = jnp.sqrt(-jnp.expm1(2.0 * log_a))
    bx = mult * (gate_x * x).astype(jnp.float32)

    def step(h, ab):
        a_t, b_t = ab
        h = a_t * h + b_t
        return h, h

    h_last, hs = lax.scan(step, h0.astype(jnp.float32),
                          (jnp.transpose(a, (1, 0, 2)), jnp.transpose(bx, (1, 0, 2))))
    return jnp.transpose(hs, (1, 0, 2)).astype(x.dtype), h_last.astype(x.dtype)


def hier_moe(x, w_rg, b_rg, w_re, b_re, w1, w3, w2):
    shp = x.shape
    x2 = x.reshape(-1, D_MODEL)
    t = x2.shape[0]
    lg = (x2 @ w_rg + b_rg).astype(jnp.float32)
    pg = jax.nn.softmax(lg, axis=-1)
    gsel = jnp.argmax(lg, axis=-1)
    pgsel = jnp.take_along_axis(pg, gsel[:, None], axis=-1)
    le = (x2 @ w_re + b_re).astype(jnp.float32).reshape(t, N_GROUPS, EXPERTS_PER_GROUP)
    le_sel = jnp.take_along_axis(le, gsel[:, None, None], axis=1)[:, 0]
    topv, topi = lax.top_k(le_sel, TOP_K)
    wk = jax.nn.softmax(topv, axis=-1) * pgsel
    within = jnp.sum(jax.nn.one_hot(topi, EXPERTS_PER_GROUP, dtype=jnp.float32) * wk[..., None], axis=1)
    gates = (jax.nn.one_hot(gsel, N_GROUPS, dtype=jnp.float32)[:, :, None]
             * within[:, None, :]).reshape(t, N_EXPERTS).astype(x.dtype)

    def expert_step(acc, p):
        w1e, w3e, w2e, ge = p
        hdn = jax.nn.silu(x2 @ w1e) * (x2 @ w3e)
        return acc + ge[:, None] * (hdn @ w2e), None

    acc, _ = lax.scan(expert_step, jnp.zeros_like(x2), (w1, w3, w2, jnp.transpose(gates)))
    return acc.reshape(shp)


def hybrid_layer(x, c, h0, conv0, w_ada, b_ada, w_in, w_s, b_s, lnv_g, lnv_b,
                 conv_w, conv_b, lru_wa, lru_ba, lru_wx, lru_bx, lru_lam, w_out,
                 ln1_g, ln1_b, w_rg, b_rg, w_re, b_re, w1, w3, w2, ln2_g, ln2_b):
    mod = jax.nn.silu(c) @ w_ada + b_ada
    sh1, sc1, g1, sh2, sc2, g2 = jnp.split(mod[:, None, :], 6, axis=-1)
    h = x * (1.0 + sc1) + sh1
    proj = h @ w_in
    u, v, xr, yr = jnp.split(proj, [D_A, 2 * D_A, 2 * D_A + D_B], axis=-1)
    v = layer_norm(v, lnv_g, lnv_b)
    out_a = gmlp_spatial(u, v, w_s, b_s)
    xc, conv_new = causal_conv(xr, conv0, conv_w, conv_b)
    hs, h_last = rglru(xc, h0, lru_wa, lru_ba, lru_wx, lru_bx, lru_lam)
    out_b = hs * jax.nn.gelu(yr)
    mix = jnp.concatenate([out_a, out_b], axis=-1) @ w_out
    x = layer_norm(ALPHA * x + (1.0 + g1) * mix, ln1_g, ln1_b)
    h = x * (1.0 + sc2) + sh2
    x = layer_norm(ALPHA * x + (1.0 + g2) * hier_moe(h, w_rg, b_rg, w_re, b_re, w1, w3, w2), ln2_g, ln2_b)
    return x, h_last, conv_new, v


def setup_inputs(seed: int = 0) -> dict:
    key = jax.random.key(seed)
    ks = jax.random.split(key, 40)
    f = jnp.float32
    nrm = lambda k, shp, s: jax.random.normal(k, shp, f) * s
    a0 = jax.random.uniform(ks[20], (DEPTH, D_B), f, 0.9, 0.999)
    sig = a0 ** (1.0 / LRU_C)
    lru_lam = jnp.log(sig) - jnp.log1p(-sig)
    return {
        "x_prompt": nrm(ks[0], (BATCH, SEQ, D_MODEL), 1.0),
        "x_sample": nrm(ks[1], (DEC_BATCH, DEC_SEQ, D_MODEL), 1.0),
        "state_rglru_h": nrm(ks[2], (DEPTH, DEC_BATCH, D_B), 1.0),
        "state_conv": nrm(ks[3], (DEPTH, DEC_BATCH, CONV_W - 1, D_B), 1.0),
        "c_prompt": nrm(ks[4], (BATCH, D_MODEL), 1.0),
        "c_sample": nrm(ks[5], (DEC_BATCH, D_MODEL), 1.0),
        "w_ada": nrm(ks[6], (DEPTH, D_MODEL, 6 * D_MODEL), 0.1 * D_MODEL ** -0.5),
        "b_ada": nrm(ks[7], (DEPTH, 6 * D_MODEL), 0.01),
        "w_in": nrm(ks[8], (DEPTH, D_MODEL, D_IN), D_MODEL ** -0.5),
        "w_s": nrm(ks[9], (DEPTH, A_HEADS, CHUNK, CHUNK), CHUNK ** -0.5),
        "b_s": 1.0 + nrm(ks[10], (DEPTH, A_HEADS, CHUNK), 0.02),
        "lnv_g": 1.0 + nrm(ks[11], (DEPTH, D_A), 0.02),
        "lnv_b": nrm(ks[12], (DEPTH, D_A), 0.02),
        "conv_w": nrm(ks[13], (DEPTH, CONV_W, D_B), CONV_W ** -0.5),
        "conv_b": nrm(ks[14], (DEPTH, D_B), 0.02),
        "lru_wa": nrm(ks[15], (DEPTH, B_HEADS, B_HEAD_DIM, B_HEAD_DIM), B_HEAD_DIM ** -0.5),
        "lru_ba": nrm(ks[16], (DEPTH, B_HEADS, B_HEAD_DIM), 0.02),
        "lru_wx": nrm(ks[17], (DEPTH, B_HEADS, B_HEAD_DIM, B_HEAD_DIM), B_HEAD_DIM ** -0.5),
        "lru_bx": nrm(ks[18], (DEPTH, B_HEADS, B_HEAD_DIM), 0.02),
        "lru_lam": lru_lam,
        "w_out": nrm(ks[21], (DEPTH, D_MIX, D_MODEL), BETA * D_MIX ** -0.5),
        "ln1_g": 1.0 + nrm(ks[22], (DEPTH, D_MODEL), 0.02),
        "ln1_b": nrm(ks[23], (DEPTH, D_MODEL), 0.02),
        "w_rg": nrm(ks[24], (DEPTH, D_MODEL, N_GROUPS), D_MODEL ** -0.5),
        "b_rg": nrm(ks[25], (DEPTH, N_GROUPS), 0.01),
        "w_re": nrm(ks[26], (DEPTH, D_MODEL, N_EXPERTS), D_MODEL ** -0.5),
        "b_re": nrm(ks[27], (DEPTH, N_EXPERTS), 0.01),
        "w1": nrm(ks[28], (DEPTH, N_EXPERTS, D_MODEL, D_EXPERT), D_MODEL ** -0.5),
        "w3": nrm(ks[29], (DEPTH, N_EXPERTS, D_MODEL, D_EXPERT), D_MODEL ** -0.5),
        "w2": nrm(ks[30], (DEPTH, N_EXPERTS, D_EXPERT, D_MODEL), BETA * D_EXPERT ** -0.5),
        "ln2_g": 1.0 + nrm(ks[31], (DEPTH, D_MODEL), 0.02),
        "ln2_b": nrm(ks[32], (DEPTH, D_MODEL), 0.02),
    }


def reference(x_prompt, x_sample, state_rglru_h, state_conv, c_prompt, c_sample,
              w_ada, b_ada, w_in, w_s, b_s, lnv_g, lnv_b, conv_w, conv_b,
              lru_wa, lru_ba, lru_wx, lru_bx, lru_lam, w_out, ln1_g, ln1_b,
              w_rg, b_rg, w_re, b_re, w1, w3, w2, ln2_g, ln2_b):
    xp, xs = x_prompt, x_sample
    n_prompt = x_prompt.shape[0]
    hp_list, cp_list, hs_list, cs_list, vs_list = [], [], [], [], []
    for l in range(DEPTH):
        lw = (w_ada[l], b_ada[l], w_in[l], w_s[l], b_s[l], lnv_g[l], lnv_b[l],
              conv_w[l], conv_b[l], lru_wa[l], lru_ba[l], lru_wx[l], lru_bx[l],
              lru_lam[l], w_out[l], ln1_g[l], ln1_b[l], w_rg[l], b_rg[l],
              w_re[l], b_re[l], w1[l], w3[l], w2[l], ln2_g[l], ln2_b[l])
        h0p = jnp.zeros((n_prompt, D_B), x_prompt.dtype)
        conv0p = jnp.zeros((n_prompt, CONV_W - 1, D_B), x_prompt.dtype)
        xp, hp, cp, _ = hybrid_layer(xp, c_prompt, h0p, conv0p, *lw)
        xs, hsn, csn, vsn = hybrid_layer(xs, c_sample, state_rglru_h[l], state_conv[l], *lw)
        hp_list.append(hp)
        cp_list.append(cp)
        hs_list.append(hsn)
        cs_list.append(csn)
        vs_list.append(vsn)
    new_h_prompt = jnp.stack(hp_list)
    new_conv_prompt = jnp.stack(cp_list)
    new_h_sample = jnp.stack(hs_list)
    new_conv_sample = jnp.stack(cs_list)
    new_chunk_v_sample = jnp.stack(vs_list)
    return (xp, xs, new_h_prompt, new_conv_prompt, new_h_sample, new_conv_sample, new_chunk_v_sample)
```

```python
import functools
import math

import jax
import jax.numpy as jnp
from jax import lax
from jax.experimental import pallas as pl
from jax.experimental.pallas import tpu as pltpu

D_MODEL = 1024
D_A = 512
A_HEADS = 4
A_HEAD_DIM = 128
CHUNK = 128
D_B = 512
B_HEADS = 8
B_HEAD_DIM = 64
CONV_W = 4
LRU_C = 8.0
D_IN = 2 * D_A + 2 * D_B
N_GROUPS = 4
EXPERTS_PER_GROUP = 8
N_EXPERTS = 32
D_EXPERT = 512
LN_EPS = 1e-5
DEPTH = 1
ALPHA = (2.0 * DEPTH) ** 0.25

LANES = 128
SUBLANES = 8
MXU_DIM = 256
VMEM_LIMIT = 56 * 1024 * 1024

ROWS = 256
MOE_ROWS = 1024
BF16 = jnp.bfloat16
F32 = jnp.float32


def _dot(a, b):
    return jnp.dot(a, b, preferred_element_type=F32)


def _layer_norm(x, g, b):
    mu = jnp.mean(x, axis=-1, keepdims=True)
    xc = x - mu
    var = jnp.mean(xc * xc, axis=-1, keepdims=True)
    return xc * lax.rsqrt(var + LN_EPS) * g + b


def _gelu_tanh(x):
    c = math.sqrt(2.0 / math.pi)
    return x * (0.5 * (1.0 + jnp.tanh(c * (x + 0.044715 * (x * x * x)))))


def _ada_kernel(c_ref, w_ref, b_ref, o_ref):
    c = c_ref[...]
    a = (c * jax.nn.sigmoid(c)).astype(BF16)
    o_ref[...] = _dot(a, w_ref[...].astype(BF16)) + b_ref[...]


def _ada_call(c_all, w_ada, b_ada):
    n_rows = c_all.shape[0]
    tn = 1536
    return pl.pallas_call(
        _ada_kernel,
        grid=(6 * D_MODEL // tn,),
        in_specs=[
            pl.BlockSpec((n_rows, D_MODEL), lambda j: (0, 0)),
            pl.BlockSpec((D_MODEL, tn), lambda j: (0, j)),
            pl.BlockSpec((1, tn), lambda j: (0, j)),
        ],
        out_specs=pl.BlockSpec((n_rows, tn), lambda j: (0, j)),
        out_shape=jax.ShapeDtypeStruct((n_rows, 6 * D_MODEL), F32),
        compiler_params=pltpu.CompilerParams(
            dimension_semantics=("arbitrary",), vmem_limit_bytes=VMEM_LIMIT),
        name="ada_mod",
    )(c_all, w_ada, b_ada)


def _mixer_kernel(sample, *refs):
    if sample:
        (x_ref, mod_ref, win_ref, wsm_ref, bsf_ref, lnvg_ref, lnvb_ref, convw_ref, convb_ref,
         wg_ref, bg_ref, lam_ref, wout_ref, ln1g_ref, ln1b_ref, wrh_ref, wrl_ref, br_ref,
         h0_ref, cpad_ref, _x1_in, _h2_in, _lg_in,
         x1_ref, h2_ref, lg_ref, xr_ref, v_ref, hs_ref) = refs
    else:
        (x_ref, mod_ref, win_ref, wsm_ref, bsf_ref, lnvg_ref, lnvb_ref, convw_ref, convb_ref,
         wg_ref, bg_ref, lam_ref, wout_ref, ln1g_ref, ln1b_ref, wrh_ref, wrl_ref, br_ref,
         x1_ref, h2_ref, lg_ref, cst_ref, hst_ref, prev_ref, hcar_ref) = refs

    rows = ROWS
    groups = rows // SUBLANES

    if sample:
        x = x_ref[...]
        mod = mod_ref[...]
    else:
        x = x_ref[0]
        mod = mod_ref[0]

        @pl.when(pl.program_id(1) == 0)
        def _():
            prev_ref[...] = jnp.zeros_like(prev_ref)
            hcar_ref[...] = jnp.zeros_like(hcar_ref)

    sh1, sc1, g1, sh2, sc2, g2 = [mod[:, i * D_MODEL:(i + 1) * D_MODEL] for i in range(6)]

    h = x * (1.0 + sc1) + sh1
    proj = _dot(h.astype(BF16), win_ref[...])
    u = proj[:, :D_A]
    v = _layer_norm(proj[:, D_A:2 * D_A], lnvg_ref[...], lnvb_ref[...])
    xr = proj[:, 2 * D_A:2 * D_A + D_B]
    yr = proj[:, 2 * D_A + D_B:]

    vb = v.astype(BF16)
    halves = []
    for r in range(rows // MXU_DIM):
        heads = [
            _dot(wsm_ref[hd], vb[r * MXU_DIM:(r + 1) * MXU_DIM, hd * A_HEAD_DIM:(hd + 1) * A_HEAD_DIM])
            for hd in range(A_HEADS)
        ]
        halves.append(jnp.concatenate(heads, axis=1) + bsf_ref[...])
    mixed = halves[0] if len(halves) == 1 else jnp.concatenate(halves, axis=0)
    out_a = u * mixed

    xr3 = xr.reshape(groups, SUBLANES, D_B)
    t3 = lax.broadcasted_iota(jnp.int32, (1, SUBLANES, D_B), 1)
    cw = convw_ref[...]
    if sample:
        cpad3 = cpad_ref[...].reshape(groups, SUBLANES, D_B)
    else:
        prev8 = prev_ref[...]
    xc3 = convb_ref[...][None]
    for k in range(CONV_W):
        s = CONV_W - 1 - k
        if s == 0:
            shifted = xr3
        else:
            rot = pltpu.roll(xr3, s, 1)
            if sample:
                rot_prev = pltpu.roll(cpad3, s, 1)
            else:
                first = pltpu.roll(prev8, s, 0)[None]
                rot_prev = jnp.concatenate([first, rot[:-1]], axis=0) if groups > 1 else first
            shifted = jnp.where(t3 >= s, rot, rot_prev)
        xc3 = xc3 + shifted * cw[k:k + 1, :][None]
    xc = xc3.reshape(rows, D_B)

    xcb = xc.astype(BF16)
    gq = [_dot(xcb[:, q * MXU_DIM:(q + 1) * MXU_DIM], wg_ref[q]) for q in range(D_B // MXU_DIM)]
    bg = bg_ref[...]
    pre_a = jnp.concatenate([g[:, :MXU_DIM] for g in gq], axis=1) + bg[:, :D_B]
    pre_x = jnp.concatenate([g[:, MXU_DIM:] for g in gq], axis=1) + bg[:, D_B:]
    gate_a = jax.nn.sigmoid(pre_a)
    gate_x = jax.nn.sigmoid(pre_x)
    nl = -lam_ref[...]
    softplus = jnp.maximum(nl, 0.0) + jnp.log1p(jnp.exp(-jnp.abs(nl)))
    log_a = -LRU_C * gate_a * softplus
    a = jnp.exp(log_a)
    mult = jnp.sqrt(-jnp.tanh(log_a) * (1.0 + a * a))
    bx = mult * (gate_x * xc)

    a3 = a.reshape(groups, SUBLANES, D_B)
    b3 = bx.reshape(groups, SUBLANES, D_B)
    for d in (1, 2, 4):
        m = t3 >= d
        a_sh = jnp.where(m, pltpu.roll(a3, d, 1), 1.0)
        b_sh = jnp.where(m, pltpu.roll(b3, d, 1), 0.0)
        b3 = a3 * b_sh + b3
        a3 = a3 * a_sh
    if sample:
        hs3 = a3 * h0_ref[...].reshape(groups, SUBLANES, D_B) + b3
    else:
        h_prev = hcar_ref[...][SUBLANES - 1:SUBLANES, :]
        hs_list = []
        for g in range(groups):
            hg = a3[g] * h_prev + b3[g]
            hs_list.append(hg)
            h_prev = hg[SUBLANES - 1:SUBLANES, :]
        hs3 = jnp.stack(hs_list, axis=0)
    hs = hs3.reshape(rows, D_B)

    out_b = hs * _gelu_tanh(yr)
    mix = _dot(out_a.astype(BF16), wout_ref[:D_A, :]) + _dot(out_b.astype(BF16), wout_ref[D_A:, :])
    x1 = _layer_norm(ALPHA * x + (1.0 + g1) * mix, ln1g_ref[...], ln1b_ref[...])
    h2 = x1 * (1.0 + sc2) + sh2

    h2_hi = h2.astype(BF16)
    h2_lo = (h2 - h2_hi.astype(F32)).astype(BF16)
    logits = (_dot(h2_hi, wrh_ref[...]) + _dot(h2_lo, wrh_ref[...]) + _dot(h2_hi, wrl_ref[...])
              + br_ref[...])

    x1_ref[...] = x1
    h2_ref[...] = h2
    lg_ref[...] = logits
    if sample:
        xr_ref[...] = xr
        v_ref[...] = v
        hs_ref[...] = hs
    else:
        last_x = xr3[groups - 1]
        last_h = hs3[groups - 1]
        prev_ref[...] = last_x
        hcar_ref[...] = last_h
        cst_ref[0] = last_x
        hst_ref[0] = last_h


def _const_spec(shape):
    nd = len(shape)
    return pl.BlockSpec(shape, lambda *_: (0,) * nd)


def _mixer_weights(w_in, wsm, bsf, lnv_g, lnv_b, conv_w, conv_b, wg, bg, lam, w_out, ln1_g, ln1_b,
                   wr_hi, wr_lo, br):
    arrs = [w_in, wsm, bsf, lnv_g, lnv_b, conv_w, conv_b, wg, bg, lam, w_out, ln1_g, ln1_b,
            wr_hi, wr_lo, br]
    return arrs, [_const_spec(a.shape) for a in arrs]


def _mixer_prompt(x, mod3, weights, n_tokens):
    batch, seq, _ = x.shape
    tiles = seq // ROWS
    arrs, specs = weights
    row_spec = lambda w: pl.BlockSpec((ROWS, w), lambda b, j: (b * tiles + j, 0))
    st_spec = pl.BlockSpec((1, SUBLANES, D_B), lambda b, j: (b, 0, 0))
    return pl.pallas_call(
        functools.partial(_mixer_kernel, False),
        grid=(batch, tiles),
        in_specs=[pl.BlockSpec((1, ROWS, D_MODEL), lambda b, j: (b, j, 0)),
                  pl.BlockSpec((1, 1, 6 * D_MODEL), lambda b, j: (b, 0, 0))] + specs,
        out_specs=[row_spec(D_MODEL), row_spec(D_MODEL), row_spec(LANES), st_spec, st_spec],
        out_shape=[jax.ShapeDtypeStruct((n_tokens, D_MODEL), F32),
                   jax.ShapeDtypeStruct((n_tokens, D_MODEL), F32),
                   jax.ShapeDtypeStruct((n_tokens, LANES), F32),
                   jax.ShapeDtypeStruct((batch, SUBLANES, D_B), F32),
                   jax.ShapeDtypeStruct((batch, SUBLANES, D_B), F32)],
        scratch_shapes=[pltpu.VMEM((SUBLANES, D_B), F32), pltpu.VMEM((SUBLANES, D_B), F32)],
        compiler_params=pltpu.CompilerParams(
            dimension_semantics=("arbitrary", "arbitrary"), vmem_limit_bytes=VMEM_LIMIT),
        name="mixer_prompt",
    )(x, mod3, *arrs)


def _mixer_sample(x2, mod_tok, weights, h0_tok, cpad, x1_all, h2_all, lg_all, row0):
    n_rows = x2.shape[0]
    arrs, specs = weights
    blk0 = row0 // ROWS
    in_row = lambda w: pl.BlockSpec((ROWS, w), lambda i: (i, 0))
    out_row = lambda w: pl.BlockSpec((ROWS, w), lambda i: (blk0 + i, 0))
    any_spec = pl.BlockSpec(memory_space=pl.ANY)
    n_in = 2 + len(arrs) + 2
    return pl.pallas_call(
        functools.partial(_mixer_kernel, True),
        grid=(n_rows // ROWS,),
        in_specs=[in_row(D_MODEL), in_row(6 * D_MODEL)] + specs + [in_row(D_B), in_row(D_B)]
                 + [any_spec, any_spec, any_spec],
        out_specs=[out_row(D_MODEL), out_row(D_MODEL), out_row(LANES),
                   in_row(D_B), in_row(D_B), in_row(D_B)],
        out_shape=[jax.ShapeDtypeStruct(x1_all.shape, F32),
                   jax.ShapeDtypeStruct(h2_all.shape, F32),
                   jax.ShapeDtypeStruct(lg_all.shape, F32),
                   jax.ShapeDtypeStruct((n_rows, D_B), F32),
                   jax.ShapeDtypeStruct((n_rows, D_B), F32),
                   jax.ShapeDtypeStruct((n_rows, D_B), F32)],
        input_output_aliases={n_in: 0, n_in + 1: 1, n_in + 2: 2},
        compiler_params=pltpu.CompilerParams(
            dimension_semantics=("arbitrary",), vmem_limit_bytes=VMEM_LIMIT),
        name="mixer_sample",
    )(x2, mod_tok, *arrs, h0_tok, cpad, x1_all, h2_all, lg_all)


def _router_kernel(lg_ref, gates_ref):
    l = lg_ref[...]
    lane = lax.broadcasted_iota(jnp.int32, l.shape, 1)
    neg = -jnp.inf
    big = jnp.int32(LANES)
    lgm = jnp.where(lane < N_GROUPS, l, neg)
    m = jnp.max(lgm, axis=1, keepdims=True)
    gsel = jnp.min(jnp.where(lgm == m, lane, big), axis=1, keepdims=True)
    denom = jnp.sum(jnp.where(lane < N_GROUPS, jnp.exp(l - m), 0.0), axis=1, keepdims=True)
    pgsel = 1.0 / denom
    lo = N_GROUPS + EXPERTS_PER_GROUP * gsel
    le = jnp.where(lane >= lo, jnp.where(lane < lo + EXPERTS_PER_GROUP, l, neg), neg)
    v1 = jnp.max(le, axis=1, keepdims=True)
    i1 = jnp.min(jnp.where(le == v1, lane, big), axis=1, keepdims=True)
    le2 = jnp.where(lane == i1, neg, le)
    v2 = jnp.max(le2, axis=1, keepdims=True)
    i2 = jnp.min(jnp.where(le2 == v2, lane, big), axis=1, keepdims=True)
    ex = jnp.exp(v2 - v1)
    wk1 = pgsel / (1.0 + ex)
    wk2 = pgsel * ex / (1.0 + ex)
    lane_e = lane + N_GROUPS
    gates_ref[...] = jnp.where(lane_e == i1, wk1, 0.0) + jnp.where(lane_e == i2, wk2, 0.0)


def _router_call(lg_all):
    n_tokens = lg_all.shape[0]
    tm = 1024
    return pl.pallas_call(
        _router_kernel,
        grid=(n_tokens // tm,),
        in_specs=[pl.BlockSpec((tm, LANES), lambda i: (i, 0))],
        out_specs=pl.BlockSpec((tm, LANES), lambda i: (i, 0)),
        out_shape=jax.ShapeDtypeStruct((n_tokens, LANES), F32),
        compiler_params=pltpu.CompilerParams(
            dimension_semantics=("arbitrary",), vmem_limit_bytes=VMEM_LIMIT),
        name="router",
    )(lg_all)


def _moe_dense_kernel(h2_ref, gates_ref, w1_ref, w3_ref, w2_ref, o_ref):
    e = pl.program_id(1)

    @pl.when(e == 0)
    def _():
        o_ref[...] = jnp.zeros_like(o_ref)

    xb = h2_ref[...].astype(BF16)
    a = _dot(xb, w1_ref[0].astype(BF16))
    b = _dot(xb, w3_ref[0].astype(BF16))
    hdn = (a * jax.nn.sigmoid(a)) * b
    y = _dot(hdn.astype(BF16), w2_ref[0].astype(BF16))
    gates = gates_ref[...]
    lane = lax.broadcasted_iota(jnp.int32, gates.shape, 1)
    ge = jnp.sum(jnp.where(lane == e, gates, 0.0), axis=1, keepdims=True)
    o_ref[...] += ge * y


def _moe_dense_call(h2_all, gates, w1, w3, w2):
    n_tokens = h2_all.shape[0]
    tm = MOE_ROWS
    return pl.pallas_call(
        _moe_dense_kernel,
        grid=(n_tokens // tm, N_EXPERTS),
        in_specs=[
            pl.BlockSpec((tm, D_MODEL), lambda i, e: (i, 0)),
            pl.BlockSpec((tm, LANES), lambda i, e: (i, 0)),
            pl.BlockSpec((1, D_MODEL, D_EXPERT), lambda i, e: (e, 0, 0)),
            pl.BlockSpec((1, D_MODEL, D_EXPERT), lambda i, e: (e, 0, 0)),
            pl.BlockSpec((1, D_EXPERT, D_MODEL), lambda i, e: (e, 0, 0)),
        ],
        out_specs=pl.BlockSpec((tm, D_MODEL), lambda i, e: (i, 0)),
        out_shape=jax.ShapeDtypeStruct((n_tokens, D_MODEL), F32),
        compiler_params=pltpu.CompilerParams(
            dimension_semantics=("arbitrary", "arbitrary"), vmem_limit_bytes=VMEM_LIMIT),
        name="moe_dense",
    )(h2_all, gates, w1, w3, w2)


def _final_kernel(three_d, x1_ref, moe_ref, mod_ref, g_ref, b_ref, o_ref):
    mod = mod_ref[0] if three_d else mod_ref[...]
    g2 = mod[:, 5 * D_MODEL:]
    y = _layer_norm(ALPHA * x1_ref[...] + (1.0 + g2) * moe_ref[...], g_ref[...], b_ref[...])
    if three_d:
        o_ref[0] = y
    else:
        o_ref[...] = y


def _final_prompt(x1_all, moe_all, mod3, ln2_g, ln2_b, batch, seq):
    tm = 1024
    tiles = seq // tm
    return pl.pallas_call(
        functools.partial(_final_kernel, True),
        grid=(batch, tiles),
        in_specs=[pl.BlockSpec((tm, D_MODEL), lambda b, j: (b * tiles + j, 0)),
                  pl.BlockSpec((tm, D_MODEL), lambda b, j: (b * tiles + j, 0)),
                  pl.BlockSpec((1, 1, 6 * D_MODEL), lambda b, j: (b, 0, 0)),
                  _const_spec(ln2_g.shape), _const_spec(ln2_b.shape)],
        out_specs=pl.BlockSpec((1, tm, D_MODEL), lambda b, j: (b, j, 0)),
        out_shape=jax.ShapeDtypeStruct((batch, seq, D_MODEL), F32),
        compiler_params=pltpu.CompilerParams(
            dimension_semantics=("arbitrary", "arbitrary"), vmem_limit_bytes=VMEM_LIMIT),
        name="final_prompt",
    )(x1_all, moe_all, mod3, ln2_g, ln2_b)


def _final_sample(x1_all, moe_all, mod_tok, ln2_g, ln2_b, row0, n_rows):
    tm = 256
    blk0 = row0 // tm
    return pl.pallas_call(
        functools.partial(_final_kernel, False),
        grid=(n_rows // tm,),
        in_specs=[pl.BlockSpec((tm, D_MODEL), lambda i: (blk0 + i, 0)),
                  pl.BlockSpec((tm, D_MODEL), lambda i: (blk0 + i, 0)),
                  pl.BlockSpec((tm, 6 * D_MODEL), lambda i: (i, 0)),
                  _const_spec(ln2_g.shape), _const_spec(ln2_b.shape)],
        out_specs=pl.BlockSpec((tm, D_MODEL), lambda i: (i, 0)),
        out_shape=jax.ShapeDtypeStruct((n_rows, D_MODEL), F32),
        compiler_params=pltpu.CompilerParams(
            dimension_semantics=("arbitrary",), vmem_limit_bytes=VMEM_LIMIT),
        name="final_sample",
    )(x1_all, moe_all, mod_tok, ln2_g, ln2_b)


def _block_diag(w):
    heads, n, _ = w.shape
    eye = jnp.eye(heads, dtype=w.dtype)
    return (eye[:, None, :, None] * w[:, :, None, :]).reshape(heads * n, heads * n)


def kernel(x_prompt, x_sample, state_rglru_h, state_conv, c_prompt, c_sample, w_ada, b_ada, w_in, w_s, b_s, lnv_g, lnv_b, conv_w, conv_b, lru_wa, lru_ba, lru_wx, lru_bx, lru_lam, w_out, ln1_g, ln1_b, w_rg, b_rg, w_re, b_re, w1, w3, w2, ln2_g, ln2_b):
    batch, seq, _ = x_prompt.shape
    dec_batch, dec_seq, _ = x_sample.shape
    assert dec_seq == SUBLANES and seq % ROWS == 0 and (dec_batch * dec_seq) % ROWS == 0
    n_prompt = batch * seq
    n_sample = dec_batch * dec_seq
    n_tokens = n_prompt + n_sample
    l = 0

    mod = _ada_call(jnp.concatenate([c_prompt, c_sample], axis=0), w_ada[l], b_ada[l][None])
    mod_p3 = mod[:batch][:, None, :]
    mod_s_tok = jnp.repeat(mod[batch:], dec_seq, axis=0)

    tri = jnp.tril(jnp.ones((CHUNK, CHUNK), dtype=bool))
    ws_tri = jnp.where(tri[None], w_s[l], 0.0)
    reps_p = MXU_DIM // CHUNK
    eye_p = jnp.eye(reps_p, dtype=F32)
    wsm_p = (eye_p[None, :, None, :, None] * ws_tri[:, None, :, None, :]).reshape(
        A_HEADS, MXU_DIM, MXU_DIM).astype(BF16)
    reps_s = MXU_DIM // dec_seq
    eye_s = jnp.eye(reps_s, dtype=F32)
    wsm_s = (eye_s[None, :, None, :, None] * ws_tri[:, None, :dec_seq, None, :dec_seq]).reshape(
        A_HEADS, MXU_DIM, MXU_DIM).astype(BF16)
    bs_pos = jnp.repeat(jnp.transpose(b_s[l]), A_HEAD_DIM, axis=1)
    bsf_p = jnp.tile(bs_pos, (reps_p, 1))
    bsf_s = jnp.tile(bs_pos[:dec_seq], (reps_s, 1))

    wa_bd = _block_diag(lru_wa[l])
    wx_bd = _block_diag(lru_wx[l])
    wg = jnp.stack([
        jnp.concatenate([wa_bd[q * MXU_DIM:(q + 1) * MXU_DIM, q * MXU_DIM:(q + 1) * MXU_DIM],
                         wx_bd[q * MXU_DIM:(q + 1) * MXU_DIM, q * MXU_DIM:(q + 1) * MXU_DIM]], axis=1)
        for q in range(D_B // MXU_DIM)]).astype(BF16)
    bg = jnp.concatenate([lru_ba[l].reshape(1, D_B), lru_bx[l].reshape(1, D_B)], axis=1)

    wr = jnp.zeros((D_MODEL, LANES), F32)
    wr = wr.at[:, :N_GROUPS].set(w_rg[l]).at[:, N_GROUPS:N_GROUPS + N_EXPERTS].set(w_re[l])
    wr_hi = wr.astype(BF16)
    wr_lo = (wr - wr_hi.astype(F32)).astype(BF16)
    br = jnp.zeros((1, LANES), F32)
    br = br.at[0, :N_GROUPS].set(b_rg[l]).at[0, N_GROUPS:N_GROUPS + N_EXPERTS].set(b_re[l])

    common = dict(lnv_g=lnv_g[l][None], lnv_b=lnv_b[l][None], conv_w=conv_w[l], conv_b=conv_b[l][None],
                  wg=wg, bg=bg, lam=lru_lam[l][None], w_out=w_out[l].astype(BF16),
                  ln1_g=ln1_g[l][None], ln1_b=ln1_b[l][None], wr_hi=wr_hi, wr_lo=wr_lo, br=br)
    w_in_b = w_in[l].astype(BF16)
    weights_p = _mixer_weights(w_in_b, wsm_p, bsf_p, **common)
    weights_s = _mixer_weights(w_in_b, wsm_s, bsf_s, **common)

    x1_all, h2_all, lg_all, cst_p, hst_p = _mixer_prompt(x_prompt, mod_p3, weights_p, n_tokens)

    h0_tok = jnp.repeat(state_rglru_h[l], dec_seq, axis=0)
    cpad = jnp.concatenate(
        [jnp.zeros((dec_batch, SUBLANES - (CONV_W - 1), D_B), F32), state_conv[l]], axis=1
    ).reshape(n_sample, D_B)
    x1_all, h2_all, lg_all, xr_s, v_s, hs_s = _mixer_sample(
        x_sample.reshape(n_sample, D_MODEL), mod_s_tok, weights_s, h0_tok, cpad,
        x1_all, h2_all, lg_all, n_prompt)

    gates = _router_call(lg_all)
    moe_all = _moe_dense_call(h2_all, gates, w1[l], w3[l], w2[l])

    y_prompt = _final_prompt(x1_all, moe_all, mod_p3, ln2_g[l][None], ln2_b[l][None], batch, seq)
    y_sample = _final_sample(x1_all, moe_all, mod_s_tok, ln2_g[l][None], ln2_b[l][None],
                             n_prompt, n_sample).reshape(dec_batch, dec_seq, D_MODEL)

    new_h_prompt = hst_p[:, SUBLANES - 1, :][None]
    new_conv_prompt = cst_p[:, SUBLANES - (CONV_W - 1):, :][None]
    xr_s3 = xr_s.reshape(dec_batch, dec_seq, D_B)
    new_h_sample = hs_s.reshape(dec_batch, dec_seq, D_B)[:, dec_seq - 1, :][None]
    new_conv_sample = xr_s3[:, dec_seq - (CONV_W - 1):, :][None]
    new_chunk_v_sample = v_s.reshape(dec_batch, dec_seq, D_A)[None]
    return (y_prompt, y_sample, new_h_prompt, new_conv_prompt, new_h_sample, new_conv_sample,
            new_chunk_v_sample)
```

```python
import functools
import math

import jax
import jax.numpy as jnp
from jax import lax
from jax.experimental import pallas as pl
from jax.experimental.pallas import tpu as pltpu

D_MODEL = 1024
D_A = 512
A_HEADS = 4
A_HEAD_DIM = 128
CHUNK = 128
D_B = 512
B_HEADS = 8
B_HEAD_DIM = 64
CONV_W = 4
LRU_C = 8.0
D_IN = 2 * D_A + 2 * D_B
N_GROUPS = 4
EXPERTS_PER_GROUP = 8
N_EXPERTS = 32
TOP_K = 2
D_EXPERT = 512
LN_EPS = 1e-5
DEPTH = 1
ALPHA = (2.0 * DEPTH) ** 0.25

LANES = 128
SUBLANES = 8
MXU_DIM = 256
VMEM_LIMIT = 56 * 1024 * 1024

ROWS = 256
ROUTER_ROWS = 512
DISPATCH_ROWS = 256
FFN_ROWS = 256
COMBINE_ROWS = 256
BF16 = jnp.bfloat16
F32 = jnp.float32


def _dot(a, b):
    return jnp.dot(a, b, preferred_element_type=F32)


def _layer_norm(x, g, b):
    mu = jnp.mean(x, axis=-1, keepdims=True)
    xc = x - mu
    var = jnp.mean(xc * xc, axis=-1, keepdims=True)
    return xc * lax.rsqrt(var + LN_EPS) * g + b


def _gelu_tanh(x):
    c = math.sqrt(2.0 / math.pi)
    return x * (0.5 * (1.0 + jnp.tanh(c * (x + 0.044715 * (x * x * x)))))


def _const_spec(shape):
    nd = len(shape)
    return pl.BlockSpec(shape, lambda *_: (0,) * nd)


def _ada_kernel(c_ref, w_ref, b_ref, o_ref):
    c = c_ref[...]
    a = (c * jax.nn.sigmoid(c)).astype(BF16)
    o_ref[...] = _dot(a, w_ref[...].astype(BF16)) + b_ref[...]


def _ada_call(c_all, w_ada, b_ada):
    n_rows = c_all.shape[0]
    tn = 1536
    return pl.pallas_call(
        _ada_kernel,
        grid=(6 * D_MODEL // tn,),
        in_specs=[
            pl.BlockSpec((n_rows, D_MODEL), lambda j: (0, 0)),
            pl.BlockSpec((D_MODEL, tn), lambda j: (0, j)),
            pl.BlockSpec((1, tn), lambda j: (0, j)),
        ],
        out_specs=pl.BlockSpec((n_rows, tn), lambda j: (0, j)),
        out_shape=jax.ShapeDtypeStruct((n_rows, 6 * D_MODEL), F32),
        compiler_params=pltpu.CompilerParams(
            dimension_semantics=("arbitrary",), vmem_limit_bytes=VMEM_LIMIT),
        name="ada_mod",
    )(c_all, w_ada, b_ada)


def _mixer_kernel(sample, *refs):
    if sample:
        (x_ref, mod_ref, win_ref, wsm_ref, bsf_ref, lnvg_ref, lnvb_ref, convw_ref, convb_ref,
         wg_ref, bg_ref, lam_ref, wout_ref, ln1g_ref, ln1b_ref, wrh_ref, wrl_ref, br_ref,
         h0_ref, cpad_ref,
         x1_ref, h2_ref, lg_ref, xr_ref, v_ref, hs_ref) = refs
    else:
        (x_ref, mod_ref, win_ref, wsm_ref, bsf_ref, lnvg_ref, lnvb_ref, convw_ref, convb_ref,
         wg_ref, bg_ref, lam_ref, wout_ref, ln1g_ref, ln1b_ref, wrh_ref, wrl_ref, br_ref,
         x1_ref, h2_ref, lg_ref, cst_ref, hst_ref, prev_ref, hcar_ref) = refs

    rows = ROWS
    groups = rows // SUBLANES

    if sample:
        x = x_ref[...]
        mod = mod_ref[...]
    else:
        x = x_ref[0]
        mod = mod_ref[0]

        @pl.when(pl.program_id(1) == 0)
        def _():
            prev_ref[...] = jnp.zeros_like(prev_ref)
            hcar_ref[...] = jnp.zeros_like(hcar_ref)

    sh1, sc1, g1, sh2, sc2, g2 = [mod[:, i * D_MODEL:(i + 1) * D_MODEL] for i in range(6)]

    h = x * (1.0 + sc1) + sh1
    proj = _dot(h.astype(BF16), win_ref[...])
    u = proj[:, :D_A]
    v = _layer_norm(proj[:, D_A:2 * D_A], lnvg_ref[...], lnvb_ref[...])
    xr = proj[:, 2 * D_A:2 * D_A + D_B]
    yr = proj[:, 2 * D_A + D_B:]

    vb = v.astype(BF16)
    halves = []
    for r in range(rows // MXU_DIM):
        heads = [
            _dot(wsm_ref[hd], vb[r * MXU_DIM:(r + 1) * MXU_DIM, hd * A_HEAD_DIM:(hd + 1) * A_HEAD_DIM])
            for hd in range(A_HEADS)
        ]
        halves.append(jnp.concatenate(heads, axis=1) + bsf_ref[...])
    mixed = halves[0] if len(halves) == 1 else jnp.concatenate(halves, axis=0)
    out_a = u * mixed

    xr3 = xr.reshape(groups, SUBLANES, D_B)
    t3 = lax.broadcasted_iota(jnp.int32, (1, SUBLANES, D_B), 1)
    cw = convw_ref[...]
    if sample:
        cpad3 = cpad_ref[...].reshape(groups, SUBLANES, D_B)
    else:
        prev8 = prev_ref[...]
    xc3 = convb_ref[...][None]
    for k in range(CONV_W):
        s = CONV_W - 1 - k
        if s == 0:
            shifted = xr3
        else:
            rot = pltpu.roll(xr3, s, 1)
            if sample:
                rot_prev = pltpu.roll(cpad3, s, 1)
            else:
                first = pltpu.roll(prev8, s, 0)[None]
                rot_prev = jnp.concatenate([first, rot[:-1]], axis=0) if groups > 1 else first
            shifted = jnp.where(t3 >= s, rot, rot_prev)
        xc3 = xc3 + shifted * cw[k:k + 1, :][None]
    xc = xc3.reshape(rows, D_B)

    xcb = xc.astype(BF16)
    gq = [_dot(xcb[:, q * MXU_DIM:(q + 1) * MXU_DIM], wg_ref[q]) for q in range(D_B // MXU_DIM)]
    bg = bg_ref[...]
    pre_a = jnp.concatenate([g[:, :MXU_DIM] for g in gq], axis=1) + bg[:, :D_B]
    pre_x = jnp.concatenate([g[:, MXU_DIM:] for g in gq], axis=1) + bg[:, D_B:]
    gate_a = jax.nn.sigmoid(pre_a)
    gate_x = jax.nn.sigmoid(pre_x)
    nl = -lam_ref[...]
    softplus = jnp.maximum(nl, 0.0) + jnp.log1p(jnp.exp(-jnp.abs(nl)))
    log_a = -LRU_C * gate_a * softplus
    a = jnp.exp(log_a)
    mult = jnp.sqrt(-jnp.tanh(log_a) * (1.0 + a * a))
    bx = mult * (gate_x * xc)

    a3 = a.reshape(groups, SUBLANES, D_B)
    b3 = bx.reshape(groups, SUBLANES, D_B)
    for d in (1, 2, 4):
        m = t3 >= d
        a_sh = jnp.where(m, pltpu.roll(a3, d, 1), 1.0)
        b_sh = jnp.where(m, pltpu.roll(b3, d, 1), 0.0)
        b3 = a3 * b_sh + b3
        a3 = a3 * a_sh
    if sample:
        hs3 = a3 * h0_ref[...].reshape(groups, SUBLANES, D_B) + b3
    else:
        h_prev = hcar_ref[...][SUBLANES - 1:SUBLANES, :]
        hs_list = []
        for g in range(groups):
            hg = a3[g] * h_prev + b3[g]
            hs_list.append(hg)
            h_prev = hg[SUBLANES - 1:SUBLANES, :]
        hs3 = jnp.stack(hs_list, axis=0)
    hs = hs3.reshape(rows, D_B)

    out_b = hs * _gelu_tanh(yr)
    mix = _dot(out_a.astype(BF16), wout_ref[:D_A, :]) + _dot(out_b.astype(BF16), wout_ref[D_A:, :])
    x1 = _layer_norm(ALPHA * x + (1.0 + g1) * mix, ln1g_ref[...], ln1b_ref[...])
    h2 = x1 * (1.0 + sc2) + sh2

    h2_hi = h2.astype(BF16)
    h2_lo = (h2 - h2_hi.astype(F32)).astype(BF16)
    logits = (_dot(h2_hi, wrh_ref[...]) + _dot(h2_lo, wrh_ref[...]) + _dot(h2_hi, wrl_ref[...])
              + br_ref[...])

    x1_ref[...] = x1
    h2_ref[...] = h2
    lg_ref[...] = logits
    if sample:
        xr_ref[...] = xr
        v_ref[...] = v
        hs_ref[...] = hs
    else:
        last_x = xr3[groups - 1]
        last_h = hs3[groups - 1]
        prev_ref[...] = last_x
        hcar_ref[...] = last_h
        cst_ref[0] = last_x
        hst_ref[0] = last_h


def _mixer_weights(w_in, wsm, bsf, lnv_g, lnv_b, conv_w, conv_b, wg, bg, lam, w_out, ln1_g, ln1_b,
                   wr_hi, wr_lo, br):
    arrs = [w_in, wsm, bsf, lnv_g, lnv_b, conv_w, conv_b, wg, bg, lam, w_out, ln1_g, ln1_b,
            wr_hi, wr_lo, br]
    return arrs, [_const_spec(a.shape) for a in arrs]


def _mixer_prompt(x, mod3, weights):
    batch, seq, _ = x.shape
    n_tokens = batch * seq
    tiles = seq // ROWS
    arrs, specs = weights
    row_spec = lambda w: pl.BlockSpec((ROWS, w), lambda b, j: (b * tiles + j, 0))
    st_spec = pl.BlockSpec((1, SUBLANES, D_B), lambda b, j: (b, 0, 0))
    return pl.pallas_call(
        functools.partial(_mixer_kernel, False),
        grid=(batch, tiles),
        in_specs=[pl.BlockSpec((1, ROWS, D_MODEL), lambda b, j: (b, j, 0)),
                  pl.BlockSpec((1, 1, 6 * D_MODEL), lambda b, j: (b, 0, 0))] + specs,
        out_specs=[row_spec(D_MODEL), row_spec(D_MODEL), row_spec(LANES), st_spec, st_spec],
        out_shape=[jax.ShapeDtypeStruct((n_tokens, D_MODEL), F32),
                   jax.ShapeDtypeStruct((n_tokens, D_MODEL), F32),
                   jax.ShapeDtypeStruct((n_tokens, LANES), F32),
                   jax.ShapeDtypeStruct((batch, SUBLANES, D_B), F32),
                   jax.ShapeDtypeStruct((batch, SUBLANES, D_B), F32)],
        scratch_shapes=[pltpu.VMEM((SUBLANES, D_B), F32), pltpu.VMEM((SUBLANES, D_B), F32)],
        compiler_params=pltpu.CompilerParams(
            dimension_semantics=("arbitrary", "arbitrary"), vmem_limit_bytes=VMEM_LIMIT),
        name="mixer_prompt",
    )(x, mod3, *arrs)


def _mixer_sample(x2, mod_tok, weights, h0_tok, cpad):
    n_rows = x2.shape[0]
    arrs, specs = weights
    row = lambda w: pl.BlockSpec((ROWS, w), lambda i: (i, 0))
    return pl.pallas_call(
        functools.partial(_mixer_kernel, True),
        grid=(n_rows // ROWS,),
        in_specs=[row(D_MODEL), row(6 * D_MODEL)] + specs + [row(D_B), row(D_B)],
        out_specs=[row(D_MODEL), row(D_MODEL), row(LANES), row(D_B), row(D_B), row(D_B)],
        out_shape=[jax.ShapeDtypeStruct((n_rows, D_MODEL), F32),
                   jax.ShapeDtypeStruct((n_rows, D_MODEL), F32),
                   jax.ShapeDtypeStruct((n_rows, LANES), F32),
                   jax.ShapeDtypeStruct((n_rows, D_B), F32),
                   jax.ShapeDtypeStruct((n_rows, D_B), F32),
                   jax.ShapeDtypeStruct((n_rows, D_B), F32)],
        compiler_params=pltpu.CompilerParams(
            dimension_semantics=("arbitrary",), vmem_limit_bytes=VMEM_LIMIT),
        name="mixer_sample",
    )(x2, mod_tok, *arrs, h0_tok, cpad)


def _router_kernel(lg_ref, tri_ref, mi_ref, mf_ref, cnt_ref, carry_ref):
    @pl.when(pl.program_id(0) == 0)
    def _():
        carry_ref[...] = jnp.zeros_like(carry_ref)

    l = lg_ref[...]
    lane = lax.broadcasted_iota(jnp.int32, l.shape, 1)
    neg = -jnp.inf
    big = jnp.int32(LANES)
    lgm = jnp.where(lane < N_GROUPS, l, neg)
    m = jnp.max(lgm, axis=1, keepdims=True)
    gsel = jnp.min(jnp.where(lgm == m, lane, big), axis=1, keepdims=True)
    denom = jnp.sum(jnp.where(lane < N_GROUPS, jnp.exp(l - m), 0.0), axis=1, keepdims=True)
    pgsel = 1.0 / denom
    lo = N_GROUPS + EXPERTS_PER_GROUP * gsel
    le = jnp.where(lane >= lo, jnp.where(lane < lo + EXPERTS_PER_GROUP, l, neg), neg)
    v1 = jnp.max(le, axis=1, keepdims=True)
    i1 = jnp.min(jnp.where(le == v1, lane, big), axis=1, keepdims=True)
    le2 = jnp.where(lane == i1, neg, le)
    v2 = jnp.max(le2, axis=1, keepdims=True)
    i2 = jnp.min(jnp.where(le2 == v2, lane, big), axis=1, keepdims=True)
    ex = jnp.exp(v2 - v1)
    wk1 = pgsel / (1.0 + ex)
    wk2 = pgsel * ex / (1.0 + ex)
    e1 = i1 - N_GROUPS
    e2 = i2 - N_GROUPS

    oh1 = jnp.where(lane == e1, 1.0, 0.0)
    oh2 = jnp.where(lane == e2, 1.0, 0.0)
    oh = oh1 + oh2
    before = _dot(tri_ref[...], oh.astype(BF16)) + carry_ref[0:1, :]
    r1 = jnp.sum(oh1 * before, axis=1, keepdims=True).astype(jnp.int32)
    r2 = jnp.sum(oh2 * before, axis=1, keepdims=True).astype(jnp.int32)
    carry_ref[...] = carry_ref[...] + jnp.sum(oh, axis=0, keepdims=True)

    zero_i = jnp.zeros_like(lane)
    mi_ref[...] = jnp.where(lane == 0, e1, jnp.where(lane == 1, e2, jnp.where(
        lane == 2, r1, jnp.where(lane == 3, r2, zero_i))))
    mf_ref[...] = jnp.where(lane == 0, wk1, jnp.where(lane == 1, wk2, 0.0))
    cnt_ref[...] = carry_ref[...].astype(jnp.int32)


def _router_call(lg_all):
    n_tokens = lg_all.shape[0]
    tm = ROUTER_ROWS
    r = lax.broadcasted_iota(jnp.int32, (tm, tm), 0)
    c = lax.broadcasted_iota(jnp.int32, (tm, tm), 1)
    tri = jnp.where(c < r, 1.0, 0.0).astype(BF16)
    row = pl.BlockSpec((tm, LANES), lambda i: (i, 0))
    return pl.pallas_call(
        _router_kernel,
        grid=(n_tokens // tm,),
        in_specs=[row, _const_spec((tm, tm))],
        out_specs=[row, row, _const_spec((SUBLANES, LANES))],
        out_shape=[jax.ShapeDtypeStruct((n_tokens, LANES), jnp.int32),
                   jax.ShapeDtypeStruct((n_tokens, LANES), F32),
                   jax.ShapeDtypeStruct((SUBLANES, LANES), jnp.int32)],
        scratch_shapes=[pltpu.VMEM((SUBLANES, LANES), F32)],
        compiler_params=pltpu.CompilerParams(
            dimension_semantics=("arbitrary",), vmem_limit_bytes=VMEM_LIMIT),
        name="router",
    )(lg_all, tri)


def _dispatch_plan(meta_i, counts, n_pairs):
    e1, e2, r1, r2 = (meta_i[:, k] for k in range(4))
    ends = jnp.cumsum(counts)
    off = ends - counts
    pos1 = jnp.take(off, e1) + r1
    pos2 = jnp.take(off, e2) + r2

    n_tiles = n_pairs // FFN_ROWS
    n_items = n_tiles + N_EXPERTS
    first_tile = off // FFN_ROWS
    last_tile = (ends - 1) // FFN_ROWS
    items_e = jnp.where(counts > 0, last_tile - first_tile + 1, 0)
    items_end = jnp.cumsum(items_e)
    items_start = items_end - items_e
    n_work = items_end[-1]
    w = jnp.arange(n_items, dtype=jnp.int32)
    wv = jnp.minimum(w, n_work - 1)
    e_w = jnp.minimum(jnp.searchsorted(items_end, wv, side="right"), N_EXPERTS - 1).astype(jnp.int32)
    t_w = jnp.take(first_tile, e_w) + (wv - jnp.take(items_start, e_w))
    lo_w = jnp.maximum(jnp.take(off, e_w), t_w * FFN_ROWS) - t_w * FFN_ROWS
    hi_w = jnp.minimum(jnp.take(ends, e_w), (t_w + 1) * FFN_ROWS) - t_w * FFN_ROWS
    valid = w < n_work
    lo_w = jnp.where(valid, lo_w, 0)
    hi_w = jnp.where(valid, hi_w, 0)
    i32 = lambda a: a.astype(jnp.int32)
    return i32(pos1), i32(pos2), i32(t_w), i32(e_w), i32(lo_w), i32(hi_w)


def _scatter_kernel(tiles_p, pos1_s, pos2_s, h2p_ref, h2s_ref, xs_hbm, sem):
    i = pl.program_id(0)
    base = i * DISPATCH_ROWS

    def scatter_rows(src_ref):
        def row_copy(r, pos_s):
            p = pos_s[base + r]
            return pltpu.make_async_copy(src_ref.at[pl.ds(r, 1), :], xs_hbm.at[pl.ds(p, 1), :], sem)

        def start(r, carry):
            row_copy(r, pos1_s).start()
            row_copy(r, pos2_s).start()
            return carry

        def wait(r, carry):
            row_copy(r, pos1_s).wait()
            row_copy(r, pos2_s).wait()
            return carry

        lax.fori_loop(0, DISPATCH_ROWS, start, 0, unroll=8)
        lax.fori_loop(0, DISPATCH_ROWS, wait, 0, unroll=8)

    @pl.when(i < tiles_p)
    def _():
        scatter_rows(h2p_ref)

    @pl.when(i >= tiles_p)
    def _():
        scatter_rows(h2s_ref)


def _scatter_call(pos1, pos2, h2_p, h2_s):
    tm = DISPATCH_ROWS
    tiles_p = h2_p.shape[0] // tm
    tiles_s = h2_s.shape[0] // tm
    n_pairs = TOP_K * (h2_p.shape[0] + h2_s.shape[0])
    return pl.pallas_call(
        functools.partial(_scatter_kernel, tiles_p),
        grid_spec=pltpu.PrefetchScalarGridSpec(
            num_scalar_prefetch=2,
            grid=(tiles_p + tiles_s,),
            in_specs=[pl.BlockSpec((tm, D_MODEL), lambda i, *_: (jnp.minimum(i, tiles_p - 1), 0)),
                      pl.BlockSpec((tm, D_MODEL), lambda i, *_: (jnp.maximum(i - tiles_p, 0), 0))],
            out_specs=pl.BlockSpec(memory_space=pl.ANY),
            scratch_shapes=[pltpu.SemaphoreType.DMA(())],
        ),
        out_shape=jax.ShapeDtypeStruct((n_pairs, D_MODEL), F32),
        compiler_params=pltpu.CompilerParams(
            dimension_semantics=("arbitrary",), vmem_limit_bytes=VMEM_LIMIT),
        name="dispatch_scatter",
    )(pos1, pos2, h2_p, h2_s)


def _ffn_kernel(wt_s, we_s, wlo_s, whi_s, x_ref, w1_ref, w3_ref, w2_ref, o_ref, w13b_ref, w2b_ref):
    del wt_s
    w = pl.program_id(0)
    e = we_s[w]
    e_prev = we_s[jnp.maximum(w - 1, 0)]

    @pl.when(jnp.logical_or(w == 0, e != e_prev))
    def _():
        w13b_ref[:, :D_EXPERT] = w1_ref[0].astype(BF16)
        w13b_ref[:, D_EXPERT:] = w3_ref[0].astype(BF16)
        w2b_ref[...] = w2_ref[0].astype(BF16)

    lo = wlo_s[w]
    hi = whi_s[w]

    @pl.when(hi > lo)
    def _():
        h13 = _dot(x_ref[...].astype(BF16), w13b_ref[...])
        a = h13[:, :D_EXPERT]
        b = h13[:, D_EXPERT:]
        hdn = (a * jax.nn.sigmoid(a)) * b
        y = _dot(hdn.astype(BF16), w2b_ref[...])
        rows = lax.broadcasted_iota(jnp.int32, y.shape, 0)
        mine = jnp.logical_and(rows >= lo, rows < hi)

        @pl.when(lo == 0)
        def _():
            o_ref[...] = jnp.where(mine, y, 0.0)

        @pl.when(lo > 0)
        def _():
            o_ref[...] = jnp.where(mine, y, o_ref[...])


def _ffn_call(plan, xs, w1, w3, w2):
    t_w, e_w, lo_w, hi_w = plan
    n_pairs = xs.shape[0]
    tm = FFN_ROWS
    wspec = lambda shape: pl.BlockSpec((1,) + shape, lambda i, wt, we, wlo, whi: (we[i], 0, 0))
    row = pl.BlockSpec((tm, D_MODEL), lambda i, wt, we, wlo, whi: (wt[i], 0))
    return pl.pallas_call(
        _ffn_kernel,
        grid_spec=pltpu.PrefetchScalarGridSpec(
            num_scalar_prefetch=4,
            grid=(t_w.shape[0],),
            in_specs=[row, wspec((D_MODEL, D_EXPERT)), wspec((D_MODEL, D_EXPERT)),
                      wspec((D_EXPERT, D_MODEL))],
            out_specs=row,
            scratch_shapes=[pltpu.VMEM((D_MODEL, 2 * D_EXPERT), BF16),
                            pltpu.VMEM((D_EXPERT, D_MODEL), BF16)],
        ),
        out_shape=jax.ShapeDtypeStruct((n_pairs, D_MODEL), F32),
        compiler_params=pltpu.CompilerParams(
            dimension_semantics=("arbitrary",), vmem_limit_bytes=VMEM_LIMIT),
        name="expert_ffn",
    )(t_w, e_w, lo_w, hi_w, xs, w1, w3, w2)


def _combine_kernel(three_d, row0, pos1_s, pos2_s, x1_ref, wk_ref, mod_ref, g_ref, b_ref, ys_hbm,
                    o_ref, ybuf, sem):
    base = row0 + pl.program_id(0) * COMBINE_ROWS

    def row_copy(r, k, pos_s):
        p = pos_s[base + r]
        return pltpu.make_async_copy(ys_hbm.at[pl.ds(p, 1), :], ybuf.at[k, pl.ds(r, 1), :], sem)

    def start(r, carry):
        row_copy(r, 0, pos1_s).start()
        row_copy(r, 1, pos2_s).start()
        return carry

    def wait(r, carry):
        row_copy(r, 0, pos1_s).wait()
        row_copy(r, 1, pos2_s).wait()
        return carry

    lax.fori_loop(0, COMBINE_ROWS, start, 0, unroll=8)
    lax.fori_loop(0, COMBINE_ROWS, wait, 0, unroll=8)

    mod = mod_ref[0] if three_d else mod_ref[...]
    g2 = mod[:, 5 * D_MODEL:]
    wk = wk_ref[...]
    moe = wk[:, 0:1] * ybuf[0] + wk[:, 1:2] * ybuf[1]
    y = _layer_norm(ALPHA * x1_ref[...] + (1.0 + g2) * moe, g_ref[...], b_ref[...])
    if three_d:
        o_ref[0] = y
    else:
        o_ref[...] = y


def _combine_call(three_d, pos1, pos2, x1_all, mf_all, mod, ln2_g, ln2_b, ys, row0, batch, seq):
    tm = COMBINE_ROWS
    blk0 = row0 // tm
    if three_d:
        tiles = seq // tm
        grid = (batch * tiles,)
        mod_spec = pl.BlockSpec((1, 1, 6 * D_MODEL), lambda i, *_: (i // tiles, 0, 0))
        out_spec = pl.BlockSpec((1, tm, D_MODEL), lambda i, *_: (i // tiles, i % tiles, 0))
        out_shape = jax.ShapeDtypeStruct((batch, seq, D_MODEL), F32)
    else:
        grid = (batch * seq // tm,)
        mod_spec = pl.BlockSpec((tm, 6 * D_MODEL), lambda i, *_: (i, 0))
        out_spec = pl.BlockSpec((tm, D_MODEL), lambda i, *_: (i, 0))
        out_shape = jax.ShapeDtypeStruct((batch * seq, D_MODEL), F32)
    return pl.pallas_call(
        functools.partial(_combine_kernel, three_d, row0),
        grid_spec=pltpu.PrefetchScalarGridSpec(
            num_scalar_prefetch=2,
            grid=grid,
            in_specs=[pl.BlockSpec((tm, D_MODEL), lambda i, *_: (i, 0)),
                      pl.BlockSpec((tm, LANES), lambda i, *_: (blk0 + i, 0)),
                      mod_spec,
                      pl.BlockSpec(ln2_g.shape, lambda i, *_: (0, 0)),
                      pl.BlockSpec(ln2_b.shape, lambda i, *_: (0, 0)),
                      pl.BlockSpec(memory_space=pl.ANY)],
            out_specs=out_spec,
            scratch_shapes=[pltpu.VMEM((TOP_K, tm, D_MODEL), F32), pltpu.SemaphoreType.DMA(())],
        ),
        out_shape=out_shape,
        compiler_params=pltpu.CompilerParams(
            dimension_semantics=("arbitrary",), vmem_limit_bytes=VMEM_LIMIT),
        name="combine_prompt" if three_d else "combine_sample",
    )(pos1, pos2, x1_all, mf_all, mod, ln2_g, ln2_b, ys)


def _block_diag(w):
    heads, n, _ = w.shape
    eye = jnp.eye(heads, dtype=w.dtype)
    return (eye[:, None, :, None] * w[:, :, None, :]).reshape(heads * n, heads * n)


def kernel(x_prompt, x_sample, state_rglru_h, state_conv, c_prompt, c_sample, w_ada, b_ada, w_in, w_s, b_s, lnv_g, lnv_b, conv_w, conv_b, lru_wa, lru_ba, lru_wx, lru_bx, lru_lam, w_out, ln1_g, ln1_b, w_rg, b_rg, w_re, b_re, w1, w3, w2, ln2_g, ln2_b):
    batch, seq, _ = x_prompt.shape
    dec_batch, dec_seq, _ = x_sample.shape
    assert dec_seq == SUBLANES and seq % ROWS == 0 and (dec_batch * dec_seq) % ROWS == 0
    n_prompt = batch * seq
    n_sample = dec_batch * dec_seq
    n_tokens = n_prompt + n_sample
    n_pairs = TOP_K * n_tokens
    assert n_tokens % ROUTER_ROWS == 0 and n_pairs % FFN_ROWS == 0
    l = 0

    mod = _ada_call(jnp.concatenate([c_prompt, c_sample], axis=0), w_ada[l], b_ada[l][None])
    mod_p3 = mod[:batch][:, None, :]
    mod_s_tok = jnp.repeat(mod[batch:], dec_seq, axis=0)

    tri = jnp.tril(jnp.ones((CHUNK, CHUNK), dtype=bool))
    ws_tri = jnp.where(tri[None], w_s[l], 0.0)
    reps_p = MXU_DIM // CHUNK
    eye_p = jnp.eye(reps_p, dtype=F32)
    wsm_p = (eye_p[None, :, None, :, None] * ws_tri[:, None, :, None, :]).reshape(
        A_HEADS, MXU_DIM, MXU_DIM).astype(BF16)
    reps_s = MXU_DIM // dec_seq
    eye_s = jnp.eye(reps_s, dtype=F32)
    wsm_s = (eye_s[None, :, None, :, None] * ws_tri[:, None, :dec_seq, None, :dec_seq]).reshape(
        A_HEADS, MXU_DIM, MXU_DIM).astype(BF16)
    bs_pos = jnp.repeat(jnp.transpose(b_s[l]), A_HEAD_DIM, axis=1)
    bsf_p = jnp.tile(bs_pos, (reps_p, 1))
    bsf_s = jnp.tile(bs_pos[:dec_seq], (reps_s, 1))

    wa_bd = _block_diag(lru_wa[l])
    wx_bd = _block_diag(lru_wx[l])
    wg = jnp.stack([
        jnp.concatenate([wa_bd[q * MXU_DIM:(q + 1) * MXU_DIM, q * MXU_DIM:(q + 1) * MXU_DIM],
                         wx_bd[q * MXU_DIM:(q + 1) * MXU_DIM, q * MXU_DIM:(q + 1) * MXU_DIM]], axis=1)
        for q in range(D_B // MXU_DIM)]).astype(BF16)
    bg = jnp.concatenate([lru_ba[l].reshape(1, D_B), lru_bx[l].reshape(1, D_B)], axis=1)

    wr = jnp.zeros((D_MODEL, LANES), F32)
    wr = wr.at[:, :N_GROUPS].set(w_rg[l]).at[:, N_GROUPS:N_GROUPS + N_EXPERTS].set(w_re[l])
    wr_hi = wr.astype(BF16)
    wr_lo = (wr - wr_hi.astype(F32)).astype(BF16)
    br = jnp.zeros((1, LANES), F32)
    br = br.at[0, :N_GROUPS].set(b_rg[l]).at[0, N_GROUPS:N_GROUPS + N_EXPERTS].set(b_re[l])

    common = dict(lnv_g=lnv_g[l][None], lnv_b=lnv_b[l][None], conv_w=conv_w[l], conv_b=conv_b[l][None],
                  wg=wg, bg=bg, lam=lru_lam[l][None], w_out=w_out[l].astype(BF16),
                  ln1_g=ln1_g[l][None], ln1_b=ln1_b[l][None], wr_hi=wr_hi, wr_lo=wr_lo, br=br)
    w_in_b = w_in[l].astype(BF16)
    weights_p = _mixer_weights(w_in_b, wsm_p, bsf_p, **common)
    weights_s = _mixer_weights(w_in_b, wsm_s, bsf_s, **common)

    x1_p, h2_p, lg_p, cst_p, hst_p = _mixer_prompt(x_prompt, mod_p3, weights_p)

    h0_tok = jnp.repeat(state_rglru_h[l], dec_seq, axis=0)
    cpad = jnp.concatenate(
        [jnp.zeros((dec_batch, SUBLANES - (CONV_W - 1), D_B), F32), state_conv[l]], axis=1
    ).reshape(n_sample, D_B)
    x1_s, h2_s, lg_s, xr_s, v_s, hs_s = _mixer_sample(
        x_sample.reshape(n_sample, D_MODEL), mod_s_tok, weights_s, h0_tok, cpad)

    meta_i, meta_f, counts8 = _router_call(jnp.concatenate([lg_p, lg_s], axis=0))
    pos1, pos2, t_w, e_w, lo_w, hi_w = _dispatch_plan(meta_i, counts8[0, :N_EXPERTS], n_pairs)
    xs = _scatter_call(pos1, pos2, h2_p, h2_s)
    ys = _ffn_call((t_w, e_w, lo_w, hi_w), xs, w1[l], w3[l], w2[l])

    ln2g, ln2b = ln2_g[l][None], ln2_b[l][None]
    y_prompt = _combine_call(True, pos1, pos2, x1_p, meta_f, mod_p3, ln2g, ln2b, ys,
                             0, batch, seq)
    y_sample = _combine_call(False, pos1, pos2, x1_s, meta_f, mod_s_tok, ln2g, ln2b, ys,
                             n_prompt, dec_batch, dec_seq).reshape(dec_batch, dec_seq, D_MODEL)

    new_h_prompt = hst_p[:, SUBLANES - 1, :][None]
    new_conv_prompt = cst_p[:, SUBLANES - (CONV_W - 1):, :][None]
    xr_s3 = xr_s.reshape(dec_batch, dec_seq, D_B)
    new_h_sample = hs_s.reshape(dec_batch, dec_seq, D_B)[:, dec_seq - 1, :][None]
    new_conv_sample = xr_s3[:, dec_seq - (CONV_W - 1):, :][None]
    new_chunk_v_sample = v_s.reshape(dec_batch, dec_seq, D_A)[None]
    return (y_prompt, y_sample, new_h_prompt, new_conv_prompt, new_h_sample, new_conv_sample,
            new_chunk_v_sample)
```

```python
import functools
import math

import jax
import jax.numpy as jnp
from jax import lax
from jax.experimental import pallas as pl
from jax.experimental.pallas import tpu as pltpu

D_MODEL = 1024
D_A = 512
A_HEADS = 4
A_HEAD_DIM = 128
CHUNK = 128
D_B = 512
B_HEADS = 8
B_HEAD_DIM = 64
CONV_W = 4
LRU_C = 8.0
D_IN = 2 * D_A + 2 * D_B
N_GROUPS = 4
EXPERTS_PER_GROUP = 8
N_EXPERTS = 32
TOP_K = 2
D_EXPERT = 512
LN_EPS = 1e-5
DEPTH = 1
ALPHA = (2.0 * DEPTH) ** 0.25

LANES = 128
SUBLANES = 8
MXU_DIM = 256
VMEM_LIMIT = 56 * 1024 * 1024

ROWS = 256
ROUTER_COLS = 1024
LOGIT_ROWS = SUBLANES + N_EXPERTS
DISPATCH_ROWS = 256
FFN_ROWS = 256
COMBINE_ROWS = 256
BF16 = jnp.bfloat16
F32 = jnp.float32


def _dot(a, b):
    return jnp.dot(a, b, preferred_element_type=F32)


def _dot_nt(a, b):
    return lax.dot_general(a, b, (((1,), (1,)), ((), ())), preferred_element_type=F32)


def _layer_norm(x, g, b):
    mu = jnp.mean(x, axis=-1, keepdims=True)
    xc = x - mu
    var = jnp.mean(xc * xc, axis=-1, keepdims=True)
    return xc * lax.rsqrt(var + LN_EPS) * g + b


def _gelu_tanh(x):
    c = math.sqrt(2.0 / math.pi)
    return x * (0.5 * (1.0 + jnp.tanh(c * (x + 0.044715 * (x * x * x)))))


def _const_spec(shape):
    nd = len(shape)
    return pl.BlockSpec(shape, lambda *_: (0,) * nd)


def _ada_kernel(c_ref, w_ref, b_ref, o_ref):
    c = c_ref[...]
    a = (c * jax.nn.sigmoid(c)).astype(BF16)
    o_ref[...] = _dot(a, w_ref[...].astype(BF16)) + b_ref[...]


def _ada_call(c_all, w_ada, b_ada):
    n_rows = c_all.shape[0]
    tn = 1536
    return pl.pallas_call(
        _ada_kernel,
        grid=(6 * D_MODEL // tn,),
        in_specs=[
            pl.BlockSpec((n_rows, D_MODEL), lambda j: (0, 0)),
            pl.BlockSpec((D_MODEL, tn), lambda j: (0, j)),
            pl.BlockSpec((1, tn), lambda j: (0, j)),
        ],
        out_specs=pl.BlockSpec((n_rows, tn), lambda j: (0, j)),
        out_shape=jax.ShapeDtypeStruct((n_rows, 6 * D_MODEL), F32),
        compiler_params=pltpu.CompilerParams(
            dimension_semantics=("arbitrary",), vmem_limit_bytes=VMEM_LIMIT),
        name="ada_mod",
    )(c_all, w_ada, b_ada)


def _mixer_kernel(sample, *refs):
    if sample:
        (x_ref, mod_ref, win_ref, wsm_ref, bsf_ref, lnvg_ref, lnvb_ref, convw_ref, convb_ref,
         wg_ref, bg_ref, lam_ref, wout_ref, ln1g_ref, ln1b_ref, wrh_ref, wrl_ref, br_ref,
         h0_ref, cpad_ref,
         x1_ref, h2_ref, lg_ref, xr_ref, v_ref, hs_ref) = refs
    else:
        (x_ref, mod_ref, win_ref, wsm_ref, bsf_ref, lnvg_ref, lnvb_ref, convw_ref, convb_ref,
         wg_ref, bg_ref, lam_ref, wout_ref, ln1g_ref, ln1b_ref, wrh_ref, wrl_ref, br_ref,
         x1_ref, h2_ref, lg_ref, cst_ref, hst_ref, prev_ref, hcar_ref) = refs

    rows = ROWS
    groups = rows // SUBLANES

    if sample:
        x = x_ref[...]
        mod = mod_ref[...]
    else:
        x = x_ref[0]
        mod = mod_ref[0]

        @pl.when(pl.program_id(1) == 0)
        def _():
            prev_ref[...] = jnp.zeros_like(prev_ref)
            hcar_ref[...] = jnp.zeros_like(hcar_ref)

    sh1, sc1, g1, sh2, sc2, g2 = [mod[:, i * D_MODEL:(i + 1) * D_MODEL] for i in range(6)]

    h = x * (1.0 + sc1) + sh1
    proj = _dot(h.astype(BF16), win_ref[...])
    u = proj[:, :D_A]
    v = _layer_norm(proj[:, D_A:2 * D_A], lnvg_ref[...], lnvb_ref[...])
    xr = proj[:, 2 * D_A:2 * D_A + D_B]
    yr = proj[:, 2 * D_A + D_B:]

    vb = v.astype(BF16)
    halves = []
    for r in range(rows // MXU_DIM):
        heads = [
            _dot(wsm_ref[hd], vb[r * MXU_DIM:(r + 1) * MXU_DIM, hd * A_HEAD_DIM:(hd + 1) * A_HEAD_DIM])
            for hd in range(A_HEADS)
        ]
        halves.append(jnp.concatenate(heads, axis=1) + bsf_ref[...])
    mixed = halves[0] if len(halves) == 1 else jnp.concatenate(halves, axis=0)
    out_a = u * mixed

    xr3 = xr.reshape(groups, SUBLANES, D_B)
    t3 = lax.broadcasted_iota(jnp.int32, (1, SUBLANES, D_B), 1)
    cw = convw_ref[...]
    if sample:
        cpad3 = cpad_ref[...].reshape(groups, SUBLANES, D_B)
    else:
        prev8 = prev_ref[...]
    xc3 = convb_ref[...][None]
    for k in range(CONV_W):
        s = CONV_W - 1 - k
        if s == 0:
            shifted = xr3
        else:
            rot = pltpu.roll(xr3, s, 1)
            if sample:
                rot_prev = pltpu.roll(cpad3, s, 1)
            else:
                first = pltpu.roll(prev8, s, 0)[None]
                rot_prev = jnp.concatenate([first, rot[:-1]], axis=0) if groups > 1 else first
            shifted = jnp.where(t3 >= s, rot, rot_prev)
        xc3 = xc3 + shifted * cw[k:k + 1, :][None]
    xc = xc3.reshape(rows, D_B)

    xcb = xc.astype(BF16)
    gq = [_dot(xcb[:, q * MXU_DIM:(q + 1) * MXU_DIM], wg_ref[q]) for q in range(D_B // MXU_DIM)]
    bg = bg_ref[...]
    pre_a = jnp.concatenate([g[:, :MXU_DIM] for g in gq], axis=1) + bg[:, :D_B]
    pre_x = jnp.concatenate([g[:, MXU_DIM:] for g in gq], axis=1) + bg[:, D_B:]
    gate_a = jax.nn.sigmoid(pre_a)
    gate_x = jax.nn.sigmoid(pre_x)
    nl = -lam_ref[...]
    softplus = jnp.maximum(nl, 0.0) + jnp.log1p(jnp.exp(-jnp.abs(nl)))
    log_a = -LRU_C * gate_a * softplus
    a = jnp.exp(log_a)
    mult = jnp.sqrt(-jnp.tanh(log_a) * (1.0 + a * a))
    bx = mult * (gate_x * xc)

    a3 = a.reshape(groups, SUBLANES, D_B)
    b3 = bx.reshape(groups, SUBLANES, D_B)
    for d in (1, 2, 4):
        m = t3 >= d
        a_sh = jnp.where(m, pltpu.roll(a3, d, 1), 1.0)
        b_sh = jnp.where(m, pltpu.roll(b3, d, 1), 0.0)
        b3 = a3 * b_sh + b3
        a3 = a3 * a_sh
    if sample:
        hs3 = a3 * h0_ref[...].reshape(groups, SUBLANES, D_B) + b3
    else:
        h_prev = hcar_ref[...][SUBLANES - 1:SUBLANES, :]
        hs_list = []
        for g in range(groups):
            hg = a3[g] * h_prev + b3[g]
            hs_list.append(hg)
            h_prev = hg[SUBLANES - 1:SUBLANES, :]
        hs3 = jnp.stack(hs_list, axis=0)
    hs = hs3.reshape(rows, D_B)

    out_b = hs * _gelu_tanh(yr)
    mix = _dot(out_a.astype(BF16), wout_ref[:D_A, :]) + _dot(out_b.astype(BF16), wout_ref[D_A:, :])
    x1 = _layer_norm(ALPHA * x + (1.0 + g1) * mix, ln1g_ref[...], ln1b_ref[...])
    h2 = x1 * (1.0 + sc2) + sh2

    h2_hi = h2.astype(BF16)
    h2_lo = (h2 - h2_hi.astype(F32)).astype(BF16)
    logits = (_dot_nt(wrh_ref[...], h2_hi) + _dot_nt(wrh_ref[...], h2_lo)
              + _dot_nt(wrl_ref[...], h2_hi) + br_ref[...])

    x1_ref[...] = x1
    h2_ref[...] = h2
    lg_ref[...] = logits
    if sample:
        xr_ref[...] = xr
        v_ref[...] = v
        hs_ref[...] = hs
    else:
        last_x = xr3[groups - 1]
        last_h = hs3[groups - 1]
        prev_ref[...] = last_x
        hcar_ref[...] = last_h
        cst_ref[0] = last_x
        hst_ref[0] = last_h


def _mixer_weights(w_in, wsm, bsf, lnv_g, lnv_b, conv_w, conv_b, wg, bg, lam, w_out, ln1_g, ln1_b,
                   wr_hi, wr_lo, br):
    arrs = [w_in, wsm, bsf, lnv_g, lnv_b, conv_w, conv_b, wg, bg, lam, w_out, ln1_g, ln1_b,
            wr_hi, wr_lo, br]
    return arrs, [_const_spec(a.shape) for a in arrs]


def _mixer_prompt(x, mod3, weights):
    batch, seq, _ = x.shape
    n_tokens = batch * seq
    tiles = seq // ROWS
    arrs, specs = weights
    row_spec = lambda w: pl.BlockSpec((ROWS, w), lambda b, j: (b * tiles + j, 0))
    st_spec = pl.BlockSpec((1, SUBLANES, D_B), lambda b, j: (b, 0, 0))
    return pl.pallas_call(
        functools.partial(_mixer_kernel, False),
        grid=(batch, tiles),
        in_specs=[pl.BlockSpec((1, ROWS, D_MODEL), lambda b, j: (b, j, 0)),
                  pl.BlockSpec((1, 1, 6 * D_MODEL), lambda b, j: (b, 0, 0))] + specs,
        out_specs=[row_spec(D_MODEL), row_spec(D_MODEL),
                   pl.BlockSpec((LOGIT_ROWS, ROWS), lambda b, j: (0, b * tiles + j)), st_spec, st_spec],
        out_shape=[jax.ShapeDtypeStruct((n_tokens, D_MODEL), F32),
                   jax.ShapeDtypeStruct((n_tokens, D_MODEL), F32),
                   jax.ShapeDtypeStruct((LOGIT_ROWS, n_tokens), F32),
                   jax.ShapeDtypeStruct((batch, SUBLANES, D_B), F32),
                   jax.ShapeDtypeStruct((batch, SUBLANES, D_B), F32)],
        scratch_shapes=[pltpu.VMEM((SUBLANES, D_B), F32), pltpu.VMEM((SUBLANES, D_B), F32)],
        compiler_params=pltpu.CompilerParams(
            dimension_semantics=("arbitrary", "arbitrary"), vmem_limit_bytes=VMEM_LIMIT),
        name="mixer_prompt",
    )(x, mod3, *arrs)


def _mixer_sample(x2, mod_tok, weights, h0_tok, cpad):
    n_rows = x2.shape[0]
    arrs, specs = weights
    row = lambda w: pl.BlockSpec((ROWS, w), lambda i: (i, 0))
    return pl.pallas_call(
        functools.partial(_mixer_kernel, True),
        grid=(n_rows // ROWS,),
        in_specs=[row(D_MODEL), row(6 * D_MODEL)] + specs + [row(D_B), row(D_B)],
        out_specs=[row(D_MODEL), row(D_MODEL), pl.BlockSpec((LOGIT_ROWS, ROWS), lambda i: (0, i)),
                   row(D_B), row(D_B), row(D_B)],
        out_shape=[jax.ShapeDtypeStruct((n_rows, D_MODEL), F32),
                   jax.ShapeDtypeStruct((n_rows, D_MODEL), F32),
                   jax.ShapeDtypeStruct((LOGIT_ROWS, n_rows), F32),
                   jax.ShapeDtypeStruct((n_rows, D_B), F32),
                   jax.ShapeDtypeStruct((n_rows, D_B), F32),
                   jax.ShapeDtypeStruct((n_rows, D_B), F32)],
        compiler_params=pltpu.CompilerParams(
            dimension_semantics=("arbitrary",), vmem_limit_bytes=VMEM_LIMIT),
        name="mixer_sample",
    )(x2, mod_tok, *arrs, h0_tok, cpad)


def _router_kernel(lg_ref, upper_ref, lower_ref, mi_ref, mf_ref, cnt_ref, tot_ref, run_ref):
    phase = pl.program_id(0)
    step = pl.program_id(1)
    l = lg_ref[...]
    cols = l.shape[1]
    row8 = lax.broadcasted_iota(jnp.int32, (SUBLANES, cols), 0)
    neg = -jnp.inf
    big = jnp.int32(SUBLANES)

    def first_max(vals):
        top = jnp.max(vals, axis=0, keepdims=True)
        return top, jnp.min(jnp.where(vals == top, row8, big), axis=0, keepdims=True)

    is_group = row8 < N_GROUPS
    lg = l[:SUBLANES]
    m, gsel = first_max(jnp.where(is_group, lg, neg))
    denom = jnp.sum(jnp.where(is_group, jnp.exp(lg - m), 0.0), axis=0, keepdims=True)
    pgsel = 1.0 / denom
    le = l[SUBLANES:2 * SUBLANES]
    for g in range(1, N_GROUPS):
        le = jnp.where(gsel == g, l[(g + 1) * SUBLANES:(g + 2) * SUBLANES], le)
    v1, i1 = first_max(le)
    v2, i2 = first_max(jnp.where(row8 == i1, neg, le))
    ex = jnp.exp(v2 - v1)
    wk1 = pgsel / (1.0 + ex)
    wk2 = pgsel * ex / (1.0 + ex)
    e1 = gsel * EXPERTS_PER_GROUP + i1
    e2 = gsel * EXPERTS_PER_GROUP + i2
    row_e = lax.broadcasted_iota(jnp.int32, (N_EXPERTS, cols), 0)
    oh1 = jnp.where(row_e == e1, 1.0, 0.0)
    oh2 = jnp.where(row_e == e2, 1.0, 0.0)
    oh = oh1 + oh2

    @pl.when(jnp.logical_and(phase == 0, step == 0))
    def _():
        tot_ref[...] = jnp.zeros_like(tot_ref)

    @pl.when(phase == 0)
    def _():
        tot_ref[...] = tot_ref[...] + jnp.sum(oh, axis=1, keepdims=True)

    @pl.when(jnp.logical_and(phase == 1, step == 0))
    def _():
        tot = tot_ref[...]
        hi = (tot * (1.0 / MXU_DIM)).astype(jnp.int32).astype(F32)
        lo = tot - hi * MXU_DIM
        run_ref[...] = (_dot(lower_ref[...], hi.astype(BF16)) * MXU_DIM
                        + _dot(lower_ref[...], lo.astype(BF16)))

    @pl.when(phase == 1)
    def _():
        blocks = cols // LANES
        stacked = jnp.concatenate([oh[:, b * LANES:(b + 1) * LANES] for b in range(blocks)], axis=0)
        local = _dot(stacked.astype(BF16), upper_ref[...])
        run = run_ref[...]
        p1, p2 = [], []
        for b in range(blocks):
            sl = slice(b * LANES, (b + 1) * LANES)
            before = local[b * N_EXPERTS:(b + 1) * N_EXPERTS] + run
            p1.append(jnp.sum(oh1[:, sl] * before, axis=0, keepdims=True))
            p2.append(jnp.sum(oh2[:, sl] * before, axis=0, keepdims=True))
            run = run + jnp.sum(oh[:, sl], axis=1, keepdims=True)
        run_ref[...] = run
        pos1 = jnp.concatenate(p1, axis=1).astype(jnp.int32)
        pos2 = jnp.concatenate(p2, axis=1).astype(jnp.int32)
        zero_i = jnp.zeros_like(row8)
        mi_ref[...] = jnp.where(row8 == 0, pos1, jnp.where(row8 == 1, pos2, jnp.where(
            row8 == 2, e1, jnp.where(row8 == 3, e2, zero_i))))
        mf_ref[...] = jnp.where(row8 == 0, wk1, jnp.where(row8 == 1, wk2, 0.0))
        cnt_ref[...] = tot_ref[...].astype(jnp.int32)


def _router_call(lg_t):
    n_tokens = lg_t.shape[1]
    tc = ROUTER_COLS
    r = lax.broadcasted_iota(jnp.int32, (LANES, LANES), 0)
    c = lax.broadcasted_iota(jnp.int32, (LANES, LANES), 1)
    upper = jnp.where(r < c, 1.0, 0.0).astype(BF16)
    lower = jnp.where(c < r, 1.0, 0.0)[:N_EXPERTS, :N_EXPERTS].astype(BF16)
    col = pl.BlockSpec((SUBLANES, tc), lambda p, i: (0, i * p))
    return pl.pallas_call(
        _router_kernel,
        grid=(2, n_tokens // tc),
        in_specs=[pl.BlockSpec((LOGIT_ROWS, tc), lambda p, i: (0, i)),
                  _const_spec((LANES, LANES)), _const_spec((N_EXPERTS, N_EXPERTS))],
        out_specs=[col, col, _const_spec((N_EXPERTS, LANES))],
        out_shape=[jax.ShapeDtypeStruct((SUBLANES, n_tokens), jnp.int32),
                   jax.ShapeDtypeStruct((SUBLANES, n_tokens), F32),
                   jax.ShapeDtypeStruct((N_EXPERTS, LANES), jnp.int32)],
        scratch_shapes=[pltpu.VMEM((N_EXPERTS, LANES), F32), pltpu.VMEM((N_EXPERTS, LANES), F32)],
        compiler_params=pltpu.CompilerParams(
            dimension_semantics=("arbitrary", "arbitrary"), vmem_limit_bytes=VMEM_LIMIT),
        name="router",
    )(lg_t, upper, lower)


def _ffn_plan(counts, n_pairs):
    ends = jnp.cumsum(counts)
    off = ends - counts
    ids = jnp.arange(N_EXPERTS, dtype=jnp.int32)

    n_tiles = n_pairs // FFN_ROWS
    n_items = n_tiles + N_EXPERTS
    first_tile = off // FFN_ROWS
    last_tile = (ends - 1) // FFN_ROWS
    items_e = jnp.where(counts > 0, last_tile - first_tile + 1, 0)
    items_end = jnp.cumsum(items_e)
    items_start = items_end - items_e
    n_work = items_end[-1]
    w = jnp.arange(n_items, dtype=jnp.int32)
    wv = jnp.minimum(w, n_work - 1)
    e_w = jnp.sum((items_end[None, :] <= wv[:, None]).astype(jnp.int32), axis=1)
    e_w = jnp.minimum(e_w, N_EXPERTS - 1)
    sel = e_w[:, None] == ids[None, :]
    pick = lambda a: jnp.sum(jnp.where(sel, a[None, :], 0), axis=1)
    t_w = pick(first_tile) + (wv - pick(items_start))
    lo_w = jnp.maximum(pick(off), t_w * FFN_ROWS) - t_w * FFN_ROWS
    hi_w = jnp.minimum(pick(ends), (t_w + 1) * FFN_ROWS) - t_w * FFN_ROWS
    valid = w < n_work
    lo_w = jnp.where(valid, lo_w, 0)
    hi_w = jnp.where(valid, hi_w, 0)
    i32 = lambda a: a.astype(jnp.int32)
    return i32(t_w), i32(e_w), i32(lo_w), i32(hi_w)


def _scatter_kernel(tiles_p, pos1_s, pos2_s, h2p_ref, h2s_ref, xs_hbm, sem):
    i = pl.program_id(0)
    base = i * DISPATCH_ROWS

    def scatter_rows(src_ref):
        def row_copy(r, pos_s):
            p = pos_s[base + r]
            return pltpu.make_async_copy(src_ref.at[pl.ds(r, 1), :], xs_hbm.at[pl.ds(p, 1), :], sem)

        def start(r, carry):
            row_copy(r, pos1_s).start(priority=0)
            row_copy(r, pos2_s).start(priority=1)
            return carry

        def wait(r, carry):
            row_copy(r, pos1_s).wait()
            row_copy(r, pos2_s).wait()
            return carry

        lax.fori_loop(0, DISPATCH_ROWS, start, 0, unroll=8)
        lax.fori_loop(0, DISPATCH_ROWS, wait, 0, unroll=8)

    @pl.when(i < tiles_p)
    def _():
        scatter_rows(h2p_ref)

    @pl.when(i >= tiles_p)
    def _():
        scatter_rows(h2s_ref)


def _scatter_call(pos1, pos2, h2_p, h2_s):
    tm = DISPATCH_ROWS
    tiles_p = h2_p.shape[0] // tm
    tiles_s = h2_s.shape[0] // tm
    n_pairs = TOP_K * (h2_p.shape[0] + h2_s.shape[0])
    return pl.pallas_call(
        functools.partial(_scatter_kernel, tiles_p),
        grid_spec=pltpu.PrefetchScalarGridSpec(
            num_scalar_prefetch=2,
            grid=(tiles_p + tiles_s,),
            in_specs=[pl.BlockSpec((tm, D_MODEL), lambda i, *_: (jnp.minimum(i, tiles_p - 1), 0)),
                      pl.BlockSpec((tm, D_MODEL), lambda i, *_: (jnp.maximum(i - tiles_p, 0), 0))],
            out_specs=pl.BlockSpec(memory_space=pl.ANY),
            scratch_shapes=[pltpu.SemaphoreType.DMA(())],
        ),
        out_shape=jax.ShapeDtypeStruct((n_pairs, D_MODEL), F32),
        compiler_params=pltpu.CompilerParams(
            dimension_semantics=("arbitrary",), vmem_limit_bytes=VMEM_LIMIT),
        name="dispatch_scatter",
    )(pos1, pos2, h2_p, h2_s)


def _ffn_kernel(wt_s, we_s, wlo_s, whi_s, x_ref, w1_ref, w3_ref, w2_ref, o_ref, w13b_ref, w2b_ref):
    del wt_s
    w = pl.program_id(0)
    e = we_s[w]
    e_prev = we_s[jnp.maximum(w - 1, 0)]

    @pl.when(jnp.logical_or(w == 0, e != e_prev))
    def _():
        w13b_ref[:, :D_EXPERT] = w1_ref[0].astype(BF16)
        w13b_ref[:, D_EXPERT:] = w3_ref[0].astype(BF16)
        w2b_ref[...] = w2_ref[0].astype(BF16)

    lo = wlo_s[w]
    hi = whi_s[w]

    @pl.when(hi > lo)
    def _():
        h13 = _dot(x_ref[...].astype(BF16), w13b_ref[...])
        a = h13[:, :D_EXPERT]
        b = h13[:, D_EXPERT:]
        hdn = (a * jax.nn.sigmoid(a)) * b
        y = _dot(hdn.astype(BF16), w2b_ref[...])
        rows = lax.broadcasted_iota(jnp.int32, y.shape, 0)
        mine = jnp.logical_and(rows >= lo, rows < hi)

        @pl.when(lo == 0)
        def _():
            o_ref[...] = jnp.where(mine, y, 0.0)

        @pl.when(lo > 0)
        def _():
            o_ref[...] = jnp.where(mine, y, o_ref[...])


def _ffn_call(plan, xs, w1, w3, w2):
    t_w, e_w, lo_w, hi_w = plan
    n_pairs = xs.shape[0]
    tm = FFN_ROWS
    wspec = lambda shape: pl.BlockSpec((1,) + shape, lambda i, wt, we, wlo, whi: (we[i], 0, 0))
    row = pl.BlockSpec((tm, D_MODEL), lambda i, wt, we, wlo, whi: (wt[i], 0))
    return pl.pallas_call(
        _ffn_kernel,
        grid_spec=pltpu.PrefetchScalarGridSpec(
            num_scalar_prefetch=4,
            grid=(t_w.shape[0],),
            in_specs=[row, wspec((D_MODEL, D_EXPERT)), wspec((D_MODEL, D_EXPERT)),
                      wspec((D_EXPERT, D_MODEL))],
            out_specs=row,
            scratch_shapes=[pltpu.VMEM((D_MODEL, 2 * D_EXPERT), BF16),
                            pltpu.VMEM((D_EXPERT, D_MODEL), BF16)],
        ),
        out_shape=jax.ShapeDtypeStruct((n_pairs, D_MODEL), F32),
        compiler_params=pltpu.CompilerParams(
            dimension_semantics=("arbitrary",), vmem_limit_bytes=VMEM_LIMIT),
        name="expert_ffn",
    )(t_w, e_w, lo_w, hi_w, xs, w1, w3, w2)


def _combine_kernel(three_d, row0, n_steps, pos1_s, pos2_s, x1_ref, wk_ref, mod_ref, g_ref, b_ref,
                    ys_hbm, o_ref, ybuf, sems):
    step = pl.program_id(0)
    slot = step % 2

    def gather(tile, tile_slot):
        base = row0 + tile * COMBINE_ROWS

        def row_copy(r, k, pos_s):
            p = pos_s[base + r]
            return pltpu.make_async_copy(ys_hbm.at[pl.ds(p, 1), :],
                                         ybuf.at[tile_slot, k, pl.ds(r, 1), :], sems.at[tile_slot])

        def start(r, carry):
            row_copy(r, 0, pos1_s).start(priority=0)
            row_copy(r, 1, pos2_s).start(priority=1)
            return carry

        def wait(r, carry):
            row_copy(r, 0, pos1_s).wait()
            row_copy(r, 1, pos2_s).wait()
            return carry

        return start, wait

    @pl.when(step == 0)
    def _():
        lax.fori_loop(0, COMBINE_ROWS, gather(step, slot)[0], 0, unroll=8)

    @pl.when(step + 1 < n_steps)
    def _():
        lax.fori_loop(0, COMBINE_ROWS, gather(step + 1, 1 - slot)[0], 0, unroll=8)

    lax.fori_loop(0, COMBINE_ROWS, gather(step, slot)[1], 0, unroll=8)

    mod = mod_ref[0] if three_d else mod_ref[...]
    g2 = mod[:, 5 * D_MODEL:]
    wk = wk_ref[...]
    moe = wk[:, 0:1] * ybuf[slot, 0] + wk[:, 1:2] * ybuf[slot, 1]
    y = _layer_norm(ALPHA * x1_ref[...] + (1.0 + g2) * moe, g_ref[...], b_ref[...])
    if three_d:
        o_ref[0] = y
    else:
        o_ref[...] = y


def _combine_call(three_d, pos1, pos2, x1_all, mf_all, mod, ln2_g, ln2_b, ys, row0, batch, seq):
    tm = COMBINE_ROWS
    blk0 = row0 // tm
    if three_d:
        tiles = seq // tm
        grid = (batch * tiles,)
        mod_spec = pl.BlockSpec((1, 1, 6 * D_MODEL), lambda i, *_: (i // tiles, 0, 0))
        out_spec = pl.BlockSpec((1, tm, D_MODEL), lambda i, *_: (i // tiles, i % tiles, 0))
        out_shape = jax.ShapeDtypeStruct((batch, seq, D_MODEL), F32)
    else:
        grid = (batch * seq // tm,)
        mod_spec = pl.BlockSpec((tm, 6 * D_MODEL), lambda i, *_: (i, 0))
        out_spec = pl.BlockSpec((tm, D_MODEL), lambda i, *_: (i, 0))
        out_shape = jax.ShapeDtypeStruct((batch * seq, D_MODEL), F32)
    return pl.pallas_call(
        functools.partial(_combine_kernel, three_d, row0, grid[0]),
        grid_spec=pltpu.PrefetchScalarGridSpec(
            num_scalar_prefetch=2,
            grid=grid,
            in_specs=[pl.BlockSpec((tm, D_MODEL), lambda i, *_: (i, 0)),
                      pl.BlockSpec((tm, LANES), lambda i, *_: (blk0 + i, 0)),
                      mod_spec,
                      pl.BlockSpec(ln2_g.shape, lambda i, *_: (0, 0)),
                      pl.BlockSpec(ln2_b.shape, lambda i, *_: (0, 0)),
                      pl.BlockSpec(memory_space=pl.ANY)],
            out_specs=out_spec,
            scratch_shapes=[pltpu.VMEM((2, TOP_K, tm, D_MODEL), F32),
                            pltpu.SemaphoreType.DMA((2,))],
        ),
        out_shape=out_shape,
        compiler_params=pltpu.CompilerParams(
            dimension_semantics=("arbitrary",), vmem_limit_bytes=VMEM_LIMIT),
        name="combine_prompt" if three_d else "combine_sample",
    )(pos1, pos2, x1_all, mf_all, mod, ln2_g, ln2_b, ys)


def _block_diag(w):
    heads, n, _ = w.shape
    eye = jnp.eye(heads, dtype=w.dtype)
    return (eye[:, None, :, None] * w[:, :, None, :]).reshape(heads * n, heads * n)


def kernel(x_prompt, x_sample, state_rglru_h, state_conv, c_prompt, c_sample, w_ada, b_ada, w_in, w_s, b_s, lnv_g, lnv_b, conv_w, conv_b, lru_wa, lru_ba, lru_wx, lru_bx, lru_lam, w_out, ln1_g, ln1_b, w_rg, b_rg, w_re, b_re, w1, w3, w2, ln2_g, ln2_b):
    batch, seq, _ = x_prompt.shape
    dec_batch, dec_seq, _ = x_sample.shape
    assert dec_seq == SUBLANES and seq % ROWS == 0 and (dec_batch * dec_seq) % ROWS == 0
    n_prompt = batch * seq
    n_sample = dec_batch * dec_seq
    n_tokens = n_prompt + n_sample
    n_pairs = TOP_K * n_tokens
    assert n_tokens % ROUTER_COLS == 0 and n_pairs % FFN_ROWS == 0
    l = 0

    mod = _ada_call(jnp.concatenate([c_prompt, c_sample], axis=0), w_ada[l], b_ada[l][None])
    mod_p3 = mod[:batch][:, None, :]
    mod_s_tok = jnp.repeat(mod[batch:], dec_seq, axis=0)

    tri = jnp.tril(jnp.ones((CHUNK, CHUNK), dtype=bool))
    ws_tri = jnp.where(tri[None], w_s[l], 0.0)
    reps_p = MXU_DIM // CHUNK
    eye_p = jnp.eye(reps_p, dtype=F32)
    wsm_p = (eye_p[None, :, None, :, None] * ws_tri[:, None, :, None, :]).reshape(
        A_HEADS, MXU_DIM, MXU_DIM).astype(BF16)
    reps_s = MXU_DIM // dec_seq
    eye_s = jnp.eye(reps_s, dtype=F32)
    wsm_s = (eye_s[None, :, None, :, None] * ws_tri[:, None, :dec_seq, None, :dec_seq]).reshape(
        A_HEADS, MXU_DIM, MXU_DIM).astype(BF16)
    bs_pos = jnp.repeat(jnp.transpose(b_s[l]), A_HEAD_DIM, axis=1)
    bsf_p = jnp.tile(bs_pos, (reps_p, 1))
    bsf_s = jnp.tile(bs_pos[:dec_seq], (reps_s, 1))

    wa_bd = _block_diag(lru_wa[l])
    wx_bd = _block_diag(lru_wx[l])
    wg = jnp.stack([
        jnp.concatenate([wa_bd[q * MXU_DIM:(q + 1) * MXU_DIM, q * MXU_DIM:(q + 1) * MXU_DIM],
                         wx_bd[q * MXU_DIM:(q + 1) * MXU_DIM, q * MXU_DIM:(q + 1) * MXU_DIM]], axis=1)
        for q in range(D_B // MXU_DIM)]).astype(BF16)
    bg = jnp.concatenate([lru_ba[l].reshape(1, D_B), lru_bx[l].reshape(1, D_B)], axis=1)

    group_pad = jnp.zeros((SUBLANES - N_GROUPS, D_MODEL), F32)
    wr = jnp.concatenate([jnp.transpose(w_rg[l]), group_pad, jnp.transpose(w_re[l])], axis=0)
    wr_hi = wr.astype(BF16)
    wr_lo = (wr - wr_hi.astype(F32)).astype(BF16)
    br = jnp.concatenate([b_rg[l], jnp.zeros((SUBLANES - N_GROUPS,), F32), b_re[l]])[:, None]

    common = dict(lnv_g=lnv_g[l][None], lnv_b=lnv_b[l][None], conv_w=conv_w[l], conv_b=conv_b[l][None],
                  wg=wg, bg=bg, lam=lru_lam[l][None], w_out=w_out[l].astype(BF16),
                  ln1_g=ln1_g[l][None], ln1_b=ln1_b[l][None], wr_hi=wr_hi, wr_lo=wr_lo, br=br)
    w_in_b = w_in[l].astype(BF16)
    weights_p = _mixer_weights(w_in_b, wsm_p, bsf_p, **common)
    weights_s = _mixer_weights(w_in_b, wsm_s, bsf_s, **common)

    x1_p, h2_p, lg_p, cst_p, hst_p = _mixer_prompt(x_prompt, mod_p3, weights_p)

    h0_tok = jnp.repeat(state_rglru_h[l], dec_seq, axis=0)
    cpad = jnp.concatenate(
        [jnp.zeros((dec_batch, SUBLANES - (CONV_W - 1), D_B), F32), state_conv[l]], axis=1
    ).reshape(n_sample, D_B)
    x1_s, h2_s, lg_s, xr_s, v_s, hs_s = _mixer_sample(
        x_sample.reshape(n_sample, D_MODEL), mod_s_tok, weights_s, h0_tok, cpad)

    meta_i, meta_f, counts = _router_call(jnp.concatenate([lg_p, lg_s], axis=1))
    pos1, pos2 = meta_i[0], meta_i[1]
    wk_tok = jnp.pad(jnp.transpose(meta_f[:TOP_K]), ((0, 0), (0, LANES - TOP_K)))
    xs = _scatter_call(pos1, pos2, h2_p, h2_s)
    ys = _ffn_call(_ffn_plan(counts[:, 0], n_pairs), xs, w1[l], w3[l], w2[l])

    ln2g, ln2b = ln2_g[l][None], ln2_b[l][None]
    y_prompt = _combine_call(True, pos1, pos2, x1_p, wk_tok, mod_p3, ln2g, ln2b, ys,
                             0, batch, seq)
    y_sample = _combine_call(False, pos1, pos2, x1_s, wk_tok, mod_s_tok, ln2g, ln2b, ys,
                             n_prompt, dec_batch, dec_seq).reshape(dec_batch, dec_seq, D_MODEL)

    new_h_prompt = hst_p[:, SUBLANES - 1, :][None]
    new_conv_prompt = cst_p[:, SUBLANES - (CONV_W - 1):, :][None]
    xr_s3 = xr_s.reshape(dec_batch, dec_seq, D_B)
    new_h_sample = hs_s.reshape(dec_batch, dec_seq, D_B)[:, dec_seq - 1, :][None]
    new_conv_sample = xr_s3[:, dec_seq - (CONV_W - 1):, :][None]
    new_chunk_v_sample = v_s.reshape(dec_batch, dec_seq, D_A)[None]
    return (y_prompt, y_sample, new_h_prompt, new_conv_prompt, new_h_sample, new_conv_sample,
            new_chunk_v_sample)
```

```python
import functools
import math

import jax
import jax.numpy as jnp
from jax import lax
from jax.experimental import pallas as pl
from jax.experimental.pallas import tpu as pltpu

D_MODEL = 1024
D_A = 512
A_HEADS = 4
A_HEAD_DIM = 128
CHUNK = 128
D_B = 512
B_HEADS = 8
B_HEAD_DIM = 64
CONV_W = 4
LRU_C = 8.0
D_IN = 2 * D_A + 2 * D_B
N_GROUPS = 4
EXPERTS_PER_GROUP = 8
N_EXPERTS = 32
TOP_K = 2
D_EXPERT = 512
LN_EPS = 1e-5
DEPTH = 1
ALPHA = (2.0 * DEPTH) ** 0.25

LANES = 128
SUBLANES = 8
MXU_DIM = 256
VMEM_LIMIT = 56 * 1024 * 1024

ROWS = 256
ROUTER_COLS = 1024
LOGIT_ROWS = SUBLANES + N_EXPERTS
DISPATCH_ROWS = 256
FFN_ROWS = 256
COMBINE_ROWS = 256
BF16 = jnp.bfloat16
F32 = jnp.float32


def _dot(a, b):
    return jnp.dot(a, b, preferred_element_type=F32)


def _dot_nt(a, b):
    return lax.dot_general(a, b, (((1,), (1,)), ((), ())), preferred_element_type=F32)


def _token_tile(r):
    return pl.ds(pl.multiple_of(r * SUBLANES, SUBLANES), SUBLANES)


def _load_token_tiles(ref, lead, n_rows):
    parts = [ref[lead + (pl.ds(s, n_rows, stride=SUBLANES), slice(None))] for s in range(SUBLANES)]
    return jnp.concatenate(parts, axis=1)


def _store_token_tiles(ref, val):
    n_rows = val.shape[0]
    for s in range(SUBLANES):
        ref[pl.ds(s, n_rows, stride=SUBLANES), :] = val[:, s * LANES:(s + 1) * LANES]


def _layer_norm(x, g, b):
    mu = jnp.mean(x, axis=-1, keepdims=True)
    xc = x - mu
    var = jnp.mean(xc * xc, axis=-1, keepdims=True)
    return xc * lax.rsqrt(var + LN_EPS) * g + b


def _gelu_tanh(x):
    c = math.sqrt(2.0 / math.pi)
    return x * (0.5 * (1.0 + jnp.tanh(c * (x + 0.044715 * (x * x * x)))))


def _const_spec(shape):
    nd = len(shape)
    return pl.BlockSpec(shape, lambda *_: (0,) * nd)


def _ada_kernel(c_ref, w_ref, b_ref, o_ref):
    c = c_ref[...]
    a = (c * jax.nn.sigmoid(c)).astype(BF16)
    o_ref[...] = _dot(a, w_ref[...].astype(BF16)) + b_ref[...]


def _ada_call(c_all, w_ada, b_ada):
    n_rows = c_all.shape[0]
    tn = 1536
    return pl.pallas_call(
        _ada_kernel,
        grid=(6 * D_MODEL // tn,),
        in_specs=[
            pl.BlockSpec((n_rows, D_MODEL), lambda j: (0, 0)),
            pl.BlockSpec((D_MODEL, tn), lambda j: (0, j)),
            pl.BlockSpec((1, tn), lambda j: (0, j)),
        ],
        out_specs=pl.BlockSpec((n_rows, tn), lambda j: (0, j)),
        out_shape=jax.ShapeDtypeStruct((n_rows, 6 * D_MODEL), F32),
        compiler_params=pltpu.CompilerParams(
            dimension_semantics=("arbitrary",), vmem_limit_bytes=VMEM_LIMIT),
        name="ada_mod",
    )(c_all, w_ada, b_ada)


def _mixer_kernel(sample, *refs):
    if sample:
        (x_ref, mod_ref, win_ref, wsm_ref, bsf_ref, lnvg_ref, lnvb_ref, convw_ref, convb_ref,
         wg_ref, bg_ref, lam_ref, wout_ref, ln1g_ref, ln1b_ref, wrh_ref, wrl_ref, br_ref,
         h0_ref, cpad_ref,
         x1_ref, h2_ref, lg_ref, xr_ref, v_ref, hs_ref) = refs
    else:
        (x_ref, mod_ref, win_ref, wsm_ref, bsf_ref, lnvg_ref, lnvb_ref, convw_ref, convb_ref,
         wg_ref, bg_ref, lam_ref, wout_ref, ln1g_ref, ln1b_ref, wrh_ref, wrl_ref, br_ref,
         x1_ref, h2_ref, lg_ref, cst_ref, hst_ref, prev_ref, hcar_ref) = refs

    rows = ROWS
    groups = rows // SUBLANES

    if sample:
        x = x_ref[...]
        mod = mod_ref[...]
    else:
        x = x_ref[0]
        mod = mod_ref[0]

        @pl.when(pl.program_id(1) == 0)
        def _():
            prev_ref[...] = jnp.zeros_like(prev_ref)
            hcar_ref[...] = jnp.zeros_like(hcar_ref)

    sh1, sc1, g1, sh2, sc2, g2 = [mod[:, i * D_MODEL:(i + 1) * D_MODEL] for i in range(6)]

    h = x * (1.0 + sc1) + sh1
    proj = _dot(h.astype(BF16), win_ref[...])
    u = proj[:, :D_A]
    v = _layer_norm(proj[:, D_A:2 * D_A], lnvg_ref[...], lnvb_ref[...])
    xr = proj[:, 2 * D_A:2 * D_A + D_B]
    yr = proj[:, 2 * D_A + D_B:]

    vb = v.astype(BF16)
    halves = []
    for r in range(rows // MXU_DIM):
        heads = [
            _dot(wsm_ref[hd], vb[r * MXU_DIM:(r + 1) * MXU_DIM, hd * A_HEAD_DIM:(hd + 1) * A_HEAD_DIM])
            for hd in range(A_HEADS)
        ]
        halves.append(jnp.concatenate(heads, axis=1) + bsf_ref[...])
    mixed = halves[0] if len(halves) == 1 else jnp.concatenate(halves, axis=0)
    out_a = u * mixed

    xr3 = xr.reshape(groups, SUBLANES, D_B)
    t3 = lax.broadcasted_iota(jnp.int32, (1, SUBLANES, D_B), 1)
    cw = convw_ref[...]
    if sample:
        cpad3 = cpad_ref[...].reshape(groups, SUBLANES, D_B)
    else:
        prev8 = prev_ref[...]
    xc3 = convb_ref[...][None]
    for k in range(CONV_W):
        s = CONV_W - 1 - k
        if s == 0:
            shifted = xr3
        else:
            rot = pltpu.roll(xr3, s, 1)
            if sample:
                rot_prev = pltpu.roll(cpad3, s, 1)
            else:
                first = pltpu.roll(prev8, s, 0)[None]
                rot_prev = jnp.concatenate([first, rot[:-1]], axis=0) if groups > 1 else first
            shifted = jnp.where(t3 >= s, rot, rot_prev)
        xc3 = xc3 + shifted * cw[k:k + 1, :][None]
    xc = xc3.reshape(rows, D_B)

    xcb = xc.astype(BF16)
    gq = [_dot(xcb[:, q * MXU_DIM:(q + 1) * MXU_DIM], wg_ref[q]) for q in range(D_B // MXU_DIM)]
    bg = bg_ref[...]
    pre_a = jnp.concatenate([g[:, :MXU_DIM] for g in gq], axis=1) + bg[:, :D_B]
    pre_x = jnp.concatenate([g[:, MXU_DIM:] for g in gq], axis=1) + bg[:, D_B:]
    gate_a = jax.nn.sigmoid(pre_a)
    gate_x = jax.nn.sigmoid(pre_x)
    nl = -lam_ref[...]
    softplus = jnp.maximum(nl, 0.0) + jnp.log1p(jnp.exp(-jnp.abs(nl)))
    log_a = -LRU_C * gate_a * softplus
    a = jnp.exp(log_a)
    mult = jnp.sqrt(-jnp.tanh(log_a) * (1.0 + a * a))
    bx = mult * (gate_x * xc)

    a3 = a.reshape(groups, SUBLANES, D_B)
    b3 = bx.reshape(groups, SUBLANES, D_B)
    for d in (1, 2, 4):
        m = t3 >= d
        a_sh = jnp.where(m, pltpu.roll(a3, d, 1), 1.0)
        b_sh = jnp.where(m, pltpu.roll(b3, d, 1), 0.0)
        b3 = a3 * b_sh + b3
        a3 = a3 * a_sh
    if sample:
        hs3 = a3 * h0_ref[...].reshape(groups, SUBLANES, D_B) + b3
    else:
        h_prev = hcar_ref[...][SUBLANES - 1:SUBLANES, :]
        hs_list = []
        for g in range(groups):
            hg = a3[g] * h_prev + b3[g]
            hs_list.append(hg)
            h_prev = hg[SUBLANES - 1:SUBLANES, :]
        hs3 = jnp.stack(hs_list, axis=0)
    hs = hs3.reshape(rows, D_B)

    out_b = hs * _gelu_tanh(yr)
    mix = _dot(out_a.astype(BF16), wout_ref[:D_A, :]) + _dot(out_b.astype(BF16), wout_ref[D_A:, :])
    x1 = _layer_norm(ALPHA * x + (1.0 + g1) * mix, ln1g_ref[...], ln1b_ref[...])
    h2 = x1 * (1.0 + sc2) + sh2

    h2_hi = h2.astype(BF16)
    h2_lo = (h2 - h2_hi.astype(F32)).astype(BF16)
    logits = (_dot_nt(wrh_ref[...], h2_hi) + _dot_nt(wrh_ref[...], h2_lo)
              + _dot_nt(wrl_ref[...], h2_hi) + br_ref[...])

    x1_ref[...] = x1
    _store_token_tiles(h2_ref, h2)
    lg_ref[...] = logits
    if sample:
        xr_ref[...] = xr
        v_ref[...] = v
        hs_ref[...] = hs
    else:
        last_x = xr3[groups - 1]
        last_h = hs3[groups - 1]
        prev_ref[...] = last_x
        hcar_ref[...] = last_h
        cst_ref[0] = last_x
        hst_ref[0] = last_h


def _mixer_weights(w_in, wsm, bsf, lnv_g, lnv_b, conv_w, conv_b, wg, bg, lam, w_out, ln1_g, ln1_b,
                   wr_hi, wr_lo, br):
    arrs = [w_in, wsm, bsf, lnv_g, lnv_b, conv_w, conv_b, wg, bg, lam, w_out, ln1_g, ln1_b,
            wr_hi, wr_lo, br]
    return arrs, [_const_spec(a.shape) for a in arrs]


def _mixer_prompt(x, mod3, weights):
    batch, seq, _ = x.shape
    n_tokens = batch * seq
    tiles = seq // ROWS
    arrs, specs = weights
    row_spec = lambda w: pl.BlockSpec((ROWS, w), lambda b, j: (b * tiles + j, 0))
    st_spec = pl.BlockSpec((1, SUBLANES, D_B), lambda b, j: (b, 0, 0))
    return pl.pallas_call(
        functools.partial(_mixer_kernel, False),
        grid=(batch, tiles),
        in_specs=[pl.BlockSpec((1, ROWS, D_MODEL), lambda b, j: (b, j, 0)),
                  pl.BlockSpec((1, 1, 6 * D_MODEL), lambda b, j: (b, 0, 0))] + specs,
        out_specs=[row_spec(D_MODEL),
                   pl.BlockSpec((ROWS * SUBLANES, LANES), lambda b, j: (b * tiles + j, 0)),
                   pl.BlockSpec((LOGIT_ROWS, ROWS), lambda b, j: (0, b * tiles + j)), st_spec, st_spec],
        out_shape=[jax.ShapeDtypeStruct((n_tokens, D_MODEL), F32),
                   jax.ShapeDtypeStruct((n_tokens * SUBLANES, LANES), F32),
                   jax.ShapeDtypeStruct((LOGIT_ROWS, n_tokens), F32),
                   jax.ShapeDtypeStruct((batch, SUBLANES, D_B), F32),
                   jax.ShapeDtypeStruct((batch, SUBLANES, D_B), F32)],
        scratch_shapes=[pltpu.VMEM((SUBLANES, D_B), F32), pltpu.VMEM((SUBLANES, D_B), F32)],
        compiler_params=pltpu.CompilerParams(
            dimension_semantics=("arbitrary", "arbitrary"), vmem_limit_bytes=VMEM_LIMIT),
        name="mixer_prompt",
    )(x, mod3, *arrs)


def _mixer_sample(x2, mod_tok, weights, h0_tok, cpad):
    n_rows = x2.shape[0]
    arrs, specs = weights
    row = lambda w: pl.BlockSpec((ROWS, w), lambda i: (i, 0))
    return pl.pallas_call(
        functools.partial(_mixer_kernel, True),
        grid=(n_rows // ROWS,),
        in_specs=[row(D_MODEL), row(6 * D_MODEL)] + specs + [row(D_B), row(D_B)],
        out_specs=[row(D_MODEL), pl.BlockSpec((ROWS * SUBLANES, LANES), lambda i: (i, 0)),
                   pl.BlockSpec((LOGIT_ROWS, ROWS), lambda i: (0, i)),
                   row(D_B), row(D_B), row(D_B)],
        out_shape=[jax.ShapeDtypeStruct((n_rows, D_MODEL), F32),
                   jax.ShapeDtypeStruct((n_rows * SUBLANES, LANES), F32),
                   jax.ShapeDtypeStruct((LOGIT_ROWS, n_rows), F32),
                   jax.ShapeDtypeStruct((n_rows, D_B), F32),
                   jax.ShapeDtypeStruct((n_rows, D_B), F32),
                   jax.ShapeDtypeStruct((n_rows, D_B), F32)],
        compiler_params=pltpu.CompilerParams(
            dimension_semantics=("arbitrary",), vmem_limit_bytes=VMEM_LIMIT),
        name="mixer_sample",
    )(x2, mod_tok, *arrs, h0_tok, cpad)


def _router_kernel(lg_ref, upper_ref, lower_ref, mi_ref, mf_ref, cnt_ref, tot_ref, run_ref):
    phase = pl.program_id(0)
    step = pl.program_id(1)
    l = lg_ref[...]
    cols = l.shape[1]
    row8 = lax.broadcasted_iota(jnp.int32, (SUBLANES, cols), 0)
    neg = -jnp.inf
    big = jnp.int32(SUBLANES)

    def first_max(vals):
        top = jnp.max(vals, axis=0, keepdims=True)
        return top, jnp.min(jnp.where(vals == top, row8, big), axis=0, keepdims=True)

    is_group = row8 < N_GROUPS
    lg = l[:SUBLANES]
    m, gsel = first_max(jnp.where(is_group, lg, neg))
    denom = jnp.sum(jnp.where(is_group, jnp.exp(lg - m), 0.0), axis=0, keepdims=True)
    pgsel = 1.0 / denom
    le = l[SUBLANES:2 * SUBLANES]
    for g in range(1, N_GROUPS):
        le = jnp.where(gsel == g, l[(g + 1) * SUBLANES:(g + 2) * SUBLANES], le)
    v1, i1 = first_max(le)
    v2, i2 = first_max(jnp.where(row8 == i1, neg, le))
    ex = jnp.exp(v2 - v1)
    wk1 = pgsel / (1.0 + ex)
    wk2 = pgsel * ex / (1.0 + ex)
    e1 = gsel * EXPERTS_PER_GROUP + i1
    e2 = gsel * EXPERTS_PER_GROUP + i2
    row_e = lax.broadcasted_iota(jnp.int32, (N_EXPERTS, cols), 0)
    oh1 = jnp.where(row_e == e1, 1.0, 0.0)
    oh2 = jnp.where(row_e == e2, 1.0, 0.0)
    oh = oh1 + oh2

    @pl.when(jnp.logical_and(phase == 0, step == 0))
    def _():
        tot_ref[...] = jnp.zeros_like(tot_ref)

    @pl.when(phase == 0)
    def _():
        tot_ref[...] = tot_ref[...] + jnp.sum(oh, axis=1, keepdims=True)

    @pl.when(jnp.logical_and(phase == 1, step == 0))
    def _():
        tot = tot_ref[...]
        hi = (tot * (1.0 / MXU_DIM)).astype(jnp.int32).astype(F32)
        lo = tot - hi * MXU_DIM
        run_ref[...] = (_dot(lower_ref[...], hi.astype(BF16)) * MXU_DIM
                        + _dot(lower_ref[...], lo.astype(BF16)))

    @pl.when(phase == 1)
    def _():
        blocks = cols // LANES
        stacked = jnp.concatenate([oh[:, b * LANES:(b + 1) * LANES] for b in range(blocks)], axis=0)
        local = _dot(stacked.astype(BF16), upper_ref[...])
        run = run_ref[...]
        p1, p2 = [], []
        for b in range(blocks):
            sl = slice(b * LANES, (b + 1) * LANES)
            before = local[b * N_EXPERTS:(b + 1) * N_EXPERTS] + run
            p1.append(jnp.sum(oh1[:, sl] * before, axis=0, keepdims=True))
            p2.append(jnp.sum(oh2[:, sl] * before, axis=0, keepdims=True))
            run = run + jnp.sum(oh[:, sl], axis=1, keepdims=True)
        run_ref[...] = run
        pos1 = jnp.concatenate(p1, axis=1).astype(jnp.int32)
        pos2 = jnp.concatenate(p2, axis=1).astype(jnp.int32)
        zero_i = jnp.zeros_like(row8)
        mi_ref[...] = jnp.where(row8 == 0, pos1, jnp.where(row8 == 1, pos2, jnp.where(
            row8 == 2, e1, jnp.where(row8 == 3, e2, zero_i))))
        mf_ref[...] = jnp.where(row8 == 0, wk1, jnp.where(row8 == 1, wk2, 0.0))
        cnt_ref[...] = tot_ref[...].astype(jnp.int32)


def _router_call(lg_t):
    n_tokens = lg_t.shape[1]
    tc = ROUTER_COLS
    r = lax.broadcasted_iota(jnp.int32, (LANES, LANES), 0)
    c = lax.broadcasted_iota(jnp.int32, (LANES, LANES), 1)
    upper = jnp.where(r < c, 1.0, 0.0).astype(BF16)
    lower = jnp.where(c < r, 1.0, 0.0)[:N_EXPERTS, :N_EXPERTS].astype(BF16)
    col = pl.BlockSpec((SUBLANES, tc), lambda p, i: (0, i * p))
    return pl.pallas_call(
        _router_kernel,
        grid=(2, n_tokens // tc),
        in_specs=[pl.BlockSpec((LOGIT_ROWS, tc), lambda p, i: (0, i)),
                  _const_spec((LANES, LANES)), _const_spec((N_EXPERTS, N_EXPERTS))],
        out_specs=[col, col, _const_spec((N_EXPERTS, LANES))],
        out_shape=[jax.ShapeDtypeStruct((SUBLANES, n_tokens), jnp.int32),
                   jax.ShapeDtypeStruct((SUBLANES, n_tokens), F32),
                   jax.ShapeDtypeStruct((N_EXPERTS, LANES), jnp.int32)],
        scratch_shapes=[pltpu.VMEM((N_EXPERTS, LANES), F32), pltpu.VMEM((N_EXPERTS, LANES), F32)],
        compiler_params=pltpu.CompilerParams(
            dimension_semantics=("arbitrary", "arbitrary"), vmem_limit_bytes=VMEM_LIMIT),
        name="router",
    )(lg_t, upper, lower)


def _ffn_plan(counts, n_pairs):
    ends = jnp.cumsum(counts)
    off = ends - counts
    ids = jnp.arange(N_EXPERTS, dtype=jnp.int32)

    n_tiles = n_pairs // FFN_ROWS
    n_items = n_tiles + N_EXPERTS
    first_tile = off // FFN_ROWS
    last_tile = (ends - 1) // FFN_ROWS
    items_e = jnp.where(counts > 0, last_tile - first_tile + 1, 0)
    items_end = jnp.cumsum(items_e)
    items_start = items_end - items_e
    n_work = items_end[-1]
    w = jnp.arange(n_items, dtype=jnp.int32)
    wv = jnp.minimum(w, n_work - 1)
    e_w = jnp.sum((items_end[None, :] <= wv[:, None]).astype(jnp.int32), axis=1)
    e_w = jnp.minimum(e_w, N_EXPERTS - 1)
    sel = e_w[:, None] == ids[None, :]
    pick = lambda a: jnp.sum(jnp.where(sel, a[None, :], 0), axis=1)
    t_w = pick(first_tile) + (wv - pick(items_start))
    lo_w = jnp.maximum(pick(off), t_w * FFN_ROWS) - t_w * FFN_ROWS
    hi_w = jnp.minimum(pick(ends), (t_w + 1) * FFN_ROWS) - t_w * FFN_ROWS
    valid = w < n_work
    lo_w = jnp.where(valid, lo_w, 0)
    hi_w = jnp.where(valid, hi_w, 0)
    i32 = lambda a: a.astype(jnp.int32)
    return i32(t_w), i32(e_w), i32(lo_w), i32(hi_w)


def _scatter_kernel(tiles_p, pos1_s, pos2_s, h2p_ref, h2s_ref, xs_hbm, sem):
    i = pl.program_id(0)
    base = i * DISPATCH_ROWS

    def scatter_rows(src_ref):
        def row_copy(r, pos_s):
            p = pos_s[base + r]
            return pltpu.make_async_copy(src_ref.at[_token_tile(r), :], xs_hbm.at[_token_tile(p), :],
                                         sem)

        def start(r, carry):
            row_copy(r, pos1_s).start(priority=0)
            row_copy(r, pos2_s).start(priority=1)
            return carry

        def wait(r, carry):
            row_copy(r, pos1_s).wait()
            row_copy(r, pos2_s).wait()
            return carry

        lax.fori_loop(0, DISPATCH_ROWS, start, 0, unroll=8)
        lax.fori_loop(0, DISPATCH_ROWS, wait, 0, unroll=8)

    @pl.when(i < tiles_p)
    def _():
        scatter_rows(h2p_ref)

    @pl.when(i >= tiles_p)
    def _():
        scatter_rows(h2s_ref)


def _scatter_call(pos1, pos2, h2_p, h2_s):
    tm = DISPATCH_ROWS
    tiles_p = h2_p.shape[0] // (tm * SUBLANES)
    tiles_s = h2_s.shape[0] // (tm * SUBLANES)
    n_pairs = TOP_K * (h2_p.shape[0] + h2_s.shape[0]) // SUBLANES
    blk = (tm * SUBLANES, LANES)
    return pl.pallas_call(
        functools.partial(_scatter_kernel, tiles_p),
        grid_spec=pltpu.PrefetchScalarGridSpec(
            num_scalar_prefetch=2,
            grid=(tiles_p + tiles_s,),
            in_specs=[pl.BlockSpec(blk, lambda i, *_: (jnp.minimum(i, tiles_p - 1), 0)),
                      pl.BlockSpec(blk, lambda i, *_: (jnp.maximum(i - tiles_p, 0), 0))],
            out_specs=pl.BlockSpec(memory_space=pl.ANY),
            scratch_shapes=[pltpu.SemaphoreType.DMA(())],
        ),
        out_shape=jax.ShapeDtypeStruct((n_pairs * SUBLANES, LANES), F32),
        compiler_params=pltpu.CompilerParams(
            dimension_semantics=("arbitrary",), vmem_limit_bytes=VMEM_LIMIT),
        name="dispatch_scatter",
    )(pos1, pos2, h2_p, h2_s)


def _ffn_kernel(wt_s, we_s, wlo_s, whi_s, x_ref, w1_ref, w3_ref, w2_ref, o_ref, w13b_ref, w2b_ref):
    del wt_s
    w = pl.program_id(0)
    e = we_s[w]
    e_prev = we_s[jnp.maximum(w - 1, 0)]

    @pl.when(jnp.logical_or(w == 0, e != e_prev))
    def _():
        w13b_ref[:, :D_EXPERT] = w1_ref[0].astype(BF16)
        w13b_ref[:, D_EXPERT:] = w3_ref[0].astype(BF16)
        w2b_ref[...] = w2_ref[0].astype(BF16)

    lo = wlo_s[w]
    hi = whi_s[w]

    @pl.when(hi > lo)
    def _():
        x = _load_token_tiles(x_ref, (), FFN_ROWS)
        h13 = _dot(x.astype(BF16), w13b_ref[...])
        a = h13[:, :D_EXPERT]
        b = h13[:, D_EXPERT:]
        hdn = (a * jax.nn.sigmoid(a)) * b
        y = _dot(hdn.astype(BF16), w2b_ref[...])
        rows = lax.broadcasted_iota(jnp.int32, y.shape, 0)
        mine = jnp.logical_and(rows >= lo, rows < hi)

        @pl.when(lo == 0)
        def _():
            _store_token_tiles(o_ref, jnp.where(mine, y, 0.0))

        @pl.when(lo > 0)
        def _():
            _store_token_tiles(o_ref, jnp.where(mine, y, _load_token_tiles(o_ref, (), FFN_ROWS)))


def _ffn_call(plan, xs, w1, w3, w2):
    t_w, e_w, lo_w, hi_w = plan
    n_pairs = xs.shape[0] // SUBLANES
    tm = FFN_ROWS
    wspec = lambda shape: pl.BlockSpec((1,) + shape, lambda i, wt, we, wlo, whi: (we[i], 0, 0))
    row = pl.BlockSpec((tm * SUBLANES, LANES), lambda i, wt, we, wlo, whi: (wt[i], 0))
    return pl.pallas_call(
        _ffn_kernel,
        grid_spec=pltpu.PrefetchScalarGridSpec(
            num_scalar_prefetch=4,
            grid=(t_w.shape[0],),
            in_specs=[row, wspec((D_MODEL, D_EXPERT)), wspec((D_MODEL, D_EXPERT)),
                      wspec((D_EXPERT, D_MODEL))],
            out_specs=row,
            scratch_shapes=[pltpu.VMEM((D_MODEL, 2 * D_EXPERT), BF16),
                            pltpu.VMEM((D_EXPERT, D_MODEL), BF16)],
        ),
        out_shape=jax.ShapeDtypeStruct((n_pairs * SUBLANES, LANES), F32),
        compiler_params=pltpu.CompilerParams(
            dimension_semantics=("arbitrary",), vmem_limit_bytes=VMEM_LIMIT),
        name="expert_ffn",
    )(t_w, e_w, lo_w, hi_w, xs, w1, w3, w2)


def _combine_kernel(three_d, row0, n_steps, pos1_s, pos2_s, x1_ref, wk_ref, mod_ref, g_ref, b_ref,
                    ys_hbm, o_ref, ybuf, sems):
    step = pl.program_id(0)
    slot = step % 2

    def gather(tile, tile_slot):
        base = row0 + tile * COMBINE_ROWS

        def row_copy(r, k, pos_s):
            p = pos_s[base + r]
            return pltpu.make_async_copy(ys_hbm.at[_token_tile(p), :],
                                         ybuf.at[tile_slot, k, _token_tile(r), :], sems.at[tile_slot])

        def start(r, carry):
            row_copy(r, 0, pos1_s).start(priority=0)
            row_copy(r, 1, pos2_s).start(priority=1)
            return carry

        def wait(r, carry):
            row_copy(r, 0, pos1_s).wait()
            row_copy(r, 1, pos2_s).wait()
            return carry

        return start, wait

    @pl.when(step == 0)
    def _():
        lax.fori_loop(0, COMBINE_ROWS, gather(step, slot)[0], 0, unroll=8)

    @pl.when(step + 1 < n_steps)
    def _():
        lax.fori_loop(0, COMBINE_ROWS, gather(step + 1, 1 - slot)[0], 0, unroll=8)

    lax.fori_loop(0, COMBINE_ROWS, gather(step, slot)[1], 0, unroll=8)

    mod = mod_ref[0] if three_d else mod_ref[...]
    g2 = mod[:, 5 * D_MODEL:]
    wk = wk_ref[...]
    moe = (wk[:, 0:1] * _load_token_tiles(ybuf, (slot, 0), COMBINE_ROWS)
           + wk[:, 1:2] * _load_token_tiles(ybuf, (slot, 1), COMBINE_ROWS))
    y = _layer_norm(ALPHA * x1_ref[...] + (1.0 + g2) * moe, g_ref[...], b_ref[...])
    if three_d:
        o_ref[0] = y
    else:
        o_ref[...] = y


def _combine_call(three_d, pos1, pos2, x1_all, mf_all, mod, ln2_g, ln2_b, ys, row0, batch, seq):
    tm = COMBINE_ROWS
    blk0 = row0 // tm
    if three_d:
        tiles = seq // tm
        grid = (batch * tiles,)
        mod_spec = pl.BlockSpec((1, 1, 6 * D_MODEL), lambda i, *_: (i // tiles, 0, 0))
        out_spec = pl.BlockSpec((1, tm, D_MODEL), lambda i, *_: (i // tiles, i % tiles, 0))
        out_shape = jax.ShapeDtypeStruct((batch, seq, D_MODEL), F32)
    else:
        grid = (batch * seq // tm,)
        mod_spec = pl.BlockSpec((tm, 6 * D_MODEL), lambda i, *_: (i, 0))
        out_spec = pl.BlockSpec((tm, D_MODEL), lambda i, *_: (i, 0))
        out_shape = jax.ShapeDtypeStruct((batch * seq, D_MODEL), F32)
    return pl.pallas_call(
        functools.partial(_combine_kernel, three_d, row0, grid[0]),
        grid_spec=pltpu.PrefetchScalarGridSpec(
            num_scalar_prefetch=2,
            grid=grid,
            in_specs=[pl.BlockSpec((tm, D_MODEL), lambda i, *_: (i, 0)),
                      pl.BlockSpec((tm, LANES), lambda i, *_: (blk0 + i, 0)),
                      mod_spec,
                      pl.BlockSpec(ln2_g.shape, lambda i, *_: (0, 0)),
                      pl.BlockSpec(ln2_b.shape, lambda i, *_: (0, 0)),
                      pl.BlockSpec(memory_space=pl.ANY)],
            out_specs=out_spec,
            scratch_shapes=[pltpu.VMEM((2, TOP_K, tm * SUBLANES, LANES), F32),
                            pltpu.SemaphoreType.DMA((2,))],
        ),
        out_shape=out_shape,
        compiler_params=pltpu.CompilerParams(
            dimension_semantics=("arbitrary",), vmem_limit_bytes=VMEM_LIMIT),
        name="combine_prompt" if three_d else "combine_sample",
    )(pos1, pos2, x1_all, mf_all, mod, ln2_g, ln2_b, ys)


def _block_diag(w):
    heads, n, _ = w.shape
    eye = jnp.eye(heads, dtype=w.dtype)
    return (eye[:, None, :, None] * w[:, :, None, :]).reshape(heads * n, heads * n)


def kernel(x_prompt, x_sample, state_rglru_h, state_conv, c_prompt, c_sample, w_ada, b_ada, w_in, w_s, b_s, lnv_g, lnv_b, conv_w, conv_b, lru_wa, lru_ba, lru_wx, lru_bx, lru_lam, w_out, ln1_g, ln1_b, w_rg, b_rg, w_re, b_re, w1, w3, w2, ln2_g, ln2_b):
    batch, seq, _ = x_prompt.shape
    dec_batch, dec_seq, _ = x_sample.shape
    assert dec_seq == SUBLANES and seq % ROWS == 0 and (dec_batch * dec_seq) % ROWS == 0
    n_prompt = batch * seq
    n_sample = dec_batch * dec_seq
    n_tokens = n_prompt + n_sample
    n_pairs = TOP_K * n_tokens
    assert n_tokens % ROUTER_COLS == 0 and n_pairs % FFN_ROWS == 0
    l = 0

    mod = _ada_call(jnp.concatenate([c_prompt, c_sample], axis=0), w_ada[l], b_ada[l][None])
    mod_p3 = mod[:batch][:, None, :]
    mod_s_tok = jnp.repeat(mod[batch:], dec_seq, axis=0)

    tri = jnp.tril(jnp.ones((CHUNK, CHUNK), dtype=bool))
    ws_tri = jnp.where(tri[None], w_s[l], 0.0)
    reps_p = MXU_DIM // CHUNK
    eye_p = jnp.eye(reps_p, dtype=F32)
    wsm_p = (eye_p[None, :, None, :, None] * ws_tri[:, None, :, None, :]).reshape(
        A_HEADS, MXU_DIM, MXU_DIM).astype(BF16)
    reps_s = MXU_DIM // dec_seq
    eye_s = jnp.eye(reps_s, dtype=F32)
    wsm_s = (eye_s[None, :, None, :, None] * ws_tri[:, None, :dec_seq, None, :dec_seq]).reshape(
        A_HEADS, MXU_DIM, MXU_DIM).astype(BF16)
    bs_pos = jnp.repeat(jnp.transpose(b_s[l]), A_HEAD_DIM, axis=1)
    bsf_p = jnp.tile(bs_pos, (reps_p, 1))
    bsf_s = jnp.tile(bs_pos[:dec_seq], (reps_s, 1))

    wa_bd = _block_diag(lru_wa[l])
    wx_bd = _block_diag(lru_wx[l])
    wg = jnp.stack([
        jnp.concatenate([wa_bd[q * MXU_DIM:(q + 1) * MXU_DIM, q * MXU_DIM:(q + 1) * MXU_DIM],
                         wx_bd[q * MXU_DIM:(q + 1) * MXU_DIM, q * MXU_DIM:(q + 1) * MXU_DIM]], axis=1)
        for q in range(D_B // MXU_DIM)]).astype(BF16)
    bg = jnp.concatenate([lru_ba[l].reshape(1, D_B), lru_bx[l].reshape(1, D_B)], axis=1)

    group_pad = jnp.zeros((SUBLANES - N_GROUPS, D_MODEL), F32)
    wr = jnp.concatenate([jnp.transpose(w_rg[l]), group_pad, jnp.transpose(w_re[l])], axis=0)
    wr_hi = wr.astype(BF16)
    wr_lo = (wr - wr_hi.astype(F32)).astype(BF16)
    br = jnp.concatenate([b_rg[l], jnp.zeros((SUBLANES - N_GROUPS,), F32), b_re[l]])[:, None]

    common = dict(lnv_g=lnv_g[l][None], lnv_b=lnv_b[l][None], conv_w=conv_w[l], conv_b=conv_b[l][None],
                  wg=wg, bg=bg, lam=lru_lam[l][None], w_out=w_out[l].astype(BF16),
                  ln1_g=ln1_g[l][None], ln1_b=ln1_b[l][None], wr_hi=wr_hi, wr_lo=wr_lo, br=br)
    w_in_b = w_in[l].astype(BF16)
    weights_p = _mixer_weights(w_in_b, wsm_p, bsf_p, **common)
    weights_s = _mixer_weights(w_in_b, wsm_s, bsf_s, **common)

    x1_p, h2_p, lg_p, cst_p, hst_p = _mixer_prompt(x_prompt, mod_p3, weights_p)

    h0_tok = jnp.repeat(state_rglru_h[l], dec_seq, axis=0)
    cpad = jnp.concatenate(
        [jnp.zeros((dec_batch, SUBLANES - (CONV_W - 1), D_B), F32), state_conv[l]], axis=1
    ).reshape(n_sample, D_B)
    x1_s, h2_s, lg_s, xr_s, v_s, hs_s = _mixer_sample(
        x_sample.reshape(n_sample, D_MODEL), mod_s_tok, weights_s, h0_tok, cpad)

    meta_i, meta_f, counts = _router_call(jnp.concatenate([lg_p, lg_s], axis=1))
    pos1, pos2 = meta_i[0], meta_i[1]
    wk_tok = jnp.pad(jnp.transpose(meta_f[:TOP_K]), ((0, 0), (0, LANES - TOP_K)))
    xs = _scatter_call(pos1, pos2, h2_p, h2_s)
    ys = _ffn_call(_ffn_plan(counts[:, 0], n_pairs), xs, w1[l], w3[l], w2[l])

    ln2g, ln2b = ln2_g[l][None], ln2_b[l][None]
    y_prompt = _combine_call(True, pos1, pos2, x1_p, wk_tok, mod_p3, ln2g, ln2b, ys,
                             0, batch, seq)
    y_sample = _combine_call(False, pos1, pos2, x1_s, wk_tok, mod_s_tok, ln2g, ln2b, ys,
                             n_prompt, dec_batch, dec_seq).reshape(dec_batch, dec_seq, D_MODEL)

    new_h_prompt = hst_p[:, SUBLANES - 1, :][None]
    new_conv_prompt = cst_p[:, SUBLANES - (CONV_W - 1):, :][None]
    xr_s3 = xr_s.reshape(dec_batch, dec_seq, D_B)
    new_h_sample = hs_s.reshape(dec_batch, dec_seq, D_B)[:, dec_seq - 1, :][None]
    new_conv_sample = xr_s3[:, dec_seq - (CONV_W - 1):, :][None]
    new_chunk_v_sample = v_s.reshape(dec_batch, dec_seq, D_A)[None]
    return (y_prompt, y_sample, new_h_prompt, new_conv_prompt, new_h_sample, new_conv_sample,
            new_chunk_v_sample)
```

```python
import functools
import math

import jax
import jax.numpy as jnp
from jax import lax
from jax.experimental import pallas as pl
from jax.experimental.pallas import tpu as pltpu

D_MODEL = 1024
D_A = 512
A_HEADS = 4
A_HEAD_DIM = 128
CHUNK = 128
D_B = 512
B_HEADS = 8
B_HEAD_DIM = 64
CONV_W = 4
LRU_C = 8.0
D_IN = 2 * D_A + 2 * D_B
N_GROUPS = 4
EXPERTS_PER_GROUP = 8
N_EXPERTS = 32
TOP_K = 2
D_EXPERT = 512
LN_EPS = 1e-5
DEPTH = 1
ALPHA = (2.0 * DEPTH) ** 0.25

LANES = 128
SUBLANES = 8
MXU_DIM = 256
VMEM_LIMIT = 56 * 1024 * 1024

ROWS = 512
ROUTER_COLS = 1024
LOGIT_ROWS = SUBLANES + N_EXPERTS
DISPATCH_ROWS = 1024
FFN_ROWS = 256
COMBINE_ROWS = 256
BF16 = jnp.bfloat16
F32 = jnp.float32


def _dot(a, b):
    return jnp.dot(a, b, preferred_element_type=F32)


def _dot_nt(a, b):
    return lax.dot_general(a, b, (((1,), (1,)), ((), ())), preferred_element_type=F32)


def _token_tile(r):
    return pl.ds(pl.multiple_of(r * SUBLANES, SUBLANES), SUBLANES)


def _load_token_tiles(ref, lead, n_rows):
    parts = [ref[lead + (pl.ds(s, n_rows, stride=SUBLANES), slice(None))] for s in range(SUBLANES)]
    return jnp.concatenate(parts, axis=1)


def _store_token_tiles(ref, val):
    n_rows = val.shape[0]
    for s in range(SUBLANES):
        ref[pl.ds(s, n_rows, stride=SUBLANES), :] = val[:, s * LANES:(s + 1) * LANES]


def _layer_norm(x, g, b):
    mu = jnp.mean(x, axis=-1, keepdims=True)
    xc = x - mu
    var = jnp.mean(xc * xc, axis=-1, keepdims=True)
    return xc * lax.rsqrt(var + LN_EPS) * g + b


def _gelu_tanh(x):
    c = math.sqrt(2.0 / math.pi)
    return x * (0.5 * (1.0 + jnp.tanh(c * (x + 0.044715 * (x * x * x)))))


def _const_spec(shape):
    nd = len(shape)
    return pl.BlockSpec(shape, lambda *_: (0,) * nd)


def _ada_kernel(c_ref, w_ref, b_ref, o_ref):
    c = c_ref[...]
    a = (c * jax.nn.sigmoid(c)).astype(BF16)
    o_ref[...] = _dot(a, w_ref[...].astype(BF16)) + b_ref[...]


def _ada_call(c_all, w_ada, b_ada):
    n_rows = c_all.shape[0]
    tn = 1536
    return pl.pallas_call(
        _ada_kernel,
        grid=(6 * D_MODEL // tn,),
        in_specs=[
            pl.BlockSpec((n_rows, D_MODEL), lambda j: (0, 0)),
            pl.BlockSpec((D_MODEL, tn), lambda j: (0, j)),
            pl.BlockSpec((1, tn), lambda j: (0, j)),
        ],
        out_specs=pl.BlockSpec((n_rows, tn), lambda j: (0, j)),
        out_shape=jax.ShapeDtypeStruct((n_rows, 6 * D_MODEL), F32),
        compiler_params=pltpu.CompilerParams(
            dimension_semantics=("arbitrary",), vmem_limit_bytes=VMEM_LIMIT),
        name="ada_mod",
    )(c_all, w_ada, b_ada)


def _mixer_kernel(sample, *refs):
    if sample:
        (x_ref, mod_ref, win_ref, wsm_ref, bsf_ref, lnvg_ref, lnvb_ref, convw_ref, convb_ref,
         wg_ref, bg_ref, lam_ref, wout_ref, ln1g_ref, ln1b_ref, wrh_ref, wrl_ref, br_ref,
         h0_ref, cpad_ref,
         x1_ref, h2_ref, lg_ref, xr_ref, v_ref, hs_ref) = refs
    else:
        (x_ref, mod_ref, win_ref, wsm_ref, bsf_ref, lnvg_ref, lnvb_ref, convw_ref, convb_ref,
         wg_ref, bg_ref, lam_ref, wout_ref, ln1g_ref, ln1b_ref, wrh_ref, wrl_ref, br_ref,
         x1_ref, h2_ref, lg_ref, cst_ref, hst_ref, prev_ref, hcar_ref) = refs

    rows = ROWS
    groups = rows // SUBLANES

    if sample:
        x = x_ref[...]

        def chunk(c):
            return lambda b: mod_ref[b:b + 1, c * D_MODEL:(c + 1) * D_MODEL]
    else:
        x = x_ref[0]

        def chunk(c):
            return lambda b: mod_ref[0, :, c * D_MODEL:(c + 1) * D_MODEL]

        @pl.when(pl.program_id(1) == 0)
        def _():
            prev_ref[...] = jnp.zeros_like(prev_ref)
            hcar_ref[...] = jnp.zeros_like(hcar_ref)

    def modulated(fn, *vals):
        if not sample:
            return fn(0, *vals)
        parts = [fn(b, *(v[b * SUBLANES:(b + 1) * SUBLANES] for v in vals)) for b in range(groups)]
        return jnp.concatenate(parts, axis=0)

    sh1, sc1, g1, sh2, sc2, g2 = [chunk(c) for c in range(6)]

    h = modulated(lambda b, xv: xv * (1.0 + sc1(b)) + sh1(b), x)
    proj = _dot(h.astype(BF16), win_ref[...])
    u = proj[:, :D_A]
    v = _layer_norm(proj[:, D_A:2 * D_A], lnvg_ref[...], lnvb_ref[...])
    xr = proj[:, 2 * D_A:2 * D_A + D_B]
    yr = proj[:, 2 * D_A + D_B:]

    vb = v.astype(BF16)
    halves = []
    for r in range(rows // MXU_DIM):
        heads = [
            _dot(wsm_ref[hd], vb[r * MXU_DIM:(r + 1) * MXU_DIM, hd * A_HEAD_DIM:(hd + 1) * A_HEAD_DIM])
            for hd in range(A_HEADS)
        ]
        halves.append(jnp.concatenate(heads, axis=1) + bsf_ref[...])
    mixed = halves[0] if len(halves) == 1 else jnp.concatenate(halves, axis=0)
    out_a = u * mixed

    xr3 = xr.reshape(groups, SUBLANES, D_B)
    t3 = lax.broadcasted_iota(jnp.int32, (1, SUBLANES, D_B), 1)
    cw = convw_ref[...]
    if sample:
        cpad3 = cpad_ref[...].reshape(groups, SUBLANES, D_B)
    else:
        prev8 = prev_ref[...]
    xc3 = convb_ref[...][None]
    for k in range(CONV_W):
        s = CONV_W - 1 - k
        if s == 0:
            shifted = xr3
        else:
            rot = pltpu.roll(xr3, s, 1)
            if sample:
                rot_prev = pltpu.roll(cpad3, s, 1)
            else:
                first = pltpu.roll(prev8, s, 0)[None]
                rot_prev = jnp.concatenate([first, rot[:-1]], axis=0) if groups > 1 else first
            shifted = jnp.where(t3 >= s, rot, rot_prev)
        xc3 = xc3 + shifted * cw[k:k + 1, :][None]
    xc = xc3.reshape(rows, D_B)

    xcb = xc.astype(BF16)
    gq = [_dot(xcb[:, q * MXU_DIM:(q + 1) * MXU_DIM], wg_ref[q]) for q in range(D_B // MXU_DIM)]
    bg = bg_ref[...]
    pre_a = jnp.concatenate([g[:, :MXU_DIM] for g in gq], axis=1) + bg[:, :D_B]
    pre_x = jnp.concatenate([g[:, MXU_DIM:] for g in gq], axis=1) + bg[:, D_B:]
    gate_a = jax.nn.sigmoid(pre_a)
    gate_x = jax.nn.sigmoid(pre_x)
    nl = -lam_ref[...]
    softplus = jnp.maximum(nl, 0.0) + jnp.log1p(jnp.exp(-jnp.abs(nl)))
    log_a = -LRU_C * gate_a * softplus
    a = jnp.exp(log_a)
    mult = jnp.sqrt(-jnp.tanh(log_a) * (1.0 + a * a))
    bx = mult * (gate_x * xc)

    a3 = a.reshape(groups, SUBLANES, D_B)
    b3 = bx.reshape(groups, SUBLANES, D_B)
    for d in (1, 2, 4):
        m = t3 >= d
        a_sh = jnp.where(m, pltpu.roll(a3, d, 1), 1.0)
        b_sh = jnp.where(m, pltpu.roll(b3, d, 1), 0.0)
        b3 = a3 * b_sh + b3
        a3 = a3 * a_sh
    if sample:
        hs3 = jnp.stack([a3[g] * h0_ref[g:g + 1, :] + b3[g] for g in range(groups)], axis=0)
    else:
        h_prev = hcar_ref[...][SUBLANES - 1:SUBLANES, :]
        hs_list = []
        for g in range(groups):
            hg = a3[g] * h_prev + b3[g]
            hs_list.append(hg)
            h_prev = hg[SUBLANES - 1:SUBLANES, :]
        hs3 = jnp.stack(hs_list, axis=0)
    hs = hs3.reshape(rows, D_B)

    out_b = hs * _gelu_tanh(yr)
    mix = _dot(out_a.astype(BF16), wout_ref[:D_A, :]) + _dot(out_b.astype(BF16), wout_ref[D_A:, :])
    y1 = modulated(lambda b, xv, mv: ALPHA * xv + (1.0 + g1(b)) * mv, x, mix)
    x1 = _layer_norm(y1, ln1g_ref[...], ln1b_ref[...])
    h2 = modulated(lambda b, xv: xv * (1.0 + sc2(b)) + sh2(b), x1)

    h2_hi = h2.astype(BF16)
    h2_lo = (h2 - h2_hi.astype(F32)).astype(BF16)
    logits = (_dot_nt(wrh_ref[...], h2_hi) + _dot_nt(wrh_ref[...], h2_lo)
              + _dot_nt(wrl_ref[...], h2_hi) + br_ref[...])

    x1_ref[...] = x1
    _store_token_tiles(h2_ref, h2)
    lg_ref[...] = logits
    if sample:
        xr_ref[...] = xr
        v_ref[...] = v
        hs_ref[...] = hs
    else:
        last_x = xr3[groups - 1]
        last_h = hs3[groups - 1]
        prev_ref[...] = last_x
        hcar_ref[...] = last_h
        cst_ref[0] = last_x
        hst_ref[0] = last_h


def _mixer_weights(w_in, wsm, bsf, lnv_g, lnv_b, conv_w, conv_b, wg, bg, lam, w_out, ln1_g, ln1_b,
                   wr_hi, wr_lo, br):
    arrs = [w_in, wsm, bsf, lnv_g, lnv_b, conv_w, conv_b, wg, bg, lam, w_out, ln1_g, ln1_b,
            wr_hi, wr_lo, br]
    return arrs, [_const_spec(a.shape) for a in arrs]


def _mixer_prompt(x, mod3, weights):
    batch, seq, _ = x.shape
    n_tokens = batch * seq
    tiles = seq // ROWS
    arrs, specs = weights
    row_spec = lambda w: pl.BlockSpec((ROWS, w), lambda b, j: (b * tiles + j, 0))
    st_spec = pl.BlockSpec((1, SUBLANES, D_B), lambda b, j: (b, 0, 0))
    return pl.pallas_call(
        functools.partial(_mixer_kernel, False),
        grid=(batch, tiles),
        in_specs=[pl.BlockSpec((1, ROWS, D_MODEL), lambda b, j: (b, j, 0)),
                  pl.BlockSpec((1, 1, 6 * D_MODEL), lambda b, j: (b, 0, 0))] + specs,
        out_specs=[row_spec(D_MODEL),
                   pl.BlockSpec((ROWS * SUBLANES, LANES), lambda b, j: (b * tiles + j, 0)),
                   pl.BlockSpec((LOGIT_ROWS, ROWS), lambda b, j: (0, b * tiles + j)), st_spec, st_spec],
        out_shape=[jax.ShapeDtypeStruct((n_tokens, D_MODEL), F32),
                   jax.ShapeDtypeStruct((n_tokens * SUBLANES, LANES), F32),
                   jax.ShapeDtypeStruct((LOGIT_ROWS, n_tokens), F32),
                   jax.ShapeDtypeStruct((batch, SUBLANES, D_B), F32),
                   jax.ShapeDtypeStruct((batch, SUBLANES, D_B), F32)],
        scratch_shapes=[pltpu.VMEM((SUBLANES, D_B), F32), pltpu.VMEM((SUBLANES, D_B), F32)],
        compiler_params=pltpu.CompilerParams(
            dimension_semantics=("arbitrary", "arbitrary"), vmem_limit_bytes=VMEM_LIMIT),
        name="mixer_prompt",
    )(x, mod3, *arrs)


def _mixer_sample(x2, mod_seq, weights, h0, cpad):
    n_rows = x2.shape[0]
    arrs, specs = weights
    row = lambda w: pl.BlockSpec((ROWS, w), lambda i: (i, 0))
    seq_row = lambda w: pl.BlockSpec((ROWS // SUBLANES, w), lambda i: (i, 0))
    return pl.pallas_call(
        functools.partial(_mixer_kernel, True),
        grid=(n_rows // ROWS,),
        in_specs=[row(D_MODEL), seq_row(6 * D_MODEL)] + specs + [seq_row(D_B), row(D_B)],
        out_specs=[row(D_MODEL), pl.BlockSpec((ROWS * SUBLANES, LANES), lambda i: (i, 0)),
                   pl.BlockSpec((LOGIT_ROWS, ROWS), lambda i: (0, i)),
                   row(D_B), row(D_B), row(D_B)],
        out_shape=[jax.ShapeDtypeStruct((n_rows, D_MODEL), F32),
                   jax.ShapeDtypeStruct((n_rows * SUBLANES, LANES), F32),
                   jax.ShapeDtypeStruct((LOGIT_ROWS, n_rows), F32),
                   jax.ShapeDtypeStruct((n_rows, D_B), F32),
                   jax.ShapeDtypeStruct((n_rows, D_B), F32),
                   jax.ShapeDtypeStruct((n_rows, D_B), F32)],
        compiler_params=pltpu.CompilerParams(
            dimension_semantics=("arbitrary",), vmem_limit_bytes=VMEM_LIMIT),
        name="mixer_sample",
    )(x2, mod_seq, *arrs, h0, cpad)


def _router_kernel(lg_ref, upper_ref, lower_ref, mi_ref, mf_ref, cnt_ref, tot_ref, run_ref):
    phase = pl.program_id(0)
    step = pl.program_id(1)
    l = lg_ref[...]
    cols = l.shape[1]
    row8 = lax.broadcasted_iota(jnp.int32, (SUBLANES, cols), 0)
    neg = -jnp.inf
    big = jnp.int32(SUBLANES)

    def first_max(vals):
        top = jnp.max(vals, axis=0, keepdims=True)
        return top, jnp.min(jnp.where(vals == top, row8, big), axis=0, keepdims=True)

    is_group = row8 < N_GROUPS
    lg = l[:SUBLANES]
    m, gsel = first_max(jnp.where(is_group, lg, neg))
    denom = jnp.sum(jnp.where(is_group, jnp.exp(lg - m), 0.0), axis=0, keepdims=True)
    pgsel = 1.0 / denom
    le = l[SUBLANES:2 * SUBLANES]
    for g in range(1, N_GROUPS):
        le = jnp.where(gsel == g, l[(g + 1) * SUBLANES:(g + 2) * SUBLANES], le)
    v1, i1 = first_max(le)
    v2, i2 = first_max(jnp.where(row8 == i1, neg, le))
    ex = jnp.exp(v2 - v1)
    wk1 = pgsel / (1.0 + ex)
    wk2 = pgsel * ex / (1.0 + ex)
    e1 = gsel * EXPERTS_PER_GROUP + i1
    e2 = gsel * EXPERTS_PER_GROUP + i2
    row_e = lax.broadcasted_iota(jnp.int32, (N_EXPERTS, cols), 0)
    oh1 = jnp.where(row_e == e1, 1.0, 0.0)
    oh2 = jnp.where(row_e == e2, 1.0, 0.0)
    oh = oh1 + oh2

    @pl.when(jnp.logical_and(phase == 0, step == 0))
    def _():
        tot_ref[...] = jnp.zeros_like(tot_ref)

    @pl.when(phase == 0)
    def _():
        tot_ref[...] = tot_ref[...] + jnp.sum(oh, axis=1, keepdims=True)

    @pl.when(jnp.logical_and(phase == 1, step == 0))
    def _():
        tot = tot_ref[...]
        hi = (tot * (1.0 / MXU_DIM)).astype(jnp.int32).astype(F32)
        lo = tot - hi * MXU_DIM
        run_ref[...] = (_dot(lower_ref[...], hi.astype(BF16)) * MXU_DIM
                        + _dot(lower_ref[...], lo.astype(BF16)))

    @pl.when(phase == 1)
    def _():
        blocks = cols // LANES
        stacked = jnp.concatenate([oh[:, b * LANES:(b + 1) * LANES] for b in range(blocks)], axis=0)
        local = _dot(stacked.astype(BF16), upper_ref[...])
        run = run_ref[...]
        p1, p2 = [], []
        for b in range(blocks):
            sl = slice(b * LANES, (b + 1) * LANES)
            before = local[b * N_EXPERTS:(b + 1) * N_EXPERTS] + run
            p1.append(jnp.sum(oh1[:, sl] * before, axis=0, keepdims=True))
            p2.append(jnp.sum(oh2[:, sl] * before, axis=0, keepdims=True))
            run = run + jnp.sum(oh[:, sl], axis=1, keepdims=True)
        run_ref[...] = run
        pos1 = jnp.concatenate(p1, axis=1).astype(jnp.int32)
        pos2 = jnp.concatenate(p2, axis=1).astype(jnp.int32)
        zero_i = jnp.zeros_like(row8)
        mi_ref[...] = jnp.where(row8 == 0, pos1, jnp.where(row8 == 1, pos2, jnp.where(
            row8 == 2, e1, jnp.where(row8 == 3, e2, zero_i))))
        mf_ref[...] = jnp.where(row8 == 0, wk1, jnp.where(row8 == 1, wk2, 0.0))
        cnt_ref[...] = tot_ref[...].astype(jnp.int32)


def _router_call(lg_t):
    n_tokens = lg_t.shape[1]
    tc = ROUTER_COLS
    r = lax.broadcasted_iota(jnp.int32, (LANES, LANES), 0)
    c = lax.broadcasted_iota(jnp.int32, (LANES, LANES), 1)
    upper = jnp.where(r < c, 1.0, 0.0).astype(BF16)
    lower = jnp.where(c < r, 1.0, 0.0)[:N_EXPERTS, :N_EXPERTS].astype(BF16)
    col = pl.BlockSpec((SUBLANES, tc), lambda p, i: (0, i * p))
    return pl.pallas_call(
        _router_kernel,
        grid=(2, n_tokens // tc),
        in_specs=[pl.BlockSpec((LOGIT_ROWS, tc), lambda p, i: (0, i)),
                  _const_spec((LANES, LANES)), _const_spec((N_EXPERTS, N_EXPERTS))],
        out_specs=[col, col, _const_spec((N_EXPERTS, LANES))],
        out_shape=[jax.ShapeDtypeStruct((SUBLANES, n_tokens), jnp.int32),
                   jax.ShapeDtypeStruct((SUBLANES, n_tokens), F32),
                   jax.ShapeDtypeStruct((N_EXPERTS, LANES), jnp.int32)],
        scratch_shapes=[pltpu.VMEM((N_EXPERTS, LANES), F32), pltpu.VMEM((N_EXPERTS, LANES), F32)],
        compiler_params=pltpu.CompilerParams(
            dimension_semantics=("arbitrary", "arbitrary"), vmem_limit_bytes=VMEM_LIMIT),
        name="router",
    )(lg_t, upper, lower)


def _ffn_plan(counts, n_pairs):
    ends = jnp.cumsum(counts)
    off = ends - counts
    ids = jnp.arange(N_EXPERTS, dtype=jnp.int32)

    n_tiles = n_pairs // FFN_ROWS
    n_items = n_tiles + N_EXPERTS
    first_tile = off // FFN_ROWS
    last_tile = (ends - 1) // FFN_ROWS
    items_e = jnp.where(counts > 0, last_tile - first_tile + 1, 0)
    items_end = jnp.cumsum(items_e)
    items_start = items_end - items_e
    n_work = items_end[-1]
    w = jnp.arange(n_items, dtype=jnp.int32)
    wv = jnp.minimum(w, n_work - 1)
    e_w = jnp.sum((items_end[None, :] <= wv[:, None]).astype(jnp.int32), axis=1)
    e_w = jnp.minimum(e_w, N_EXPERTS - 1)
    sel = e_w[:, None] == ids[None, :]
    pick = lambda a: jnp.sum(jnp.where(sel, a[None, :], 0), axis=1)
    t_w = pick(first_tile) + (wv - pick(items_start))
    lo_w = jnp.maximum(pick(off), t_w * FFN_ROWS) - t_w * FFN_ROWS
    hi_w = jnp.minimum(pick(ends), (t_w + 1) * FFN_ROWS) - t_w * FFN_ROWS
    valid = w < n_work
    lo_w = jnp.where(valid, lo_w, 0)
    hi_w = jnp.where(valid, hi_w, 0)
    i32 = lambda a: a.astype(jnp.int32)
    return i32(t_w), i32(e_w), i32(lo_w), i32(hi_w)


def _scatter_kernel(tiles_p, pos1_s, pos2_s, h2p_ref, h2s_ref, xs_hbm, sem):
    i = pl.program_id(0)
    base = i * DISPATCH_ROWS

    def scatter_rows(src_ref):
        def row_copy(r, pos_s):
            p = pos_s[base + r]
            return pltpu.make_async_copy(src_ref.at[_token_tile(r), :], xs_hbm.at[_token_tile(p), :],
                                         sem)

        def start(r, carry):
            row_copy(r, pos1_s).start(priority=0)
            row_copy(r, pos2_s).start(priority=1)
            return carry

        def wait(r, carry):
            row_copy(r, pos1_s).wait()
            row_copy(r, pos2_s).wait()
            return carry

        lax.fori_loop(0, DISPATCH_ROWS, start, 0, unroll=8)
        lax.fori_loop(0, DISPATCH_ROWS, wait, 0, unroll=8)

    @pl.when(i < tiles_p)
    def _():
        scatter_rows(h2p_ref)

    @pl.when(i >= tiles_p)
    def _():
        scatter_rows(h2s_ref)


def _scatter_call(pos1, pos2, h2_p, h2_s):
    tm = DISPATCH_ROWS
    tiles_p = h2_p.shape[0] // (tm * SUBLANES)
    tiles_s = h2_s.shape[0] // (tm * SUBLANES)
    n_pairs = TOP_K * (h2_p.shape[0] + h2_s.shape[0]) // SUBLANES
    blk = (tm * SUBLANES, LANES)
    return pl.pallas_call(
        functools.partial(_scatter_kernel, tiles_p),
        grid_spec=pltpu.PrefetchScalarGridSpec(
            num_scalar_prefetch=2,
            grid=(tiles_p + tiles_s,),
            in_specs=[pl.BlockSpec(blk, lambda i, *_: (jnp.minimum(i, tiles_p - 1), 0)),
                      pl.BlockSpec(blk, lambda i, *_: (jnp.maximum(i - tiles_p, 0), 0))],
            out_specs=pl.BlockSpec(memory_space=pl.ANY),
            scratch_shapes=[pltpu.SemaphoreType.DMA(())],
        ),
        out_shape=jax.ShapeDtypeStruct((n_pairs * SUBLANES, LANES), F32),
        compiler_params=pltpu.CompilerParams(
            dimension_semantics=("arbitrary",), vmem_limit_bytes=VMEM_LIMIT),
        name="dispatch_scatter",
    )(pos1, pos2, h2_p, h2_s)


def _ffn_kernel(wt_s, we_s, wlo_s, whi_s, x_ref, w1_ref, w3_ref, w2_ref, o_ref, w13b_ref, w2b_ref):
    del wt_s
    w = pl.program_id(0)
    e = we_s[w]
    e_prev = we_s[jnp.maximum(w - 1, 0)]

    @pl.when(jnp.logical_or(w == 0, e != e_prev))
    def _():
        w13b_ref[:, :D_EXPERT] = w1_ref[0].astype(BF16)
        w13b_ref[:, D_EXPERT:] = w3_ref[0].astype(BF16)
        w2b_ref[...] = w2_ref[0].astype(BF16)

    lo = wlo_s[w]
    hi = whi_s[w]

    @pl.when(hi > lo)
    def _():
        x = _load_token_tiles(x_ref, (), FFN_ROWS)
        h13 = _dot(x.astype(BF16), w13b_ref[...])
        a = h13[:, :D_EXPERT]
        b = h13[:, D_EXPERT:]
        hdn = (a * jax.nn.sigmoid(a)) * b
        y = _dot(hdn.astype(BF16), w2b_ref[...])
        rows = lax.broadcasted_iota(jnp.int32, y.shape, 0)
        mine = jnp.logical_and(rows >= lo, rows < hi)

        @pl.when(lo == 0)
        def _():
            _store_token_tiles(o_ref, jnp.where(mine, y, 0.0))

        @pl.when(lo > 0)
        def _():
            _store_token_tiles(o_ref, jnp.where(mine, y, _load_token_tiles(o_ref, (), FFN_ROWS)))


def _ffn_call(plan, xs, w1, w3, w2):
    t_w, e_w, lo_w, hi_w = plan
    n_pairs = xs.shape[0] // SUBLANES
    tm = FFN_ROWS
    wspec = lambda shape: pl.BlockSpec((1,) + shape, lambda i, wt, we, wlo, whi: (we[i], 0, 0))
    row = pl.BlockSpec((tm * SUBLANES, LANES), lambda i, wt, we, wlo, whi: (wt[i], 0))
    return pl.pallas_call(
        _ffn_kernel,
        grid_spec=pltpu.PrefetchScalarGridSpec(
            num_scalar_prefetch=4,
            grid=(t_w.shape[0],),
            in_specs=[row, wspec((D_MODEL, D_EXPERT)), wspec((D_MODEL, D_EXPERT)),
                      wspec((D_EXPERT, D_MODEL))],
            out_specs=row,
            scratch_shapes=[pltpu.VMEM((D_MODEL, 2 * D_EXPERT), BF16),
                            pltpu.VMEM((D_EXPERT, D_MODEL), BF16)],
        ),
        out_shape=jax.ShapeDtypeStruct((n_pairs * SUBLANES, LANES), F32),
        compiler_params=pltpu.CompilerParams(
            dimension_semantics=("arbitrary",), vmem_limit_bytes=VMEM_LIMIT),
        name="expert_ffn",
    )(t_w, e_w, lo_w, hi_w, xs, w1, w3, w2)


def _combine_kernel(three_d, row0, n_steps, pos1_s, pos2_s, x1_ref, wk_ref, mod_ref, g_ref, b_ref,
                    ys_hbm, o_ref, ybuf, sems):
    step = pl.program_id(0)
    slot = step % 2

    def gather(tile, tile_slot):
        base = row0 + tile * COMBINE_ROWS

        def row_copy(r, k, pos_s):
            p = pos_s[base + r]
            return pltpu.make_async_copy(ys_hbm.at[_token_tile(p), :],
                                         ybuf.at[tile_slot, k, _token_tile(r), :], sems.at[tile_slot])

        def start(r, carry):
            row_copy(r, 0, pos1_s).start(priority=0)
            row_copy(r, 1, pos2_s).start(priority=1)
            return carry

        def wait(r, carry):
            row_copy(r, 0, pos1_s).wait()
            row_copy(r, 1, pos2_s).wait()
            return carry

        return start, wait

    @pl.when(step == 0)
    def _():
        lax.fori_loop(0, COMBINE_ROWS, gather(step, slot)[0], 0, unroll=8)

    @pl.when(step + 1 < n_steps)
    def _():
        lax.fori_loop(0, COMBINE_ROWS, gather(step + 1, 1 - slot)[0], 0, unroll=8)

    lax.fori_loop(0, COMBINE_ROWS, gather(step, slot)[1], 0, unroll=8)

    wk = wk_ref[...]
    moe = (wk[:, 0:1] * _load_token_tiles(ybuf, (slot, 0), COMBINE_ROWS)
           + wk[:, 1:2] * _load_token_tiles(ybuf, (slot, 1), COMBINE_ROWS))
    x1 = x1_ref[...]
    if three_d:
        g2 = mod_ref[0, :, 5 * D_MODEL:]
        o_ref[0] = _layer_norm(ALPHA * x1 + (1.0 + g2) * moe, g_ref[...], b_ref[...])
    else:
        def seq_rows(v, b):
            return v[b * SUBLANES:(b + 1) * SUBLANES]

        resid = jnp.concatenate([
            ALPHA * seq_rows(x1, b) + (1.0 + mod_ref[b:b + 1, 5 * D_MODEL:]) * seq_rows(moe, b)
            for b in range(COMBINE_ROWS // SUBLANES)], axis=0)
        o_ref[...] = _layer_norm(resid, g_ref[...], b_ref[...])


def _combine_call(three_d, pos1, pos2, x1_all, mf_all, mod, ln2_g, ln2_b, ys, row0, batch, seq):
    tm = COMBINE_ROWS
    blk0 = row0 // tm
    if three_d:
        tiles = seq // tm
        grid = (batch * tiles,)
        mod_spec = pl.BlockSpec((1, 1, 6 * D_MODEL), lambda i, *_: (i // tiles, 0, 0))
        out_spec = pl.BlockSpec((1, tm, D_MODEL), lambda i, *_: (i // tiles, i % tiles, 0))
        out_shape = jax.ShapeDtypeStruct((batch, seq, D_MODEL), F32)
    else:
        grid = (batch * seq // tm,)
        mod_spec = pl.BlockSpec((tm // SUBLANES, 6 * D_MODEL), lambda i, *_: (i, 0))
        out_spec = pl.BlockSpec((tm, D_MODEL), lambda i, *_: (i, 0))
        out_shape = jax.ShapeDtypeStruct((batch * seq, D_MODEL), F32)
    return pl.pallas_call(
        functools.partial(_combine_kernel, three_d, row0, grid[0]),
        grid_spec=pltpu.PrefetchScalarGridSpec(
            num_scalar_prefetch=2,
            grid=grid,
            in_specs=[pl.BlockSpec((tm, D_MODEL), lambda i, *_: (i, 0)),
                      pl.BlockSpec((tm, LANES), lambda i, *_: (blk0 + i, 0)),
                      mod_spec,
                      pl.BlockSpec(ln2_g.shape, lambda i, *_: (0, 0)),
                      pl.BlockSpec(ln2_b.shape, lambda i, *_: (0, 0)),
                      pl.BlockSpec(memory_space=pl.ANY)],
            out_specs=out_spec,
            scratch_shapes=[pltpu.VMEM((2, TOP_K, tm * SUBLANES, LANES), F32),
                            pltpu.SemaphoreType.DMA((2,))],
        ),
        out_shape=out_shape,
        compiler_params=pltpu.CompilerParams(
            dimension_semantics=("arbitrary",), vmem_limit_bytes=VMEM_LIMIT),
        name="combine_prompt" if three_d else "combine_sample",
    )(pos1, pos2, x1_all, mf_all, mod, ln2_g, ln2_b, ys)


def _block_diag(w):
    heads, n, _ = w.shape
    eye = jnp.eye(heads, dtype=w.dtype)
    return (eye[:, None, :, None] * w[:, :, None, :]).reshape(heads * n, heads * n)


def kernel(x_prompt, x_sample, state_rglru_h, state_conv, c_prompt, c_sample, w_ada, b_ada, w_in, w_s, b_s, lnv_g, lnv_b, conv_w, conv_b, lru_wa, lru_ba, lru_wx, lru_bx, lru_lam, w_out, ln1_g, ln1_b, w_rg, b_rg, w_re, b_re, w1, w3, w2, ln2_g, ln2_b):
    batch, seq, _ = x_prompt.shape
    dec_batch, dec_seq, _ = x_sample.shape
    assert dec_seq == SUBLANES and seq % ROWS == 0 and (dec_batch * dec_seq) % ROWS == 0
    n_prompt = batch * seq
    n_sample = dec_batch * dec_seq
    n_tokens = n_prompt + n_sample
    n_pairs = TOP_K * n_tokens
    assert n_tokens % ROUTER_COLS == 0 and n_pairs % FFN_ROWS == 0
    l = 0

    mod = _ada_call(jnp.concatenate([c_prompt, c_sample], axis=0), w_ada[l], b_ada[l][None])
    mod_p3 = mod[:batch][:, None, :]
    mod_s = mod[batch:]

    tri = jnp.tril(jnp.ones((CHUNK, CHUNK), dtype=bool))
    ws_tri = jnp.where(tri[None], w_s[l], 0.0)
    reps_p = MXU_DIM // CHUNK
    eye_p = jnp.eye(reps_p, dtype=F32)
    wsm_p = (eye_p[None, :, None, :, None] * ws_tri[:, None, :, None, :]).reshape(
        A_HEADS, MXU_DIM, MXU_DIM).astype(BF16)
    reps_s = MXU_DIM // dec_seq
    eye_s = jnp.eye(reps_s, dtype=F32)
    wsm_s = (eye_s[None, :, None, :, None] * ws_tri[:, None, :dec_seq, None, :dec_seq]).reshape(
        A_HEADS, MXU_DIM, MXU_DIM).astype(BF16)
    bs_pos = jnp.repeat(jnp.transpose(b_s[l]), A_HEAD_DIM, axis=1)
    bsf_p = jnp.tile(bs_pos, (reps_p, 1))
    bsf_s = jnp.tile(bs_pos[:dec_seq], (reps_s, 1))

    wa_bd = _block_diag(lru_wa[l])
    wx_bd = _block_diag(lru_wx[l])
    wg = jnp.stack([
        jnp.concatenate([wa_bd[q * MXU_DIM:(q + 1) * MXU_DIM, q * MXU_DIM:(q + 1) * MXU_DIM],
                         wx_bd[q * MXU_DIM:(q + 1) * MXU_DIM, q * MXU_DIM:(q + 1) * MXU_DIM]], axis=1)
        for q in range(D_B // MXU_DIM)]).astype(BF16)
    bg = jnp.concatenate([lru_ba[l].reshape(1, D_B), lru_bx[l].reshape(1, D_B)], axis=1)

    group_pad = jnp.zeros((SUBLANES - N_GROUPS, D_MODEL), F32)
    wr = jnp.concatenate([jnp.transpose(w_rg[l]), group_pad, jnp.transpose(w_re[l])], axis=0)
    wr_hi = wr.astype(BF16)
    wr_lo = (wr - wr_hi.astype(F32)).astype(BF16)
    br = jnp.concatenate([b_rg[l], jnp.zeros((SUBLANES - N_GROUPS,), F32), b_re[l]])[:, None]

    common = dict(lnv_g=lnv_g[l][None], lnv_b=lnv_b[l][None], conv_w=conv_w[l], conv_b=conv_b[l][None],
                  wg=wg, bg=bg, lam=lru_lam[l][None], w_out=w_out[l].astype(BF16),
                  ln1_g=ln1_g[l][None], ln1_b=ln1_b[l][None], wr_hi=wr_hi, wr_lo=wr_lo, br=br)
    w_in_b = w_in[l].astype(BF16)
    weights_p = _mixer_weights(w_in_b, wsm_p, bsf_p, **common)
    weights_s = _mixer_weights(w_in_b, wsm_s, bsf_s, **common)

    x1_p, h2_p, lg_p, cst_p, hst_p = _mixer_prompt(x_prompt, mod_p3, weights_p)

    cpad = jnp.concatenate(
        [jnp.zeros((dec_batch, SUBLANES - (CONV_W - 1), D_B), F32), state_conv[l]], axis=1
    ).reshape(n_sample, D_B)
    x1_s, h2_s, lg_s, xr_s, v_s, hs_s = _mixer_sample(
        x_sample.reshape(n_sample, D_MODEL), mod_s, weights_s, state_rglru_h[l], cpad)

    meta_i, meta_f, counts = _router_call(jnp.concatenate([lg_p, lg_s], axis=1))
    pos1, pos2 = meta_i[0], meta_i[1]
    wk_tok = jnp.pad(jnp.transpose(meta_f[:TOP_K]), ((0, 0), (0, LANES - TOP_K)))
    xs = _scatter_call(pos1, pos2, h2_p, h2_s)
    ys = _ffn_call(_ffn_plan(counts[:, 0], n_pairs), xs, w1[l], w3[l], w2[l])

    ln2g, ln2b = ln2_g[l][None], ln2_b[l][None]
    y_prompt = _combine_call(True, pos1, pos2, x1_p, wk_tok, mod_p3, ln2g, ln2b, ys,
                             0, batch, seq)
    y_sample = _combine_call(False, pos1, pos2, x1_s, wk_tok, mod_s, ln2g, ln2b, ys,
                             n_prompt, dec_batch, dec_seq).reshape(dec_batch, dec_seq, D_MODEL)

    new_h_prompt = hst_p[:, SUBLANES - 1, :][None]
    new_conv_prompt = cst_p[:, SUBLANES - (CONV_W - 1):, :][None]
    xr_s3 = xr_s.reshape(dec_batch, dec_seq, D_B)
    new_h_sample = hs_s.reshape(dec_batch, dec_seq, D_B)[:, dec_seq - 1, :][None]
    new_conv_sample = xr_s3[:, dec_seq - (CONV_W - 1):, :][None]
    new_chunk_v_sample = v_s.reshape(dec_batch, dec_seq, D_A)[None]
    return (y_prompt, y_sample, new_h_prompt, new_conv_prompt, new_h_sample, new_conv_sample,
            new_chunk_v_sample)
```

```python
import functools
import math

import jax
import jax.numpy as jnp
from jax import lax
from jax.experimental import pallas as pl
from jax.experimental.pallas import tpu as pltpu

D_MODEL = 1024
D_A = 512
A_HEADS = 4
A_HEAD_DIM = 128
CHUNK = 128
D_B = 512
B_HEADS = 8
B_HEAD_DIM = 64
CONV_W = 4
LRU_C = 8.0
D_IN = 2 * D_A + 2 * D_B
N_GROUPS = 4
EXPERTS_PER_GROUP = 8
N_EXPERTS = 32
TOP_K = 2
D_EXPERT = 512
LN_EPS = 1e-5
DEPTH = 1
ALPHA = (2.0 * DEPTH) ** 0.25

LANES = 128
SUBLANES = 8
MXU_DIM = 256
VMEM_LIMIT = 56 * 1024 * 1024

ROWS = 512
ROUTER_COLS = 1024
LOGIT_ROWS = SUBLANES + N_EXPERTS
DISPATCH_ROWS = 1024
FFN_ROWS = 256
COMBINE_ROWS = 256
BF16 = jnp.bfloat16
F32 = jnp.float32


def _dot(a, b):
    return jnp.dot(a, b, preferred_element_type=F32)


def _dot_nt(a, b):
    return lax.dot_general(a, b, (((1,), (1,)), ((), ())), preferred_element_type=F32)


def _token_tile(r):
    return pl.ds(pl.multiple_of(r * SUBLANES, SUBLANES), SUBLANES)


def _load_token_tiles(ref, lead, n_rows):
    parts = [ref[lead + (pl.ds(s, n_rows, stride=SUBLANES), slice(None))] for s in range(SUBLANES)]
    return jnp.concatenate(parts, axis=1)


def _store_token_tiles(ref, val):
    n_rows = val.shape[0]
    for s in range(SUBLANES):
        ref[pl.ds(s, n_rows, stride=SUBLANES), :] = val[:, s * LANES:(s + 1) * LANES]


def _layer_norm(x, g, b):
    mu = jnp.mean(x, axis=-1, keepdims=True)
    xc = x - mu
    var = jnp.mean(xc * xc, axis=-1, keepdims=True)
    return xc * lax.rsqrt(var + LN_EPS) * g + b


def _gelu_tanh(x):
    c = math.sqrt(2.0 / math.pi)
    return x * (0.5 * (1.0 + jnp.tanh(c * (x + 0.044715 * (x * x * x)))))


def _const_spec(shape):
    nd = len(shape)
    return pl.BlockSpec(shape, lambda *_: (0,) * nd)


def _ada_kernel(c_ref, w_ref, b_ref, o_ref):
    c = c_ref[...]
    a = (c * jax.nn.sigmoid(c)).astype(BF16)
    o_ref[...] = _dot(a, w_ref[...].astype(BF16)) + b_ref[...]


def _ada_call(c_all, w_ada, b_ada):
    n_rows = c_all.shape[0]
    tn = 1536
    return pl.pallas_call(
        _ada_kernel,
        grid=(6 * D_MODEL // tn,),
        in_specs=[
            pl.BlockSpec((n_rows, D_MODEL), lambda j: (0, 0)),
            pl.BlockSpec((D_MODEL, tn), lambda j: (0, j)),
            pl.BlockSpec((1, tn), lambda j: (0, j)),
        ],
        out_specs=pl.BlockSpec((n_rows, tn), lambda j: (0, j)),
        out_shape=jax.ShapeDtypeStruct((n_rows, 6 * D_MODEL), F32),
        compiler_params=pltpu.CompilerParams(
            dimension_semantics=("arbitrary",), vmem_limit_bytes=VMEM_LIMIT),
        name="ada_mod",
    )(c_all, w_ada, b_ada)


def _mixer_kernel(sample, *refs):
    if sample:
        (x_ref, mod_ref, win_ref, wsm_ref, bsf_ref, lnvg_ref, lnvb_ref, convw_ref, convb_ref,
         wg_ref, bg_ref, lam_ref, wout_ref, ln1g_ref, ln1b_ref, wrh_ref, wrl_ref, br_ref,
         h0_ref, cpad_ref,
         x1_ref, h2_ref, lg_ref, xr_ref, v_ref, hs_ref) = refs
    else:
        (x_ref, mod_ref, win_ref, wsm_ref, bsf_ref, lnvg_ref, lnvb_ref, convw_ref, convb_ref,
         wg_ref, bg_ref, lam_ref, wout_ref, ln1g_ref, ln1b_ref, wrh_ref, wrl_ref, br_ref,
         x1_ref, h2_ref, lg_ref, cst_ref, hst_ref, prev_ref, hcar_ref) = refs

    rows = ROWS
    groups = rows // SUBLANES

    if sample:
        x = x_ref[...]

        def chunk(c):
            return lambda b: mod_ref[b:b + 1, c * D_MODEL:(c + 1) * D_MODEL]
    else:
        x = x_ref[0]

        def chunk(c):
            return lambda b: mod_ref[0, :, c * D_MODEL:(c + 1) * D_MODEL]

        @pl.when(pl.program_id(1) == 0)
        def _():
            prev_ref[...] = jnp.zeros_like(prev_ref)
            hcar_ref[...] = jnp.zeros_like(hcar_ref)

    def modulated(fn, *vals):
        if not sample:
            return fn(0, *vals)
        parts = [fn(b, *(v[b * SUBLANES:(b + 1) * SUBLANES] for v in vals)) for b in range(groups)]
        return jnp.concatenate(parts, axis=0)

    sh1, sc1, g1, sh2, sc2, g2 = [chunk(c) for c in range(6)]

    h = modulated(lambda b, xv: xv * (1.0 + sc1(b)) + sh1(b), x)
    proj = _dot(h.astype(BF16), win_ref[...])
    u = proj[:, :D_A]
    v = _layer_norm(proj[:, D_A:2 * D_A], lnvg_ref[...], lnvb_ref[...])
    xr = proj[:, 2 * D_A:2 * D_A + D_B]
    yr = proj[:, 2 * D_A + D_B:]

    vb = v.astype(BF16)
    halves = []
    for r in range(rows // MXU_DIM):
        heads = [
            _dot(wsm_ref[hd], vb[r * MXU_DIM:(r + 1) * MXU_DIM, hd * A_HEAD_DIM:(hd + 1) * A_HEAD_DIM])
            for hd in range(A_HEADS)
        ]
        halves.append(jnp.concatenate(heads, axis=1) + bsf_ref[...])
    mixed = halves[0] if len(halves) == 1 else jnp.concatenate(halves, axis=0)
    out_a = u * mixed

    xr3 = xr.reshape(groups, SUBLANES, D_B)
    t3 = lax.broadcasted_iota(jnp.int32, (1, SUBLANES, D_B), 1)
    cw = convw_ref[...]
    if sample:
        cpad3 = cpad_ref[...].reshape(groups, SUBLANES, D_B)
    else:
        prev8 = prev_ref[...]
    xc3 = convb_ref[...][None]
    for k in range(CONV_W):
        s = CONV_W - 1 - k
        if s == 0:
            shifted = xr3
        else:
            rot = pltpu.roll(xr3, s, 1)
            if sample:
                rot_prev = pltpu.roll(cpad3, s, 1)
            else:
                first = pltpu.roll(prev8, s, 0)[None]
                rot_prev = jnp.concatenate([first, rot[:-1]], axis=0) if groups > 1 else first
            shifted = jnp.where(t3 >= s, rot, rot_prev)
        xc3 = xc3 + shifted * cw[k:k + 1, :][None]
    xc = xc3.reshape(rows, D_B)

    xcb = xc.astype(BF16)
    gq = [_dot(xcb[:, q * MXU_DIM:(q + 1) * MXU_DIM], wg_ref[q]) for q in range(D_B // MXU_DIM)]
    bg = bg_ref[...]
    pre_a = jnp.concatenate([g[:, :MXU_DIM] for g in gq], axis=1) + bg[:, :D_B]
    pre_x = jnp.concatenate([g[:, MXU_DIM:] for g in gq], axis=1) + bg[:, D_B:]
    gate_a = jax.nn.sigmoid(pre_a)
    gate_x = jax.nn.sigmoid(pre_x)
    nl = -lam_ref[...]
    softplus = jnp.maximum(nl, 0.0) + jnp.log1p(jnp.exp(-jnp.abs(nl)))
    log_a = -LRU_C * gate_a * softplus
    a = jnp.exp(log_a)
    mult = jnp.sqrt(-jnp.tanh(log_a) * (1.0 + a * a))
    bx = mult * (gate_x * xc)

    a3 = a.reshape(groups, SUBLANES, D_B)
    b3 = bx.reshape(groups, SUBLANES, D_B)
    for d in (1, 2, 4):
        m = t3 >= d
        a_sh = jnp.where(m, pltpu.roll(a3, d, 1), 1.0)
        b_sh = jnp.where(m, pltpu.roll(b3, d, 1), 0.0)
        b3 = a3 * b_sh + b3
        a3 = a3 * a_sh
    if sample:
        hs3 = jnp.stack([a3[g] * h0_ref[g:g + 1, :] + b3[g] for g in range(groups)], axis=0)
    else:
        h_prev = hcar_ref[...][SUBLANES - 1:SUBLANES, :]
        hs_list = []
        for g in range(groups):
            hg = a3[g] * h_prev + b3[g]
            hs_list.append(hg)
            h_prev = hg[SUBLANES - 1:SUBLANES, :]
        hs3 = jnp.stack(hs_list, axis=0)
    hs = hs3.reshape(rows, D_B)

    out_b = hs * _gelu_tanh(yr)
    mix = _dot(out_a.astype(BF16), wout_ref[:D_A, :]) + _dot(out_b.astype(BF16), wout_ref[D_A:, :])
    y1 = modulated(lambda b, xv, mv: ALPHA * xv + (1.0 + g1(b)) * mv, x, mix)
    x1 = _layer_norm(y1, ln1g_ref[...], ln1b_ref[...])
    h2 = modulated(lambda b, xv: xv * (1.0 + sc2(b)) + sh2(b), x1)

    h2_hi = h2.astype(BF16)
    h2_lo = (h2 - h2_hi.astype(F32)).astype(BF16)
    logits = (_dot_nt(wrh_ref[...], h2_hi) + _dot_nt(wrh_ref[...], h2_lo)
              + _dot_nt(wrl_ref[...], h2_hi) + br_ref[...])

    x1_ref[...] = x1
    _store_token_tiles(h2_ref, h2)
    lg_ref[...] = logits
    if sample:
        xr_ref[...] = xr
        v_ref[...] = v
        hs_ref[...] = hs
    else:
        last_x = xr3[groups - 1]
        last_h = hs3[groups - 1]
        prev_ref[...] = last_x
        hcar_ref[...] = last_h
        cst_ref[0] = last_x
        hst_ref[0] = last_h


def _mixer_weights(w_in, wsm, bsf, lnv_g, lnv_b, conv_w, conv_b, wg, bg, lam, w_out, ln1_g, ln1_b,
                   wr_hi, wr_lo, br):
    arrs = [w_in, wsm, bsf, lnv_g, lnv_b, conv_w, conv_b, wg, bg, lam, w_out, ln1_g, ln1_b,
            wr_hi, wr_lo, br]
    return arrs, [_const_spec(a.shape) for a in arrs]


def _mixer_prompt(x, mod3, weights):
    batch, seq, _ = x.shape
    n_tokens = batch * seq
    tiles = seq // ROWS
    arrs, specs = weights
    row_spec = lambda w: pl.BlockSpec((ROWS, w), lambda b, j: (b * tiles + j, 0))
    st_spec = pl.BlockSpec((1, SUBLANES, D_B), lambda b, j: (b, 0, 0))
    return pl.pallas_call(
        functools.partial(_mixer_kernel, False),
        grid=(batch, tiles),
        in_specs=[pl.BlockSpec((1, ROWS, D_MODEL), lambda b, j: (b, j, 0)),
                  pl.BlockSpec((1, 1, 6 * D_MODEL), lambda b, j: (b, 0, 0))] + specs,
        out_specs=[row_spec(D_MODEL),
                   pl.BlockSpec((ROWS * SUBLANES, LANES), lambda b, j: (b * tiles + j, 0)),
                   pl.BlockSpec((LOGIT_ROWS, ROWS), lambda b, j: (0, b * tiles + j)), st_spec, st_spec],
        out_shape=[jax.ShapeDtypeStruct((n_tokens, D_MODEL), F32),
                   jax.ShapeDtypeStruct((n_tokens * SUBLANES, LANES), F32),
                   jax.ShapeDtypeStruct((LOGIT_ROWS, n_tokens), F32),
                   jax.ShapeDtypeStruct((batch, SUBLANES, D_B), F32),
                   jax.ShapeDtypeStruct((batch, SUBLANES, D_B), F32)],
        scratch_shapes=[pltpu.VMEM((SUBLANES, D_B), F32), pltpu.VMEM((SUBLANES, D_B), F32)],
        compiler_params=pltpu.CompilerParams(
            dimension_semantics=("arbitrary", "arbitrary"), vmem_limit_bytes=VMEM_LIMIT),
        name="mixer_prompt",
    )(x, mod3, *arrs)


def _mixer_sample(x2, mod_seq, weights, h0, cpad):
    n_rows = x2.shape[0]
    arrs, specs = weights
    row = lambda w: pl.BlockSpec((ROWS, w), lambda i: (i, 0))
    seq_row = lambda w: pl.BlockSpec((ROWS // SUBLANES, w), lambda i: (i, 0))
    return pl.pallas_call(
        functools.partial(_mixer_kernel, True),
        grid=(n_rows // ROWS,),
        in_specs=[row(D_MODEL), seq_row(6 * D_MODEL)] + specs + [seq_row(D_B), row(D_B)],
        out_specs=[row(D_MODEL), pl.BlockSpec((ROWS * SUBLANES, LANES), lambda i: (i, 0)),
                   pl.BlockSpec((LOGIT_ROWS, ROWS), lambda i: (0, i)),
                   row(D_B), row(D_B), row(D_B)],
        out_shape=[jax.ShapeDtypeStruct((n_rows, D_MODEL), F32),
                   jax.ShapeDtypeStruct((n_rows * SUBLANES, LANES), F32),
                   jax.ShapeDtypeStruct((LOGIT_ROWS, n_rows), F32),
                   jax.ShapeDtypeStruct((n_rows, D_B), F32),
                   jax.ShapeDtypeStruct((n_rows, D_B), F32),
                   jax.ShapeDtypeStruct((n_rows, D_B), F32)],
        compiler_params=pltpu.CompilerParams(
            dimension_semantics=("arbitrary",), vmem_limit_bytes=VMEM_LIMIT),
        name="mixer_sample",
    )(x2, mod_seq, *arrs, h0, cpad)


def _router_kernel(lg_ref, upper_ref, lower_ref, mi_ref, mf_ref, cnt_ref, tot_ref, run_ref):
    phase = pl.program_id(0)
    step = pl.program_id(1)
    l = lg_ref[...]
    cols = l.shape[1]
    row8 = lax.broadcasted_iota(jnp.int32, (SUBLANES, cols), 0)
    neg = -jnp.inf
    big = jnp.int32(SUBLANES)

    def first_max(vals):
        top = jnp.max(vals, axis=0, keepdims=True)
        return top, jnp.min(jnp.where(vals == top, row8, big), axis=0, keepdims=True)

    is_group = row8 < N_GROUPS
    lg = l[:SUBLANES]
    m, gsel = first_max(jnp.where(is_group, lg, neg))
    denom = jnp.sum(jnp.where(is_group, jnp.exp(lg - m), 0.0), axis=0, keepdims=True)
    pgsel = 1.0 / denom
    le = l[SUBLANES:2 * SUBLANES]
    for g in range(1, N_GROUPS):
        le = jnp.where(gsel == g, l[(g + 1) * SUBLANES:(g + 2) * SUBLANES], le)
    v1, i1 = first_max(le)
    v2, i2 = first_max(jnp.where(row8 == i1, neg, le))
    ex = jnp.exp(v2 - v1)
    wk1 = pgsel / (1.0 + ex)
    wk2 = pgsel * ex / (1.0 + ex)
    e1 = gsel * EXPERTS_PER_GROUP + i1
    e2 = gsel * EXPERTS_PER_GROUP + i2
    row_e = lax.broadcasted_iota(jnp.int32, (N_EXPERTS, cols), 0)
    oh1 = jnp.where(row_e == e1, 1.0, 0.0)
    oh2 = jnp.where(row_e == e2, 1.0, 0.0)
    oh = oh1 + oh2

    @pl.when(jnp.logical_and(phase == 0, step == 0))
    def _():
        tot_ref[...] = jnp.zeros_like(tot_ref)

    @pl.when(phase == 0)
    def _():
        tot_ref[...] = tot_ref[...] + jnp.sum(oh, axis=1, keepdims=True)

    @pl.when(jnp.logical_and(phase == 1, step == 0))
    def _():
        tot = tot_ref[...]
        hi = (tot * (1.0 / MXU_DIM)).astype(jnp.int32).astype(F32)
        lo = tot - hi * MXU_DIM
        run_ref[...] = (_dot(lower_ref[...], hi.astype(BF16)) * MXU_DIM
                        + _dot(lower_ref[...], lo.astype(BF16)))

    @pl.when(phase == 1)
    def _():
        blocks = cols // LANES
        stacked = jnp.concatenate([oh[:, b * LANES:(b + 1) * LANES] for b in range(blocks)], axis=0)
        local = _dot(stacked.astype(BF16), upper_ref[...])
        run = run_ref[...]
        p1, p2 = [], []
        for b in range(blocks):
            sl = slice(b * LANES, (b + 1) * LANES)
            before = local[b * N_EXPERTS:(b + 1) * N_EXPERTS] + run
            p1.append(jnp.sum(oh1[:, sl] * before, axis=0, keepdims=True))
            p2.append(jnp.sum(oh2[:, sl] * before, axis=0, keepdims=True))
            run = run + jnp.sum(oh[:, sl], axis=1, keepdims=True)
        run_ref[...] = run
        pos1 = jnp.concatenate(p1, axis=1).astype(jnp.int32)
        pos2 = jnp.concatenate(p2, axis=1).astype(jnp.int32)
        zero_i = jnp.zeros_like(row8)
        mi_ref[...] = jnp.where(row8 == 0, pos1, jnp.where(row8 == 1, pos2, jnp.where(
            row8 == 2, e1, jnp.where(row8 == 3, e2, zero_i))))
        mf_ref[...] = jnp.where(row8 == 0, wk1, jnp.where(row8 == 1, wk2, 0.0))
        cnt_ref[...] = tot_ref[...].astype(jnp.int32)


def _router_call(lg_t):
    n_tokens = lg_t.shape[1]
    tc = ROUTER_COLS
    r = lax.broadcasted_iota(jnp.int32, (LANES, LANES), 0)
    c = lax.broadcasted_iota(jnp.int32, (LANES, LANES), 1)
    upper = jnp.where(r < c, 1.0, 0.0).astype(BF16)
    lower = jnp.where(c < r, 1.0, 0.0)[:N_EXPERTS, :N_EXPERTS].astype(BF16)
    col = pl.BlockSpec((SUBLANES, tc), lambda p, i: (0, i * p))
    return pl.pallas_call(
        _router_kernel,
        grid=(2, n_tokens // tc),
        in_specs=[pl.BlockSpec((LOGIT_ROWS, tc), lambda p, i: (0, i)),
                  _const_spec((LANES, LANES)), _const_spec((N_EXPERTS, N_EXPERTS))],
        out_specs=[col, col, _const_spec((N_EXPERTS, LANES))],
        out_shape=[jax.ShapeDtypeStruct((SUBLANES, n_tokens), jnp.int32),
                   jax.ShapeDtypeStruct((SUBLANES, n_tokens), F32),
                   jax.ShapeDtypeStruct((N_EXPERTS, LANES), jnp.int32)],
        scratch_shapes=[pltpu.VMEM((N_EXPERTS, LANES), F32), pltpu.VMEM((N_EXPERTS, LANES), F32)],
        compiler_params=pltpu.CompilerParams(
            dimension_semantics=("arbitrary", "arbitrary"), vmem_limit_bytes=VMEM_LIMIT),
        name="router",
    )(lg_t, upper, lower)


def _ffn_plan(counts, n_pairs):
    ends = jnp.cumsum(counts)
    off = ends - counts
    ids = jnp.arange(N_EXPERTS, dtype=jnp.int32)

    n_tiles = n_pairs // FFN_ROWS
    n_items = n_tiles + N_EXPERTS
    first_tile = off // FFN_ROWS
    last_tile = (ends - 1) // FFN_ROWS
    items_e = jnp.where(counts > 0, last_tile - first_tile + 1, 0)
    items_end = jnp.cumsum(items_e)
    items_start = items_end - items_e
    n_work = items_end[-1]
    w = jnp.arange(n_items, dtype=jnp.int32)
    wv = jnp.minimum(w, n_work - 1)
    e_w = jnp.sum((items_end[None, :] <= wv[:, None]).astype(jnp.int32), axis=1)
    e_w = jnp.minimum(e_w, N_EXPERTS - 1)
    sel = e_w[:, None] == ids[None, :]
    pick = lambda a: jnp.sum(jnp.where(sel, a[None, :], 0), axis=1)
    t_w = pick(first_tile) + (wv - pick(items_start))
    lo_w = jnp.maximum(pick(off), t_w * FFN_ROWS) - t_w * FFN_ROWS
    hi_w = jnp.minimum(pick(ends), (t_w + 1) * FFN_ROWS) - t_w * FFN_ROWS
    valid = w < n_work
    lo_w = jnp.where(valid, lo_w, 0)
    hi_w = jnp.where(valid, hi_w, 0)
    used = counts > 0
    order_e = jnp.cumsum(used.astype(jnp.int32)) - 1
    later_used = jnp.logical_and(used[None, :], ids[None, :] > ids[:, None])
    next_e = jnp.min(jnp.where(later_used, ids[None, :], N_EXPERTS), axis=1)
    next_e = jnp.where(next_e == N_EXPERTS, -1, next_e)
    i32 = lambda a: a.astype(jnp.int32)
    return i32(t_w), i32(e_w), i32(lo_w), i32(hi_w), i32(pick(order_e)), i32(pick(next_e))


def _scatter_kernel(tiles_p, pos1_s, pos2_s, h2p_ref, h2s_ref, xs_hbm, sem):
    i = pl.program_id(0)
    base = i * DISPATCH_ROWS

    def scatter_rows(src_ref):
        def row_copy(r, pos_s):
            p = pos_s[base + r]
            return pltpu.make_async_copy(src_ref.at[_token_tile(r), :], xs_hbm.at[_token_tile(p), :],
                                         sem)

        def start(r, carry):
            row_copy(r, pos1_s).start(priority=0)
            row_copy(r, pos2_s).start(priority=1)
            return carry

        def wait(r, carry):
            row_copy(r, pos1_s).wait()
            row_copy(r, pos2_s).wait()
            return carry

        lax.fori_loop(0, DISPATCH_ROWS, start, 0, unroll=32)
        lax.fori_loop(0, DISPATCH_ROWS, wait, 0, unroll=8)

    @pl.when(i < tiles_p)
    def _():
        scatter_rows(h2p_ref)

    @pl.when(i >= tiles_p)
    def _():
        scatter_rows(h2s_ref)


def _scatter_call(pos1, pos2, h2_p, h2_s):
    tm = DISPATCH_ROWS
    tiles_p = h2_p.shape[0] // (tm * SUBLANES)
    tiles_s = h2_s.shape[0] // (tm * SUBLANES)
    n_pairs = TOP_K * (h2_p.shape[0] + h2_s.shape[0]) // SUBLANES
    blk = (tm * SUBLANES, LANES)
    return pl.pallas_call(
        functools.partial(_scatter_kernel, tiles_p),
        grid_spec=pltpu.PrefetchScalarGridSpec(
            num_scalar_prefetch=2,
            grid=(tiles_p + tiles_s,),
            in_specs=[pl.BlockSpec(blk, lambda i, *_: (jnp.minimum(i, tiles_p - 1), 0)),
                      pl.BlockSpec(blk, lambda i, *_: (jnp.maximum(i - tiles_p, 0), 0))],
            out_specs=pl.BlockSpec(memory_space=pl.ANY),
            scratch_shapes=[pltpu.SemaphoreType.DMA(())],
        ),
        out_shape=jax.ShapeDtypeStruct((n_pairs * SUBLANES, LANES), F32),
        compiler_params=pltpu.CompilerParams(
            dimension_semantics=("arbitrary",), vmem_limit_bytes=VMEM_LIMIT),
        name="dispatch_scatter",
    )(pos1, pos2, h2_p, h2_s)


def _ffn_kernel(wt_s, we_s, wlo_s, whi_s, word_s, wnext_s, x_ref, w1_hbm, w3_hbm, w2_hbm, o_ref,
                w1f_ref, w3f_ref, w2f_ref, w13b_ref, w2b_ref, sems):
    del wt_s
    w = pl.program_id(0)
    e = we_s[w]
    e_prev = we_s[jnp.maximum(w - 1, 0)]
    slot = word_s[w] % 2

    def weight_copies(expert, buf):
        return [pltpu.make_async_copy(src.at[expert], dst.at[buf], sems.at[buf])
                for src, dst in ((w1_hbm, w1f_ref), (w3_hbm, w3f_ref), (w2_hbm, w2f_ref))]

    @pl.when(w == 0)
    def _():
        for cp in weight_copies(e, slot):
            cp.start()

    @pl.when(jnp.logical_or(w == 0, e != e_prev))
    def _():
        for cp in weight_copies(e, slot):
            cp.wait()
        w13b_ref[:, :D_EXPERT] = w1f_ref[slot].astype(BF16)
        w13b_ref[:, D_EXPERT:] = w3f_ref[slot].astype(BF16)
        w2b_ref[...] = w2f_ref[slot].astype(BF16)
        nxt = wnext_s[w]

        @pl.when(nxt >= 0)
        def _():
            for cp in weight_copies(nxt, 1 - slot):
                cp.start()

    lo = wlo_s[w]
    hi = whi_s[w]

    @pl.when(hi > lo)
    def _():
        x = _load_token_tiles(x_ref, (), FFN_ROWS)
        h13 = _dot(x.astype(BF16), w13b_ref[...])
        a = h13[:, :D_EXPERT]
        b = h13[:, D_EXPERT:]
        hdn = (a * jax.nn.sigmoid(a)) * b
        y = _dot(hdn.astype(BF16), w2b_ref[...])
        rows = lax.broadcasted_iota(jnp.int32, y.shape, 0)
        mine = jnp.logical_and(rows >= lo, rows < hi)

        @pl.when(lo == 0)
        def _():
            _store_token_tiles(o_ref, jnp.where(mine, y, 0.0))

        @pl.when(lo > 0)
        def _():
            _store_token_tiles(o_ref, jnp.where(mine, y, _load_token_tiles(o_ref, (), FFN_ROWS)))


def _ffn_call(plan, xs, w1, w3, w2):
    t_w = plan[0]
    n_pairs = xs.shape[0] // SUBLANES
    tm = FFN_ROWS
    any_spec = pl.BlockSpec(memory_space=pl.ANY)
    row = pl.BlockSpec((tm * SUBLANES, LANES), lambda i, wt, *_: (wt[i], 0))
    return pl.pallas_call(
        _ffn_kernel,
        grid_spec=pltpu.PrefetchScalarGridSpec(
            num_scalar_prefetch=len(plan),
            grid=(t_w.shape[0],),
            in_specs=[row, any_spec, any_spec, any_spec],
            out_specs=row,
            scratch_shapes=[pltpu.VMEM((2, D_MODEL, D_EXPERT), F32),
                            pltpu.VMEM((2, D_MODEL, D_EXPERT), F32),
                            pltpu.VMEM((2, D_EXPERT, D_MODEL), F32),
                            pltpu.VMEM((D_MODEL, 2 * D_EXPERT), BF16),
                            pltpu.VMEM((D_EXPERT, D_MODEL), BF16),
                            pltpu.SemaphoreType.DMA((2,))],
        ),
        out_shape=jax.ShapeDtypeStruct((n_pairs * SUBLANES, LANES), F32),
        compiler_params=pltpu.CompilerParams(
            dimension_semantics=("arbitrary",), vmem_limit_bytes=VMEM_LIMIT),
        name="expert_ffn",
    )(*plan, xs, w1, w3, w2)


def _combine_kernel(three_d, row0, n_steps, pos1_s, pos2_s, x1_ref, wk_ref, mod_ref, g_ref, b_ref,
                    ys_hbm, o_ref, ybuf, sems):
    step = pl.program_id(0)
    slot = step % 2

    def gather(tile, tile_slot):
        base = row0 + tile * COMBINE_ROWS

        def row_copy(r, k, pos_s):
            p = pos_s[base + r]
            return pltpu.make_async_copy(ys_hbm.at[_token_tile(p), :],
                                         ybuf.at[tile_slot, k, _token_tile(r), :], sems.at[tile_slot])

        def start(r, carry):
            row_copy(r, 0, pos1_s).start(priority=0)
            row_copy(r, 1, pos2_s).start(priority=1)
            return carry

        def wait(r, carry):
            row_copy(r, 0, pos1_s).wait()
            row_copy(r, 1, pos2_s).wait()
            return carry

        return start, wait

    @pl.when(step == 0)
    def _():
        lax.fori_loop(0, COMBINE_ROWS, gather(step, slot)[0], 0, unroll=32)

    @pl.when(step + 1 < n_steps)
    def _():
        lax.fori_loop(0, COMBINE_ROWS, gather(step + 1, 1 - slot)[0], 0, unroll=32)

    lax.fori_loop(0, COMBINE_ROWS, gather(step, slot)[1], 0, unroll=8)

    wk = wk_ref[...]
    moe = (wk[:, 0:1] * _load_token_tiles(ybuf, (slot, 0), COMBINE_ROWS)
           + wk[:, 1:2] * _load_token_tiles(ybuf, (slot, 1), COMBINE_ROWS))
    x1 = x1_ref[...]
    if three_d:
        g2 = mod_ref[0, :, 5 * D_MODEL:]
        o_ref[0] = _layer_norm(ALPHA * x1 + (1.0 + g2) * moe, g_ref[...], b_ref[...])
    else:
        def seq_rows(v, b):
            return v[b * SUBLANES:(b + 1) * SUBLANES]

        resid = jnp.concatenate([
            ALPHA * seq_rows(x1, b) + (1.0 + mod_ref[b:b + 1, 5 * D_MODEL:]) * seq_rows(moe, b)
            for b in range(COMBINE_ROWS // SUBLANES)], axis=0)
        o_ref[...] = _layer_norm(resid, g_ref[...], b_ref[...])


def _combine_call(three_d, pos1, pos2, x1_all, mf_all, mod, ln2_g, ln2_b, ys, row0, batch, seq):
    tm = COMBINE_ROWS
    blk0 = row0 // tm
    if three_d:
        tiles = seq // tm
        grid = (batch * tiles,)
        mod_spec = pl.BlockSpec((1, 1, 6 * D_MODEL), lambda i, *_: (i // tiles, 0, 0))
        out_spec = pl.BlockSpec((1, tm, D_MODEL), lambda i, *_: (i // tiles, i % tiles, 0))
        out_shape = jax.ShapeDtypeStruct((batch, seq, D_MODEL), F32)
    else:
        grid = (batch * seq // tm,)
        mod_spec = pl.BlockSpec((tm // SUBLANES, 6 * D_MODEL), lambda i, *_: (i, 0))
        out_spec = pl.BlockSpec((tm, D_MODEL), lambda i, *_: (i, 0))
        out_shape = jax.ShapeDtypeStruct((batch * seq, D_MODEL), F32)
    return pl.pallas_call(
        functools.partial(_combine_kernel, three_d, row0, grid[0]),
        grid_spec=pltpu.PrefetchScalarGridSpec(
            num_scalar_prefetch=2,
            grid=grid,
            in_specs=[pl.BlockSpec((tm, D_MODEL), lambda i, *_: (i, 0)),
                      pl.BlockSpec((tm, LANES), lambda i, *_: (blk0 + i, 0)),
                      mod_spec,
                      pl.BlockSpec(ln2_g.shape, lambda i, *_: (0, 0)),
                      pl.BlockSpec(ln2_b.shape, lambda i, *_: (0, 0)),
                      pl.BlockSpec(memory_space=pl.ANY)],
            out_specs=out_spec,
            scratch_shapes=[pltpu.VMEM((2, TOP_K, tm * SUBLANES, LANES), F32),
                            pltpu.SemaphoreType.DMA((2,))],
        ),
        out_shape=out_shape,
        compiler_params=pltpu.CompilerParams(
            dimension_semantics=("arbitrary",), vmem_limit_bytes=VMEM_LIMIT),
        name="combine_prompt" if three_d else "combine_sample",
    )(pos1, pos2, x1_all, mf_all, mod, ln2_g, ln2_b, ys)


def _block_diag(w):
    heads, n, _ = w.shape
    eye = jnp.eye(heads, dtype=w.dtype)
    return (eye[:, None, :, None] * w[:, :, None, :]).reshape(heads * n, heads * n)


def kernel(x_prompt, x_sample, state_rglru_h, state_conv, c_prompt, c_sample, w_ada, b_ada, w_in, w_s, b_s, lnv_g, lnv_b, conv_w, conv_b, lru_wa, lru_ba, lru_wx, lru_bx, lru_lam, w_out, ln1_g, ln1_b, w_rg, b_rg, w_re, b_re, w1, w3, w2, ln2_g, ln2_b):
    batch, seq, _ = x_prompt.shape
    dec_batch, dec_seq, _ = x_sample.shape
    assert dec_seq == SUBLANES and seq % ROWS == 0 and (dec_batch * dec_seq) % ROWS == 0
    n_prompt = batch * seq
    n_sample = dec_batch * dec_seq
    n_tokens = n_prompt + n_sample
    n_pairs = TOP_K * n_tokens
    assert n_tokens % ROUTER_COLS == 0 and n_pairs % FFN_ROWS == 0
    l = 0

    mod = _ada_call(jnp.concatenate([c_prompt, c_sample], axis=0), w_ada[l], b_ada[l][None])
    mod_p3 = mod[:batch][:, None, :]
    mod_s = mod[batch:]

    tri = jnp.tril(jnp.ones((CHUNK, CHUNK), dtype=bool))
    ws_tri = jnp.where(tri[None], w_s[l], 0.0)
    reps_p = MXU_DIM // CHUNK
    eye_p = jnp.eye(reps_p, dtype=F32)
    wsm_p = (eye_p[None, :, None, :, None] * ws_tri[:, None, :, None, :]).reshape(
        A_HEADS, MXU_DIM, MXU_DIM).astype(BF16)
    reps_s = MXU_DIM // dec_seq
    eye_s = jnp.eye(reps_s, dtype=F32)
    wsm_s = (eye_s[None, :, None, :, None] * ws_tri[:, None, :dec_seq, None, :dec_seq]).reshape(
        A_HEADS, MXU_DIM, MXU_DIM).astype(BF16)
    bs_pos = jnp.repeat(jnp.transpose(b_s[l]), A_HEAD_DIM, axis=1)
    bsf_p = jnp.tile(bs_pos, (reps_p, 1))
    bsf_s = jnp.tile(bs_pos[:dec_seq], (reps_s, 1))

    wa_bd = _block_diag(lru_wa[l])
    wx_bd = _block_diag(lru_wx[l])
    wg = jnp.stack([
        jnp.concatenate([wa_bd[q * MXU_DIM:(q + 1) * MXU_DIM, q * MXU_DIM:(q + 1) * MXU_DIM],
                         wx_bd[q * MXU_DIM:(q + 1) * MXU_DIM, q * MXU_DIM:(q + 1) * MXU_DIM]], axis=1)
        for q in range(D_B // MXU_DIM)]).astype(BF16)
    bg = jnp.concatenate([lru_ba[l].reshape(1, D_B), lru_bx[l].reshape(1, D_B)], axis=1)

    group_pad = jnp.zeros((SUBLANES - N_GROUPS, D_MODEL), F32)
    wr = jnp.concatenate([jnp.transpose(w_rg[l]), group_pad, jnp.transpose(w_re[l])], axis=0)
    wr_hi = wr.astype(BF16)
    wr_lo = (wr - wr_hi.astype(F32)).astype(BF16)
    br = jnp.concatenate([b_rg[l], jnp.zeros((SUBLANES - N_GROUPS,), F32), b_re[l]])[:, None]

    common = dict(lnv_g=lnv_g[l][None], lnv_b=lnv_b[l][None], conv_w=conv_w[l], conv_b=conv_b[l][None],
                  wg=wg, bg=bg, lam=lru_lam[l][None], w_out=w_out[l].astype(BF16),
                  ln1_g=ln1_g[l][None], ln1_b=ln1_b[l][None], wr_hi=wr_hi, wr_lo=wr_lo, br=br)
    w_in_b = w_in[l].astype(BF16)
    weights_p = _mixer_weights(w_in_b, wsm_p, bsf_p, **common)
    weights_s = _mixer_weights(w_in_b, wsm_s, bsf_s, **common)

    x1_p, h2_p, lg_p, cst_p, hst_p = _mixer_prompt(x_prompt, mod_p3, weights_p)

    cpad = jnp.concatenate(
        [jnp.zeros((dec_batch, SUBLANES - (CONV_W - 1), D_B), F32), state_conv[l]], axis=1
    ).reshape(n_sample, D_B)
    x1_s, h2_s, lg_s, xr_s, v_s, hs_s = _mixer_sample(
        x_sample.reshape(n_sample, D_MODEL), mod_s, weights_s, state_rglru_h[l], cpad)

    meta_i, meta_f, counts = _router_call(jnp.concatenate([lg_p, lg_s], axis=1))
    pos1, pos2 = meta_i[0], meta_i[1]
    wk_tok = jnp.pad(jnp.transpose(meta_f[:TOP_K]), ((0, 0), (0, LANES - TOP_K)))
    xs = _scatter_call(pos1, pos2, h2_p, h2_s)
    ys = _ffn_call(_ffn_plan(counts[:, 0], n_pairs), xs, w1[l], w3[l], w2[l])

    ln2g, ln2b = ln2_g[l][None], ln2_b[l][None]
    y_prompt = _combine_call(True, pos1, pos2, x1_p, wk_tok, mod_p3, ln2g, ln2b, ys,
                             0, batch, seq)
    y_sample = _combine_call(False, pos1, pos2, x1_s, wk_tok, mod_s, ln2g, ln2b, ys,
                             n_prompt, dec_batch, dec_seq).reshape(dec_batch, dec_seq, D_MODEL)

    new_h_prompt = hst_p[:, SUBLANES - 1, :][None]
    new_conv_prompt = cst_p[:, SUBLANES - (CONV_W - 1):, :][None]
    xr_s3 = xr_s.reshape(dec_batch, dec_seq, D_B)
    new_h_sample = hs_s.reshape(dec_batch, dec_seq, D_B)[:, dec_seq - 1, :][None]
    new_conv_sample = xr_s3[:, dec_seq - (CONV_W - 1):, :][None]
    new_chunk_v_sample = v_s.reshape(dec_batch, dec_seq, D_A)[None]
    return (y_prompt, y_sample, new_h_prompt, new_conv_prompt, new_h_sample, new_conv_sample,
            new_chunk_v_sample)
```

```python
import functools
import math

import jax
import jax.numpy as jnp
from jax import lax
from jax.experimental import pallas as pl
from jax.experimental.pallas import tpu as pltpu

D_MODEL = 1024
D_A = 512
A_HEADS = 4
A_HEAD_DIM = 128
CHUNK = 128
D_B = 512
B_HEADS = 8
B_HEAD_DIM = 64
CONV_W = 4
LRU_C = 8.0
D_IN = 2 * D_A + 2 * D_B
N_GROUPS = 4
EXPERTS_PER_GROUP = 8
N_EXPERTS = 32
TOP_K = 2
D_EXPERT = 512
LN_EPS = 1e-5
DEPTH = 1
ALPHA = (2.0 * DEPTH) ** 0.25

LANES = 128
SUBLANES = 8
MXU_DIM = 256
VMEM_LIMIT = 56 * 1024 * 1024

ROWS = 512
ROUTER_COLS = 1024
LOGIT_ROWS = SUBLANES + N_EXPERTS
DISPATCH_ROWS = 1024
FFN_ROWS = 256
COMBINE_ROWS = 256
BF16 = jnp.bfloat16
F32 = jnp.float32


def _dot(a, b):
    return jnp.dot(a, b, preferred_element_type=F32)


def _dot_nt(a, b):
    return lax.dot_general(a, b, (((1,), (1,)), ((), ())), preferred_element_type=F32)


TILE_ROWS = D_MODEL // LANES


def _token_tile(r):
    return pl.ds(pl.multiple_of(r * TILE_ROWS, TILE_ROWS), TILE_ROWS)


def _load_token_tiles(ref, lead, n_rows):
    parts = [ref[lead + (pl.ds(s, n_rows, stride=TILE_ROWS), slice(None))] for s in range(TILE_ROWS)]
    return jnp.concatenate(parts, axis=1)


def _store_token_tiles(ref, val, keep=None):
    n_rows = val.shape[0]
    for s in range(TILE_ROWS):
        rows = pl.ds(s, n_rows, stride=TILE_ROWS)
        new = val[:, s * LANES:(s + 1) * LANES]
        if keep is not None:
            new = jnp.where(keep, new, ref[rows, :])
        ref[rows, :] = new


def _layer_norm(x, g, b):
    mu = jnp.mean(x, axis=-1, keepdims=True)
    xc = x - mu
    var = jnp.mean(xc * xc, axis=-1, keepdims=True)
    return xc * lax.rsqrt(var + LN_EPS) * g + b


def _gelu_tanh(x):
    c = math.sqrt(2.0 / math.pi)
    return x * (0.5 * (1.0 + jnp.tanh(c * (x + 0.044715 * (x * x * x)))))


def _const_spec(shape):
    nd = len(shape)
    return pl.BlockSpec(shape, lambda *_: (0,) * nd)


def _ada_kernel(cp_ref, cs_ref, w_ref, b_ref, op_ref, os_ref):
    w = w_ref[...].astype(BF16)
    for c_ref, o_ref in ((cp_ref, op_ref), (cs_ref, os_ref)):
        c = c_ref[...]
        a = (c * jax.nn.sigmoid(c)).astype(BF16)
        o_ref[...] = _dot(a, w) + b_ref[...]


def _ada_call(c_prompt, c_sample, w_ada, b_ada):
    tn = 1536
    rows_p, rows_s = c_prompt.shape[0], c_sample.shape[0]
    return pl.pallas_call(
        _ada_kernel,
        grid=(6 * D_MODEL // tn,),
        in_specs=[
            pl.BlockSpec((rows_p, D_MODEL), lambda j: (0, 0)),
            pl.BlockSpec((rows_s, D_MODEL), lambda j: (0, 0)),
            pl.BlockSpec((D_MODEL, tn), lambda j: (0, j)),
            pl.BlockSpec((1, tn), lambda j: (0, j)),
        ],
        out_specs=[pl.BlockSpec((rows_p, tn), lambda j: (0, j)),
                   pl.BlockSpec((rows_s, tn), lambda j: (0, j))],
        out_shape=[jax.ShapeDtypeStruct((rows_p, 6 * D_MODEL), F32),
                   jax.ShapeDtypeStruct((rows_s, 6 * D_MODEL), F32)],
        compiler_params=pltpu.CompilerParams(
            dimension_semantics=("arbitrary",), vmem_limit_bytes=VMEM_LIMIT),
        name="ada_mod",
    )(c_prompt, c_sample, w_ada, b_ada)


def _mixer_kernel(sample, *refs):
    if sample:
        (x_ref, mod_ref, win_ref, wsm_ref, bsf_ref, lnvg_ref, lnvb_ref, convw_ref, convb_ref,
         wg_ref, bg_ref, lam_ref, wout_ref, ln1g_ref, ln1b_ref, wrh_ref, wrl_ref, br_ref,
         h0_ref, cpad_ref,
         x1_ref, h2_ref, lg_ref, xr_ref, v_ref, hs_ref) = refs
    else:
        (x_ref, mod_ref, win_ref, wsm_ref, bsf_ref, lnvg_ref, lnvb_ref, convw_ref, convb_ref,
         wg_ref, bg_ref, lam_ref, wout_ref, ln1g_ref, ln1b_ref, wrh_ref, wrl_ref, br_ref,
         x1_ref, h2_ref, lg_ref, cst_ref, hst_ref, prev_ref, hcar_ref) = refs

    rows = ROWS
    groups = rows // SUBLANES

    if sample:
        x = x_ref[...]

        def chunk(c):
            return lambda b: mod_ref[b:b + 1, c * D_MODEL:(c + 1) * D_MODEL]
    else:
        x = x_ref[0]

        def chunk(c):
            return lambda b: mod_ref[0, :, c * D_MODEL:(c + 1) * D_MODEL]

        @pl.when(pl.program_id(1) == 0)
        def _():
            prev_ref[...] = jnp.zeros_like(prev_ref)
            hcar_ref[...] = jnp.zeros_like(hcar_ref)

    def modulated(fn, *vals):
        if not sample:
            return fn(0, *vals)
        parts = [fn(b, *(v[b * SUBLANES:(b + 1) * SUBLANES] for v in vals)) for b in range(groups)]
        return jnp.concatenate(parts, axis=0)

    sh1, sc1, g1, sh2, sc2, g2 = [chunk(c) for c in range(6)]

    h = modulated(lambda b, xv: xv * (1.0 + sc1(b)) + sh1(b), x)
    proj = _dot(h.astype(BF16), win_ref[...])
    u = proj[:, :D_A]
    v = _layer_norm(proj[:, D_A:2 * D_A], lnvg_ref[...], lnvb_ref[...])
    xr = proj[:, 2 * D_A:2 * D_A + D_B]
    yr = proj[:, 2 * D_A + D_B:]

    vb = v.astype(BF16)
    halves = []
    for r in range(rows // MXU_DIM):
        heads = [
            _dot(wsm_ref[hd], vb[r * MXU_DIM:(r + 1) * MXU_DIM, hd * A_HEAD_DIM:(hd + 1) * A_HEAD_DIM])
            for hd in range(A_HEADS)
        ]
        halves.append(jnp.concatenate(heads, axis=1) + bsf_ref[...])
    mixed = halves[0] if len(halves) == 1 else jnp.concatenate(halves, axis=0)
    out_a = u * mixed

    xr3 = xr.reshape(groups, SUBLANES, D_B)
    t3 = lax.broadcasted_iota(jnp.int32, (1, SUBLANES, D_B), 1)
    cw = convw_ref[...]
    if sample:
        cpad3 = cpad_ref[...].reshape(groups, SUBLANES, D_B)
    else:
        prev8 = prev_ref[...]
    xc3 = convb_ref[...][None]
    for k in range(CONV_W):
        s = CONV_W - 1 - k
        if s == 0:
            shifted = xr3
        else:
            rot = pltpu.roll(xr3, s, 1)
            if sample:
                rot_prev = pltpu.roll(cpad3, s, 1)
            else:
                first = pltpu.roll(prev8, s, 0)[None]
                rot_prev = jnp.concatenate([first, rot[:-1]], axis=0) if groups > 1 else first
            shifted = jnp.where(t3 >= s, rot, rot_prev)
        xc3 = xc3 + shifted * cw[k:k + 1, :][None]
    xc = xc3.reshape(rows, D_B)

    xcb = xc.astype(BF16)
    gq = [_dot(xcb[:, q * MXU_DIM:(q + 1) * MXU_DIM], wg_ref[q]) for q in range(D_B // MXU_DIM)]
    bg = bg_ref[...]
    pre_a = jnp.concatenate([g[:, :MXU_DIM] for g in gq], axis=1) + bg[:, :D_B]
    pre_x = jnp.concatenate([g[:, MXU_DIM:] for g in gq], axis=1) + bg[:, D_B:]
    gate_a = jax.nn.sigmoid(pre_a)
    gate_x = jax.nn.sigmoid(pre_x)
    nl = -lam_ref[...]
    softplus = jnp.maximum(nl, 0.0) + jnp.log1p(jnp.exp(-jnp.abs(nl)))
    log_a = -LRU_C * gate_a * softplus
    a = jnp.exp(log_a)
    mult = jnp.sqrt(-jnp.tanh(log_a) * (1.0 + a * a))
    bx = mult * (gate_x * xc)

    a3 = a.reshape(groups, SUBLANES, D_B)
    b3 = bx.reshape(groups, SUBLANES, D_B)
    for d in (1, 2, 4):
        m = t3 >= d
        a_sh = jnp.where(m, pltpu.roll(a3, d, 1), 1.0)
        b_sh = jnp.where(m, pltpu.roll(b3, d, 1), 0.0)
        b3 = a3 * b_sh + b3
        a3 = a3 * a_sh
    if sample:
        hs3 = jnp.stack([a3[g] * h0_ref[g:g + 1, :] + b3[g] for g in range(groups)], axis=0)
    else:
        h_prev = hcar_ref[...][SUBLANES - 1:SUBLANES, :]
        hs_list = []
        for g in range(groups):
            hg = a3[g] * h_prev + b3[g]
            hs_list.append(hg)
            h_prev = hg[SUBLANES - 1:SUBLANES, :]
        hs3 = jnp.stack(hs_list, axis=0)
    hs = hs3.reshape(rows, D_B)

    out_b = hs * _gelu_tanh(yr)
    mix = _dot(out_a.astype(BF16), wout_ref[:D_A, :]) + _dot(out_b.astype(BF16), wout_ref[D_A:, :])
    y1 = modulated(lambda b, xv, mv: ALPHA * xv + (1.0 + g1(b)) * mv, x, mix)
    x1 = _layer_norm(y1, ln1g_ref[...], ln1b_ref[...])
    h2 = modulated(lambda b, xv: xv * (1.0 + sc2(b)) + sh2(b), x1)

    h2_hi = h2.astype(BF16)
    h2_lo = (h2 - h2_hi.astype(F32)).astype(BF16)
    logits = (_dot_nt(wrh_ref[...], h2_hi) + _dot_nt(wrh_ref[...], h2_lo)
              + _dot_nt(wrl_ref[...], h2_hi) + br_ref[...])

    x1_ref[...] = x1
    _store_token_tiles(h2_ref, h2)
    lg_ref[...] = logits
    if sample:
        xr_ref[...] = xr
        v_ref[...] = v
        hs_ref[...] = hs
    else:
        last_x = xr3[groups - 1]
        last_h = hs3[groups - 1]
        prev_ref[...] = last_x
        hcar_ref[...] = last_h
        cst_ref[0] = last_x
        hst_ref[0] = last_h


def _mixer_weights(w_in, wsm, bsf, lnv_g, lnv_b, conv_w, conv_b, wg, bg, lam, w_out, ln1_g, ln1_b,
                   wr_hi, wr_lo, br):
    arrs = [w_in, wsm, bsf, lnv_g, lnv_b, conv_w, conv_b, wg, bg, lam, w_out, ln1_g, ln1_b,
            wr_hi, wr_lo, br]
    return arrs, [_const_spec(a.shape) for a in arrs]


def _mixer_prompt(x, mod3, weights):
    batch, seq, _ = x.shape
    n_tokens = batch * seq
    tiles = seq // ROWS
    arrs, specs = weights
    row_spec = lambda w: pl.BlockSpec((ROWS, w), lambda b, j: (b * tiles + j, 0))
    st_spec = pl.BlockSpec((1, SUBLANES, D_B), lambda b, j: (b, 0, 0))
    return pl.pallas_call(
        functools.partial(_mixer_kernel, False),
        grid=(batch, tiles),
        in_specs=[pl.BlockSpec((1, ROWS, D_MODEL), lambda b, j: (b, j, 0)),
                  pl.BlockSpec((1, 1, 6 * D_MODEL), lambda b, j: (b, 0, 0))] + specs,
        out_specs=[row_spec(D_MODEL),
                   pl.BlockSpec((ROWS * TILE_ROWS, LANES), lambda b, j: (b * tiles + j, 0)),
                   pl.BlockSpec((LOGIT_ROWS, ROWS), lambda b, j: (0, b * tiles + j)), st_spec, st_spec],
        out_shape=[jax.ShapeDtypeStruct((n_tokens, D_MODEL), F32),
                   jax.ShapeDtypeStruct((n_tokens * TILE_ROWS, LANES), F32),
                   jax.ShapeDtypeStruct((LOGIT_ROWS, n_tokens), F32),
                   jax.ShapeDtypeStruct((batch, SUBLANES, D_B), F32),
                   jax.ShapeDtypeStruct((batch, SUBLANES, D_B), F32)],
        scratch_shapes=[pltpu.VMEM((SUBLANES, D_B), F32), pltpu.VMEM((SUBLANES, D_B), F32)],
        compiler_params=pltpu.CompilerParams(
            dimension_semantics=("arbitrary", "arbitrary"), vmem_limit_bytes=VMEM_LIMIT),
        name="mixer_prompt",
    )(x, mod3, *arrs)


def _mixer_sample(x2, mod_seq, weights, h0, cpad):
    n_rows = x2.shape[0]
    arrs, specs = weights
    row = lambda w: pl.BlockSpec((ROWS, w), lambda i: (i, 0))
    seq_row = lambda w: pl.BlockSpec((ROWS // SUBLANES, w), lambda i: (i, 0))
    return pl.pallas_call(
        functools.partial(_mixer_kernel, True),
        grid=(n_rows // ROWS,),
        in_specs=[row(D_MODEL), seq_row(6 * D_MODEL)] + specs + [seq_row(D_B), row(D_B)],
        out_specs=[row(D_MODEL), pl.BlockSpec((ROWS * TILE_ROWS, LANES), lambda i: (i, 0)),
                   pl.BlockSpec((LOGIT_ROWS, ROWS), lambda i: (0, i)),
                   row(D_B), row(D_B), row(D_B)],
        out_shape=[jax.ShapeDtypeStruct((n_rows, D_MODEL), F32),
                   jax.ShapeDtypeStruct((n_rows * TILE_ROWS, LANES), F32),
                   jax.ShapeDtypeStruct((LOGIT_ROWS, n_rows), F32),
                   jax.ShapeDtypeStruct((n_rows, D_B), F32),
                   jax.ShapeDtypeStruct((n_rows, D_B), F32),
                   jax.ShapeDtypeStruct((n_rows, D_B), F32)],
        compiler_params=pltpu.CompilerParams(
            dimension_semantics=("arbitrary",), vmem_limit_bytes=VMEM_LIMIT),
        name="mixer_sample",
    )(x2, mod_seq, *arrs, h0, cpad)


def _router_kernel(lg_ref, upper_ref, lower_ref, mi_ref, mf_ref, cnt_ref, tot_ref, run_ref):
    phase = pl.program_id(0)
    step = pl.program_id(1)
    l = lg_ref[...]
    cols = l.shape[1]
    row8 = lax.broadcasted_iota(jnp.int32, (SUBLANES, cols), 0)
    neg = -jnp.inf
    big = jnp.int32(SUBLANES)

    def first_max(vals):
        top = jnp.max(vals, axis=0, keepdims=True)
        return top, jnp.min(jnp.where(vals == top, row8, big), axis=0, keepdims=True)

    is_group = row8 < N_GROUPS
    lg = l[:SUBLANES]
    m, gsel = first_max(jnp.where(is_group, lg, neg))
    denom = jnp.sum(jnp.where(is_group, jnp.exp(lg - m), 0.0), axis=0, keepdims=True)
    pgsel = 1.0 / denom
    le = l[SUBLANES:2 * SUBLANES]
    for g in range(1, N_GROUPS):
        le = jnp.where(gsel == g, l[(g + 1) * SUBLANES:(g + 2) * SUBLANES], le)
    v1, i1 = first_max(le)
    v2, i2 = first_max(jnp.where(row8 == i1, neg, le))
    ex = jnp.exp(v2 - v1)
    wk1 = pgsel / (1.0 + ex)
    wk2 = pgsel * ex / (1.0 + ex)
    e1 = gsel * EXPERTS_PER_GROUP + i1
    e2 = gsel * EXPERTS_PER_GROUP + i2
    row_e = lax.broadcasted_iota(jnp.int32, (N_EXPERTS, cols), 0)
    oh1 = jnp.where(row_e == e1, 1.0, 0.0)
    oh2 = jnp.where(row_e == e2, 1.0, 0.0)
    oh = oh1 + oh2

    @pl.when(jnp.logical_and(phase == 0, step == 0))
    def _():
        tot_ref[...] = jnp.zeros_like(tot_ref)

    @pl.when(phase == 0)
    def _():
        tot_ref[...] = tot_ref[...] + jnp.sum(oh, axis=1, keepdims=True)

    @pl.when(jnp.logical_and(phase == 1, step == 0))
    def _():
        tot = tot_ref[...]
        hi = (tot * (1.0 / MXU_DIM)).astype(jnp.int32).astype(F32)
        lo = tot - hi * MXU_DIM
        run_ref[...] = (_dot(lower_ref[...], hi.astype(BF16)) * MXU_DIM
                        + _dot(lower_ref[...], lo.astype(BF16)))

    @pl.when(phase == 1)
    def _():
        blocks = cols // LANES
        stacked = jnp.concatenate([oh[:, b * LANES:(b + 1) * LANES] for b in range(blocks)], axis=0)
        local = _dot(stacked.astype(BF16), upper_ref[...])
        run = run_ref[...]
        p1, p2 = [], []
        for b in range(blocks):
            sl = slice(b * LANES, (b + 1) * LANES)
            before = local[b * N_EXPERTS:(b + 1) * N_EXPERTS] + run
            p1.append(jnp.sum(oh1[:, sl] * before, axis=0, keepdims=True))
            p2.append(jnp.sum(oh2[:, sl] * before, axis=0, keepdims=True))
            run = run + jnp.sum(oh[:, sl], axis=1, keepdims=True)
        run_ref[...] = run
        pos1 = jnp.concatenate(p1, axis=1).astype(jnp.int32)
        pos2 = jnp.concatenate(p2, axis=1).astype(jnp.int32)
        zero_i = jnp.zeros_like(row8)
        mi_ref[...] = jnp.where(row8 == 0, pos1, jnp.where(row8 == 1, pos2, jnp.where(
            row8 == 2, e1, jnp.where(row8 == 3, e2, zero_i))))
        mf_ref[...] = jnp.where(row8 == 0, wk1, jnp.where(row8 == 1, wk2, 0.0))
        cnt_ref[...] = tot_ref[...].astype(jnp.int32)


def _router_call(lg_t):
    n_tokens = lg_t.shape[1]
    tc = ROUTER_COLS
    r = lax.broadcasted_iota(jnp.int32, (LANES, LANES), 0)
    c = lax.broadcasted_iota(jnp.int32, (LANES, LANES), 1)
    upper = jnp.where(r < c, 1.0, 0.0).astype(BF16)
    lower = jnp.where(c < r, 1.0, 0.0)[:N_EXPERTS, :N_EXPERTS].astype(BF16)
    col = pl.BlockSpec((SUBLANES, tc), lambda p, i: (0, i * p))
    return pl.pallas_call(
        _router_kernel,
        grid=(2, n_tokens // tc),
        in_specs=[pl.BlockSpec((LOGIT_ROWS, tc), lambda p, i: (0, i)),
                  _const_spec((LANES, LANES)), _const_spec((N_EXPERTS, N_EXPERTS))],
        out_specs=[col, col, _const_spec((N_EXPERTS, LANES))],
        out_shape=[jax.ShapeDtypeStruct((SUBLANES, n_tokens), jnp.int32),
                   jax.ShapeDtypeStruct((SUBLANES, n_tokens), F32),
                   jax.ShapeDtypeStruct((N_EXPERTS, LANES), jnp.int32)],
        scratch_shapes=[pltpu.VMEM((N_EXPERTS, LANES), F32), pltpu.VMEM((N_EXPERTS, LANES), F32)],
        compiler_params=pltpu.CompilerParams(
            dimension_semantics=("arbitrary", "arbitrary"), vmem_limit_bytes=VMEM_LIMIT),
        name="router",
    )(lg_t, upper, lower)


def _ffn_plan(counts, n_pairs):
    ends = jnp.cumsum(counts)
    off = ends - counts
    ids = jnp.arange(N_EXPERTS, dtype=jnp.int32)

    n_tiles = n_pairs // FFN_ROWS
    n_items = n_tiles + N_EXPERTS
    first_tile = off // FFN_ROWS
    last_tile = (ends - 1) // FFN_ROWS
    items_e = jnp.where(counts > 0, last_tile - first_tile + 1, 0)
    items_end = jnp.cumsum(items_e)
    items_start = items_end - items_e
    n_work = items_end[-1]
    w = jnp.arange(n_items, dtype=jnp.int32)
    wv = jnp.minimum(w, n_work - 1)
    e_w = jnp.sum((items_end[None, :] <= wv[:, None]).astype(jnp.int32), axis=1)
    e_w = jnp.minimum(e_w, N_EXPERTS - 1)
    sel = e_w[:, None] == ids[None, :]
    pick = lambda a: jnp.sum(jnp.where(sel, a[None, :], 0), axis=1)
    t_w = pick(first_tile) + (wv - pick(items_start))
    lo_w = jnp.maximum(pick(off), t_w * FFN_ROWS) - t_w * FFN_ROWS
    hi_w = jnp.minimum(pick(ends), (t_w + 1) * FFN_ROWS) - t_w * FFN_ROWS
    valid = w < n_work
    lo_w = jnp.where(valid, lo_w, 0)
    hi_w = jnp.where(valid, hi_w, 0)
    used = counts > 0
    order_e = jnp.cumsum(used.astype(jnp.int32)) - 1
    later_used = jnp.logical_and(used[None, :], ids[None, :] > ids[:, None])
    next_e = jnp.min(jnp.where(later_used, ids[None, :], N_EXPERTS), axis=1)
    next_e = jnp.where(next_e == N_EXPERTS, -1, next_e)
    i32 = lambda a: a.astype(jnp.int32)
    return i32(t_w), i32(e_w), i32(lo_w), i32(hi_w), i32(pick(order_e)), i32(pick(next_e))


def _scatter_kernel(tiles_p, pos1_s, pos2_s, h2p_ref, h2s_ref, xs_hbm, sem):
    i = pl.program_id(0)
    base = i * DISPATCH_ROWS

    def scatter_rows(src_ref):
        def row_copy(r, pos_s):
            p = pos_s[base + r]
            return pltpu.make_async_copy(src_ref.at[_token_tile(r), :], xs_hbm.at[_token_tile(p), :],
                                         sem)

        def start(r, carry):
            row_copy(r, pos1_s).start(priority=0)
            row_copy(r, pos2_s).start(priority=1)
            return carry

        def wait(r, carry):
            row_copy(r, pos1_s).wait()
            row_copy(r, pos2_s).wait()
            return carry

        lax.fori_loop(0, DISPATCH_ROWS, start, 0, unroll=32)
        lax.fori_loop(0, DISPATCH_ROWS, wait, 0, unroll=8)

    @pl.when(i < tiles_p)
    def _():
        scatter_rows(h2p_ref)

    @pl.when(i >= tiles_p)
    def _():
        scatter_rows(h2s_ref)


def _scatter_call(pos1, pos2, h2_p, h2_s):
    tm = DISPATCH_ROWS
    tiles_p = h2_p.shape[0] // (tm * TILE_ROWS)
    tiles_s = h2_s.shape[0] // (tm * TILE_ROWS)
    n_pairs = TOP_K * (h2_p.shape[0] + h2_s.shape[0]) // TILE_ROWS
    blk = (tm * TILE_ROWS, LANES)
    return pl.pallas_call(
        functools.partial(_scatter_kernel, tiles_p),
        grid_spec=pltpu.PrefetchScalarGridSpec(
            num_scalar_prefetch=2,
            grid=(tiles_p + tiles_s,),
            in_specs=[pl.BlockSpec(blk, lambda i, *_: (jnp.minimum(i, tiles_p - 1), 0)),
                      pl.BlockSpec(blk, lambda i, *_: (jnp.maximum(i - tiles_p, 0), 0))],
            out_specs=pl.BlockSpec(memory_space=pl.ANY),
            scratch_shapes=[pltpu.SemaphoreType.DMA(())],
        ),
        out_shape=jax.ShapeDtypeStruct((n_pairs * TILE_ROWS, LANES), F32),
        compiler_params=pltpu.CompilerParams(
            dimension_semantics=("arbitrary",), vmem_limit_bytes=VMEM_LIMIT),
        name="dispatch_scatter",
    )(pos1, pos2, h2_p, h2_s)


def _ffn_kernel(wt_s, we_s, wlo_s, whi_s, word_s, wnext_s, x_ref, w1_hbm, w3_hbm, w2_hbm, o_ref,
                w1f_ref, w3f_ref, w2f_ref, w13b_ref, w2b_ref, sems):
    del wt_s
    w = pl.program_id(0)
    e = we_s[w]
    e_prev = we_s[jnp.maximum(w - 1, 0)]
    slot = word_s[w] % 2

    def weight_copies(expert, buf):
        return [pltpu.make_async_copy(src.at[expert], dst.at[buf], sems.at[buf])
                for src, dst in ((w1_hbm, w1f_ref), (w3_hbm, w3f_ref), (w2_hbm, w2f_ref))]

    @pl.when(w == 0)
    def _():
        for cp in weight_copies(e, slot):
            cp.start()

    @pl.when(jnp.logical_or(w == 0, e != e_prev))
    def _():
        for cp in weight_copies(e, slot):
            cp.wait()
        w13b_ref[:, :D_EXPERT] = w1f_ref[slot].astype(BF16)
        w13b_ref[:, D_EXPERT:] = w3f_ref[slot].astype(BF16)
        w2b_ref[...] = w2f_ref[slot].astype(BF16)
        nxt = wnext_s[w]

        @pl.when(nxt >= 0)
        def _():
            for cp in weight_copies(nxt, 1 - slot):
                cp.start()

    lo = wlo_s[w]
    hi = whi_s[w]

    @pl.when(hi > lo)
    def _():
        x = _load_token_tiles(x_ref, (), FFN_ROWS)
        h13 = _dot(x.astype(BF16), w13b_ref[...])
        a = h13[:, :D_EXPERT]
        b = h13[:, D_EXPERT:]
        hdn = (a * jax.nn.sigmoid(a)) * b
        y = _dot(hdn.astype(BF16), w2b_ref[...])
        rows = lax.broadcasted_iota(jnp.int32, (FFN_ROWS, LANES), 0)
        mine = jnp.logical_and(rows >= lo, rows < hi)

        @pl.when(lo == 0)
        def _():
            _store_token_tiles(o_ref, y)

        @pl.when(lo > 0)
        def _():
            _store_token_tiles(o_ref, y, keep=mine)


def _ffn_call(plan, xs, w1, w3, w2):
    t_w = plan[0]
    n_pairs = xs.shape[0] // TILE_ROWS
    tm = FFN_ROWS
    any_spec = pl.BlockSpec(memory_space=pl.ANY)
    row = pl.BlockSpec((tm * TILE_ROWS, LANES), lambda i, wt, *_: (wt[i], 0))
    return pl.pallas_call(
        _ffn_kernel,
        grid_spec=pltpu.PrefetchScalarGridSpec(
            num_scalar_prefetch=len(plan),
            grid=(t_w.shape[0],),
            in_specs=[row, any_spec, any_spec, any_spec],
            out_specs=row,
            scratch_shapes=[pltpu.VMEM((2, D_MODEL, D_EXPERT), F32),
                            pltpu.VMEM((2, D_MODEL, D_EXPERT), F32),
                            pltpu.VMEM((2, D_EXPERT, D_MODEL), F32),
                            pltpu.VMEM((D_MODEL, 2 * D_EXPERT), BF16),
                            pltpu.VMEM((D_EXPERT, D_MODEL), BF16),
                            pltpu.SemaphoreType.DMA((2,))],
        ),
        out_shape=jax.ShapeDtypeStruct((n_pairs * TILE_ROWS, LANES), F32),
        compiler_params=pltpu.CompilerParams(
            dimension_semantics=("arbitrary",), vmem_limit_bytes=VMEM_LIMIT),
        name="expert_ffn",
    )(*plan, xs, w1, w3, w2)


def _combine_kernel(three_d, row0, n_steps, pos1_s, pos2_s, x1_ref, wk_ref, mod_ref, g_ref, b_ref,
                    ys_hbm, o_ref, ybuf, sems):
    step = pl.program_id(0)
    slot = step % 2

    def gather(tile, tile_slot):
        base = row0 + tile * COMBINE_ROWS

        def row_copy(r, k, pos_s):
            p = pos_s[base + r]
            return pltpu.make_async_copy(ys_hbm.at[_token_tile(p), :],
                                         ybuf.at[tile_slot, k, _token_tile(r), :], sems.at[tile_slot])

        def start(r, carry):
            row_copy(r, 0, pos1_s).start(priority=0)
            row_copy(r, 1, pos2_s).start(priority=1)
            return carry

        def wait(r, carry):
            row_copy(r, 0, pos1_s).wait()
            row_copy(r, 1, pos2_s).wait()
            return carry

        return start, wait

    @pl.when(step == 0)
    def _():
        lax.fori_loop(0, COMBINE_ROWS, gather(step, slot)[0], 0, unroll=32)

    @pl.when(step + 1 < n_steps)
    def _():
        lax.fori_loop(0, COMBINE_ROWS, gather(step + 1, 1 - slot)[0], 0, unroll=32)

    lax.fori_loop(0, COMBINE_ROWS, gather(step, slot)[1], 0, unroll=8)

    wk = wk_ref[...]
    r = lax.broadcasted_iota(jnp.int32, (COMBINE_ROWS, COMBINE_ROWS), 0)
    c = lax.broadcasted_iota(jnp.int32, (COMBINE_ROWS, COMBINE_ROWS), 1)
    wk_col = [jnp.sum(jnp.where(r == c, wk[k:k + 1, :], 0.0), axis=1, keepdims=True)
              for k in range(TOP_K)]
    moe = (wk_col[0] * _load_token_tiles(ybuf, (slot, 0), COMBINE_ROWS)
           + wk_col[1] * _load_token_tiles(ybuf, (slot, 1), COMBINE_ROWS))
    x1 = x1_ref[...]
    if three_d:
        g2 = mod_ref[0, :, 5 * D_MODEL:]
        o_ref[0] = _layer_norm(ALPHA * x1 + (1.0 + g2) * moe, g_ref[...], b_ref[...])
    else:
        def seq_rows(v, b):
            return v[b * SUBLANES:(b + 1) * SUBLANES]

        resid = jnp.concatenate([
            ALPHA * seq_rows(x1, b) + (1.0 + mod_ref[b:b + 1, 5 * D_MODEL:]) * seq_rows(moe, b)
            for b in range(COMBINE_ROWS // SUBLANES)], axis=0)
        o_ref[...] = _layer_norm(resid, g_ref[...], b_ref[...])


def _combine_call(three_d, pos1, pos2, x1_all, mf_all, mod, ln2_g, ln2_b, ys, row0, batch, seq):
    tm = COMBINE_ROWS
    blk0 = row0 // tm
    if three_d:
        tiles = seq // tm
        grid = (batch * tiles,)
        mod_spec = pl.BlockSpec((1, 1, 6 * D_MODEL), lambda i, *_: (i // tiles, 0, 0))
        out_spec = pl.BlockSpec((1, tm, D_MODEL), lambda i, *_: (i // tiles, i % tiles, 0))
        out_shape = jax.ShapeDtypeStruct((batch, seq, D_MODEL), F32)
    else:
        grid = (batch * seq // tm,)
        mod_spec = pl.BlockSpec((tm // SUBLANES, 6 * D_MODEL), lambda i, *_: (i, 0))
        out_spec = pl.BlockSpec((tm, D_MODEL), lambda i, *_: (i, 0))
        out_shape = jax.ShapeDtypeStruct((batch * seq, D_MODEL), F32)
    return pl.pallas_call(
        functools.partial(_combine_kernel, three_d, row0, grid[0]),
        grid_spec=pltpu.PrefetchScalarGridSpec(
            num_scalar_prefetch=2,
            grid=grid,
            in_specs=[pl.BlockSpec((tm, D_MODEL), lambda i, *_: (i, 0)),
                      pl.BlockSpec((SUBLANES, tm), lambda i, *_: (0, blk0 + i)),
                      mod_spec,
                      pl.BlockSpec(ln2_g.shape, lambda i, *_: (0, 0)),
                      pl.BlockSpec(ln2_b.shape, lambda i, *_: (0, 0)),
                      pl.BlockSpec(memory_space=pl.ANY)],
            out_specs=out_spec,
            scratch_shapes=[pltpu.VMEM((2, TOP_K, tm * TILE_ROWS, LANES), F32),
                            pltpu.SemaphoreType.DMA((2,))],
        ),
        out_shape=out_shape,
        compiler_params=pltpu.CompilerParams(
            dimension_semantics=("arbitrary",), vmem_limit_bytes=VMEM_LIMIT),
        name="combine_prompt" if three_d else "combine_sample",
    )(pos1, pos2, x1_all, mf_all, mod, ln2_g, ln2_b, ys)


def _block_diag(w):
    heads, n, _ = w.shape
    eye = jnp.eye(heads, dtype=w.dtype)
    return (eye[:, None, :, None] * w[:, :, None, :]).reshape(heads * n, heads * n)


def kernel(x_prompt, x_sample, state_rglru_h, state_conv, c_prompt, c_sample, w_ada, b_ada, w_in, w_s, b_s, lnv_g, lnv_b, conv_w, conv_b, lru_wa, lru_ba, lru_wx, lru_bx, lru_lam, w_out, ln1_g, ln1_b, w_rg, b_rg, w_re, b_re, w1, w3, w2, ln2_g, ln2_b):
    batch, seq, _ = x_prompt.shape
    dec_batch, dec_seq, _ = x_sample.shape
    assert dec_seq == SUBLANES and seq % ROWS == 0 and (dec_batch * dec_seq) % ROWS == 0
    n_prompt = batch * seq
    n_sample = dec_batch * dec_seq
    n_tokens = n_prompt + n_sample
    n_pairs = TOP_K * n_tokens
    assert n_tokens % ROUTER_COLS == 0 and n_pairs % FFN_ROWS == 0
    l = 0

    mod_p, mod_s = _ada_call(c_prompt, c_sample, w_ada[l], b_ada[l][None])
    mod_p3 = mod_p[:, None, :]

    tri = jnp.tril(jnp.ones((CHUNK, CHUNK), dtype=bool))
    ws_tri = jnp.where(tri[None], w_s[l], 0.0)
    reps_p = MXU_DIM // CHUNK
    eye_p = jnp.eye(reps_p, dtype=F32)
    wsm_p = (eye_p[None, :, None, :, None] * ws_tri[:, None, :, None, :]).reshape(
        A_HEADS, MXU_DIM, MXU_DIM).astype(BF16)
    reps_s = MXU_DIM // dec_seq
    pos_in_seq = jnp.arange(MXU_DIM, dtype=jnp.int32) % dec_seq
    seq_of_row = jnp.arange(MXU_DIM, dtype=jnp.int32) // dec_seq
    spread = (pos_in_seq[:, None] == jnp.arange(dec_seq, dtype=jnp.int32)[None, :]).astype(F32)
    same_seq = seq_of_row[:, None] == seq_of_row[None, :]
    wsm_s = jnp.einsum("ri,hij,cj->hrc", spread, ws_tri[:, :dec_seq, :dec_seq], spread,
                       precision=lax.Precision.HIGHEST)
    wsm_s = jnp.where(same_seq[None], wsm_s, 0.0).astype(BF16)
    bs_pos = jnp.repeat(jnp.transpose(b_s[l]), A_HEAD_DIM, axis=1)
    bsf_p = jnp.tile(bs_pos, (reps_p, 1))
    bsf_s = jnp.tile(bs_pos[:dec_seq], (reps_s, 1))

    wa_bd = _block_diag(lru_wa[l])
    wx_bd = _block_diag(lru_wx[l])
    wg = jnp.stack([
        jnp.concatenate([wa_bd[q * MXU_DIM:(q + 1) * MXU_DIM, q * MXU_DIM:(q + 1) * MXU_DIM],
                         wx_bd[q * MXU_DIM:(q + 1) * MXU_DIM, q * MXU_DIM:(q + 1) * MXU_DIM]], axis=1)
        for q in range(D_B // MXU_DIM)]).astype(BF16)
    bg = jnp.concatenate([lru_ba[l].reshape(1, D_B), lru_bx[l].reshape(1, D_B)], axis=1)

    group_pad = jnp.zeros((SUBLANES - N_GROUPS, D_MODEL), F32)
    wr = jnp.concatenate([jnp.transpose(w_rg[l]), group_pad, jnp.transpose(w_re[l])], axis=0)
    wr_hi = wr.astype(BF16)
    wr_lo = (wr - wr_hi.astype(F32)).astype(BF16)
    br = jnp.concatenate([b_rg[l], jnp.zeros((SUBLANES - N_GROUPS,), F32), b_re[l]])[:, None]

    common = dict(lnv_g=lnv_g[l][None], lnv_b=lnv_b[l][None], conv_w=conv_w[l], conv_b=conv_b[l][None],
                  wg=wg, bg=bg, lam=lru_lam[l][None], w_out=w_out[l].astype(BF16),
                  ln1_g=ln1_g[l][None], ln1_b=ln1_b[l][None], wr_hi=wr_hi, wr_lo=wr_lo, br=br)
    w_in_b = w_in[l].astype(BF16)
    weights_p = _mixer_weights(w_in_b, wsm_p, bsf_p, **common)
    weights_s = _mixer_weights(w_in_b, wsm_s, bsf_s, **common)

    x1_p, h2_p, lg_p, cst_p, hst_p = _mixer_prompt(x_prompt, mod_p3, weights_p)

    cpad = jnp.concatenate(
        [jnp.zeros((dec_batch, SUBLANES - (CONV_W - 1), D_B), F32), state_conv[l]], axis=1
    ).reshape(n_sample, D_B)
    x1_s, h2_s, lg_s, xr_s, v_s, hs_s = _mixer_sample(
        x_sample.reshape(n_sample, D_MODEL), mod_s, weights_s, state_rglru_h[l], cpad)

    meta_i, meta_f, counts = _router_call(jnp.concatenate([lg_p, lg_s], axis=1))
    pos1, pos2 = meta_i[0], meta_i[1]
    xs = _scatter_call(pos1, pos2, h2_p, h2_s)
    ys = _ffn_call(_ffn_plan(counts[:, 0], n_pairs), xs, w1[l], w3[l], w2[l])

    ln2g, ln2b = ln2_g[l][None], ln2_b[l][None]
    y_prompt = _combine_call(True, pos1, pos2, x1_p, meta_f, mod_p3, ln2g, ln2b, ys,
                             0, batch, seq)
    y_sample = _combine_call(False, pos1, pos2, x1_s, meta_f, mod_s, ln2g, ln2b, ys,
                             n_prompt, dec_batch, dec_seq).reshape(dec_batch, dec_seq, D_MODEL)

    new_h_prompt = hst_p[:, SUBLANES - 1, :][None]
    new_conv_prompt = cst_p[:, SUBLANES - (CONV_W - 1):, :][None]
    xr_s3 = xr_s.reshape(dec_batch, dec_seq, D_B)
    new_h_sample = hs_s.reshape(dec_batch, dec_seq, D_B)[:, dec_seq - 1, :][None]
    new_conv_sample = xr_s3[:, dec_seq - (CONV_W - 1):, :][None]
    new_chunk_v_sample = v_s.reshape(dec_batch, dec_seq, D_A)[None]
    return (y_prompt, y_sample, new_h_prompt, new_conv_prompt, new_h_sample, new_conv_sample,
            new_chunk_v_sample)
```

```python
import functools
import math

import jax
import jax.numpy as jnp
from jax import lax
from jax.experimental import pallas as pl
from jax.experimental.pallas import tpu as pltpu

D_MODEL = 1024
D_A = 512
A_HEADS = 4
A_HEAD_DIM = 128
CHUNK = 128
D_B = 512
B_HEADS = 8
B_HEAD_DIM = 64
CONV_W = 4
LRU_C = 8.0
D_IN = 2 * D_A + 2 * D_B
N_GROUPS = 4
EXPERTS_PER_GROUP = 8
N_EXPERTS = 32
TOP_K = 2
D_EXPERT = 512
LN_EPS = 1e-5
DEPTH = 1
ALPHA = (2.0 * DEPTH) ** 0.25

LANES = 128
SUBLANES = 8
MXU_DIM = 256
VMEM_LIMIT = 56 * 1024 * 1024

ROWS = 512
ROUTER_COLS = 1024
LOGIT_ROWS = SUBLANES + N_EXPERTS
DISPATCH_ROWS = 1024
FFN_ROWS = 512
COMBINE_ROWS = 256
BF16 = jnp.bfloat16
F32 = jnp.float32


def _dot(a, b):
    return jnp.dot(a, b, preferred_element_type=F32)


def _dot_nt(a, b):
    return lax.dot_general(a, b, (((1,), (1,)), ((), ())), preferred_element_type=F32)


TILE_ROWS = D_MODEL // LANES


def _token_tile(r):
    return pl.ds(pl.multiple_of(r * TILE_ROWS, TILE_ROWS), TILE_ROWS)


def _load_token_tiles(ref, lead, n_rows):
    parts = [ref[lead + (pl.ds(s, n_rows, stride=TILE_ROWS), slice(None))] for s in range(TILE_ROWS)]
    return jnp.concatenate(parts, axis=1)


def _store_token_tiles(ref, val, keep=None):
    n_rows = val.shape[0]
    for s in range(TILE_ROWS):
        rows = pl.ds(s, n_rows, stride=TILE_ROWS)
        new = val[:, s * LANES:(s + 1) * LANES]
        if keep is not None:
            new = jnp.where(keep, new, ref[rows, :])
        ref[rows, :] = new


def _layer_norm(x, g, b):
    mu = jnp.mean(x, axis=-1, keepdims=True)
    xc = x - mu
    var = jnp.mean(xc * xc, axis=-1, keepdims=True)
    return xc * lax.rsqrt(var + LN_EPS) * g + b


def _gelu_tanh(x):
    c = math.sqrt(2.0 / math.pi)
    return x * (0.5 * (1.0 + jnp.tanh(c * (x + 0.044715 * (x * x * x)))))


def _const_spec(shape):
    nd = len(shape)
    return pl.BlockSpec(shape, lambda *_: (0,) * nd)


def _ada_kernel(cp_ref, cs_ref, w_ref, b_ref, op_ref, os_ref):
    w = w_ref[...].astype(BF16)
    for c_ref, o_ref in ((cp_ref, op_ref), (cs_ref, os_ref)):
        c = c_ref[...]
        a = (c * jax.nn.sigmoid(c)).astype(BF16)
        o_ref[...] = _dot(a, w) + b_ref[...]


def _ada_call(c_prompt, c_sample, w_ada, b_ada):
    tn = 1536
    rows_p, rows_s = c_prompt.shape[0], c_sample.shape[0]
    return pl.pallas_call(
        _ada_kernel,
        grid=(6 * D_MODEL // tn,),
        in_specs=[
            pl.BlockSpec((rows_p, D_MODEL), lambda j: (0, 0)),
            pl.BlockSpec((rows_s, D_MODEL), lambda j: (0, 0)),
            pl.BlockSpec((D_MODEL, tn), lambda j: (0, j)),
            pl.BlockSpec((1, tn), lambda j: (0, j)),
        ],
        out_specs=[pl.BlockSpec((rows_p, tn), lambda j: (0, j)),
                   pl.BlockSpec((rows_s, tn), lambda j: (0, j))],
        out_shape=[jax.ShapeDtypeStruct((rows_p, 6 * D_MODEL), F32),
                   jax.ShapeDtypeStruct((rows_s, 6 * D_MODEL), F32)],
        compiler_params=pltpu.CompilerParams(
            dimension_semantics=("arbitrary",), vmem_limit_bytes=VMEM_LIMIT),
        name="ada_mod",
    )(c_prompt, c_sample, w_ada, b_ada)


def _mixer_kernel(sample, *refs):
    if sample:
        (x_ref, mod_ref, win_ref, wsm_ref, bsf_ref, lnvg_ref, lnvb_ref, convw_ref, convb_ref,
         wg_ref, bg_ref, lam_ref, wout_ref, ln1g_ref, ln1b_ref, wrh_ref, wrl_ref, br_ref,
         h0_ref, cpad_ref,
         x1_ref, h2_ref, lg_ref, xr_ref, v_ref, hs_ref) = refs
    else:
        (x_ref, mod_ref, win_ref, wsm_ref, bsf_ref, lnvg_ref, lnvb_ref, convw_ref, convb_ref,
         wg_ref, bg_ref, lam_ref, wout_ref, ln1g_ref, ln1b_ref, wrh_ref, wrl_ref, br_ref,
         x1_ref, h2_ref, lg_ref, cst_ref, hst_ref, prev_ref, hcar_ref) = refs

    rows = ROWS
    groups = rows // SUBLANES

    if sample:
        x = x_ref[...]

        def chunk(c):
            return lambda b: mod_ref[b:b + 1, c * D_MODEL:(c + 1) * D_MODEL]
    else:
        x = x_ref[0]

        def chunk(c):
            return lambda b: mod_ref[0, :, c * D_MODEL:(c + 1) * D_MODEL]

        @pl.when(pl.program_id(1) == 0)
        def _():
            prev_ref[...] = jnp.zeros_like(prev_ref)
            hcar_ref[...] = jnp.zeros_like(hcar_ref)

    def modulated(fn, *vals):
        if not sample:
            return fn(0, *vals)
        parts = [fn(b, *(v[b * SUBLANES:(b + 1) * SUBLANES] for v in vals)) for b in range(groups)]
        return jnp.concatenate(parts, axis=0)

    sh1, sc1, g1, sh2, sc2, g2 = [chunk(c) for c in range(6)]

    h = modulated(lambda b, xv: xv * (1.0 + sc1(b)) + sh1(b), x)
    proj = _dot(h.astype(BF16), win_ref[...])
    u = proj[:, :D_A]
    v = _layer_norm(proj[:, D_A:2 * D_A], lnvg_ref[...], lnvb_ref[...])
    xr = proj[:, 2 * D_A:2 * D_A + D_B]
    yr = proj[:, 2 * D_A + D_B:]

    vb = v.astype(BF16)
    halves = []
    for r in range(rows // MXU_DIM):
        heads = [
            _dot(wsm_ref[hd], vb[r * MXU_DIM:(r + 1) * MXU_DIM, hd * A_HEAD_DIM:(hd + 1) * A_HEAD_DIM])
            for hd in range(A_HEADS)
        ]
        halves.append(jnp.concatenate(heads, axis=1) + bsf_ref[...])
    mixed = halves[0] if len(halves) == 1 else jnp.concatenate(halves, axis=0)
    out_a = u * mixed

    xr3 = xr.reshape(groups, SUBLANES, D_B)
    t3 = lax.broadcasted_iota(jnp.int32, (1, SUBLANES, D_B), 1)
    cw = convw_ref[...]
    if sample:
        cpad3 = cpad_ref[...].reshape(groups, SUBLANES, D_B)
    else:
        prev8 = prev_ref[...]
    xc3 = convb_ref[...][None]
    for k in range(CONV_W):
        s = CONV_W - 1 - k
        if s == 0:
            shifted = xr3
        else:
            rot = pltpu.roll(xr3, s, 1)
            if sample:
                rot_prev = pltpu.roll(cpad3, s, 1)
            else:
                first = pltpu.roll(prev8, s, 0)[None]
                rot_prev = jnp.concatenate([first, rot[:-1]], axis=0) if groups > 1 else first
            shifted = jnp.where(t3 >= s, rot, rot_prev)
        xc3 = xc3 + shifted * cw[k:k + 1, :][None]
    xc = xc3.reshape(rows, D_B)

    xcb = xc.astype(BF16)
    gq = [_dot(xcb[:, q * MXU_DIM:(q + 1) * MXU_DIM], wg_ref[q]) for q in range(D_B // MXU_DIM)]
    bg = bg_ref[...]
    pre_a = jnp.concatenate([g[:, :MXU_DIM] for g in gq], axis=1) + bg[:, :D_B]
    pre_x = jnp.concatenate([g[:, MXU_DIM:] for g in gq], axis=1) + bg[:, D_B:]
    gate_a = jax.nn.sigmoid(pre_a)
    gate_x = jax.nn.sigmoid(pre_x)
    nl = -lam_ref[...]
    softplus = jnp.maximum(nl, 0.0) + jnp.log1p(jnp.exp(-jnp.abs(nl)))
    log_a = -LRU_C * gate_a * softplus
    a = jnp.exp(log_a)
    mult = jnp.sqrt(-jnp.tanh(log_a) * (1.0 + a * a))
    bx = mult * (gate_x * xc)

    a3 = a.reshape(groups, SUBLANES, D_B)
    b3 = bx.reshape(groups, SUBLANES, D_B)
    for d in (1, 2, 4):
        m = t3 >= d
        a_sh = jnp.where(m, pltpu.roll(a3, d, 1), 1.0)
        b_sh = jnp.where(m, pltpu.roll(b3, d, 1), 0.0)
        b3 = a3 * b_sh + b3
        a3 = a3 * a_sh
    if sample:
        hs3 = jnp.stack([a3[g] * h0_ref[g:g + 1, :] + b3[g] for g in range(groups)], axis=0)
    else:
        h_prev = hcar_ref[...][SUBLANES - 1:SUBLANES, :]
        hs_list = []
        for g in range(groups):
            hg = a3[g] * h_prev + b3[g]
            hs_list.append(hg)
            h_prev = hg[SUBLANES - 1:SUBLANES, :]
        hs3 = jnp.stack(hs_list, axis=0)
    hs = hs3.reshape(rows, D_B)

    out_b = hs * _gelu_tanh(yr)
    mix = _dot(out_a.astype(BF16), wout_ref[:D_A, :]) + _dot(out_b.astype(BF16), wout_ref[D_A:, :])
    y1 = modulated(lambda b, xv, mv: ALPHA * xv + (1.0 + g1(b)) * mv, x, mix)
    x1 = _layer_norm(y1, ln1g_ref[...], ln1b_ref[...])
    h2 = modulated(lambda b, xv: xv * (1.0 + sc2(b)) + sh2(b), x1)

    h2_hi = h2.astype(BF16)
    h2_lo = (h2 - h2_hi.astype(F32)).astype(BF16)
    logits = (_dot_nt(wrh_ref[...], h2_hi) + _dot_nt(wrh_ref[...], h2_lo)
              + _dot_nt(wrl_ref[...], h2_hi) + br_ref[...])

    x1_ref[...] = x1
    _store_token_tiles(h2_ref, h2)
    lg_ref[...] = logits
    if sample:
        xr_ref[...] = xr
        v_ref[...] = v
        hs_ref[...] = hs
    else:
        last_x = xr3[groups - 1]
        last_h = hs3[groups - 1]
        prev_ref[...] = last_x
        hcar_ref[...] = last_h
        cst_ref[0] = last_x
        hst_ref[0] = last_h


def _mixer_weights(w_in, wsm, bsf, lnv_g, lnv_b, conv_w, conv_b, wg, bg, lam, w_out, ln1_g, ln1_b,
                   wr_hi, wr_lo, br):
    arrs = [w_in, wsm, bsf, lnv_g, lnv_b, conv_w, conv_b, wg, bg, lam, w_out, ln1_g, ln1_b,
            wr_hi, wr_lo, br]
    return arrs, [_const_spec(a.shape) for a in arrs]


def _mixer_prompt(x, mod3, weights):
    batch, seq, _ = x.shape
    n_tokens = batch * seq
    tiles = seq // ROWS
    arrs, specs = weights
    row_spec = lambda w: pl.BlockSpec((ROWS, w), lambda b, j: (b * tiles + j, 0))
    st_spec = pl.BlockSpec((1, SUBLANES, D_B), lambda b, j: (b, 0, 0))
    return pl.pallas_call(
        functools.partial(_mixer_kernel, False),
        grid=(batch, tiles),
        in_specs=[pl.BlockSpec((1, ROWS, D_MODEL), lambda b, j: (b, j, 0)),
                  pl.BlockSpec((1, 1, 6 * D_MODEL), lambda b, j: (b, 0, 0))] + specs,
        out_specs=[row_spec(D_MODEL),
                   pl.BlockSpec((ROWS * TILE_ROWS, LANES), lambda b, j: (b * tiles + j, 0)),
                   pl.BlockSpec((LOGIT_ROWS, ROWS), lambda b, j: (0, b * tiles + j)), st_spec, st_spec],
        out_shape=[jax.ShapeDtypeStruct((n_tokens, D_MODEL), F32),
                   jax.ShapeDtypeStruct((n_tokens * TILE_ROWS, LANES), F32),
                   jax.ShapeDtypeStruct((LOGIT_ROWS, n_tokens), F32),
                   jax.ShapeDtypeStruct((batch, SUBLANES, D_B), F32),
                   jax.ShapeDtypeStruct((batch, SUBLANES, D_B), F32)],
        scratch_shapes=[pltpu.VMEM((SUBLANES, D_B), F32), pltpu.VMEM((SUBLANES, D_B), F32)],
        compiler_params=pltpu.CompilerParams(
            dimension_semantics=("arbitrary", "arbitrary"), vmem_limit_bytes=VMEM_LIMIT),
        name="mixer_prompt",
    )(x, mod3, *arrs)


def _mixer_sample(x2, mod_seq, weights, h0, cpad):
    n_rows = x2.shape[0]
    arrs, specs = weights
    row = lambda w: pl.BlockSpec((ROWS, w), lambda i: (i, 0))
    seq_row = lambda w: pl.BlockSpec((ROWS // SUBLANES, w), lambda i: (i, 0))
    return pl.pallas_call(
        functools.partial(_mixer_kernel, True),
        grid=(n_rows // ROWS,),
        in_specs=[row(D_MODEL), seq_row(6 * D_MODEL)] + specs + [seq_row(D_B), row(D_B)],
        out_specs=[row(D_MODEL), pl.BlockSpec((ROWS * TILE_ROWS, LANES), lambda i: (i, 0)),
                   pl.BlockSpec((LOGIT_ROWS, ROWS), lambda i: (0, i)),
                   row(D_B), row(D_B), row(D_B)],
        out_shape=[jax.ShapeDtypeStruct((n_rows, D_MODEL), F32),
                   jax.ShapeDtypeStruct((n_rows * TILE_ROWS, LANES), F32),
                   jax.ShapeDtypeStruct((LOGIT_ROWS, n_rows), F32),
                   jax.ShapeDtypeStruct((n_rows, D_B), F32),
                   jax.ShapeDtypeStruct((n_rows, D_B), F32),
                   jax.ShapeDtypeStruct((n_rows, D_B), F32)],
        compiler_params=pltpu.CompilerParams(
            dimension_semantics=("arbitrary",), vmem_limit_bytes=VMEM_LIMIT),
        name="mixer_sample",
    )(x2, mod_seq, *arrs, h0, cpad)


def _router_kernel(lg_ref, upper_ref, lower_ref, mi_ref, mf_ref, cnt_ref, tot_ref, run_ref):
    phase = pl.program_id(0)
    step = pl.program_id(1)
    l = lg_ref[...]
    cols = l.shape[1]
    row8 = lax.broadcasted_iota(jnp.int32, (SUBLANES, cols), 0)
    neg = -jnp.inf
    big = jnp.int32(SUBLANES)

    def first_max(vals):
        top = jnp.max(vals, axis=0, keepdims=True)
        return top, jnp.min(jnp.where(vals == top, row8, big), axis=0, keepdims=True)

    is_group = row8 < N_GROUPS
    lg = l[:SUBLANES]
    m, gsel = first_max(jnp.where(is_group, lg, neg))
    denom = jnp.sum(jnp.where(is_group, jnp.exp(lg - m), 0.0), axis=0, keepdims=True)
    pgsel = 1.0 / denom
    le = l[SUBLANES:2 * SUBLANES]
    for g in range(1, N_GROUPS):
        le = jnp.where(gsel == g, l[(g + 1) * SUBLANES:(g + 2) * SUBLANES], le)
    v1, i1 = first_max(le)
    v2, i2 = first_max(jnp.where(row8 == i1, neg, le))
    ex = jnp.exp(v2 - v1)
    wk1 = pgsel / (1.0 + ex)
    wk2 = pgsel * ex / (1.0 + ex)
    e1 = gsel * EXPERTS_PER_GROUP + i1
    e2 = gsel * EXPERTS_PER_GROUP + i2
    row_e = lax.broadcasted_iota(jnp.int32, (N_EXPERTS, cols), 0)
    oh1 = jnp.where(row_e == e1, 1.0, 0.0)
    oh2 = jnp.where(row_e == e2, 1.0, 0.0)
    oh = oh1 + oh2

    @pl.when(jnp.logical_and(phase == 0, step == 0))
    def _():
        tot_ref[...] = jnp.zeros_like(tot_ref)

    @pl.when(phase == 0)
    def _():
        tot_ref[...] = tot_ref[...] + jnp.sum(oh, axis=1, keepdims=True)

    @pl.when(jnp.logical_and(phase == 1, step == 0))
    def _():
        tot = tot_ref[...]
        hi = (tot * (1.0 / MXU_DIM)).astype(jnp.int32).astype(F32)
        lo = tot - hi * MXU_DIM
        run_ref[...] = (_dot(lower_ref[...], hi.astype(BF16)) * MXU_DIM
                        + _dot(lower_ref[...], lo.astype(BF16)))

    @pl.when(phase == 1)
    def _():
        blocks = cols // LANES
        stacked = jnp.concatenate([oh[:, b * LANES:(b + 1) * LANES] for b in range(blocks)], axis=0)
        local = _dot(stacked.astype(BF16), upper_ref[...])
        run = run_ref[...]
        p1, p2 = [], []
        for b in range(blocks):
            sl = slice(b * LANES, (b + 1) * LANES)
            before = local[b * N_EXPERTS:(b + 1) * N_EXPERTS] + run
            p1.append(jnp.sum(oh1[:, sl] * before, axis=0, keepdims=True))
            p2.append(jnp.sum(oh2[:, sl] * before, axis=0, keepdims=True))
            run = run + jnp.sum(oh[:, sl], axis=1, keepdims=True)
        run_ref[...] = run
        pos1 = jnp.concatenate(p1, axis=1).astype(jnp.int32)
        pos2 = jnp.concatenate(p2, axis=1).astype(jnp.int32)
        zero_i = jnp.zeros_like(row8)
        mi_ref[...] = jnp.where(row8 == 0, pos1, jnp.where(row8 == 1, pos2, jnp.where(
            row8 == 2, e1, jnp.where(row8 == 3, e2, zero_i))))
        mf_ref[...] = jnp.where(row8 == 0, wk1, jnp.where(row8 == 1, wk2, 0.0))
        cnt_ref[...] = tot_ref[...].astype(jnp.int32)


def _router_call(lg_t):
    n_tokens = lg_t.shape[1]
    tc = ROUTER_COLS
    r = lax.broadcasted_iota(jnp.int32, (LANES, LANES), 0)
    c = lax.broadcasted_iota(jnp.int32, (LANES, LANES), 1)
    upper = jnp.where(r < c, 1.0, 0.0).astype(BF16)
    lower = jnp.where(c < r, 1.0, 0.0)[:N_EXPERTS, :N_EXPERTS].astype(BF16)
    col = pl.BlockSpec((SUBLANES, tc), lambda p, i: (0, i * p))
    return pl.pallas_call(
        _router_kernel,
        grid=(2, n_tokens // tc),
        in_specs=[pl.BlockSpec((LOGIT_ROWS, tc), lambda p, i: (0, i)),
                  _const_spec((LANES, LANES)), _const_spec((N_EXPERTS, N_EXPERTS))],
        out_specs=[col, col, _const_spec((N_EXPERTS, LANES))],
        out_shape=[jax.ShapeDtypeStruct((SUBLANES, n_tokens), jnp.int32),
                   jax.ShapeDtypeStruct((SUBLANES, n_tokens), F32),
                   jax.ShapeDtypeStruct((N_EXPERTS, LANES), jnp.int32)],
        scratch_shapes=[pltpu.VMEM((N_EXPERTS, LANES), F32), pltpu.VMEM((N_EXPERTS, LANES), F32)],
        compiler_params=pltpu.CompilerParams(
            dimension_semantics=("arbitrary", "arbitrary"), vmem_limit_bytes=VMEM_LIMIT),
        name="router",
    )(lg_t, upper, lower)


def _ffn_plan(counts, n_pairs):
    ends = jnp.cumsum(counts)
    off = ends - counts
    ids = jnp.arange(N_EXPERTS, dtype=jnp.int32)

    n_tiles = n_pairs // FFN_ROWS
    n_items = n_tiles + N_EXPERTS
    first_tile = off // FFN_ROWS
    last_tile = (ends - 1) // FFN_ROWS
    items_e = jnp.where(counts > 0, last_tile - first_tile + 1, 0)
    items_end = jnp.cumsum(items_e)
    items_start = items_end - items_e
    n_work = items_end[-1]
    w = jnp.arange(n_items, dtype=jnp.int32)
    wv = jnp.minimum(w, n_work - 1)
    e_w = jnp.sum((items_end[None, :] <= wv[:, None]).astype(jnp.int32), axis=1)
    e_w = jnp.minimum(e_w, N_EXPERTS - 1)
    sel = e_w[:, None] == ids[None, :]
    pick = lambda a: jnp.sum(jnp.where(sel, a[None, :], 0), axis=1)
    t_w = pick(first_tile) + (wv - pick(items_start))
    lo_w = jnp.maximum(pick(off), t_w * FFN_ROWS) - t_w * FFN_ROWS
    hi_w = jnp.minimum(pick(ends), (t_w + 1) * FFN_ROWS) - t_w * FFN_ROWS
    valid = w < n_work
    lo_w = jnp.where(valid, lo_w, 0)
    hi_w = jnp.where(valid, hi_w, 0)
    used = counts > 0
    order_e = jnp.cumsum(used.astype(jnp.int32)) - 1
    later_used = jnp.logical_and(used[None, :], ids[None, :] > ids[:, None])
    next_e = jnp.min(jnp.where(later_used, ids[None, :], N_EXPERTS), axis=1)
    next_e = jnp.where(next_e == N_EXPERTS, -1, next_e)
    i32 = lambda a: a.astype(jnp.int32)
    return i32(t_w), i32(e_w), i32(lo_w), i32(hi_w), i32(pick(order_e)), i32(pick(next_e))


def _scatter_kernel(tiles_p, pos1_s, pos2_s, h2p_ref, h2s_ref, xs_hbm, sem):
    i = pl.program_id(0)
    base = i * DISPATCH_ROWS

    def scatter_rows(src_ref):
        def row_copy(r, pos_s):
            p = pos_s[base + r]
            return pltpu.make_async_copy(src_ref.at[_token_tile(r), :], xs_hbm.at[_token_tile(p), :],
                                         sem)

        def start(r, carry):
            row_copy(r, pos1_s).start(priority=0)
            row_copy(r, pos2_s).start(priority=1)
            return carry

        def wait(r, carry):
            row_copy(r, pos1_s).wait()
            row_copy(r, pos2_s).wait()
            return carry

        lax.fori_loop(0, DISPATCH_ROWS, start, 0, unroll=32)
        lax.fori_loop(0, DISPATCH_ROWS, wait, 0, unroll=8)

    @pl.when(i < tiles_p)
    def _():
        scatter_rows(h2p_ref)

    @pl.when(i >= tiles_p)
    def _():
        scatter_rows(h2s_ref)


def _scatter_call(pos1, pos2, h2_p, h2_s):
    tm = DISPATCH_ROWS
    tiles_p = h2_p.shape[0] // (tm * TILE_ROWS)
    tiles_s = h2_s.shape[0] // (tm * TILE_ROWS)
    n_pairs = TOP_K * (h2_p.shape[0] + h2_s.shape[0]) // TILE_ROWS
    blk = (tm * TILE_ROWS, LANES)
    return pl.pallas_call(
        functools.partial(_scatter_kernel, tiles_p),
        grid_spec=pltpu.PrefetchScalarGridSpec(
            num_scalar_prefetch=2,
            grid=(tiles_p + tiles_s,),
            in_specs=[pl.BlockSpec(blk, lambda i, *_: (jnp.minimum(i, tiles_p - 1), 0)),
                      pl.BlockSpec(blk, lambda i, *_: (jnp.maximum(i - tiles_p, 0), 0))],
            out_specs=pl.BlockSpec(memory_space=pl.ANY),
            scratch_shapes=[pltpu.SemaphoreType.DMA(())],
        ),
        out_shape=jax.ShapeDtypeStruct((n_pairs * TILE_ROWS, LANES), F32),
        compiler_params=pltpu.CompilerParams(
            dimension_semantics=("arbitrary",), vmem_limit_bytes=VMEM_LIMIT),
        name="dispatch_scatter",
    )(pos1, pos2, h2_p, h2_s)


def _ffn_kernel(wt_s, we_s, wlo_s, whi_s, word_s, wnext_s, x_ref, w1_hbm, w3_hbm, w2_hbm, o_ref,
                w1f_ref, w3f_ref, w2f_ref, w13b_ref, w2b_ref, sems):
    del wt_s
    w = pl.program_id(0)
    e = we_s[w]
    e_prev = we_s[jnp.maximum(w - 1, 0)]
    slot = word_s[w] % 2

    def weight_copies(expert, buf):
        return [pltpu.make_async_copy(src.at[expert], dst.at[buf], sems.at[buf])
                for src, dst in ((w1_hbm, w1f_ref), (w3_hbm, w3f_ref), (w2_hbm, w2f_ref))]

    @pl.when(w == 0)
    def _():
        for cp in weight_copies(e, slot):
            cp.start()

    @pl.when(jnp.logical_or(w == 0, e != e_prev))
    def _():
        for cp in weight_copies(e, slot):
            cp.wait()
        w13b_ref[:, :D_EXPERT] = w1f_ref[slot].astype(BF16)
        w13b_ref[:, D_EXPERT:] = w3f_ref[slot].astype(BF16)
        w2b_ref[...] = w2f_ref[slot].astype(BF16)
        nxt = wnext_s[w]

        @pl.when(nxt >= 0)
        def _():
            for cp in weight_copies(nxt, 1 - slot):
                cp.start()

    lo = wlo_s[w]
    hi = whi_s[w]

    @pl.when(hi > lo)
    def _():
        x = _load_token_tiles(x_ref, (), FFN_ROWS)
        h13 = _dot(x.astype(BF16), w13b_ref[...])
        a = h13[:, :D_EXPERT]
        b = h13[:, D_EXPERT:]
        hdn = (a * jax.nn.sigmoid(a)) * b
        y = _dot(hdn.astype(BF16), w2b_ref[...])
        rows = lax.broadcasted_iota(jnp.int32, (FFN_ROWS, LANES), 0)
        mine = jnp.logical_and(rows >= lo, rows < hi)

        @pl.when(lo == 0)
        def _():
            _store_token_tiles(o_ref, y)

        @pl.when(lo > 0)
        def _():
            _store_token_tiles(o_ref, y, keep=mine)


def _ffn_call(plan, xs, w1, w3, w2):
    t_w = plan[0]
    n_pairs = xs.shape[0] // TILE_ROWS
    tm = FFN_ROWS
    any_spec = pl.BlockSpec(memory_space=pl.ANY)
    row = pl.BlockSpec((tm * TILE_ROWS, LANES), lambda i, wt, *_: (wt[i], 0))
    return pl.pallas_call(
        _ffn_kernel,
        grid_spec=pltpu.PrefetchScalarGridSpec(
            num_scalar_prefetch=len(plan),
            grid=(t_w.shape[0],),
            in_specs=[row, any_spec, any_spec, any_spec],
            out_specs=row,
            scratch_shapes=[pltpu.VMEM((2, D_MODEL, D_EXPERT), F32),
                            pltpu.VMEM((2, D_MODEL, D_EXPERT), F32),
                            pltpu.VMEM((2, D_EXPERT, D_MODEL), F32),
                            pltpu.VMEM((D_MODEL, 2 * D_EXPERT), BF16),
                            pltpu.VMEM((D_EXPERT, D_MODEL), BF16),
                            pltpu.SemaphoreType.DMA((2,))],
        ),
        out_shape=jax.ShapeDtypeStruct((n_pairs * TILE_ROWS, LANES), F32),
        compiler_params=pltpu.CompilerParams(
            dimension_semantics=("arbitrary",), vmem_limit_bytes=VMEM_LIMIT),
        name="expert_ffn",
    )(*plan, xs, w1, w3, w2)


def _combine_kernel(three_d, row0, n_steps, pos1_s, pos2_s, x1_ref, wk_ref, mod_ref, g_ref, b_ref,
                    ys_hbm, o_ref, ybuf, sems):
    step = pl.program_id(0)
    slot = step % 2

    def gather(tile, tile_slot):
        base = row0 + tile * COMBINE_ROWS

        def row_copy(r, k, pos_s):
            p = pos_s[base + r]
            return pltpu.make_async_copy(ys_hbm.at[_token_tile(p), :],
                                         ybuf.at[tile_slot, k, _token_tile(r), :], sems.at[tile_slot])

        def start(r, carry):
            row_copy(r, 0, pos1_s).start(priority=0)
            row_copy(r, 1, pos2_s).start(priority=1)
            return carry

        def wait(r, carry):
            row_copy(r, 0, pos1_s).wait()
            row_copy(r, 1, pos2_s).wait()
            return carry

        return start, wait

    @pl.when(step == 0)
    def _():
        lax.fori_loop(0, COMBINE_ROWS, gather(step, slot)[0], 0, unroll=32)

    @pl.when(step + 1 < n_steps)
    def _():
        lax.fori_loop(0, COMBINE_ROWS, gather(step + 1, 1 - slot)[0], 0, unroll=32)

    lax.fori_loop(0, COMBINE_ROWS, gather(step, slot)[1], 0, unroll=8)

    wk = wk_ref[...]
    r = lax.broadcasted_iota(jnp.int32, (COMBINE_ROWS, COMBINE_ROWS), 0)
    c = lax.broadcasted_iota(jnp.int32, (COMBINE_ROWS, COMBINE_ROWS), 1)
    wk_col = [jnp.sum(jnp.where(r == c, wk[k:k + 1, :], 0.0), axis=1, keepdims=True)
              for k in range(TOP_K)]
    moe = (wk_col[0] * _load_token_tiles(ybuf, (slot, 0), COMBINE_ROWS)
           + wk_col[1] * _load_token_tiles(ybuf, (slot, 1), COMBINE_ROWS))
    x1 = x1_ref[...]
    if three_d:
        g2 = mod_ref[0, :, 5 * D_MODEL:]
        o_ref[0] = _layer_norm(ALPHA * x1 + (1.0 + g2) * moe, g_ref[...], b_ref[...])
    else:
        def seq_rows(v, b):
            return v[b * SUBLANES:(b + 1) * SUBLANES]

        resid = jnp.concatenate([
            ALPHA * seq_rows(x1, b) + (1.0 + mod_ref[b:b + 1, 5 * D_MODEL:]) * seq_rows(moe, b)
            for b in range(COMBINE_ROWS // SUBLANES)], axis=0)
        o_ref[...] = _layer_norm(resid, g_ref[...], b_ref[...])


def _combine_call(three_d, pos1, pos2, x1_all, mf_all, mod, ln2_g, ln2_b, ys, row0, batch, seq):
    tm = COMBINE_ROWS
    blk0 = row0 // tm
    if three_d:
        tiles = seq // tm
        grid = (batch * tiles,)
        mod_spec = pl.BlockSpec((1, 1, 6 * D_MODEL), lambda i, *_: (i // tiles, 0, 0))
        out_spec = pl.BlockSpec((1, tm, D_MODEL), lambda i, *_: (i // tiles, i % tiles, 0))
        out_shape = jax.ShapeDtypeStruct((batch, seq, D_MODEL), F32)
    else:
        grid = (batch * seq // tm,)
        mod_spec = pl.BlockSpec((tm // SUBLANES, 6 * D_MODEL), lambda i, *_: (i, 0))
        out_spec = pl.BlockSpec((tm, D_MODEL), lambda i, *_: (i, 0))
        out_shape = jax.ShapeDtypeStruct((batch * seq, D_MODEL), F32)
    return pl.pallas_call(
        functools.partial(_combine_kernel, three_d, row0, grid[0]),
        grid_spec=pltpu.PrefetchScalarGridSpec(
            num_scalar_prefetch=2,
            grid=grid,
            in_specs=[pl.BlockSpec((tm, D_MODEL), lambda i, *_: (i, 0)),
                      pl.BlockSpec((SUBLANES, tm), lambda i, *_: (0, blk0 + i)),
                      mod_spec,
                      pl.BlockSpec(ln2_g.shape, lambda i, *_: (0, 0)),
                      pl.BlockSpec(ln2_b.shape, lambda i, *_: (0, 0)),
                      pl.BlockSpec(memory_space=pl.ANY)],
            out_specs=out_spec,
            scratch_shapes=[pltpu.VMEM((2, TOP_K, tm * TILE_ROWS, LANES), F32),
                            pltpu.SemaphoreType.DMA((2,))],
        ),
        out_shape=out_shape,
        compiler_params=pltpu.CompilerParams(
            dimension_semantics=("arbitrary",), vmem_limit_bytes=VMEM_LIMIT),
        name="combine_prompt" if three_d else "combine_sample",
    )(pos1, pos2, x1_all, mf_all, mod, ln2_g, ln2_b, ys)


def _block_diag(w):
    heads, n, _ = w.shape
    eye = jnp.eye(heads, dtype=w.dtype)
    return (eye[:, None, :, None] * w[:, :, None, :]).reshape(heads * n, heads * n)


def kernel(x_prompt, x_sample, state_rglru_h, state_conv, c_prompt, c_sample, w_ada, b_ada, w_in, w_s, b_s, lnv_g, lnv_b, conv_w, conv_b, lru_wa, lru_ba, lru_wx, lru_bx, lru_lam, w_out, ln1_g, ln1_b, w_rg, b_rg, w_re, b_re, w1, w3, w2, ln2_g, ln2_b):
    batch, seq, _ = x_prompt.shape
    dec_batch, dec_seq, _ = x_sample.shape
    assert dec_seq == SUBLANES and seq % ROWS == 0 and (dec_batch * dec_seq) % ROWS == 0
    n_prompt = batch * seq
    n_sample = dec_batch * dec_seq
    n_tokens = n_prompt + n_sample
    n_pairs = TOP_K * n_tokens
    assert n_tokens % ROUTER_COLS == 0 and n_pairs % FFN_ROWS == 0
    l = 0

    mod_p, mod_s = _ada_call(c_prompt, c_sample, w_ada[l], b_ada[l][None])
    mod_p3 = mod_p[:, None, :]

    tri = jnp.tril(jnp.ones((CHUNK, CHUNK), dtype=bool))
    ws_tri = jnp.where(tri[None], w_s[l], 0.0)
    reps_p = MXU_DIM // CHUNK
    eye_p = jnp.eye(reps_p, dtype=F32)
    wsm_p = (eye_p[None, :, None, :, None] * ws_tri[:, None, :, None, :]).reshape(
        A_HEADS, MXU_DIM, MXU_DIM).astype(BF16)
    reps_s = MXU_DIM // dec_seq
    pos_in_seq = jnp.arange(MXU_DIM, dtype=jnp.int32) % dec_seq
    seq_of_row = jnp.arange(MXU_DIM, dtype=jnp.int32) // dec_seq
    spread = (pos_in_seq[:, None] == jnp.arange(dec_seq, dtype=jnp.int32)[None, :]).astype(F32)
    same_seq = seq_of_row[:, None] == seq_of_row[None, :]
    wsm_s = jnp.einsum("ri,hij,cj->hrc", spread, ws_tri[:, :dec_seq, :dec_seq], spread,
                       precision=lax.Precision.HIGHEST)
    wsm_s = jnp.where(same_seq[None], wsm_s, 0.0).astype(BF16)
    bs_pos = jnp.repeat(jnp.transpose(b_s[l]), A_HEAD_DIM, axis=1)
    bsf_p = jnp.tile(bs_pos, (reps_p, 1))
    bsf_s = jnp.tile(bs_pos[:dec_seq], (reps_s, 1))

    wa_bd = _block_diag(lru_wa[l])
    wx_bd = _block_diag(lru_wx[l])
    wg = jnp.stack([
        jnp.concatenate([wa_bd[q * MXU_DIM:(q + 1) * MXU_DIM, q * MXU_DIM:(q + 1) * MXU_DIM],
                         wx_bd[q * MXU_DIM:(q + 1) * MXU_DIM, q * MXU_DIM:(q + 1) * MXU_DIM]], axis=1)
        for q in range(D_B // MXU_DIM)]).astype(BF16)
    bg = jnp.concatenate([lru_ba[l].reshape(1, D_B), lru_bx[l].reshape(1, D_B)], axis=1)

    group_pad = jnp.zeros((SUBLANES - N_GROUPS, D_MODEL), F32)
    wr = jnp.concatenate([jnp.transpose(w_rg[l]), group_pad, jnp.transpose(w_re[l])], axis=0)
    wr_hi = wr.astype(BF16)
    wr_lo = (wr - wr_hi.astype(F32)).astype(BF16)
    br = jnp.concatenate([b_rg[l], jnp.zeros((SUBLANES - N_GROUPS,), F32), b_re[l]])[:, None]

    common = dict(lnv_g=lnv_g[l][None], lnv_b=lnv_b[l][None], conv_w=conv_w[l], conv_b=conv_b[l][None],
                  wg=wg, bg=bg, lam=lru_lam[l][None], w_out=w_out[l].astype(BF16),
                  ln1_g=ln1_g[l][None], ln1_b=ln1_b[l][None], wr_hi=wr_hi, wr_lo=wr_lo, br=br)
    w_in_b = w_in[l].astype(BF16)
    weights_p = _mixer_weights(w_in_b, wsm_p, bsf_p, **common)
    weights_s = _mixer_weights(w_in_b, wsm_s, bsf_s, **common)

    x1_p, h2_p, lg_p, cst_p, hst_p = _mixer_prompt(x_prompt, mod_p3, weights_p)

    cpad = jnp.concatenate(
        [jnp.zeros((dec_batch, SUBLANES - (CONV_W - 1), D_B), F32), state_conv[l]], axis=1
    ).reshape(n_sample, D_B)
    x1_s, h2_s, lg_s, xr_s, v_s, hs_s = _mixer_sample(
        x_sample.reshape(n_sample, D_MODEL), mod_s, weights_s, state_rglru_h[l], cpad)

    meta_i, meta_f, counts = _router_call(jnp.concatenate([lg_p, lg_s], axis=1))
    pos1, pos2 = meta_i[0], meta_i[1]
    xs = _scatter_call(pos1, pos2, h2_p, h2_s)
    ys = _ffn_call(_ffn_plan(counts[:, 0], n_pairs), xs, w1[l], w3[l], w2[l])

    ln2g, ln2b = ln2_g[l][None], ln2_b[l][None]
    y_prompt = _combine_call(True, pos1, pos2, x1_p, meta_f, mod_p3, ln2g, ln2b, ys,
                             0, batch, seq)
    y_sample = _combine_call(False, pos1, pos2, x1_s, meta_f, mod_s, ln2g, ln2b, ys,
                             n_prompt, dec_batch, dec_seq).reshape(dec_batch, dec_seq, D_MODEL)

    new_h_prompt = hst_p[:, SUBLANES - 1, :][None]
    new_conv_prompt = cst_p[:, SUBLANES - (CONV_W - 1):, :][None]
    xr_s3 = xr_s.reshape(dec_batch, dec_seq, D_B)
    new_h_sample = hs_s.reshape(dec_batch, dec_seq, D_B)[:, dec_seq - 1, :][None]
    new_conv_sample = xr_s3[:, dec_seq - (CONV_W - 1):, :][None]
    new_chunk_v_sample = v_s.reshape(dec_batch, dec_seq, D_A)[None]
    return (y_prompt, y_sample, new_h_prompt, new_conv_prompt, new_h_sample, new_conv_sample,
            new_chunk_v_sample)
```

```python
import functools
import math

import jax
import jax.numpy as jnp
from jax import lax
from jax.experimental import pallas as pl
from jax.experimental.pallas import tpu as pltpu

D_MODEL = 1024
D_A = 512
A_HEADS = 4
A_HEAD_DIM = 128
CHUNK = 128
D_B = 512
B_HEADS = 8
B_HEAD_DIM = 64
CONV_W = 4
LRU_C = 8.0
D_IN = 2 * D_A + 2 * D_B
N_GROUPS = 4
EXPERTS_PER_GROUP = 8
N_EXPERTS = 32
TOP_K = 2
D_EXPERT = 512
LN_EPS = 1e-5
DEPTH = 1
ALPHA = (2.0 * DEPTH) ** 0.25

LANES = 128
SUBLANES = 8
MXU_DIM = 256
VMEM_LIMIT = 56 * 1024 * 1024

ROWS = 512
ROUTER_COLS = 1024
LOGIT_ROWS = SUBLANES + N_EXPERTS
DISPATCH_ROWS = 1024
FFN_ROWS = 256
X_SLOTS = 3
COMBINE_ROWS = 256
BF16 = jnp.bfloat16
F32 = jnp.float32


def _dot(a, b):
    return jnp.dot(a, b, preferred_element_type=F32)


def _dot_nt(a, b):
    return lax.dot_general(a, b, (((1,), (1,)), ((), ())), preferred_element_type=F32)


TILE_ROWS = D_MODEL // LANES


def _token_tile(r):
    return pl.ds(pl.multiple_of(r * TILE_ROWS, TILE_ROWS), TILE_ROWS)


def _load_token_tiles(ref, lead, n_rows):
    parts = [ref[lead + (pl.ds(s, n_rows, stride=TILE_ROWS), slice(None))] for s in range(TILE_ROWS)]
    return jnp.concatenate(parts, axis=1)


def _store_token_tiles(ref, val, keep=None):
    n_rows = val.shape[0]
    for s in range(TILE_ROWS):
        rows = pl.ds(s, n_rows, stride=TILE_ROWS)
        new = val[:, s * LANES:(s + 1) * LANES]
        if keep is not None:
            new = jnp.where(keep, new, ref[rows, :])
        ref[rows, :] = new


def _layer_norm(x, g, b):
    mu = jnp.mean(x, axis=-1, keepdims=True)
    xc = x - mu
    var = jnp.mean(xc * xc, axis=-1, keepdims=True)
    return xc * lax.rsqrt(var + LN_EPS) * g + b


def _gelu_tanh(x):
    c = math.sqrt(2.0 / math.pi)
    return x * (0.5 * (1.0 + jnp.tanh(c * (x + 0.044715 * (x * x * x)))))


def _const_spec(shape):
    nd = len(shape)
    return pl.BlockSpec(shape, lambda *_: (0,) * nd)


def _ada_kernel(cp_ref, cs_ref, w_ref, b_ref, op_ref, os_ref):
    w = w_ref[...].astype(BF16)
    for c_ref, o_ref in ((cp_ref, op_ref), (cs_ref, os_ref)):
        c = c_ref[...]
        a = (c * jax.nn.sigmoid(c)).astype(BF16)
        o_ref[...] = _dot(a, w) + b_ref[...]


def _ada_call(c_prompt, c_sample, w_ada, b_ada):
    tn = 1536
    rows_p, rows_s = c_prompt.shape[0], c_sample.shape[0]
    return pl.pallas_call(
        _ada_kernel,
        grid=(6 * D_MODEL // tn,),
        in_specs=[
            pl.BlockSpec((rows_p, D_MODEL), lambda j: (0, 0)),
            pl.BlockSpec((rows_s, D_MODEL), lambda j: (0, 0)),
            pl.BlockSpec((D_MODEL, tn), lambda j: (0, j)),
            pl.BlockSpec((1, tn), lambda j: (0, j)),
        ],
        out_specs=[pl.BlockSpec((rows_p, tn), lambda j: (0, j)),
                   pl.BlockSpec((rows_s, tn), lambda j: (0, j))],
        out_shape=[jax.ShapeDtypeStruct((rows_p, 6 * D_MODEL), F32),
                   jax.ShapeDtypeStruct((rows_s, 6 * D_MODEL), F32)],
        compiler_params=pltpu.CompilerParams(
            dimension_semantics=("arbitrary",), vmem_limit_bytes=VMEM_LIMIT),
        name="ada_mod",
    )(c_prompt, c_sample, w_ada, b_ada)


def _mixer_kernel(sample, *refs):
    if sample:
        (x_ref, mod_ref, win_ref, wsm_ref, bsf_ref, lnvg_ref, lnvb_ref, convw_ref, convb_ref,
         wg_ref, bg_ref, lam_ref, wout_ref, ln1g_ref, ln1b_ref, wrh_ref, wrl_ref, br_ref,
         h0_ref, cpad_ref,
         x1_ref, h2_ref, lg_ref, xr_ref, v_ref, hs_ref) = refs
    else:
        (x_ref, mod_ref, win_ref, wsm_ref, bsf_ref, lnvg_ref, lnvb_ref, convw_ref, convb_ref,
         wg_ref, bg_ref, lam_ref, wout_ref, ln1g_ref, ln1b_ref, wrh_ref, wrl_ref, br_ref,
         x1_ref, h2_ref, lg_ref, cst_ref, hst_ref, prev_ref, hcar_ref) = refs

    rows = ROWS
    groups = rows // SUBLANES

    if sample:
        x = x_ref[...]

        def chunk(c):
            return lambda b: mod_ref[b:b + 1, c * D_MODEL:(c + 1) * D_MODEL]
    else:
        x = x_ref[0]

        def chunk(c):
            return lambda b: mod_ref[0, :, c * D_MODEL:(c + 1) * D_MODEL]

        @pl.when(pl.program_id(1) == 0)
        def _():
            prev_ref[...] = jnp.zeros_like(prev_ref)
            hcar_ref[...] = jnp.zeros_like(hcar_ref)

    def modulated(fn, *vals):
        if not sample:
            return fn(0, *vals)
        parts = [fn(b, *(v[b * SUBLANES:(b + 1) * SUBLANES] for v in vals)) for b in range(groups)]
        return jnp.concatenate(parts, axis=0)

    sh1, sc1, g1, sh2, sc2, g2 = [chunk(c) for c in range(6)]

    h = modulated(lambda b, xv: xv * (1.0 + sc1(b)) + sh1(b), x)
    proj = _dot(h.astype(BF16), win_ref[...])
    u = proj[:, :D_A]
    v = _layer_norm(proj[:, D_A:2 * D_A], lnvg_ref[...], lnvb_ref[...])
    xr = proj[:, 2 * D_A:2 * D_A + D_B]
    yr = proj[:, 2 * D_A + D_B:]

    vb = v.astype(BF16)
    halves = []
    for r in range(rows // MXU_DIM):
        heads = [
            _dot(wsm_ref[hd], vb[r * MXU_DIM:(r + 1) * MXU_DIM, hd * A_HEAD_DIM:(hd + 1) * A_HEAD_DIM])
            for hd in range(A_HEADS)
        ]
        halves.append(jnp.concatenate(heads, axis=1) + bsf_ref[...])
    mixed = halves[0] if len(halves) == 1 else jnp.concatenate(halves, axis=0)
    out_a = u * mixed

    xr3 = xr.reshape(groups, SUBLANES, D_B)
    t3 = lax.broadcasted_iota(jnp.int32, (1, SUBLANES, D_B), 1)
    cw = convw_ref[...]
    if sample:
        cpad3 = cpad_ref[...].reshape(groups, SUBLANES, D_B)
    else:
        prev8 = prev_ref[...]
    xc3 = convb_ref[...][None]
    for k in range(CONV_W):
        s = CONV_W - 1 - k
        if s == 0:
            shifted = xr3
        else:
            rot = pltpu.roll(xr3, s, 1)
            if sample:
                rot_prev = pltpu.roll(cpad3, s, 1)
            else:
                first = pltpu.roll(prev8, s, 0)[None]
                rot_prev = jnp.concatenate([first, rot[:-1]], axis=0) if groups > 1 else first
            shifted = jnp.where(t3 >= s, rot, rot_prev)
        xc3 = xc3 + shifted * cw[k:k + 1, :][None]
    xc = xc3.reshape(rows, D_B)

    xcb = xc.astype(BF16)
    gq = [_dot(xcb[:, q * MXU_DIM:(q + 1) * MXU_DIM], wg_ref[q]) for q in range(D_B // MXU_DIM)]
    bg = bg_ref[...]
    pre_a = jnp.concatenate([g[:, :MXU_DIM] for g in gq], axis=1) + bg[:, :D_B]
    pre_x = jnp.concatenate([g[:, MXU_DIM:] for g in gq], axis=1) + bg[:, D_B:]
    gate_a = jax.nn.sigmoid(pre_a)
    gate_x = jax.nn.sigmoid(pre_x)
    nl = -lam_ref[...]
    softplus = jnp.maximum(nl, 0.0) + jnp.log1p(jnp.exp(-jnp.abs(nl)))
    log_a = -LRU_C * gate_a * softplus
    a = jnp.exp(log_a)
    mult = jnp.sqrt(-jnp.tanh(log_a) * (1.0 + a * a))
    bx = mult * (gate_x * xc)

    a3 = a.reshape(groups, SUBLANES, D_B)
    b3 = bx.reshape(groups, SUBLANES, D_B)
    for d in (1, 2, 4):
        m = t3 >= d
        a_sh = jnp.where(m, pltpu.roll(a3, d, 1), 1.0)
        b_sh = jnp.where(m, pltpu.roll(b3, d, 1), 0.0)
        b3 = a3 * b_sh + b3
        a3 = a3 * a_sh
    if sample:
        hs3 = jnp.stack([a3[g] * h0_ref[g:g + 1, :] + b3[g] for g in range(groups)], axis=0)
    else:
        h_prev = hcar_ref[...][SUBLANES - 1:SUBLANES, :]
        hs_list = []
        for g in range(groups):
            hg = a3[g] * h_prev + b3[g]
            hs_list.append(hg)
            h_prev = hg[SUBLANES - 1:SUBLANES, :]
        hs3 = jnp.stack(hs_list, axis=0)
    hs = hs3.reshape(rows, D_B)

    out_b = hs * _gelu_tanh(yr)
    mix = _dot(out_a.astype(BF16), wout_ref[:D_A, :]) + _dot(out_b.astype(BF16), wout_ref[D_A:, :])
    y1 = modulated(lambda b, xv, mv: ALPHA * xv + (1.0 + g1(b)) * mv, x, mix)
    x1 = _layer_norm(y1, ln1g_ref[...], ln1b_ref[...])
    h2 = modulated(lambda b, xv: xv * (1.0 + sc2(b)) + sh2(b), x1)

    h2_hi = h2.astype(BF16)
    h2_lo = (h2 - h2_hi.astype(F32)).astype(BF16)
    logits = (_dot_nt(wrh_ref[...], h2_hi) + _dot_nt(wrh_ref[...], h2_lo)
              + _dot_nt(wrl_ref[...], h2_hi) + br_ref[...])

    x1_ref[...] = x1
    _store_token_tiles(h2_ref, h2)
    lg_ref[...] = logits
    if sample:
        xr_ref[...] = xr
        v_ref[...] = v
        hs_ref[...] = hs
    else:
        last_x = xr3[groups - 1]
        last_h = hs3[groups - 1]
        prev_ref[...] = last_x
        hcar_ref[...] = last_h
        cst_ref[0] = last_x
        hst_ref[0] = last_h


def _mixer_weights(w_in, wsm, bsf, lnv_g, lnv_b, conv_w, conv_b, wg, bg, lam, w_out, ln1_g, ln1_b,
                   wr_hi, wr_lo, br):
    arrs = [w_in, wsm, bsf, lnv_g, lnv_b, conv_w, conv_b, wg, bg, lam, w_out, ln1_g, ln1_b,
            wr_hi, wr_lo, br]
    return arrs, [_const_spec(a.shape) for a in arrs]


def _mixer_prompt(x, mod3, weights):
    batch, seq, _ = x.shape
    n_tokens = batch * seq
    tiles = seq // ROWS
    arrs, specs = weights
    row_spec = lambda w: pl.BlockSpec((ROWS, w), lambda b, j: (b * tiles + j, 0))
    st_spec = pl.BlockSpec((1, SUBLANES, D_B), lambda b, j: (b, 0, 0))
    return pl.pallas_call(
        functools.partial(_mixer_kernel, False),
        grid=(batch, tiles),
        in_specs=[pl.BlockSpec((1, ROWS, D_MODEL), lambda b, j: (b, j, 0)),
                  pl.BlockSpec((1, 1, 6 * D_MODEL), lambda b, j: (b, 0, 0))] + specs,
        out_specs=[row_spec(D_MODEL),
                   pl.BlockSpec((ROWS * TILE_ROWS, LANES), lambda b, j: (b * tiles + j, 0)),
                   pl.BlockSpec((LOGIT_ROWS, ROWS), lambda b, j: (0, b * tiles + j)), st_spec, st_spec],
        out_shape=[jax.ShapeDtypeStruct((n_tokens, D_MODEL), F32),
                   jax.ShapeDtypeStruct((n_tokens * TILE_ROWS, LANES), F32),
                   jax.ShapeDtypeStruct((LOGIT_ROWS, n_tokens), F32),
                   jax.ShapeDtypeStruct((batch, SUBLANES, D_B), F32),
                   jax.ShapeDtypeStruct((batch, SUBLANES, D_B), F32)],
        scratch_shapes=[pltpu.VMEM((SUBLANES, D_B), F32), pltpu.VMEM((SUBLANES, D_B), F32)],
        compiler_params=pltpu.CompilerParams(
            dimension_semantics=("arbitrary", "arbitrary"), vmem_limit_bytes=VMEM_LIMIT),
        name="mixer_prompt",
    )(x, mod3, *arrs)


def _mixer_sample(x2, mod_seq, weights, h0, cpad):
    n_rows = x2.shape[0]
    arrs, specs = weights
    row = lambda w: pl.BlockSpec((ROWS, w), lambda i: (i, 0))
    seq_row = lambda w: pl.BlockSpec((ROWS // SUBLANES, w), lambda i: (i, 0))
    return pl.pallas_call(
        functools.partial(_mixer_kernel, True),
        grid=(n_rows // ROWS,),
        in_specs=[row(D_MODEL), seq_row(6 * D_MODEL)] + specs + [seq_row(D_B), row(D_B)],
        out_specs=[row(D_MODEL), pl.BlockSpec((ROWS * TILE_ROWS, LANES), lambda i: (i, 0)),
                   pl.BlockSpec((LOGIT_ROWS, ROWS), lambda i: (0, i)),
                   row(D_B), row(D_B), row(D_B)],
        out_shape=[jax.ShapeDtypeStruct((n_rows, D_MODEL), F32),
                   jax.ShapeDtypeStruct((n_rows * TILE_ROWS, LANES), F32),
                   jax.ShapeDtypeStruct((LOGIT_ROWS, n_rows), F32),
                   jax.ShapeDtypeStruct((n_rows, D_B), F32),
                   jax.ShapeDtypeStruct((n_rows, D_B), F32),
                   jax.ShapeDtypeStruct((n_rows, D_B), F32)],
        compiler_params=pltpu.CompilerParams(
            dimension_semantics=("arbitrary",), vmem_limit_bytes=VMEM_LIMIT),
        name="mixer_sample",
    )(x2, mod_seq, *arrs, h0, cpad)


def _router_kernel(lg_ref, upper_ref, lower_ref, mi_ref, mf_ref, cnt_ref, tot_ref, run_ref):
    phase = pl.program_id(0)
    step = pl.program_id(1)
    l = lg_ref[...]
    cols = l.shape[1]
    row8 = lax.broadcasted_iota(jnp.int32, (SUBLANES, cols), 0)
    neg = -jnp.inf
    big = jnp.int32(SUBLANES)

    def first_max(vals):
        top = jnp.max(vals, axis=0, keepdims=True)
        return top, jnp.min(jnp.where(vals == top, row8, big), axis=0, keepdims=True)

    is_group = row8 < N_GROUPS
    lg = l[:SUBLANES]
    m, gsel = first_max(jnp.where(is_group, lg, neg))
    denom = jnp.sum(jnp.where(is_group, jnp.exp(lg - m), 0.0), axis=0, keepdims=True)
    pgsel = 1.0 / denom
    le = l[SUBLANES:2 * SUBLANES]
    for g in range(1, N_GROUPS):
        le = jnp.where(gsel == g, l[(g + 1) * SUBLANES:(g + 2) * SUBLANES], le)
    v1, i1 = first_max(le)
    v2, i2 = first_max(jnp.where(row8 == i1, neg, le))
    ex = jnp.exp(v2 - v1)
    wk1 = pgsel / (1.0 + ex)
    wk2 = pgsel * ex / (1.0 + ex)
    e1 = gsel * EXPERTS_PER_GROUP + i1
    e2 = gsel * EXPERTS_PER_GROUP + i2
    row_e = lax.broadcasted_iota(jnp.int32, (N_EXPERTS, cols), 0)
    oh1 = jnp.where(row_e == e1, 1.0, 0.0)
    oh2 = jnp.where(row_e == e2, 1.0, 0.0)
    oh = oh1 + oh2

    @pl.when(jnp.logical_and(phase == 0, step == 0))
    def _():
        tot_ref[...] = jnp.zeros_like(tot_ref)

    @pl.when(phase == 0)
    def _():
        tot_ref[...] = tot_ref[...] + jnp.sum(oh, axis=1, keepdims=True)

    @pl.when(jnp.logical_and(phase == 1, step == 0))
    def _():
        tot = tot_ref[...]
        hi = (tot * (1.0 / MXU_DIM)).astype(jnp.int32).astype(F32)
        lo = tot - hi * MXU_DIM
        run_ref[...] = (_dot(lower_ref[...], hi.astype(BF16)) * MXU_DIM
                        + _dot(lower_ref[...], lo.astype(BF16)))

    @pl.when(phase == 1)
    def _():
        blocks = cols // LANES
        stacked = jnp.concatenate([oh[:, b * LANES:(b + 1) * LANES] for b in range(blocks)], axis=0)
        local = _dot(stacked.astype(BF16), upper_ref[...])
        run = run_ref[...]
        p1, p2 = [], []
        for b in range(blocks):
            sl = slice(b * LANES, (b + 1) * LANES)
            before = local[b * N_EXPERTS:(b + 1) * N_EXPERTS] + run
            p1.append(jnp.sum(oh1[:, sl] * before, axis=0, keepdims=True))
            p2.append(jnp.sum(oh2[:, sl] * before, axis=0, keepdims=True))
            run = run + jnp.sum(oh[:, sl], axis=1, keepdims=True)
        run_ref[...] = run
        pos1 = jnp.concatenate(p1, axis=1).astype(jnp.int32)
        pos2 = jnp.concatenate(p2, axis=1).astype(jnp.int32)
        zero_i = jnp.zeros_like(row8)
        mi_ref[...] = jnp.where(row8 == 0, pos1, jnp.where(row8 == 1, pos2, jnp.where(
            row8 == 2, e1, jnp.where(row8 == 3, e2, zero_i))))
        mf_ref[...] = jnp.where(row8 == 0, wk1, jnp.where(row8 == 1, wk2, 0.0))
        cnt_ref[...] = tot_ref[...].astype(jnp.int32)


def _router_call(lg_t):
    n_tokens = lg_t.shape[1]
    tc = ROUTER_COLS
    r = lax.broadcasted_iota(jnp.int32, (LANES, LANES), 0)
    c = lax.broadcasted_iota(jnp.int32, (LANES, LANES), 1)
    upper = jnp.where(r < c, 1.0, 0.0).astype(BF16)
    lower = jnp.where(c < r, 1.0, 0.0)[:N_EXPERTS, :N_EXPERTS].astype(BF16)
    col = pl.BlockSpec((SUBLANES, tc), lambda p, i: (0, i * p))
    return pl.pallas_call(
        _router_kernel,
        grid=(2, n_tokens // tc),
        in_specs=[pl.BlockSpec((LOGIT_ROWS, tc), lambda p, i: (0, i)),
                  _const_spec((LANES, LANES)), _const_spec((N_EXPERTS, N_EXPERTS))],
        out_specs=[col, col, _const_spec((N_EXPERTS, LANES))],
        out_shape=[jax.ShapeDtypeStruct((SUBLANES, n_tokens), jnp.int32),
                   jax.ShapeDtypeStruct((SUBLANES, n_tokens), F32),
                   jax.ShapeDtypeStruct((N_EXPERTS, LANES), jnp.int32)],
        scratch_shapes=[pltpu.VMEM((N_EXPERTS, LANES), F32), pltpu.VMEM((N_EXPERTS, LANES), F32)],
        compiler_params=pltpu.CompilerParams(
            dimension_semantics=("arbitrary", "arbitrary"), vmem_limit_bytes=VMEM_LIMIT),
        name="router",
    )(lg_t, upper, lower)


def _ffn_plan(counts, n_pairs):
    ends = jnp.cumsum(counts)
    off = ends - counts
    ids = jnp.arange(N_EXPERTS, dtype=jnp.int32)

    n_tiles = n_pairs // FFN_ROWS
    n_items = n_tiles + N_EXPERTS
    first_tile = off // FFN_ROWS
    last_tile = (ends - 1) // FFN_ROWS
    items_e = jnp.where(counts > 0, last_tile - first_tile + 1, 0)
    items_end = jnp.cumsum(items_e)
    items_start = items_end - items_e
    n_work = items_end[-1]
    w = jnp.arange(n_items, dtype=jnp.int32)
    wv = jnp.minimum(w, n_work - 1)
    e_w = jnp.sum((items_end[None, :] <= wv[:, None]).astype(jnp.int32), axis=1)
    e_w = jnp.minimum(e_w, N_EXPERTS - 1)
    sel = e_w[:, None] == ids[None, :]
    pick = lambda a: jnp.sum(jnp.where(sel, a[None, :], 0), axis=1)
    t_w = pick(first_tile) + (wv - pick(items_start))
    lo_w = jnp.maximum(pick(off), t_w * FFN_ROWS) - t_w * FFN_ROWS
    hi_w = jnp.minimum(pick(ends), (t_w + 1) * FFN_ROWS) - t_w * FFN_ROWS
    valid = w < n_work
    lo_w = jnp.where(valid, lo_w, 0)
    hi_w = jnp.where(valid, hi_w, 0)
    used = counts > 0
    order_e = jnp.cumsum(used.astype(jnp.int32)) - 1
    later_used = jnp.logical_and(used[None, :], ids[None, :] > ids[:, None])
    next_e = jnp.min(jnp.where(later_used, ids[None, :], N_EXPERTS), axis=1)
    next_e = jnp.where(next_e == N_EXPERTS, -1, next_e)
    i32 = lambda a: a.astype(jnp.int32)
    return i32(t_w), i32(e_w), i32(lo_w), i32(hi_w), i32(pick(order_e)), i32(pick(next_e))


def _scatter_kernel(tiles_p, pos1_s, pos2_s, h2p_ref, h2s_ref, xs_hbm, sem):
    i = pl.program_id(0)
    base = i * DISPATCH_ROWS

    def scatter_rows(src_ref):
        def row_copy(r, pos_s):
            p = pos_s[base + r]
            return pltpu.make_async_copy(src_ref.at[_token_tile(r), :], xs_hbm.at[_token_tile(p), :],
                                         sem)

        def start(r, carry):
            row_copy(r, pos1_s).start(priority=0)
            row_copy(r, pos2_s).start(priority=1)
            return carry

        def wait(r, carry):
            row_copy(r, pos1_s).wait()
            row_copy(r, pos2_s).wait()
            return carry

        lax.fori_loop(0, DISPATCH_ROWS, start, 0, unroll=32)
        lax.fori_loop(0, DISPATCH_ROWS, wait, 0, unroll=8)

    @pl.when(i < tiles_p)
    def _():
        scatter_rows(h2p_ref)

    @pl.when(i >= tiles_p)
    def _():
        scatter_rows(h2s_ref)


def _scatter_call(pos1, pos2, h2_p, h2_s):
    tm = DISPATCH_ROWS
    tiles_p = h2_p.shape[0] // (tm * TILE_ROWS)
    tiles_s = h2_s.shape[0] // (tm * TILE_ROWS)
    n_pairs = TOP_K * (h2_p.shape[0] + h2_s.shape[0]) // TILE_ROWS
    blk = (tm * TILE_ROWS, LANES)
    return pl.pallas_call(
        functools.partial(_scatter_kernel, tiles_p),
        grid_spec=pltpu.PrefetchScalarGridSpec(
            num_scalar_prefetch=2,
            grid=(tiles_p + tiles_s,),
            in_specs=[pl.BlockSpec(blk, lambda i, *_: (jnp.minimum(i, tiles_p - 1), 0)),
                      pl.BlockSpec(blk, lambda i, *_: (jnp.maximum(i - tiles_p, 0), 0))],
            out_specs=pl.BlockSpec(memory_space=pl.ANY),
            scratch_shapes=[pltpu.SemaphoreType.DMA(())],
        ),
        out_shape=jax.ShapeDtypeStruct((n_pairs * TILE_ROWS, LANES), F32),
        compiler_params=pltpu.CompilerParams(
            dimension_semantics=("arbitrary",), vmem_limit_bytes=VMEM_LIMIT),
        name="dispatch_scatter",
    )(pos1, pos2, h2_p, h2_s)


def _ffn_kernel(n_items, wt_s, we_s, wlo_s, whi_s, word_s, wnext_s, x_hbm, w1_hbm, w3_hbm, w2_hbm,
                o_ref, xbuf, w1f_ref, w3f_ref, w2f_ref, w13b_ref, w2b_ref, xsems, sems):
    w = pl.program_id(0)
    e = we_s[w]
    e_prev = we_s[jnp.maximum(w - 1, 0)]
    slot = word_s[w] % 2
    tile_rows = FFN_ROWS * TILE_ROWS

    def x_copy(item):
        first = pl.multiple_of(wt_s[item] * tile_rows, tile_rows)
        return pltpu.make_async_copy(x_hbm.at[pl.ds(first, tile_rows), :], xbuf.at[item % X_SLOTS],
                                     xsems.at[item % X_SLOTS])

    @pl.when(w == 0)
    def _():
        for ahead in range(X_SLOTS - 1):
            x_copy(ahead).start()

    @pl.when(w + X_SLOTS - 1 < n_items)
    def _():
        x_copy(w + X_SLOTS - 1).start()

    def weight_copies(expert, buf):
        return [pltpu.make_async_copy(src.at[expert], dst.at[buf], sems.at[buf])
                for src, dst in ((w1_hbm, w1f_ref), (w3_hbm, w3f_ref), (w2_hbm, w2f_ref))]

    @pl.when(w == 0)
    def _():
        for cp in weight_copies(e, slot):
            cp.start()

    @pl.when(jnp.logical_or(w == 0, e != e_prev))
    def _():
        for cp in weight_copies(e, slot):
            cp.wait()
        w13b_ref[:, :D_EXPERT] = w1f_ref[slot].astype(BF16)
        w13b_ref[:, D_EXPERT:] = w3f_ref[slot].astype(BF16)
        w2b_ref[...] = w2f_ref[slot].astype(BF16)
        nxt = wnext_s[w]

        @pl.when(nxt >= 0)
        def _():
            for cp in weight_copies(nxt, 1 - slot):
                cp.start()

    lo = wlo_s[w]
    hi = whi_s[w]
    x_copy(w).wait()

    @pl.when(hi > lo)
    def _():
        x = _load_token_tiles(xbuf, (w % X_SLOTS,), FFN_ROWS)
        h13 = _dot(x.astype(BF16), w13b_ref[...])
        a = h13[:, :D_EXPERT]
        b = h13[:, D_EXPERT:]
        hdn = (a * jax.nn.sigmoid(a)) * b
        y = _dot(hdn.astype(BF16), w2b_ref[...])
        rows = lax.broadcasted_iota(jnp.int32, (FFN_ROWS, LANES), 0)
        mine = jnp.logical_and(rows >= lo, rows < hi)

        @pl.when(lo == 0)
        def _():
            _store_token_tiles(o_ref, y)

        @pl.when(lo > 0)
        def _():
            _store_token_tiles(o_ref, y, keep=mine)


def _ffn_call(plan, xs, w1, w3, w2):
    t_w = plan[0]
    n_pairs = xs.shape[0] // TILE_ROWS
    tm = FFN_ROWS
    any_spec = pl.BlockSpec(memory_space=pl.ANY)
    row = pl.BlockSpec((tm * TILE_ROWS, LANES), lambda i, wt, *_: (wt[i], 0))
    n_items = t_w.shape[0]
    assert n_items >= X_SLOTS
    return pl.pallas_call(
        functools.partial(_ffn_kernel, n_items),
        grid_spec=pltpu.PrefetchScalarGridSpec(
            num_scalar_prefetch=len(plan),
            grid=(n_items,),
            in_specs=[any_spec, any_spec, any_spec, any_spec],
            out_specs=row,
            scratch_shapes=[pltpu.VMEM((X_SLOTS, tm * TILE_ROWS, LANES), F32),
                            pltpu.VMEM((2, D_MODEL, D_EXPERT), F32),
                            pltpu.VMEM((2, D_MODEL, D_EXPERT), F32),
                            pltpu.VMEM((2, D_EXPERT, D_MODEL), F32),
                            pltpu.VMEM((D_MODEL, 2 * D_EXPERT), BF16),
                            pltpu.VMEM((D_EXPERT, D_MODEL), BF16),
                            pltpu.SemaphoreType.DMA((X_SLOTS,)),
                            pltpu.SemaphoreType.DMA((2,))],
        ),
        out_shape=jax.ShapeDtypeStruct((n_pairs * TILE_ROWS, LANES), F32),
        compiler_params=pltpu.CompilerParams(
            dimension_semantics=("arbitrary",), vmem_limit_bytes=VMEM_LIMIT),
        name="expert_ffn",
    )(*plan, xs, w1, w3, w2)


def _combine_kernel(three_d, row0, n_steps, pos1_s, pos2_s, x1_ref, wk_ref, mod_ref, g_ref, b_ref,
                    ys_hbm, o_ref, ybuf, sems):
    step = pl.program_id(0)
    slot = step % 2

    def gather(tile, tile_slot):
        base = row0 + tile * COMBINE_ROWS

        def row_copy(r, k, pos_s):
            p = pos_s[base + r]
            return pltpu.make_async_copy(ys_hbm.at[_token_tile(p), :],
                                         ybuf.at[tile_slot, k, _token_tile(r), :], sems.at[tile_slot])

        def start(r, carry):
            row_copy(r, 0, pos1_s).start(priority=0)
            row_copy(r, 1, pos2_s).start(priority=1)
            return carry

        def wait(r, carry):
            row_copy(r, 0, pos1_s).wait()
            row_copy(r, 1, pos2_s).wait()
            return carry

        return start, wait

    @pl.when(step == 0)
    def _():
        lax.fori_loop(0, COMBINE_ROWS, gather(step, slot)[0], 0, unroll=32)

    @pl.when(step + 1 < n_steps)
    def _():
        lax.fori_loop(0, COMBINE_ROWS, gather(step + 1, 1 - slot)[0], 0, unroll=32)

    lax.fori_loop(0, COMBINE_ROWS, gather(step, slot)[1], 0, unroll=8)

    wk = wk_ref[...]
    r = lax.broadcasted_iota(jnp.int32, (COMBINE_ROWS, COMBINE_ROWS), 0)
    c = lax.broadcasted_iota(jnp.int32, (COMBINE_ROWS, COMBINE_ROWS), 1)
    wk_col = [jnp.sum(jnp.where(r == c, wk[k:k + 1, :], 0.0), axis=1, keepdims=True)
              for k in range(TOP_K)]
    moe = (wk_col[0] * _load_token_tiles(ybuf, (slot, 0), COMBINE_ROWS)
           + wk_col[1] * _load_token_tiles(ybuf, (slot, 1), COMBINE_ROWS))
    x1 = x1_ref[...]
    if three_d:
        g2 = mod_ref[0, :, 5 * D_MODEL:]
        o_ref[0] = _layer_norm(ALPHA * x1 + (1.0 + g2) * moe, g_ref[...], b_ref[...])
    else:
        def seq_rows(v, b):
            return v[b * SUBLANES:(b + 1) * SUBLANES]

        resid = jnp.concatenate([
            ALPHA * seq_rows(x1, b) + (1.0 + mod_ref[b:b + 1, 5 * D_MODEL:]) * seq_rows(moe, b)
            for b in range(COMBINE_ROWS // SUBLANES)], axis=0)
        o_ref[...] = _layer_norm(resid, g_ref[...], b_ref[...])


def _combine_call(three_d, pos1, pos2, x1_all, mf_all, mod, ln2_g, ln2_b, ys, row0, batch, seq):
    tm = COMBINE_ROWS
    blk0 = row0 // tm
    if three_d:
        tiles = seq // tm
        grid = (batch * tiles,)
        mod_spec = pl.BlockSpec((1, 1, 6 * D_MODEL), lambda i, *_: (i // tiles, 0, 0))
        out_spec = pl.BlockSpec((1, tm, D_MODEL), lambda i, *_: (i // tiles, i % tiles, 0))
        out_shape = jax.ShapeDtypeStruct((batch, seq, D_MODEL), F32)
    else:
        grid = (batch * seq // tm,)
        mod_spec = pl.BlockSpec((tm // SUBLANES, 6 * D_MODEL), lambda i, *_: (i, 0))
        out_spec = pl.BlockSpec((tm, D_MODEL), lambda i, *_: (i, 0))
        out_shape = jax.ShapeDtypeStruct((batch * seq, D_MODEL), F32)
    return pl.pallas_call(
        functools.partial(_combine_kernel, three_d, row0, grid[0]),
        grid_spec=pltpu.PrefetchScalarGridSpec(
            num_scalar_prefetch=2,
            grid=grid,
            in_specs=[pl.BlockSpec((tm, D_MODEL), lambda i, *_: (i, 0)),
                      pl.BlockSpec((SUBLANES, tm), lambda i, *_: (0, blk0 + i)),
                      mod_spec,
                      pl.BlockSpec(ln2_g.shape, lambda i, *_: (0, 0)),
                      pl.BlockSpec(ln2_b.shape, lambda i, *_: (0, 0)),
                      pl.BlockSpec(memory_space=pl.ANY)],
            out_specs=out_spec,
            scratch_shapes=[pltpu.VMEM((2, TOP_K, tm * TILE_ROWS, LANES), F32),
                            pltpu.SemaphoreType.DMA((2,))],
        ),
        out_shape=out_shape,
        compiler_params=pltpu.CompilerParams(
            dimension_semantics=("arbitrary",), vmem_limit_bytes=VMEM_LIMIT),
        name="combine_prompt" if three_d else "combine_sample",
    )(pos1, pos2, x1_all, mf_all, mod, ln2_g, ln2_b, ys)


def _block_diag(w):
    heads, n, _ = w.shape
    eye = jnp.eye(heads, dtype=w.dtype)
    return (eye[:, None, :, None] * w[:, :, None, :]).reshape(heads * n, heads * n)


def kernel(x_prompt, x_sample, state_rglru_h, state_conv, c_prompt, c_sample, w_ada, b_ada, w_in, w_s, b_s, lnv_g, lnv_b, conv_w, conv_b, lru_wa, lru_ba, lru_wx, lru_bx, lru_lam, w_out, ln1_g, ln1_b, w_rg, b_rg, w_re, b_re, w1, w3, w2, ln2_g, ln2_b):
    batch, seq, _ = x_prompt.shape
    dec_batch, dec_seq, _ = x_sample.shape
    assert dec_seq == SUBLANES and seq % ROWS == 0 and (dec_batch * dec_seq) % ROWS == 0
    n_prompt = batch * seq
    n_sample = dec_batch * dec_seq
    n_tokens = n_prompt + n_sample
    n_pairs = TOP_K * n_tokens
    assert n_tokens % ROUTER_COLS == 0 and n_pairs % FFN_ROWS == 0
    l = 0

    mod_p, mod_s = _ada_call(c_prompt, c_sample, w_ada[l], b_ada[l][None])
    mod_p3 = mod_p[:, None, :]

    tri = jnp.tril(jnp.ones((CHUNK, CHUNK), dtype=bool))
    ws_tri = jnp.where(tri[None], w_s[l], 0.0)
    reps_p = MXU_DIM // CHUNK
    eye_p = jnp.eye(reps_p, dtype=F32)
    wsm_p = (eye_p[None, :, None, :, None] * ws_tri[:, None, :, None, :]).reshape(
        A_HEADS, MXU_DIM, MXU_DIM).astype(BF16)
    reps_s = MXU_DIM // dec_seq
    pos_in_seq = jnp.arange(MXU_DIM, dtype=jnp.int32) % dec_seq
    seq_of_row = jnp.arange(MXU_DIM, dtype=jnp.int32) // dec_seq
    spread = (pos_in_seq[:, None] == jnp.arange(dec_seq, dtype=jnp.int32)[None, :]).astype(F32)
    same_seq = seq_of_row[:, None] == seq_of_row[None, :]
    wsm_s = jnp.einsum("ri,hij,cj->hrc", spread, ws_tri[:, :dec_seq, :dec_seq], spread,
                       precision=lax.Precision.HIGHEST)
    wsm_s = jnp.where(same_seq[None], wsm_s, 0.0).astype(BF16)
    bs_pos = jnp.repeat(jnp.transpose(b_s[l]), A_HEAD_DIM, axis=1)
    bsf_p = jnp.tile(bs_pos, (reps_p, 1))
    bsf_s = jnp.tile(bs_pos[:dec_seq], (reps_s, 1))

    wa_bd = _block_diag(lru_wa[l])
    wx_bd = _block_diag(lru_wx[l])
    wg = jnp.stack([
        jnp.concatenate([wa_bd[q * MXU_DIM:(q + 1) * MXU_DIM, q * MXU_DIM:(q + 1) * MXU_DIM],
                         wx_bd[q * MXU_DIM:(q + 1) * MXU_DIM, q * MXU_DIM:(q + 1) * MXU_DIM]], axis=1)
        for q in range(D_B // MXU_DIM)]).astype(BF16)
    bg = jnp.concatenate([lru_ba[l].reshape(1, D_B), lru_bx[l].reshape(1, D_B)], axis=1)

    group_pad = jnp.zeros((SUBLANES - N_GROUPS, D_MODEL), F32)
    wr = jnp.concatenate([jnp.transpose(w_rg[l]), group_pad, jnp.transpose(w_re[l])], axis=0)
    wr_hi = wr.astype(BF16)
    wr_lo = (wr - wr_hi.astype(F32)).astype(BF16)
    br = jnp.concatenate([b_rg[l], jnp.zeros((SUBLANES - N_GROUPS,), F32), b_re[l]])[:, None]

    common = dict(lnv_g=lnv_g[l][None], lnv_b=lnv_b[l][None], conv_w=conv_w[l], conv_b=conv_b[l][None],
                  wg=wg, bg=bg, lam=lru_lam[l][None], w_out=w_out[l].astype(BF16),
                  ln1_g=ln1_g[l][None], ln1_b=ln1_b[l][None], wr_hi=wr_hi, wr_lo=wr_lo, br=br)
    w_in_b = w_in[l].astype(BF16)
    weights_p = _mixer_weights(w_in_b, wsm_p, bsf_p, **common)
    weights_s = _mixer_weights(w_in_b, wsm_s, bsf_s, **common)

    x1_p, h2_p, lg_p, cst_p, hst_p = _mixer_prompt(x_prompt, mod_p3, weights_p)

    cpad = jnp.concatenate(
        [jnp.zeros((dec_batch, SUBLANES - (CONV_W - 1), D_B), F32), state_conv[l]], axis=1
    ).reshape(n_sample, D_B)
    x1_s, h2_s, lg_s, xr_s, v_s, hs_s = _mixer_sample(
        x_sample.reshape(n_sample, D_MODEL), mod_s, weights_s, state_rglru_h[l], cpad)

    meta_i, meta_f, counts = _router_call(jnp.concatenate([lg_p, lg_s], axis=1))
    pos1, pos2 = meta_i[0], meta_i[1]
    xs = _scatter_call(pos1, pos2, h2_p, h2_s)
    ys = _ffn_call(_ffn_plan(counts[:, 0], n_pairs), xs, w1[l], w3[l], w2[l])

    ln2g, ln2b = ln2_g[l][None], ln2_b[l][None]
    y_prompt = _combine_call(True, pos1, pos2, x1_p, meta_f, mod_p3, ln2g, ln2b, ys,
                             0, batch, seq)
    y_sample = _combine_call(False, pos1, pos2, x1_s, meta_f, mod_s, ln2g, ln2b, ys,
                             n_prompt, dec_batch, dec_seq).reshape(dec_batch, dec_seq, D_MODEL)

    new_h_prompt = hst_p[:, SUBLANES - 1, :][None]
    new_conv_prompt = cst_p[:, SUBLANES - (CONV_W - 1):, :][None]
    xr_s3 = xr_s.reshape(dec_batch, dec_seq, D_B)
    new_h_sample = hs_s.reshape(dec_batch, dec_seq, D_B)[:, dec_seq - 1, :][None]
    new_conv_sample = xr_s3[:, dec_seq - (CONV_W - 1):, :][None]
    new_chunk_v_sample = v_s.reshape(dec_batch, dec_seq, D_A)[None]
    return (y_prompt, y_sample, new_h_prompt, new_conv_prompt, new_h_sample, new_conv_sample,
            new_chunk_v_sample)
```

```python
import functools
import math

import jax
import jax.numpy as jnp
from jax import lax
from jax.experimental import pallas as pl
from jax.experimental.pallas import tpu as pltpu

D_MODEL = 1024
D_A = 512
A_HEADS = 4
A_HEAD_DIM = 128
CHUNK = 128
D_B = 512
B_HEADS = 8
B_HEAD_DIM = 64
CONV_W = 4
LRU_C = 8.0
D_IN = 2 * D_A + 2 * D_B
N_GROUPS = 4
EXPERTS_PER_GROUP = 8
N_EXPERTS = 32
TOP_K = 2
D_EXPERT = 512
LN_EPS = 1e-5
DEPTH = 1
ALPHA = (2.0 * DEPTH) ** 0.25

LANES = 128
SUBLANES = 8
MXU_DIM = 256
VMEM_LIMIT = 56 * 1024 * 1024

ROWS = 512
ROUTER_COLS = 1024
LOGIT_ROWS = SUBLANES + N_EXPERTS
DISPATCH_ROWS = 1024
FFN_ROWS = 256
X_SLOTS = 3
COMBINE_ROWS = 512
BF16 = jnp.bfloat16
F32 = jnp.float32


def _dot(a, b):
    return jnp.dot(a, b, preferred_element_type=F32)


def _dot_nt(a, b):
    return lax.dot_general(a, b, (((1,), (1,)), ((), ())), preferred_element_type=F32)


TILE_ROWS = D_MODEL // LANES


def _token_tile(r):
    return pl.ds(pl.multiple_of(r * TILE_ROWS, TILE_ROWS), TILE_ROWS)


def _load_token_tiles(ref, lead, n_rows):
    parts = [ref[lead + (pl.ds(s, n_rows, stride=TILE_ROWS), slice(None))] for s in range(TILE_ROWS)]
    return jnp.concatenate(parts, axis=1)


def _store_token_tiles(ref, val, keep=None):
    n_rows = val.shape[0]
    for s in range(TILE_ROWS):
        rows = pl.ds(s, n_rows, stride=TILE_ROWS)
        new = val[:, s * LANES:(s + 1) * LANES]
        if keep is not None:
            new = jnp.where(keep, new, ref[rows, :])
        ref[rows, :] = new


def _layer_norm(x, g, b):
    mu = jnp.mean(x, axis=-1, keepdims=True)
    xc = x - mu
    var = jnp.mean(xc * xc, axis=-1, keepdims=True)
    return xc * lax.rsqrt(var + LN_EPS) * g + b


def _gelu_tanh(x):
    c = math.sqrt(2.0 / math.pi)
    return x * (0.5 * (1.0 + jnp.tanh(c * (x + 0.044715 * (x * x * x)))))


def _const_spec(shape):
    nd = len(shape)
    return pl.BlockSpec(shape, lambda *_: (0,) * nd)


def _ada_kernel(cp_ref, cs_ref, w_ref, b_ref, op_ref, os_ref):
    w = w_ref[...].astype(BF16)
    for c_ref, o_ref in ((cp_ref, op_ref), (cs_ref, os_ref)):
        c = c_ref[...]
        a = (c * jax.nn.sigmoid(c)).astype(BF16)
        o_ref[...] = _dot(a, w) + b_ref[...]


def _ada_call(c_prompt, c_sample, w_ada, b_ada):
    tn = 1536
    rows_p, rows_s = c_prompt.shape[0], c_sample.shape[0]
    return pl.pallas_call(
        _ada_kernel,
        grid=(6 * D_MODEL // tn,),
        in_specs=[
            pl.BlockSpec((rows_p, D_MODEL), lambda j: (0, 0)),
            pl.BlockSpec((rows_s, D_MODEL), lambda j: (0, 0)),
            pl.BlockSpec((D_MODEL, tn), lambda j: (0, j)),
            pl.BlockSpec((1, tn), lambda j: (0, j)),
        ],
        out_specs=[pl.BlockSpec((rows_p, tn), lambda j: (0, j)),
                   pl.BlockSpec((rows_s, tn), lambda j: (0, j))],
        out_shape=[jax.ShapeDtypeStruct((rows_p, 6 * D_MODEL), F32),
                   jax.ShapeDtypeStruct((rows_s, 6 * D_MODEL), F32)],
        compiler_params=pltpu.CompilerParams(
            dimension_semantics=("arbitrary",), vmem_limit_bytes=VMEM_LIMIT),
        name="ada_mod",
    )(c_prompt, c_sample, w_ada, b_ada)


def _mixer_kernel(sample, *refs):
    if sample:
        (x_ref, mod_ref, win_ref, wsm_ref, bsf_ref, lnvg_ref, lnvb_ref, convw_ref, convb_ref,
         wg_ref, bg_ref, lam_ref, wout_ref, ln1g_ref, ln1b_ref, wrh_ref, wrl_ref, br_ref,
         h0_ref, cpad_ref,
         x1_ref, h2_ref, lg_ref, xr_ref, v_ref, hs_ref) = refs
    else:
        (x_ref, mod_ref, win_ref, wsm_ref, bsf_ref, lnvg_ref, lnvb_ref, convw_ref, convb_ref,
         wg_ref, bg_ref, lam_ref, wout_ref, ln1g_ref, ln1b_ref, wrh_ref, wrl_ref, br_ref,
         x1_ref, h2_ref, lg_ref, cst_ref, hst_ref, prev_ref, hcar_ref) = refs

    rows = ROWS
    groups = rows // SUBLANES

    if sample:
        x = x_ref[...]

        def chunk(c):
            return lambda b: mod_ref[b:b + 1, c * D_MODEL:(c + 1) * D_MODEL]
    else:
        x = x_ref[0]

        def chunk(c):
            return lambda b: mod_ref[0, :, c * D_MODEL:(c + 1) * D_MODEL]

        @pl.when(pl.program_id(1) == 0)
        def _():
            prev_ref[...] = jnp.zeros_like(prev_ref)
            hcar_ref[...] = jnp.zeros_like(hcar_ref)

    def modulated(fn, *vals):
        if not sample:
            return fn(0, *vals)
        parts = [fn(b, *(v[b * SUBLANES:(b + 1) * SUBLANES] for v in vals)) for b in range(groups)]
        return jnp.concatenate(parts, axis=0)

    sh1, sc1, g1, sh2, sc2, g2 = [chunk(c) for c in range(6)]

    h = modulated(lambda b, xv: xv * (1.0 + sc1(b)) + sh1(b), x)
    proj = _dot(h.astype(BF16), win_ref[...])
    u = proj[:, :D_A]
    v = _layer_norm(proj[:, D_A:2 * D_A], lnvg_ref[...], lnvb_ref[...])
    xr = proj[:, 2 * D_A:2 * D_A + D_B]
    yr = proj[:, 2 * D_A + D_B:]

    vb = v.astype(BF16)
    halves = []
    for r in range(rows // MXU_DIM):
        heads = [
            _dot(wsm_ref[hd], vb[r * MXU_DIM:(r + 1) * MXU_DIM, hd * A_HEAD_DIM:(hd + 1) * A_HEAD_DIM])
            for hd in range(A_HEADS)
        ]
        halves.append(jnp.concatenate(heads, axis=1) + bsf_ref[...])
    mixed = halves[0] if len(halves) == 1 else jnp.concatenate(halves, axis=0)
    out_a = u * mixed

    xr3 = xr.reshape(groups, SUBLANES, D_B)
    t3 = lax.broadcasted_iota(jnp.int32, (1, SUBLANES, D_B), 1)
    cw = convw_ref[...]
    if sample:
        cpad3 = cpad_ref[...].reshape(groups, SUBLANES, D_B)
    else:
        prev8 = prev_ref[...]
    xc3 = convb_ref[...][None]
    for k in range(CONV_W):
        s = CONV_W - 1 - k
        if s == 0:
            shifted = xr3
        else:
            rot = pltpu.roll(xr3, s, 1)
            if sample:
                rot_prev = pltpu.roll(cpad3, s, 1)
            else:
                first = pltpu.roll(prev8, s, 0)[None]
                rot_prev = jnp.concatenate([first, rot[:-1]], axis=0) if groups > 1 else first
            shifted = jnp.where(t3 >= s, rot, rot_prev)
        xc3 = xc3 + shifted * cw[k:k + 1, :][None]
    xc = xc3.reshape(rows, D_B)

    xcb = xc.astype(BF16)
    gq = [_dot(xcb[:, q * MXU_DIM:(q + 1) * MXU_DIM], wg_ref[q]) for q in range(D_B // MXU_DIM)]
    bg = bg_ref[...]
    pre_a = jnp.concatenate([g[:, :MXU_DIM] for g in gq], axis=1) + bg[:, :D_B]
    pre_x = jnp.concatenate([g[:, MXU_DIM:] for g in gq], axis=1) + bg[:, D_B:]
    gate_a = jax.nn.sigmoid(pre_a)
    gate_x = jax.nn.sigmoid(pre_x)
    nl = -lam_ref[...]
    softplus = jnp.maximum(nl, 0.0) + jnp.log1p(jnp.exp(-jnp.abs(nl)))
    log_a = -LRU_C * gate_a * softplus
    a = jnp.exp(log_a)
    mult = jnp.sqrt(-jnp.tanh(log_a) * (1.0 + a * a))
    bx = mult * (gate_x * xc)

    a3 = a.reshape(groups, SUBLANES, D_B)
    b3 = bx.reshape(groups, SUBLANES, D_B)
    for d in (1, 2, 4):
        m = t3 >= d
        a_sh = jnp.where(m, pltpu.roll(a3, d, 1), 1.0)
        b_sh = jnp.where(m, pltpu.roll(b3, d, 1), 0.0)
        b3 = a3 * b_sh + b3
        a3 = a3 * a_sh
    if sample:
        hs3 = jnp.stack([a3[g] * h0_ref[g:g + 1, :] + b3[g] for g in range(groups)], axis=0)
    else:
        h_prev = hcar_ref[...][SUBLANES - 1:SUBLANES, :]
        hs_list = []
        for g in range(groups):
            hg = a3[g] * h_prev + b3[g]
            hs_list.append(hg)
            h_prev = hg[SUBLANES - 1:SUBLANES, :]
        hs3 = jnp.stack(hs_list, axis=0)
    hs = hs3.reshape(rows, D_B)

    out_b = hs * _gelu_tanh(yr)
    mix = _dot(out_a.astype(BF16), wout_ref[:D_A, :]) + _dot(out_b.astype(BF16), wout_ref[D_A:, :])
    y1 = modulated(lambda b, xv, mv: ALPHA * xv + (1.0 + g1(b)) * mv, x, mix)
    x1 = _layer_norm(y1, ln1g_ref[...], ln1b_ref[...])
    h2 = modulated(lambda b, xv: xv * (1.0 + sc2(b)) + sh2(b), x1)

    h2_hi = h2.astype(BF16)
    h2_lo = (h2 - h2_hi.astype(F32)).astype(BF16)
    logits = (_dot_nt(wrh_ref[...], h2_hi) + _dot_nt(wrh_ref[...], h2_lo)
              + _dot_nt(wrl_ref[...], h2_hi) + br_ref[...])

    x1_ref[...] = x1
    _store_token_tiles(h2_ref, h2)
    lg_ref[...] = logits
    if sample:
        xr_ref[...] = xr
        v_ref[...] = v
        hs_ref[...] = hs
    else:
        last_x = xr3[groups - 1]
        last_h = hs3[groups - 1]
        prev_ref[...] = last_x
        hcar_ref[...] = last_h
        cst_ref[0] = last_x
        hst_ref[0] = last_h


def _mixer_weights(w_in, wsm, bsf, lnv_g, lnv_b, conv_w, conv_b, wg, bg, lam, w_out, ln1_g, ln1_b,
                   wr_hi, wr_lo, br):
    arrs = [w_in, wsm, bsf, lnv_g, lnv_b, conv_w, conv_b, wg, bg, lam, w_out, ln1_g, ln1_b,
            wr_hi, wr_lo, br]
    return arrs, [_const_spec(a.shape) for a in arrs]


def _mixer_prompt(x, mod3, weights):
    batch, seq, _ = x.shape
    n_tokens = batch * seq
    tiles = seq // ROWS
    arrs, specs = weights
    row_spec = lambda w: pl.BlockSpec((ROWS, w), lambda b, j: (b * tiles + j, 0))
    st_spec = pl.BlockSpec((1, SUBLANES, D_B), lambda b, j: (b, 0, 0))
    return pl.pallas_call(
        functools.partial(_mixer_kernel, False),
        grid=(batch, tiles),
        in_specs=[pl.BlockSpec((1, ROWS, D_MODEL), lambda b, j: (b, j, 0)),
                  pl.BlockSpec((1, 1, 6 * D_MODEL), lambda b, j: (b, 0, 0))] + specs,
        out_specs=[row_spec(D_MODEL),
                   pl.BlockSpec((ROWS * TILE_ROWS, LANES), lambda b, j: (b * tiles + j, 0)),
                   pl.BlockSpec((LOGIT_ROWS, ROWS), lambda b, j: (0, b * tiles + j)), st_spec, st_spec],
        out_shape=[jax.ShapeDtypeStruct((n_tokens, D_MODEL), F32),
                   jax.ShapeDtypeStruct((n_tokens * TILE_ROWS, LANES), F32),
                   jax.ShapeDtypeStruct((LOGIT_ROWS, n_tokens), F32),
                   jax.ShapeDtypeStruct((batch, SUBLANES, D_B), F32),
                   jax.ShapeDtypeStruct((batch, SUBLANES, D_B), F32)],
        scratch_shapes=[pltpu.VMEM((SUBLANES, D_B), F32), pltpu.VMEM((SUBLANES, D_B), F32)],
        compiler_params=pltpu.CompilerParams(
            dimension_semantics=("arbitrary", "arbitrary"), vmem_limit_bytes=VMEM_LIMIT),
        name="mixer_prompt",
    )(x, mod3, *arrs)


def _mixer_sample(x2, mod_seq, weights, h0, cpad):
    n_rows = x2.shape[0]
    arrs, specs = weights
    row = lambda w: pl.BlockSpec((ROWS, w), lambda i: (i, 0))
    seq_row = lambda w: pl.BlockSpec((ROWS // SUBLANES, w), lambda i: (i, 0))
    return pl.pallas_call(
        functools.partial(_mixer_kernel, True),
        grid=(n_rows // ROWS,),
        in_specs=[row(D_MODEL), seq_row(6 * D_MODEL)] + specs + [seq_row(D_B), row(D_B)],
        out_specs=[row(D_MODEL), pl.BlockSpec((ROWS * TILE_ROWS, LANES), lambda i: (i, 0)),
                   pl.BlockSpec((LOGIT_ROWS, ROWS), lambda i: (0, i)),
                   row(D_B), row(D_B), row(D_B)],
        out_shape=[jax.ShapeDtypeStruct((n_rows, D_MODEL), F32),
                   jax.ShapeDtypeStruct((n_rows * TILE_ROWS, LANES), F32),
                   jax.ShapeDtypeStruct((LOGIT_ROWS, n_rows), F32),
                   jax.ShapeDtypeStruct((n_rows, D_B), F32),
                   jax.ShapeDtypeStruct((n_rows, D_B), F32),
                   jax.ShapeDtypeStruct((n_rows, D_B), F32)],
        compiler_params=pltpu.CompilerParams(
            dimension_semantics=("arbitrary",), vmem_limit_bytes=VMEM_LIMIT),
        name="mixer_sample",
    )(x2, mod_seq, *arrs, h0, cpad)


def _router_kernel(lg_ref, upper_ref, lower_ref, mi_ref, mf_ref, cnt_ref, tot_ref, run_ref):
    phase = pl.program_id(0)
    step = pl.program_id(1)
    l = lg_ref[...]
    cols = l.shape[1]
    row8 = lax.broadcasted_iota(jnp.int32, (SUBLANES, cols), 0)
    neg = -jnp.inf
    big = jnp.int32(SUBLANES)

    def first_max(vals):
        top = jnp.max(vals, axis=0, keepdims=True)
        return top, jnp.min(jnp.where(vals == top, row8, big), axis=0, keepdims=True)

    is_group = row8 < N_GROUPS
    lg = l[:SUBLANES]
    m, gsel = first_max(jnp.where(is_group, lg, neg))
    denom = jnp.sum(jnp.where(is_group, jnp.exp(lg - m), 0.0), axis=0, keepdims=True)
    pgsel = 1.0 / denom
    le = l[SUBLANES:2 * SUBLANES]
    for g in range(1, N_GROUPS):
        le = jnp.where(gsel == g, l[(g + 1) * SUBLANES:(g + 2) * SUBLANES], le)
    v1, i1 = first_max(le)
    v2, i2 = first_max(jnp.where(row8 == i1, neg, le))
    ex = jnp.exp(v2 - v1)
    wk1 = pgsel / (1.0 + ex)
    wk2 = pgsel * ex / (1.0 + ex)
    e1 = gsel * EXPERTS_PER_GROUP + i1
    e2 = gsel * EXPERTS_PER_GROUP + i2
    row_e = lax.broadcasted_iota(jnp.int32, (N_EXPERTS, cols), 0)
    oh1 = jnp.where(row_e == e1, 1.0, 0.0)
    oh2 = jnp.where(row_e == e2, 1.0, 0.0)
    oh = oh1 + oh2

    @pl.when(jnp.logical_and(phase == 0, step == 0))
    def _():
        tot_ref[...] = jnp.zeros_like(tot_ref)

    @pl.when(phase == 0)
    def _():
        tot_ref[...] = tot_ref[...] + jnp.sum(oh, axis=1, keepdims=True)

    @pl.when(jnp.logical_and(phase == 1, step == 0))
    def _():
        tot = tot_ref[...]
        hi = (tot * (1.0 / MXU_DIM)).astype(jnp.int32).astype(F32)
        lo = tot - hi * MXU_DIM
        run_ref[...] = (_dot(lower_ref[...], hi.astype(BF16)) * MXU_DIM
                        + _dot(lower_ref[...], lo.astype(BF16)))

    @pl.when(phase == 1)
    def _():
        blocks = cols // LANES
        stacked = jnp.concatenate([oh[:, b * LANES:(b + 1) * LANES] for b in range(blocks)], axis=0)
        local = _dot(stacked.astype(BF16), upper_ref[...])
        run = run_ref[...]
        p1, p2 = [], []
        for b in range(blocks):
            sl = slice(b * LANES, (b + 1) * LANES)
            before = local[b * N_EXPERTS:(b + 1) * N_EXPERTS] + run
            p1.append(jnp.sum(oh1[:, sl] * before, axis=0, keepdims=True))
            p2.append(jnp.sum(oh2[:, sl] * before, axis=0, keepdims=True))
            run = run + jnp.sum(oh[:, sl], axis=1, keepdims=True)
        run_ref[...] = run
        pos1 = jnp.concatenate(p1, axis=1).astype(jnp.int32)
        pos2 = jnp.concatenate(p2, axis=1).astype(jnp.int32)
        zero_i = jnp.zeros_like(row8)
        mi_ref[...] = jnp.where(row8 == 0, pos1, jnp.where(row8 == 1, pos2, jnp.where(
            row8 == 2, e1, jnp.where(row8 == 3, e2, zero_i))))
        mf_ref[...] = jnp.where(row8 == 0, wk1, jnp.where(row8 == 1, wk2, 0.0))
        cnt_ref[...] = tot_ref[...].astype(jnp.int32)


def _router_call(lg_t):
    n_tokens = lg_t.shape[1]
    tc = ROUTER_COLS
    r = lax.broadcasted_iota(jnp.int32, (LANES, LANES), 0)
    c = lax.broadcasted_iota(jnp.int32, (LANES, LANES), 1)
    upper = jnp.where(r < c, 1.0, 0.0).astype(BF16)
    lower = jnp.where(c < r, 1.0, 0.0)[:N_EXPERTS, :N_EXPERTS].astype(BF16)
    col = pl.BlockSpec((SUBLANES, tc), lambda p, i: (0, i * p))
    return pl.pallas_call(
        _router_kernel,
        grid=(2, n_tokens // tc),
        in_specs=[pl.BlockSpec((LOGIT_ROWS, tc), lambda p, i: (0, i)),
                  _const_spec((LANES, LANES)), _const_spec((N_EXPERTS, N_EXPERTS))],
        out_specs=[col, col, _const_spec((N_EXPERTS, LANES))],
        out_shape=[jax.ShapeDtypeStruct((SUBLANES, n_tokens), jnp.int32),
                   jax.ShapeDtypeStruct((SUBLANES, n_tokens), F32),
                   jax.ShapeDtypeStruct((N_EXPERTS, LANES), jnp.int32)],
        scratch_shapes=[pltpu.VMEM((N_EXPERTS, LANES), F32), pltpu.VMEM((N_EXPERTS, LANES), F32)],
        compiler_params=pltpu.CompilerParams(
            dimension_semantics=("arbitrary", "arbitrary"), vmem_limit_bytes=VMEM_LIMIT),
        name="router",
    )(lg_t, upper, lower)


def _ffn_plan(counts, n_pairs):
    ends = jnp.cumsum(counts)
    off = ends - counts
    ids = jnp.arange(N_EXPERTS, dtype=jnp.int32)

    n_tiles = n_pairs // FFN_ROWS
    n_items = n_tiles + N_EXPERTS
    first_tile = off // FFN_ROWS
    last_tile = (ends - 1) // FFN_ROWS
    items_e = jnp.where(counts > 0, last_tile - first_tile + 1, 0)
    items_end = jnp.cumsum(items_e)
    items_start = items_end - items_e
    n_work = items_end[-1]
    w = jnp.arange(n_items, dtype=jnp.int32)
    wv = jnp.minimum(w, n_work - 1)
    e_w = jnp.sum((items_end[None, :] <= wv[:, None]).astype(jnp.int32), axis=1)
    e_w = jnp.minimum(e_w, N_EXPERTS - 1)
    sel = e_w[:, None] == ids[None, :]
    pick = lambda a: jnp.sum(jnp.where(sel, a[None, :], 0), axis=1)
    t_w = pick(first_tile) + (wv - pick(items_start))
    lo_w = jnp.maximum(pick(off), t_w * FFN_ROWS) - t_w * FFN_ROWS
    hi_w = jnp.minimum(pick(ends), (t_w + 1) * FFN_ROWS) - t_w * FFN_ROWS
    valid = w < n_work
    lo_w = jnp.where(valid, lo_w, 0)
    hi_w = jnp.where(valid, hi_w, 0)
    used = counts > 0
    order_e = jnp.cumsum(used.astype(jnp.int32)) - 1
    later_used = jnp.logical_and(used[None, :], ids[None, :] > ids[:, None])
    next_e = jnp.min(jnp.where(later_used, ids[None, :], N_EXPERTS), axis=1)
    next_e = jnp.where(next_e == N_EXPERTS, -1, next_e)
    i32 = lambda a: a.astype(jnp.int32)
    return i32(t_w), i32(e_w), i32(lo_w), i32(hi_w), i32(pick(order_e)), i32(pick(next_e))


def _scatter_kernel(tiles_p, pos1_s, pos2_s, h2p_ref, h2s_ref, xs_hbm, sem):
    i = pl.program_id(0)
    base = i * DISPATCH_ROWS

    def scatter_rows(src_ref):
        def row_copy(r, pos_s):
            p = pos_s[base + r]
            return pltpu.make_async_copy(src_ref.at[_token_tile(r), :], xs_hbm.at[_token_tile(p), :],
                                         sem)

        def start(r, carry):
            row_copy(r, pos1_s).start(priority=0)
            row_copy(r, pos2_s).start(priority=1)
            return carry

        def wait(r, carry):
            row_copy(r, pos1_s).wait()
            row_copy(r, pos2_s).wait()
            return carry

        lax.fori_loop(0, DISPATCH_ROWS, start, 0, unroll=32)
        lax.fori_loop(0, DISPATCH_ROWS, wait, 0, unroll=8)

    @pl.when(i < tiles_p)
    def _():
        scatter_rows(h2p_ref)

    @pl.when(i >= tiles_p)
    def _():
        scatter_rows(h2s_ref)


def _scatter_call(pos1, pos2, h2_p, h2_s):
    tm = DISPATCH_ROWS
    tiles_p = h2_p.shape[0] // (tm * TILE_ROWS)
    tiles_s = h2_s.shape[0] // (tm * TILE_ROWS)
    n_pairs = TOP_K * (h2_p.shape[0] + h2_s.shape[0]) // TILE_ROWS
    blk = (tm * TILE_ROWS, LANES)
    return pl.pallas_call(
        functools.partial(_scatter_kernel, tiles_p),
        grid_spec=pltpu.PrefetchScalarGridSpec(
            num_scalar_prefetch=2,
            grid=(tiles_p + tiles_s,),
            in_specs=[pl.BlockSpec(blk, lambda i, *_: (jnp.minimum(i, tiles_p - 1), 0)),
                      pl.BlockSpec(blk, lambda i, *_: (jnp.maximum(i - tiles_p, 0), 0))],
            out_specs=pl.BlockSpec(memory_space=pl.ANY),
            scratch_shapes=[pltpu.SemaphoreType.DMA(())],
        ),
        out_shape=jax.ShapeDtypeStruct((n_pairs * TILE_ROWS, LANES), F32),
        compiler_params=pltpu.CompilerParams(
            dimension_semantics=("arbitrary",), vmem_limit_bytes=VMEM_LIMIT),
        name="dispatch_scatter",
    )(pos1, pos2, h2_p, h2_s)


def _ffn_kernel(n_items, wt_s, we_s, wlo_s, whi_s, word_s, wnext_s, x_hbm, w1_hbm, w3_hbm, w2_hbm,
                o_ref, xbuf, w1f_ref, w3f_ref, w2f_ref, w13b_ref, w2b_ref, xsems, sems):
    w = pl.program_id(0)
    e = we_s[w]
    e_prev = we_s[jnp.maximum(w - 1, 0)]
    slot = word_s[w] % 2
    tile_rows = FFN_ROWS * TILE_ROWS

    def x_copy(item):
        first = pl.multiple_of(wt_s[item] * tile_rows, tile_rows)
        return pltpu.make_async_copy(x_hbm.at[pl.ds(first, tile_rows), :], xbuf.at[item % X_SLOTS],
                                     xsems.at[item % X_SLOTS])

    @pl.when(w == 0)
    def _():
        for ahead in range(X_SLOTS - 1):
            x_copy(ahead).start()

    @pl.when(w + X_SLOTS - 1 < n_items)
    def _():
        x_copy(w + X_SLOTS - 1).start()

    def weight_copies(expert, buf):
        return [pltpu.make_async_copy(src.at[expert], dst.at[buf], sems.at[buf])
                for src, dst in ((w1_hbm, w1f_ref), (w3_hbm, w3f_ref), (w2_hbm, w2f_ref))]

    @pl.when(w == 0)
    def _():
        for cp in weight_copies(e, slot):
            cp.start()

    @pl.when(jnp.logical_or(w == 0, e != e_prev))
    def _():
        for cp in weight_copies(e, slot):
            cp.wait()
        w13b_ref[:, :D_EXPERT] = w1f_ref[slot].astype(BF16)
        w13b_ref[:, D_EXPERT:] = w3f_ref[slot].astype(BF16)
        w2b_ref[...] = w2f_ref[slot].astype(BF16)
        nxt = wnext_s[w]

        @pl.when(nxt >= 0)
        def _():
            for cp in weight_copies(nxt, 1 - slot):
                cp.start()

    lo = wlo_s[w]
    hi = whi_s[w]
    x_copy(w).wait()

    @pl.when(hi > lo)
    def _():
        x = _load_token_tiles(xbuf, (w % X_SLOTS,), FFN_ROWS)
        h13 = _dot(x.astype(BF16), w13b_ref[...])
        a = h13[:, :D_EXPERT]
        b = h13[:, D_EXPERT:]
        hdn = (a * jax.nn.sigmoid(a)) * b
        y = _dot(hdn.astype(BF16), w2b_ref[...])
        rows = lax.broadcasted_iota(jnp.int32, (FFN_ROWS, LANES), 0)
        mine = jnp.logical_and(rows >= lo, rows < hi)

        @pl.when(lo == 0)
        def _():
            _store_token_tiles(o_ref, y)

        @pl.when(lo > 0)
        def _():
            _store_token_tiles(o_ref, y, keep=mine)


def _ffn_call(plan, xs, w1, w3, w2):
    t_w = plan[0]
    n_pairs = xs.shape[0] // TILE_ROWS
    tm = FFN_ROWS
    any_spec = pl.BlockSpec(memory_space=pl.ANY)
    row = pl.BlockSpec((tm * TILE_ROWS, LANES), lambda i, wt, *_: (wt[i], 0))
    n_items = t_w.shape[0]
    assert n_items >= X_SLOTS
    return pl.pallas_call(
        functools.partial(_ffn_kernel, n_items),
        grid_spec=pltpu.PrefetchScalarGridSpec(
            num_scalar_prefetch=len(plan),
            grid=(n_items,),
            in_specs=[any_spec, any_spec, any_spec, any_spec],
            out_specs=row,
            scratch_shapes=[pltpu.VMEM((X_SLOTS, tm * TILE_ROWS, LANES), F32),
                            pltpu.VMEM((2, D_MODEL, D_EXPERT), F32),
                            pltpu.VMEM((2, D_MODEL, D_EXPERT), F32),
                            pltpu.VMEM((2, D_EXPERT, D_MODEL), F32),
                            pltpu.VMEM((D_MODEL, 2 * D_EXPERT), BF16),
                            pltpu.VMEM((D_EXPERT, D_MODEL), BF16),
                            pltpu.SemaphoreType.DMA((X_SLOTS,)),
                            pltpu.SemaphoreType.DMA((2,))],
        ),
        out_shape=jax.ShapeDtypeStruct((n_pairs * TILE_ROWS, LANES), F32),
        compiler_params=pltpu.CompilerParams(
            dimension_semantics=("arbitrary",), vmem_limit_bytes=VMEM_LIMIT),
        name="expert_ffn",
    )(*plan, xs, w1, w3, w2)


def _combine_kernel(three_d, row0, n_steps, pos1_s, pos2_s, x1_ref, wk_ref, mod_ref, g_ref, b_ref,
                    ys_hbm, o_ref, ybuf, sems):
    step = pl.program_id(0)
    slot = step % 2

    def gather(tile, tile_slot):
        base = row0 + tile * COMBINE_ROWS

        def row_copy(r, k, pos_s):
            p = pos_s[base + r]
            return pltpu.make_async_copy(ys_hbm.at[_token_tile(p), :],
                                         ybuf.at[tile_slot, k, _token_tile(r), :], sems.at[tile_slot])

        def start(r, carry):
            row_copy(r, 0, pos1_s).start(priority=0)
            row_copy(r, 1, pos2_s).start(priority=1)
            return carry

        def wait(r, carry):
            row_copy(r, 0, pos1_s).wait()
            row_copy(r, 1, pos2_s).wait()
            return carry

        return start, wait

    @pl.when(step == 0)
    def _():
        lax.fori_loop(0, COMBINE_ROWS, gather(step, slot)[0], 0, unroll=32)

    @pl.when(step + 1 < n_steps)
    def _():
        lax.fori_loop(0, COMBINE_ROWS, gather(step + 1, 1 - slot)[0], 0, unroll=32)

    lax.fori_loop(0, COMBINE_ROWS, gather(step, slot)[1], 0, unroll=8)

    wk = wk_ref[...]
    r = lax.broadcasted_iota(jnp.int32, (COMBINE_ROWS, COMBINE_ROWS), 0)
    c = lax.broadcasted_iota(jnp.int32, (COMBINE_ROWS, COMBINE_ROWS), 1)
    wk_col = [jnp.sum(jnp.where(r == c, wk[k:k + 1, :], 0.0), axis=1, keepdims=True)
              for k in range(TOP_K)]
    moe = (wk_col[0] * _load_token_tiles(ybuf, (slot, 0), COMBINE_ROWS)
           + wk_col[1] * _load_token_tiles(ybuf, (slot, 1), COMBINE_ROWS))
    x1 = x1_ref[...]
    if three_d:
        g2 = mod_ref[0, :, 5 * D_MODEL:]
        o_ref[0] = _layer_norm(ALPHA * x1 + (1.0 + g2) * moe, g_ref[...], b_ref[...])
    else:
        def seq_rows(v, b):
            return v[b * SUBLANES:(b + 1) * SUBLANES]

        resid = jnp.concatenate([
            ALPHA * seq_rows(x1, b) + (1.0 + mod_ref[b:b + 1, 5 * D_MODEL:]) * seq_rows(moe, b)
            for b in range(COMBINE_ROWS // SUBLANES)], axis=0)
        o_ref[...] = _layer_norm(resid, g_ref[...], b_ref[...])


def _combine_call(three_d, pos1, pos2, x1_all, mf_all, mod, ln2_g, ln2_b, ys, row0, batch, seq):
    tm = COMBINE_ROWS
    blk0 = row0 // tm
    if three_d:
        tiles = seq // tm
        grid = (batch * tiles,)
        mod_spec = pl.BlockSpec((1, 1, 6 * D_MODEL), lambda i, *_: (i // tiles, 0, 0))
        out_spec = pl.BlockSpec((1, tm, D_MODEL), lambda i, *_: (i // tiles, i % tiles, 0))
        out_shape = jax.ShapeDtypeStruct((batch, seq, D_MODEL), F32)
    else:
        grid = (batch * seq // tm,)
        mod_spec = pl.BlockSpec((tm // SUBLANES, 6 * D_MODEL), lambda i, *_: (i, 0))
        out_spec = pl.BlockSpec((tm, D_MODEL), lambda i, *_: (i, 0))
        out_shape = jax.ShapeDtypeStruct((batch * seq, D_MODEL), F32)
    return pl.pallas_call(
        functools.partial(_combine_kernel, three_d, row0, grid[0]),
        grid_spec=pltpu.PrefetchScalarGridSpec(
            num_scalar_prefetch=2,
            grid=grid,
            in_specs=[pl.BlockSpec((tm, D_MODEL), lambda i, *_: (i, 0)),
                      pl.BlockSpec((SUBLANES, tm), lambda i, *_: (0, blk0 + i)),
                      mod_spec,
                      pl.BlockSpec(ln2_g.shape, lambda i, *_: (0, 0)),
                      pl.BlockSpec(ln2_b.shape, lambda i, *_: (0, 0)),
                      pl.BlockSpec(memory_space=pl.ANY)],
            out_specs=out_spec,
            scratch_shapes=[pltpu.VMEM((2, TOP_K, tm * TILE_ROWS, LANES), F32),
                            pltpu.SemaphoreType.DMA((2,))],
        ),
        out_shape=out_shape,
        compiler_params=pltpu.CompilerParams(
            dimension_semantics=("arbitrary",), vmem_limit_bytes=VMEM_LIMIT),
        name="combine_prompt" if three_d else "combine_sample",
    )(pos1, pos2, x1_all, mf_all, mod, ln2_g, ln2_b, ys)


def _block_diag(w):
    heads, n, _ = w.shape
    eye = jnp.eye(heads, dtype=w.dtype)
    return (eye[:, None, :, None] * w[:, :, None, :]).reshape(heads * n, heads * n)


def kernel(x_prompt, x_sample, state_rglru_h, state_conv, c_prompt, c_sample, w_ada, b_ada, w_in, w_s, b_s, lnv_g, lnv_b, conv_w, conv_b, lru_wa, lru_ba, lru_wx, lru_bx, lru_lam, w_out, ln1_g, ln1_b, w_rg, b_rg, w_re, b_re, w1, w3, w2, ln2_g, ln2_b):
    batch, seq, _ = x_prompt.shape
    dec_batch, dec_seq, _ = x_sample.shape
    assert dec_seq == SUBLANES and seq % ROWS == 0 and (dec_batch * dec_seq) % ROWS == 0
    n_prompt = batch * seq
    n_sample = dec_batch * dec_seq
    n_tokens = n_prompt + n_sample
    n_pairs = TOP_K * n_tokens
    assert n_tokens % ROUTER_COLS == 0 and n_pairs % FFN_ROWS == 0
    l = 0

    mod_p, mod_s = _ada_call(c_prompt, c_sample, w_ada[l], b_ada[l][None])
    mod_p3 = mod_p[:, None, :]

    tri = jnp.tril(jnp.ones((CHUNK, CHUNK), dtype=bool))
    ws_tri = jnp.where(tri[None], w_s[l], 0.0)
    reps_p = MXU_DIM // CHUNK
    eye_p = jnp.eye(reps_p, dtype=F32)
    wsm_p = (eye_p[None, :, None, :, None] * ws_tri[:, None, :, None, :]).reshape(
        A_HEADS, MXU_DIM, MXU_DIM).astype(BF16)
    reps_s = MXU_DIM // dec_seq
    pos_in_seq = jnp.arange(MXU_DIM, dtype=jnp.int32) % dec_seq
    seq_of_row = jnp.arange(MXU_DIM, dtype=jnp.int32) // dec_seq
    spread = (pos_in_seq[:, None] == jnp.arange(dec_seq, dtype=jnp.int32)[None, :]).astype(F32)
    same_seq = seq_of_row[:, None] == seq_of_row[None, :]
    wsm_s = jnp.einsum("ri,hij,cj->hrc", spread, ws_tri[:, :dec_seq, :dec_seq], spread,
                       precision=lax.Precision.HIGHEST)
    wsm_s = jnp.where(same_seq[None], wsm_s, 0.0).astype(BF16)
    bs_pos = jnp.repeat(jnp.transpose(b_s[l]), A_HEAD_DIM, axis=1)
    bsf_p = jnp.tile(bs_pos, (reps_p, 1))
    bsf_s = jnp.tile(bs_pos[:dec_seq], (reps_s, 1))

    wa_bd = _block_diag(lru_wa[l])
    wx_bd = _block_diag(lru_wx[l])
    wg = jnp.stack([
        jnp.concatenate([wa_bd[q * MXU_DIM:(q + 1) * MXU_DIM, q * MXU_DIM:(q + 1) * MXU_DIM],
                         wx_bd[q * MXU_DIM:(q + 1) * MXU_DIM, q * MXU_DIM:(q + 1) * MXU_DIM]], axis=1)
        for q in range(D_B // MXU_DIM)]).astype(BF16)
    bg = jnp.concatenate([lru_ba[l].reshape(1, D_B), lru_bx[l].reshape(1, D_B)], axis=1)

    group_pad = jnp.zeros((SUBLANES - N_GROUPS, D_MODEL), F32)
    wr = jnp.concatenate([jnp.transpose(w_rg[l]), group_pad, jnp.transpose(w_re[l])], axis=0)
    wr_hi = wr.astype(BF16)
    wr_lo = (wr - wr_hi.astype(F32)).astype(BF16)
    br = jnp.concatenate([b_rg[l], jnp.zeros((SUBLANES - N_GROUPS,), F32), b_re[l]])[:, None]

    common = dict(lnv_g=lnv_g[l][None], lnv_b=lnv_b[l][None], conv_w=conv_w[l], conv_b=conv_b[l][None],
                  wg=wg, bg=bg, lam=lru_lam[l][None], w_out=w_out[l].astype(BF16),
                  ln1_g=ln1_g[l][None], ln1_b=ln1_b[l][None], wr_hi=wr_hi, wr_lo=wr_lo, br=br)
    w_in_b = w_in[l].astype(BF16)
    weights_p = _mixer_weights(w_in_b, wsm_p, bsf_p, **common)
    weights_s = _mixer_weights(w_in_b, wsm_s, bsf_s, **common)

    x1_p, h2_p, lg_p, cst_p, hst_p = _mixer_prompt(x_prompt, mod_p3, weights_p)

    cpad = jnp.concatenate(
        [jnp.zeros((dec_batch, SUBLANES - (CONV_W - 1), D_B), F32), state_conv[l]], axis=1
    ).reshape(n_sample, D_B)
    x1_s, h2_s, lg_s, xr_s, v_s, hs_s = _mixer_sample(
        x_sample.reshape(n_sample, D_MODEL), mod_s, weights_s, state_rglru_h[l], cpad)

    meta_i, meta_f, counts = _router_call(jnp.concatenate([lg_p, lg_s], axis=1))
    pos1, pos2 = meta_i[0], meta_i[1]
    xs = _scatter_call(pos1, pos2, h2_p, h2_s)
    ys = _ffn_call(_ffn_plan(counts[:, 0], n_pairs), xs, w1[l], w3[l], w2[l])

    ln2g, ln2b = ln2_g[l][None], ln2_b[l][None]
    y_prompt = _combine_call(True, pos1, pos2, x1_p, meta_f, mod_p3, ln2g, ln2b, ys,
                             0, batch, seq)
    y_sample = _combine_call(False, pos1, pos2, x1_s, meta_f, mod_s, ln2g, ln2b, ys,
                             n_prompt, dec_batch, dec_seq).reshape(dec_batch, dec_seq, D_MODEL)

    new_h_prompt = hst_p[:, SUBLANES - 1, :][None]
    new_conv_prompt = cst_p[:, SUBLANES - (CONV_W - 1):, :][None]
    xr_s3 = xr_s.reshape(dec_batch, dec_seq, D_B)
    new_h_sample = hs_s.reshape(dec_batch, dec_seq, D_B)[:, dec_seq - 1, :][None]
    new_conv_sample = xr_s3[:, dec_seq - (CONV_W - 1):, :][None]
    new_chunk_v_sample = v_s.reshape(dec_batch, dec_seq, D_A)[None]
    return (y_prompt, y_sample, new_h_prompt, new_conv_prompt, new_h_sample, new_conv_sample,
            new_chunk_v_sample)
```

```python
import functools
import math

import jax
import jax.numpy as jnp
from jax import lax
from jax.experimental import pallas as pl
from jax.experimental.pallas import tpu as pltpu

D_MODEL = 1024
D_A = 512
A_HEADS = 4
A_HEAD_DIM = 128
CHUNK = 128
D_B = 512
B_HEADS = 8
B_HEAD_DIM = 64
CONV_W = 4
LRU_C = 8.0
D_IN = 2 * D_A + 2 * D_B
N_GROUPS = 4
EXPERTS_PER_GROUP = 8
N_EXPERTS = 32
TOP_K = 2
D_EXPERT = 512
LN_EPS = 1e-5
DEPTH = 1
ALPHA = (2.0 * DEPTH) ** 0.25

LANES = 128
SUBLANES = 8
MXU_DIM = 256
VMEM_LIMIT = 56 * 1024 * 1024

ROWS = 512
ROUTER_COLS = 1024
LOGIT_ROWS = SUBLANES + N_EXPERTS
DISPATCH_ROWS = 1024
FFN_ROWS = 256
X_SLOTS = 3
COMBINE_ROWS = 256
BF16 = jnp.bfloat16
F32 = jnp.float32


def _dot(a, b):
    return jnp.dot(a, b, preferred_element_type=F32)


def _dot_nt(a, b):
    return lax.dot_general(a, b, (((1,), (1,)), ((), ())), preferred_element_type=F32)


TILE_ROWS = D_MODEL // LANES


def _token_tile(r):
    return pl.ds(pl.multiple_of(r * TILE_ROWS, TILE_ROWS), TILE_ROWS)


def _load_token_tiles(ref, lead, n_rows):
    parts = [ref[lead + (pl.ds(s, n_rows, stride=TILE_ROWS), slice(None))] for s in range(TILE_ROWS)]
    return jnp.concatenate(parts, axis=1)


def _store_token_tiles(ref, val, keep=None):
    n_rows = val.shape[0]
    for s in range(TILE_ROWS):
        rows = pl.ds(s, n_rows, stride=TILE_ROWS)
        new = val[:, s * LANES:(s + 1) * LANES]
        if keep is not None:
            new = jnp.where(keep, new, ref[rows, :])
        ref[rows, :] = new


def _layer_norm(x, g, b):
    mu = jnp.mean(x, axis=-1, keepdims=True)
    xc = x - mu
    var = jnp.mean(xc * xc, axis=-1, keepdims=True)
    return xc * lax.rsqrt(var + LN_EPS) * g + b


def _gelu_tanh(x):
    c = math.sqrt(2.0 / math.pi)
    return x * (0.5 * (1.0 + jnp.tanh(c * (x + 0.044715 * (x * x * x)))))


def _const_spec(shape):
    nd = len(shape)
    return pl.BlockSpec(shape, lambda *_: (0,) * nd)


def _ada_kernel(cp_ref, cs_ref, w_ref, b_ref, op_ref, os_ref):
    w = w_ref[...].astype(BF16)
    for c_ref, o_ref in ((cp_ref, op_ref), (cs_ref, os_ref)):
        c = c_ref[...]
        a = (c * jax.nn.sigmoid(c)).astype(BF16)
        o_ref[...] = _dot(a, w) + b_ref[...]


def _ada_call(c_prompt, c_sample, w_ada, b_ada):
    tn = 1536
    rows_p, rows_s = c_prompt.shape[0], c_sample.shape[0]
    return pl.pallas_call(
        _ada_kernel,
        grid=(6 * D_MODEL // tn,),
        in_specs=[
            pl.BlockSpec((rows_p, D_MODEL), lambda j: (0, 0)),
            pl.BlockSpec((rows_s, D_MODEL), lambda j: (0, 0)),
            pl.BlockSpec((D_MODEL, tn), lambda j: (0, j)),
            pl.BlockSpec((1, tn), lambda j: (0, j)),
        ],
        out_specs=[pl.BlockSpec((rows_p, tn), lambda j: (0, j)),
                   pl.BlockSpec((rows_s, tn), lambda j: (0, j))],
        out_shape=[jax.ShapeDtypeStruct((rows_p, 6 * D_MODEL), F32),
                   jax.ShapeDtypeStruct((rows_s, 6 * D_MODEL), F32)],
        compiler_params=pltpu.CompilerParams(
            dimension_semantics=("arbitrary",), vmem_limit_bytes=VMEM_LIMIT),
        name="ada_mod",
    )(c_prompt, c_sample, w_ada, b_ada)


def _mixer_kernel(sample, *refs):
    if sample:
        (x_ref, mod_ref, win_ref, wsm_ref, bsf_ref, lnvg_ref, lnvb_ref, convw_ref, convb_ref,
         wg_ref, bg_ref, lam_ref, wout_ref, ln1g_ref, ln1b_ref, wrh_ref, wrl_ref, br_ref,
         h0_ref, cpad_ref,
         x1_ref, h2_ref, lg_ref, xr_ref, v_ref, hs_ref) = refs
    else:
        (x_ref, mod_ref, win_ref, wsm_ref, bsf_ref, lnvg_ref, lnvb_ref, convw_ref, convb_ref,
         wg_ref, bg_ref, lam_ref, wout_ref, ln1g_ref, ln1b_ref, wrh_ref, wrl_ref, br_ref,
         x1_ref, h2_ref, lg_ref, cst_ref, hst_ref, prev_ref, hcar_ref) = refs

    rows = ROWS
    groups = rows // SUBLANES

    if sample:
        x = x_ref[...]

        def chunk(c):
            return lambda b: mod_ref[b:b + 1, c * D_MODEL:(c + 1) * D_MODEL]
    else:
        x = x_ref[0]

        def chunk(c):
            return lambda b: mod_ref[0, :, c * D_MODEL:(c + 1) * D_MODEL]

        @pl.when(pl.program_id(1) == 0)
        def _():
            prev_ref[...] = jnp.zeros_like(prev_ref)
            hcar_ref[...] = jnp.zeros_like(hcar_ref)

    def modulated(fn, *vals):
        if not sample:
            return fn(0, *vals)
        parts = [fn(b, *(v[b * SUBLANES:(b + 1) * SUBLANES] for v in vals)) for b in range(groups)]
        return jnp.concatenate(parts, axis=0)

    sh1, sc1, g1, sh2, sc2, g2 = [chunk(c) for c in range(6)]

    h = modulated(lambda b, xv: xv * (1.0 + sc1(b)) + sh1(b), x)
    proj = _dot(h.astype(BF16), win_ref[...])
    u = proj[:, :D_A]
    v = _layer_norm(proj[:, D_A:2 * D_A], lnvg_ref[...], lnvb_ref[...])
    xr = proj[:, 2 * D_A:2 * D_A + D_B]
    yr = proj[:, 2 * D_A + D_B:]

    vb = v.astype(BF16)
    halves = []
    for r in range(rows // MXU_DIM):
        heads = [
            _dot(wsm_ref[hd], vb[r * MXU_DIM:(r + 1) * MXU_DIM, hd * A_HEAD_DIM:(hd + 1) * A_HEAD_DIM])
            for hd in range(A_HEADS)
        ]
        halves.append(jnp.concatenate(heads, axis=1) + bsf_ref[...])
    mixed = halves[0] if len(halves) == 1 else jnp.concatenate(halves, axis=0)
    out_a = u * mixed

    xr3 = xr.reshape(groups, SUBLANES, D_B)
    t3 = lax.broadcasted_iota(jnp.int32, (1, SUBLANES, D_B), 1)
    cw = convw_ref[...]
    if sample:
        cpad3 = cpad_ref[...].reshape(groups, SUBLANES, D_B)
    else:
        prev8 = prev_ref[...]
    xc3 = convb_ref[...][None]
    for k in range(CONV_W):
        s = CONV_W - 1 - k
        if s == 0:
            shifted = xr3
        else:
            rot = pltpu.roll(xr3, s, 1)
            if sample:
                rot_prev = pltpu.roll(cpad3, s, 1)
            else:
                first = pltpu.roll(prev8, s, 0)[None]
                rot_prev = jnp.concatenate([first, rot[:-1]], axis=0) if groups > 1 else first
            shifted = jnp.where(t3 >= s, rot, rot_prev)
        xc3 = xc3 + shifted * cw[k:k + 1, :][None]
    xc = xc3.reshape(rows, D_B)

    xcb = xc.astype(BF16)
    gq = [_dot(xcb[:, q * MXU_DIM:(q + 1) * MXU_DIM], wg_ref[q]) for q in range(D_B // MXU_DIM)]
    bg = bg_ref[...]
    pre_a = jnp.concatenate([g[:, :MXU_DIM] for g in gq], axis=1) + bg[:, :D_B]
    pre_x = jnp.concatenate([g[:, MXU_DIM:] for g in gq], axis=1) + bg[:, D_B:]
    gate_a = jax.nn.sigmoid(pre_a)
    gate_x = jax.nn.sigmoid(pre_x)
    nl = -lam_ref[...]
    softplus = jnp.maximum(nl, 0.0) + jnp.log1p(jnp.exp(-jnp.abs(nl)))
    log_a = -LRU_C * gate_a * softplus
    a = jnp.exp(log_a)
    mult = jnp.sqrt(-jnp.tanh(log_a) * (1.0 + a * a))
    bx = mult * (gate_x * xc)

    a3 = a.reshape(groups, SUBLANES, D_B)
    b3 = bx.reshape(groups, SUBLANES, D_B)
    for d in (1, 2, 4):
        m = t3 >= d
        a_sh = jnp.where(m, pltpu.roll(a3, d, 1), 1.0)
        b_sh = jnp.where(m, pltpu.roll(b3, d, 1), 0.0)
        b3 = a3 * b_sh + b3
        a3 = a3 * a_sh
    if sample:
        hs3 = jnp.stack([a3[g] * h0_ref[g:g + 1, :] + b3[g] for g in range(groups)], axis=0)
    else:
        h_prev = hcar_ref[...][SUBLANES - 1:SUBLANES, :]
        hs_list = []
        for g in range(groups):
            hg = a3[g] * h_prev + b3[g]
            hs_list.append(hg)
            h_prev = hg[SUBLANES - 1:SUBLANES, :]
        hs3 = jnp.stack(hs_list, axis=0)
    hs = hs3.reshape(rows, D_B)

    out_b = hs * _gelu_tanh(yr)
    mix = _dot(out_a.astype(BF16), wout_ref[:D_A, :]) + _dot(out_b.astype(BF16), wout_ref[D_A:, :])
    y1 = modulated(lambda b, xv, mv: ALPHA * xv + (1.0 + g1(b)) * mv, x, mix)
    x1 = _layer_norm(y1, ln1g_ref[...], ln1b_ref[...])
    h2 = modulated(lambda b, xv: xv * (1.0 + sc2(b)) + sh2(b), x1)

    h2_hi = h2.astype(BF16)
    h2_lo = (h2 - h2_hi.astype(F32)).astype(BF16)
    logits = (_dot_nt(wrh_ref[...], h2_hi) + _dot_nt(wrh_ref[...], h2_lo)
              + _dot_nt(wrl_ref[...], h2_hi) + br_ref[...])

    x1_ref[...] = x1
    _store_token_tiles(h2_ref, h2)
    lg_ref[...] = logits
    if sample:
        xr_ref[...] = xr
        v_ref[...] = v
        hs_ref[...] = hs
    else:
        last_x = xr3[groups - 1]
        last_h = hs3[groups - 1]
        prev_ref[...] = last_x
        hcar_ref[...] = last_h
        cst_ref[0] = last_x
        hst_ref[0] = last_h


def _mixer_weights(w_in, wsm, bsf, lnv_g, lnv_b, conv_w, conv_b, wg, bg, lam, w_out, ln1_g, ln1_b,
                   wr_hi, wr_lo, br):
    arrs = [w_in, wsm, bsf, lnv_g, lnv_b, conv_w, conv_b, wg, bg, lam, w_out, ln1_g, ln1_b,
            wr_hi, wr_lo, br]
    return arrs, [_const_spec(a.shape) for a in arrs]


def _mixer_prompt(x, mod3, weights):
    batch, seq, _ = x.shape
    n_tokens = batch * seq
    tiles = seq // ROWS
    arrs, specs = weights
    row_spec = lambda w: pl.BlockSpec((ROWS, w), lambda b, j: (b * tiles + j, 0))
    st_spec = pl.BlockSpec((1, SUBLANES, D_B), lambda b, j: (b, 0, 0))
    return pl.pallas_call(
        functools.partial(_mixer_kernel, False),
        grid=(batch, tiles),
        in_specs=[pl.BlockSpec((1, ROWS, D_MODEL), lambda b, j: (b, j, 0)),
                  pl.BlockSpec((1, 1, 6 * D_MODEL), lambda b, j: (b, 0, 0))] + specs,
        out_specs=[row_spec(D_MODEL),
                   pl.BlockSpec((ROWS * TILE_ROWS, LANES), lambda b, j: (b * tiles + j, 0)),
                   pl.BlockSpec((LOGIT_ROWS, ROWS), lambda b, j: (0, b * tiles + j)), st_spec, st_spec],
        out_shape=[jax.ShapeDtypeStruct((n_tokens, D_MODEL), F32),
                   jax.ShapeDtypeStruct((n_tokens * TILE_ROWS, LANES), F32),
                   jax.ShapeDtypeStruct((LOGIT_ROWS, n_tokens), F32),
                   jax.ShapeDtypeStruct((batch, SUBLANES, D_B), F32),
                   jax.ShapeDtypeStruct((batch, SUBLANES, D_B), F32)],
        scratch_shapes=[pltpu.VMEM((SUBLANES, D_B), F32), pltpu.VMEM((SUBLANES, D_B), F32)],
        compiler_params=pltpu.CompilerParams(
            dimension_semantics=("arbitrary", "arbitrary"), vmem_limit_bytes=VMEM_LIMIT),
        name="mixer_prompt",
    )(x, mod3, *arrs)


def _mixer_sample(x2, mod_seq, weights, h0, cpad):
    n_rows = x2.shape[0]
    arrs, specs = weights
    row = lambda w: pl.BlockSpec((ROWS, w), lambda i: (i, 0))
    seq_row = lambda w: pl.BlockSpec((ROWS // SUBLANES, w), lambda i: (i, 0))
    return pl.pallas_call(
        functools.partial(_mixer_kernel, True),
        grid=(n_rows // ROWS,),
        in_specs=[row(D_MODEL), seq_row(6 * D_MODEL)] + specs + [seq_row(D_B), row(D_B)],
        out_specs=[row(D_MODEL), pl.BlockSpec((ROWS * TILE_ROWS, LANES), lambda i: (i, 0)),
                   pl.BlockSpec((LOGIT_ROWS, ROWS), lambda i: (0, i)),
                   row(D_B), row(D_B), row(D_B)],
        out_shape=[jax.ShapeDtypeStruct((n_rows, D_MODEL), F32),
                   jax.ShapeDtypeStruct((n_rows * TILE_ROWS, LANES), F32),
                   jax.ShapeDtypeStruct((LOGIT_ROWS, n_rows), F32),
                   jax.ShapeDtypeStruct((n_rows, D_B), F32),
                   jax.ShapeDtypeStruct((n_rows, D_B), F32),
                   jax.ShapeDtypeStruct((n_rows, D_B), F32)],
        compiler_params=pltpu.CompilerParams(
            dimension_semantics=("arbitrary",), vmem_limit_bytes=VMEM_LIMIT),
        name="mixer_sample",
    )(x2, mod_seq, *arrs, h0, cpad)


def _router_kernel(lg_ref, upper_ref, lower_ref, mi_ref, mf_ref, cnt_ref, tot_ref, run_ref):
    phase = pl.program_id(0)
    step = pl.program_id(1)
    l = lg_ref[...]
    cols = l.shape[1]
    row8 = lax.broadcasted_iota(jnp.int32, (SUBLANES, cols), 0)
    neg = -jnp.inf
    big = jnp.int32(SUBLANES)

    def first_max(vals):
        top = jnp.max(vals, axis=0, keepdims=True)
        return top, jnp.min(jnp.where(vals == top, row8, big), axis=0, keepdims=True)

    is_group = row8 < N_GROUPS
    lg = l[:SUBLANES]
    m, gsel = first_max(jnp.where(is_group, lg, neg))
    denom = jnp.sum(jnp.where(is_group, jnp.exp(lg - m), 0.0), axis=0, keepdims=True)
    pgsel = 1.0 / denom
    le = l[SUBLANES:2 * SUBLANES]
    for g in range(1, N_GROUPS):
        le = jnp.where(gsel == g, l[(g + 1) * SUBLANES:(g + 2) * SUBLANES], le)
    v1, i1 = first_max(le)
    v2, i2 = first_max(jnp.where(row8 == i1, neg, le))
    ex = jnp.exp(v2 - v1)
    wk1 = pgsel / (1.0 + ex)
    wk2 = pgsel * ex / (1.0 + ex)
    e1 = gsel * EXPERTS_PER_GROUP + i1
    e2 = gsel * EXPERTS_PER_GROUP + i2
    row_e = lax.broadcasted_iota(jnp.int32, (N_EXPERTS, cols), 0)
    oh1 = jnp.where(row_e == e1, 1.0, 0.0)
    oh2 = jnp.where(row_e == e2, 1.0, 0.0)
    oh = oh1 + oh2

    @pl.when(jnp.logical_and(phase == 0, step == 0))
    def _():
        tot_ref[...] = jnp.zeros_like(tot_ref)

    @pl.when(phase == 0)
    def _():
        tot_ref[...] = tot_ref[...] + jnp.sum(oh, axis=1, keepdims=True)

    @pl.when(jnp.logical_and(phase == 1, step == 0))
    def _():
        tot = tot_ref[...]
        hi = (tot * (1.0 / MXU_DIM)).astype(jnp.int32).astype(F32)
        lo = tot - hi * MXU_DIM
        run_ref[...] = (_dot(lower_ref[...], hi.astype(BF16)) * MXU_DIM
                        + _dot(lower_ref[...], lo.astype(BF16)))

    @pl.when(phase == 1)
    def _():
        blocks = cols // LANES
        stacked = jnp.concatenate([oh[:, b * LANES:(b + 1) * LANES] for b in range(blocks)], axis=0)
        local = _dot(stacked.astype(BF16), upper_ref[...])
        run = run_ref[...]
        p1, p2 = [], []
        for b in range(blocks):
            sl = slice(b * LANES, (b + 1) * LANES)
            before = local[b * N_EXPERTS:(b + 1) * N_EXPERTS] + run
            p1.append(jnp.sum(oh1[:, sl] * before, axis=0, keepdims=True))
            p2.append(jnp.sum(oh2[:, sl] * before, axis=0, keepdims=True))
            run = run + jnp.sum(oh[:, sl], axis=1, keepdims=True)
        run_ref[...] = run
        pos1 = jnp.concatenate(p1, axis=1).astype(jnp.int32)
        pos2 = jnp.concatenate(p2, axis=1).astype(jnp.int32)
        zero_i = jnp.zeros_like(row8)
        mi_ref[...] = jnp.where(row8 == 0, pos1, jnp.where(row8 == 1, pos2, jnp.where(
            row8 == 2, e1, jnp.where(row8 == 3, e2, zero_i))))
        mf_ref[...] = jnp.where(row8 == 0, wk1, jnp.where(row8 == 1, wk2, 0.0))
        cnt_ref[...] = tot_ref[...].astype(jnp.int32)


def _router_call(lg_t):
    n_tokens = lg_t.shape[1]
    tc = ROUTER_COLS
    r = lax.broadcasted_iota(jnp.int32, (LANES, LANES), 0)
    c = lax.broadcasted_iota(jnp.int32, (LANES, LANES), 1)
    upper = jnp.where(r < c, 1.0, 0.0).astype(BF16)
    lower = jnp.where(c < r, 1.0, 0.0)[:N_EXPERTS, :N_EXPERTS].astype(BF16)
    col = pl.BlockSpec((SUBLANES, tc), lambda p, i: (0, i * p))
    return pl.pallas_call(
        _router_kernel,
        grid=(2, n_tokens // tc),
        in_specs=[pl.BlockSpec((LOGIT_ROWS, tc), lambda p, i: (0, i)),
                  _const_spec((LANES, LANES)), _const_spec((N_EXPERTS, N_EXPERTS))],
        out_specs=[col, col, _const_spec((N_EXPERTS, LANES))],
        out_shape=[jax.ShapeDtypeStruct((SUBLANES, n_tokens), jnp.int32),
                   jax.ShapeDtypeStruct((SUBLANES, n_tokens), F32),
                   jax.ShapeDtypeStruct((N_EXPERTS, LANES), jnp.int32)],
        scratch_shapes=[pltpu.VMEM((N_EXPERTS, LANES), F32), pltpu.VMEM((N_EXPERTS, LANES), F32)],
        compiler_params=pltpu.CompilerParams(
            dimension_semantics=("arbitrary", "arbitrary"), vmem_limit_bytes=VMEM_LIMIT),
        name="router",
    )(lg_t, upper, lower)


def _ffn_plan(counts, n_pairs):
    ends = jnp.cumsum(counts)
    off = ends - counts
    ids = jnp.arange(N_EXPERTS, dtype=jnp.int32)

    n_tiles = n_pairs // FFN_ROWS
    n_items = n_tiles + N_EXPERTS
    first_tile = off // FFN_ROWS
    last_tile = (ends - 1) // FFN_ROWS
    items_e = jnp.where(counts > 0, last_tile - first_tile + 1, 0)
    items_end = jnp.cumsum(items_e)
    items_start = items_end - items_e
    n_work = items_end[-1]
    w = jnp.arange(n_items, dtype=jnp.int32)
    wv = jnp.minimum(w, n_work - 1)
    e_w = jnp.sum((items_end[None, :] <= wv[:, None]).astype(jnp.int32), axis=1)
    e_w = jnp.minimum(e_w, N_EXPERTS - 1)
    sel = e_w[:, None] == ids[None, :]
    pick = lambda a: jnp.sum(jnp.where(sel, a[None, :], 0), axis=1)
    t_w = pick(first_tile) + (wv - pick(items_start))
    lo_w = jnp.maximum(pick(off), t_w * FFN_ROWS) - t_w * FFN_ROWS
    hi_w = jnp.minimum(pick(ends), (t_w + 1) * FFN_ROWS) - t_w * FFN_ROWS
    valid = w < n_work
    lo_w = jnp.where(valid, lo_w, 0)
    hi_w = jnp.where(valid, hi_w, 0)
    used = counts > 0
    order_e = jnp.cumsum(used.astype(jnp.int32)) - 1
    later_used = jnp.logical_and(used[None, :], ids[None, :] > ids[:, None])
    next_e = jnp.min(jnp.where(later_used, ids[None, :], N_EXPERTS), axis=1)
    next_e = jnp.where(next_e == N_EXPERTS, -1, next_e)
    i32 = lambda a: a.astype(jnp.int32)
    return i32(t_w), i32(e_w), i32(lo_w), i32(hi_w), i32(pick(order_e)), i32(pick(next_e))


def _scatter_kernel(tiles_p, pos1_s, pos2_s, h2p_ref, h2s_ref, xs_hbm, sem):
    i = pl.program_id(0)
    base = i * DISPATCH_ROWS

    def scatter_rows(src_ref):
        def row_copy(r, pos_s):
            p = pos_s[base + r]
            return pltpu.make_async_copy(src_ref.at[_token_tile(r), :], xs_hbm.at[_token_tile(p), :],
                                         sem)

        def start(r, carry):
            row_copy(r, pos1_s).start(priority=0)
            row_copy(r, pos2_s).start(priority=1)
            return carry

        def wait(r, carry):
            row_copy(r, pos1_s).wait()
            row_copy(r, pos2_s).wait()
            return carry

        lax.fori_loop(0, DISPATCH_ROWS, start, 0, unroll=32)
        lax.fori_loop(0, DISPATCH_ROWS, wait, 0, unroll=8)

    @pl.when(i < tiles_p)
    def _():
        scatter_rows(h2p_ref)

    @pl.when(i >= tiles_p)
    def _():
        scatter_rows(h2s_ref)


def _scatter_call(pos1, pos2, h2_p, h2_s):
    tm = DISPATCH_ROWS
    tiles_p = h2_p.shape[0] // (tm * TILE_ROWS)
    tiles_s = h2_s.shape[0] // (tm * TILE_ROWS)
    n_pairs = TOP_K * (h2_p.shape[0] + h2_s.shape[0]) // TILE_ROWS
    blk = (tm * TILE_ROWS, LANES)
    return pl.pallas_call(
        functools.partial(_scatter_kernel, tiles_p),
        grid_spec=pltpu.PrefetchScalarGridSpec(
            num_scalar_prefetch=2,
            grid=(tiles_p + tiles_s,),
            in_specs=[pl.BlockSpec(blk, lambda i, *_: (jnp.minimum(i, tiles_p - 1), 0)),
                      pl.BlockSpec(blk, lambda i, *_: (jnp.maximum(i - tiles_p, 0), 0))],
            out_specs=pl.BlockSpec(memory_space=pl.ANY),
            scratch_shapes=[pltpu.SemaphoreType.DMA(())],
        ),
        out_shape=jax.ShapeDtypeStruct((n_pairs * TILE_ROWS, LANES), F32),
        compiler_params=pltpu.CompilerParams(
            dimension_semantics=("arbitrary",), vmem_limit_bytes=VMEM_LIMIT),
        name="dispatch_scatter",
    )(pos1, pos2, h2_p, h2_s)


def _ffn_kernel(n_items, wt_s, we_s, wlo_s, whi_s, word_s, wnext_s, x_hbm, w1_hbm, w3_hbm, w2_hbm,
                o_ref, xbuf, w1f_ref, w3f_ref, w2f_ref, w13b_ref, w2b_ref, xsems, sems):
    w = pl.program_id(0)
    e = we_s[w]
    e_prev = we_s[jnp.maximum(w - 1, 0)]
    slot = word_s[w] % 2
    tile_rows = FFN_ROWS * TILE_ROWS

    def x_copy(item):
        first = pl.multiple_of(wt_s[item] * tile_rows, tile_rows)
        return pltpu.make_async_copy(x_hbm.at[pl.ds(first, tile_rows), :], xbuf.at[item % X_SLOTS],
                                     xsems.at[item % X_SLOTS])

    @pl.when(w == 0)
    def _():
        for ahead in range(X_SLOTS - 1):
            x_copy(ahead).start()

    @pl.when(w + X_SLOTS - 1 < n_items)
    def _():
        x_copy(w + X_SLOTS - 1).start()

    def weight_copies(expert, buf):
        return [pltpu.make_async_copy(src.at[expert], dst.at[buf], sems.at[buf])
                for src, dst in ((w1_hbm, w1f_ref), (w3_hbm, w3f_ref), (w2_hbm, w2f_ref))]

    @pl.when(w == 0)
    def _():
        for cp in weight_copies(e, slot):
            cp.start()

    @pl.when(jnp.logical_or(w == 0, e != e_prev))
    def _():
        for cp in weight_copies(e, slot):
            cp.wait()
        w13b_ref[:, :D_EXPERT] = w1f_ref[slot].astype(BF16)
        w13b_ref[:, D_EXPERT:] = w3f_ref[slot].astype(BF16)
        w2b_ref[...] = w2f_ref[slot].astype(BF16)
        nxt = wnext_s[w]

        @pl.when(nxt >= 0)
        def _():
            for cp in weight_copies(nxt, 1 - slot):
                cp.start()

    lo = wlo_s[w]
    hi = whi_s[w]
    x_copy(w).wait()

    @pl.when(hi > lo)
    def _():
        x = _load_token_tiles(xbuf, (w % X_SLOTS,), FFN_ROWS)
        xb = x.astype(BF16)
        y = None
        for blk in range(D_EXPERT // MXU_DIM):
            lo_c = blk * MXU_DIM
            a = _dot(xb, w13b_ref[:, lo_c:lo_c + MXU_DIM])
            b = _dot(xb, w13b_ref[:, D_EXPERT + lo_c:D_EXPERT + lo_c + MXU_DIM])
            hdn = ((a * jax.nn.sigmoid(a)) * b).astype(BF16)
            part = _dot(hdn, w2b_ref[lo_c:lo_c + MXU_DIM, :])
            y = part if y is None else y + part
        rows = lax.broadcasted_iota(jnp.int32, (FFN_ROWS, LANES), 0)
        mine = jnp.logical_and(rows >= lo, rows < hi)

        @pl.when(lo == 0)
        def _():
            _store_token_tiles(o_ref, y)

        @pl.when(lo > 0)
        def _():
            _store_token_tiles(o_ref, y, keep=mine)


def _ffn_call(plan, xs, w1, w3, w2):
    t_w = plan[0]
    n_pairs = xs.shape[0] // TILE_ROWS
    tm = FFN_ROWS
    any_spec = pl.BlockSpec(memory_space=pl.ANY)
    row = pl.BlockSpec((tm * TILE_ROWS, LANES), lambda i, wt, *_: (wt[i], 0))
    n_items = t_w.shape[0]
    assert n_items >= X_SLOTS
    return pl.pallas_call(
        functools.partial(_ffn_kernel, n_items),
        grid_spec=pltpu.PrefetchScalarGridSpec(
            num_scalar_prefetch=len(plan),
            grid=(n_items,),
            in_specs=[any_spec, any_spec, any_spec, any_spec],
            out_specs=row,
            scratch_shapes=[pltpu.VMEM((X_SLOTS, tm * TILE_ROWS, LANES), F32),
                            pltpu.VMEM((2, D_MODEL, D_EXPERT), F32),
                            pltpu.VMEM((2, D_MODEL, D_EXPERT), F32),
                            pltpu.VMEM((2, D_EXPERT, D_MODEL), F32),
                            pltpu.VMEM((D_MODEL, 2 * D_EXPERT), BF16),
                            pltpu.VMEM((D_EXPERT, D_MODEL), BF16),
                            pltpu.SemaphoreType.DMA((X_SLOTS,)),
                            pltpu.SemaphoreType.DMA((2,))],
        ),
        out_shape=jax.ShapeDtypeStruct((n_pairs * TILE_ROWS, LANES), F32),
        compiler_params=pltpu.CompilerParams(
            dimension_semantics=("arbitrary",), vmem_limit_bytes=VMEM_LIMIT),
        name="expert_ffn",
    )(*plan, xs, w1, w3, w2)


def _combine_kernel(three_d, row0, n_steps, pos1_s, pos2_s, x1_ref, wk_ref, mod_ref, g_ref, b_ref,
                    ys_hbm, o_ref, ybuf, sems):
    step = pl.program_id(0)
    slot = step % 2

    def gather(tile, tile_slot):
        base = row0 + tile * COMBINE_ROWS

        def row_copy(r, k, pos_s):
            p = pos_s[base + r]
            return pltpu.make_async_copy(ys_hbm.at[_token_tile(p), :],
                                         ybuf.at[tile_slot, k, _token_tile(r), :], sems.at[tile_slot])

        def start(r, carry):
            row_copy(r, 0, pos1_s).start(priority=0)
            row_copy(r, 1, pos2_s).start(priority=1)
            return carry

        def wait(r, carry):
            row_copy(r, 0, pos1_s).wait()
            row_copy(r, 1, pos2_s).wait()
            return carry

        return start, wait

    @pl.when(step == 0)
    def _():
        lax.fori_loop(0, COMBINE_ROWS, gather(step, slot)[0], 0, unroll=32)

    @pl.when(step + 1 < n_steps)
    def _():
        lax.fori_loop(0, COMBINE_ROWS, gather(step + 1, 1 - slot)[0], 0, unroll=32)

    lax.fori_loop(0, COMBINE_ROWS, gather(step, slot)[1], 0, unroll=8)

    wk = wk_ref[...]
    r = lax.broadcasted_iota(jnp.int32, (COMBINE_ROWS, COMBINE_ROWS), 0)
    c = lax.broadcasted_iota(jnp.int32, (COMBINE_ROWS, COMBINE_ROWS), 1)
    wk_col = [jnp.sum(jnp.where(r == c, wk[k:k + 1, :], 0.0), axis=1, keepdims=True)
              for k in range(TOP_K)]
    moe = (wk_col[0] * _load_token_tiles(ybuf, (slot, 0), COMBINE_ROWS)
           + wk_col[1] * _load_token_tiles(ybuf, (slot, 1), COMBINE_ROWS))
    x1 = x1_ref[...]
    if three_d:
        g2 = mod_ref[0, :, 5 * D_MODEL:]
        o_ref[0] = _layer_norm(ALPHA * x1 + (1.0 + g2) * moe, g_ref[...], b_ref[...])
    else:
        def seq_rows(v, b):
            return v[b * SUBLANES:(b + 1) * SUBLANES]

        resid = jnp.concatenate([
            ALPHA * seq_rows(x1, b) + (1.0 + mod_ref[b:b + 1, 5 * D_MODEL:]) * seq_rows(moe, b)
            for b in range(COMBINE_ROWS // SUBLANES)], axis=0)
        o_ref[...] = _layer_norm(resid, g_ref[...], b_ref[...])


def _combine_call(three_d, pos1, pos2, x1_all, mf_all, mod, ln2_g, ln2_b, ys, row0, batch, seq):
    tm = COMBINE_ROWS
    blk0 = row0 // tm
    if three_d:
        tiles = seq // tm
        grid = (batch * tiles,)
        mod_spec = pl.BlockSpec((1, 1, 6 * D_MODEL), lambda i, *_: (i // tiles, 0, 0))
        out_spec = pl.BlockSpec((1, tm, D_MODEL), lambda i, *_: (i // tiles, i % tiles, 0))
        out_shape = jax.ShapeDtypeStruct((batch, seq, D_MODEL), F32)
    else:
        grid = (batch * seq // tm,)
        mod_spec = pl.BlockSpec((tm // SUBLANES, 6 * D_MODEL), lambda i, *_: (i, 0))
        out_spec = pl.BlockSpec((tm, D_MODEL), lambda i, *_: (i, 0))
        out_shape = jax.ShapeDtypeStruct((batch * seq, D_MODEL), F32)
    return pl.pallas_call(
        functools.partial(_combine_kernel, three_d, row0, grid[0]),
        grid_spec=pltpu.PrefetchScalarGridSpec(
            num_scalar_prefetch=2,
            grid=grid,
            in_specs=[pl.BlockSpec((tm, D_MODEL), lambda i, *_: (i, 0)),
                      pl.BlockSpec((SUBLANES, tm), lambda i, *_: (0, blk0 + i)),
                      mod_spec,
                      pl.BlockSpec(ln2_g.shape, lambda i, *_: (0, 0)),
                      pl.BlockSpec(ln2_b.shape, lambda i, *_: (0, 0)),
                      pl.BlockSpec(memory_space=pl.ANY)],
            out_specs=out_spec,
            scratch_shapes=[pltpu.VMEM((2, TOP_K, tm * TILE_ROWS, LANES), F32),
                            pltpu.SemaphoreType.DMA((2,))],
        ),
        out_shape=out_shape,
        compiler_params=pltpu.CompilerParams(
            dimension_semantics=("arbitrary",), vmem_limit_bytes=VMEM_LIMIT),
        name="combine_prompt" if three_d else "combine_sample",
    )(pos1, pos2, x1_all, mf_all, mod, ln2_g, ln2_b, ys)


def _block_diag(w):
    heads, n, _ = w.shape
    eye = jnp.eye(heads, dtype=w.dtype)
    return (eye[:, None, :, None] * w[:, :, None, :]).reshape(heads * n, heads * n)


def kernel(x_prompt, x_sample, state_rglru_h, state_conv, c_prompt, c_sample, w_ada, b_ada, w_in, w_s, b_s, lnv_g, lnv_b, conv_w, conv_b, lru_wa, lru_ba, lru_wx, lru_bx, lru_lam, w_out, ln1_g, ln1_b, w_rg, b_rg, w_re, b_re, w1, w3, w2, ln2_g, ln2_b):
    batch, seq, _ = x_prompt.shape
    dec_batch, dec_seq, _ = x_sample.shape
    assert dec_seq == SUBLANES and seq % ROWS == 0 and (dec_batch * dec_seq) % ROWS == 0
    n_prompt = batch * seq
    n_sample = dec_batch * dec_seq
    n_tokens = n_prompt + n_sample
    n_pairs = TOP_K * n_tokens
    assert n_tokens % ROUTER_COLS == 0 and n_pairs % FFN_ROWS == 0
    l = 0

    mod_p, mod_s = _ada_call(c_prompt, c_sample, w_ada[l], b_ada[l][None])
    mod_p3 = mod_p[:, None, :]

    tri = jnp.tril(jnp.ones((CHUNK, CHUNK), dtype=bool))
    ws_tri = jnp.where(tri[None], w_s[l], 0.0)
    reps_p = MXU_DIM // CHUNK
    eye_p = jnp.eye(reps_p, dtype=F32)
    wsm_p = (eye_p[None, :, None, :, None] * ws_tri[:, None, :, None, :]).reshape(
        A_HEADS, MXU_DIM, MXU_DIM).astype(BF16)
    reps_s = MXU_DIM // dec_seq
    pos_in_seq = jnp.arange(MXU_DIM, dtype=jnp.int32) % dec_seq
    seq_of_row = jnp.arange(MXU_DIM, dtype=jnp.int32) // dec_seq
    spread = (pos_in_seq[:, None] == jnp.arange(dec_seq, dtype=jnp.int32)[None, :]).astype(F32)
    same_seq = seq_of_row[:, None] == seq_of_row[None, :]
    wsm_s = jnp.einsum("ri,hij,cj->hrc", spread, ws_tri[:, :dec_seq, :dec_seq], spread,
                       precision=lax.Precision.HIGHEST)
    wsm_s = jnp.where(same_seq[None], wsm_s, 0.0).astype(BF16)
    bs_pos = jnp.repeat(jnp.transpose(b_s[l]), A_HEAD_DIM, axis=1)
    bsf_p = jnp.tile(bs_pos, (reps_p, 1))
    bsf_s = jnp.tile(bs_pos[:dec_seq], (reps_s, 1))

    wa_bd = _block_diag(lru_wa[l])
    wx_bd = _block_diag(lru_wx[l])
    wg = jnp.stack([
        jnp.concatenate([wa_bd[q * MXU_DIM:(q + 1) * MXU_DIM, q * MXU_DIM:(q + 1) * MXU_DIM],
                         wx_bd[q * MXU_DIM:(q + 1) * MXU_DIM, q * MXU_DIM:(q + 1) * MXU_DIM]], axis=1)
        for q in range(D_B // MXU_DIM)]).astype(BF16)
    bg = jnp.concatenate([lru_ba[l].reshape(1, D_B), lru_bx[l].reshape(1, D_B)], axis=1)

    group_pad = jnp.zeros((SUBLANES - N_GROUPS, D_MODEL), F32)
    wr = jnp.concatenate([jnp.transpose(w_rg[l]), group_pad, jnp.transpose(w_re[l])], axis=0)
    wr_hi = wr.astype(BF16)
    wr_lo = (wr - wr_hi.astype(F32)).astype(BF16)
    br = jnp.concatenate([b_rg[l], jnp.zeros((SUBLANES - N_GROUPS,), F32), b_re[l]])[:, None]

    common = dict(lnv_g=lnv_g[l][None], lnv_b=lnv_b[l][None], conv_w=conv_w[l], conv_b=conv_b[l][None],
                  wg=wg, bg=bg, lam=lru_lam[l][None], w_out=w_out[l].astype(BF16),
                  ln1_g=ln1_g[l][None], ln1_b=ln1_b[l][None], wr_hi=wr_hi, wr_lo=wr_lo, br=br)
    w_in_b = w_in[l].astype(BF16)
    weights_p = _mixer_weights(w_in_b, wsm_p, bsf_p, **common)
    weights_s = _mixer_weights(w_in_b, wsm_s, bsf_s, **common)

    x1_p, h2_p, lg_p, cst_p, hst_p = _mixer_prompt(x_prompt, mod_p3, weights_p)

    cpad = jnp.concatenate(
        [jnp.zeros((dec_batch, SUBLANES - (CONV_W - 1), D_B), F32), state_conv[l]], axis=1
    ).reshape(n_sample, D_B)
    x1_s, h2_s, lg_s, xr_s, v_s, hs_s = _mixer_sample(
        x_sample.reshape(n_sample, D_MODEL), mod_s, weights_s, state_rglru_h[l], cpad)

    meta_i, meta_f, counts = _router_call(jnp.concatenate([lg_p, lg_s], axis=1))
    pos1, pos2 = meta_i[0], meta_i[1]
    xs = _scatter_call(pos1, pos2, h2_p, h2_s)
    ys = _ffn_call(_ffn_plan(counts[:, 0], n_pairs), xs, w1[l], w3[l], w2[l])

    ln2g, ln2b = ln2_g[l][None], ln2_b[l][None]
    y_prompt = _combine_call(True, pos1, pos2, x1_p, meta_f, mod_p3, ln2g, ln2b, ys,
                             0, batch, seq)
    y_sample = _combine_call(False, pos1, pos2, x1_s, meta_f, mod_s, ln2g, ln2b, ys,
                             n_prompt, dec_batch, dec_seq).reshape(dec_batch, dec_seq, D_MODEL)

    new_h_prompt = hst_p[:, SUBLANES - 1, :][None]
    new_conv_prompt = cst_p[:, SUBLANES - (CONV_W - 1):, :][None]
    xr_s3 = xr_s.reshape(dec_batch, dec_seq, D_B)
    new_h_sample = hs_s.reshape(dec_batch, dec_seq, D_B)[:, dec_seq - 1, :][None]
    new_conv_sample = xr_s3[:, dec_seq - (CONV_W - 1):, :][None]
    new_chunk_v_sample = v_s.reshape(dec_batch, dec_seq, D_A)[None]
    return (y_prompt, y_sample, new_h_prompt, new_conv_prompt, new_h_sample, new_conv_sample,
            new_chunk_v_sample)
```

```python
import functools
import math

import jax
import jax.numpy as jnp
from jax import lax
from jax.experimental import pallas as pl
from jax.experimental.pallas import tpu as pltpu

D_MODEL = 1024
D_A = 512
A_HEADS = 4
A_HEAD_DIM = 128
CHUNK = 128
D_B = 512
B_HEADS = 8
B_HEAD_DIM = 64
CONV_W = 4
LRU_C = 8.0
D_IN = 2 * D_A + 2 * D_B
N_GROUPS = 4
EXPERTS_PER_GROUP = 8
N_EXPERTS = 32
TOP_K = 2
D_EXPERT = 512
LN_EPS = 1e-5
DEPTH = 1
ALPHA = (2.0 * DEPTH) ** 0.25

LANES = 128
SUBLANES = 8
MXU_DIM = 256
VMEM_LIMIT = 56 * 1024 * 1024

ROWS = 512
ROUTER_COLS = 1024
LOGIT_ROWS = SUBLANES + N_EXPERTS
DISPATCH_ROWS = 1024
FFN_ROWS = 256
X_SLOTS = 3
COMBINE_ROWS = 256
BF16 = jnp.bfloat16
F32 = jnp.float32


def _dot(a, b):
    return jnp.dot(a, b, preferred_element_type=F32)


def _dot_nt(a, b):
    return lax.dot_general(a, b, (((1,), (1,)), ((), ())), preferred_element_type=F32)


TILE_ROWS = D_MODEL // LANES


def _token_tile(r):
    return pl.ds(pl.multiple_of(r * TILE_ROWS, TILE_ROWS), TILE_ROWS)


def _load_token_tiles(ref, lead, n_rows):
    parts = [ref[lead + (pl.ds(s, n_rows, stride=TILE_ROWS), slice(None))] for s in range(TILE_ROWS)]
    return jnp.concatenate(parts, axis=1)


def _store_token_tiles(ref, val, keep=None):
    n_rows = val.shape[0]
    for s in range(TILE_ROWS):
        rows = pl.ds(s, n_rows, stride=TILE_ROWS)
        new = val[:, s * LANES:(s + 1) * LANES]
        if keep is not None:
            new = jnp.where(keep, new, ref[rows, :])
        ref[rows, :] = new


def _layer_norm(x, g, b):
    mu = jnp.mean(x, axis=-1, keepdims=True)
    xc = x - mu
    var = jnp.mean(xc * xc, axis=-1, keepdims=True)
    return xc * lax.rsqrt(var + LN_EPS) * g + b


def _gelu_tanh(x):
    c = math.sqrt(2.0 / math.pi)
    return x * (0.5 * (1.0 + jnp.tanh(c * (x + 0.044715 * (x * x * x)))))


def _const_spec(shape):
    nd = len(shape)
    return pl.BlockSpec(shape, lambda *_: (0,) * nd)


def _ada_kernel(cp_ref, cs_ref, w_ref, b_ref, op_ref, os_ref):
    w = w_ref[...].astype(BF16)
    for c_ref, o_ref in ((cp_ref, op_ref), (cs_ref, os_ref)):
        c = c_ref[...]
        a = (c * jax.nn.sigmoid(c)).astype(BF16)
        o_ref[...] = _dot(a, w) + b_ref[...]


def _ada_call(c_prompt, c_sample, w_ada, b_ada):
    tn = 1536
    rows_p, rows_s = c_prompt.shape[0], c_sample.shape[0]
    return pl.pallas_call(
        _ada_kernel,
        grid=(6 * D_MODEL // tn,),
        in_specs=[
            pl.BlockSpec((rows_p, D_MODEL), lambda j: (0, 0)),
            pl.BlockSpec((rows_s, D_MODEL), lambda j: (0, 0)),
            pl.BlockSpec((D_MODEL, tn), lambda j: (0, j)),
            pl.BlockSpec((1, tn), lambda j: (0, j)),
        ],
        out_specs=[pl.BlockSpec((rows_p, tn), lambda j: (0, j)),
                   pl.BlockSpec((rows_s, tn), lambda j: (0, j))],
        out_shape=[jax.ShapeDtypeStruct((rows_p, 6 * D_MODEL), F32),
                   jax.ShapeDtypeStruct((rows_s, 6 * D_MODEL), F32)],
        compiler_params=pltpu.CompilerParams(
            dimension_semantics=("arbitrary",), vmem_limit_bytes=VMEM_LIMIT),
        name="ada_mod",
    )(c_prompt, c_sample, w_ada, b_ada)


def _mixer_kernel(sample, *refs):
    if sample:
        (x_ref, mod_ref, win_ref, wsm_ref, bsf_ref, lnvg_ref, lnvb_ref, convw_ref, convb_ref,
         wg_ref, bg_ref, lam_ref, wout_ref, ln1g_ref, ln1b_ref, wrh_ref, wrl_ref, br_ref,
         h0_ref, cpad_ref,
         x1_ref, h2_ref, lg_ref, xr_ref, v_ref, hs_ref) = refs
    else:
        (x_ref, mod_ref, win_ref, wsm_ref, bsf_ref, lnvg_ref, lnvb_ref, convw_ref, convb_ref,
         wg_ref, bg_ref, lam_ref, wout_ref, ln1g_ref, ln1b_ref, wrh_ref, wrl_ref, br_ref,
         x1_ref, h2_ref, lg_ref, cst_ref, hst_ref, prev_ref, hcar_ref) = refs

    rows = ROWS
    groups = rows // SUBLANES

    if sample:
        x = x_ref[...]

        def chunk(c):
            return lambda b: mod_ref[b:b + 1, c * D_MODEL:(c + 1) * D_MODEL]
    else:
        x = x_ref[0]

        def chunk(c):
            return lambda b: mod_ref[0, :, c * D_MODEL:(c + 1) * D_MODEL]

        @pl.when(pl.program_id(1) == 0)
        def _():
            prev_ref[...] = jnp.zeros_like(prev_ref)
            hcar_ref[...] = jnp.zeros_like(hcar_ref)

    def modulated(fn, *vals):
        if not sample:
            return fn(0, *vals)
        parts = [fn(b, *(v[b * SUBLANES:(b + 1) * SUBLANES] for v in vals)) for b in range(groups)]
        return jnp.concatenate(parts, axis=0)

    sh1, sc1, g1, sh2, sc2, g2 = [chunk(c) for c in range(6)]

    h = modulated(lambda b, xv: xv * (1.0 + sc1(b)) + sh1(b), x)
    proj = _dot(h.astype(BF16), win_ref[...])
    u = proj[:, :D_A]
    v = _layer_norm(proj[:, D_A:2 * D_A], lnvg_ref[...], lnvb_ref[...])
    xr = proj[:, 2 * D_A:2 * D_A + D_B]
    yr = proj[:, 2 * D_A + D_B:]

    vb = v.astype(BF16)
    halves = []
    for r in range(rows // MXU_DIM):
        heads = [
            _dot(wsm_ref[hd], vb[r * MXU_DIM:(r + 1) * MXU_DIM, hd * A_HEAD_DIM:(hd + 1) * A_HEAD_DIM])
            for hd in range(A_HEADS)
        ]
        halves.append(jnp.concatenate(heads, axis=1) + bsf_ref[...])
    mixed = halves[0] if len(halves) == 1 else jnp.concatenate(halves, axis=0)
    out_a = u * mixed

    xr3 = xr.reshape(groups, SUBLANES, D_B)
    t3 = lax.broadcasted_iota(jnp.int32, (1, SUBLANES, D_B), 1)
    cw = convw_ref[...]
    if sample:
        cpad3 = cpad_ref[...].reshape(groups, SUBLANES, D_B)
    else:
        prev8 = prev_ref[...]
    xc3 = convb_ref[...][None]
    for k in range(CONV_W):
        s = CONV_W - 1 - k
        if s == 0:
            shifted = xr3
        else:
            rot = pltpu.roll(xr3, s, 1)
            if sample:
                rot_prev = pltpu.roll(cpad3, s, 1)
            else:
                first = pltpu.roll(prev8, s, 0)[None]
                rot_prev = jnp.concatenate([first, rot[:-1]], axis=0) if groups > 1 else first
            shifted = jnp.where(t3 >= s, rot, rot_prev)
        xc3 = xc3 + shifted * cw[k:k + 1, :][None]
    xc = xc3.reshape(rows, D_B)

    xcb = xc.astype(BF16)
    gq = [_dot(xcb[:, q * MXU_DIM:(q + 1) * MXU_DIM], wg_ref[q]) for q in range(D_B // MXU_DIM)]
    bg = bg_ref[...]
    pre_a = jnp.concatenate([g[:, :MXU_DIM] for g in gq], axis=1) + bg[:, :D_B]
    pre_x = jnp.concatenate([g[:, MXU_DIM:] for g in gq], axis=1) + bg[:, D_B:]
    gate_a = jax.nn.sigmoid(pre_a)
    gate_x = jax.nn.sigmoid(pre_x)
    nl = -lam_ref[...]
    softplus = jnp.maximum(nl, 0.0) + jnp.log1p(jnp.exp(-jnp.abs(nl)))
    log_a = -LRU_C * gate_a * softplus
    a = jnp.exp(log_a)
    mult = jnp.sqrt(-jnp.tanh(log_a) * (1.0 + a * a))
    bx = mult * (gate_x * xc)

    a3 = a.reshape(groups, SUBLANES, D_B)
    b3 = bx.reshape(groups, SUBLANES, D_B)
    for d in (1, 2, 4):
        m = t3 >= d
        a_sh = jnp.where(m, pltpu.roll(a3, d, 1), 1.0)
        b_sh = jnp.where(m, pltpu.roll(b3, d, 1), 0.0)
        b3 = a3 * b_sh + b3
        a3 = a3 * a_sh
    if sample:
        hs3 = jnp.stack([a3[g] * h0_ref[g:g + 1, :] + b3[g] for g in range(groups)], axis=0)
    else:
        h_prev = hcar_ref[...][SUBLANES - 1:SUBLANES, :]
        hs_list = []
        for g in range(groups):
            hg = a3[g] * h_prev + b3[g]
            hs_list.append(hg)
            h_prev = hg[SUBLANES - 1:SUBLANES, :]
        hs3 = jnp.stack(hs_list, axis=0)
    hs = hs3.reshape(rows, D_B)

    out_b = hs * _gelu_tanh(yr)
    mix = _dot(out_a.astype(BF16), wout_ref[:D_A, :]) + _dot(out_b.astype(BF16), wout_ref[D_A:, :])
    y1 = modulated(lambda b, xv, mv: ALPHA * xv + (1.0 + g1(b)) * mv, x, mix)
    x1 = _layer_norm(y1, ln1g_ref[...], ln1b_ref[...])
    h2 = modulated(lambda b, xv: xv * (1.0 + sc2(b)) + sh2(b), x1)

    h2_hi = h2.astype(BF16)
    h2_lo = (h2 - h2_hi.astype(F32)).astype(BF16)
    both = _dot_nt(wrl_ref[...], h2_hi)
    logits = (both[:LOGIT_ROWS] + _dot_nt(wrh_ref[...], h2_lo) + both[LOGIT_ROWS:] + br_ref[...])

    x1_ref[...] = x1
    _store_token_tiles(h2_ref, h2)
    lg_ref[...] = logits
    if sample:
        xr_ref[...] = xr
        v_ref[...] = v
        hs_ref[...] = hs
    else:
        last_x = xr3[groups - 1]
        last_h = hs3[groups - 1]
        prev_ref[...] = last_x
        hcar_ref[...] = last_h
        cst_ref[0] = last_x
        hst_ref[0] = last_h


def _mixer_weights(w_in, wsm, bsf, lnv_g, lnv_b, conv_w, conv_b, wg, bg, lam, w_out, ln1_g, ln1_b,
                   wr_hi, wr_lo, br):
    arrs = [w_in, wsm, bsf, lnv_g, lnv_b, conv_w, conv_b, wg, bg, lam, w_out, ln1_g, ln1_b,
            wr_hi, wr_lo, br]
    return arrs, [_const_spec(a.shape) for a in arrs]


def _mixer_prompt(x, mod3, weights):
    batch, seq, _ = x.shape
    n_tokens = batch * seq
    tiles = seq // ROWS
    arrs, specs = weights
    row_spec = lambda w: pl.BlockSpec((ROWS, w), lambda b, j: (b * tiles + j, 0))
    st_spec = pl.BlockSpec((1, SUBLANES, D_B), lambda b, j: (b, 0, 0))
    return pl.pallas_call(
        functools.partial(_mixer_kernel, False),
        grid=(batch, tiles),
        in_specs=[pl.BlockSpec((1, ROWS, D_MODEL), lambda b, j: (b, j, 0)),
                  pl.BlockSpec((1, 1, 6 * D_MODEL), lambda b, j: (b, 0, 0))] + specs,
        out_specs=[row_spec(D_MODEL),
                   pl.BlockSpec((ROWS * TILE_ROWS, LANES), lambda b, j: (b * tiles + j, 0)),
                   pl.BlockSpec((LOGIT_ROWS, ROWS), lambda b, j: (0, b * tiles + j)), st_spec, st_spec],
        out_shape=[jax.ShapeDtypeStruct((n_tokens, D_MODEL), F32),
                   jax.ShapeDtypeStruct((n_tokens * TILE_ROWS, LANES), F32),
                   jax.ShapeDtypeStruct((LOGIT_ROWS, n_tokens), F32),
                   jax.ShapeDtypeStruct((batch, SUBLANES, D_B), F32),
                   jax.ShapeDtypeStruct((batch, SUBLANES, D_B), F32)],
        scratch_shapes=[pltpu.VMEM((SUBLANES, D_B), F32), pltpu.VMEM((SUBLANES, D_B), F32)],
        compiler_params=pltpu.CompilerParams(
            dimension_semantics=("arbitrary", "arbitrary"), vmem_limit_bytes=VMEM_LIMIT),
        name="mixer_prompt",
    )(x, mod3, *arrs)


def _mixer_sample(x2, mod_seq, weights, h0, cpad):
    n_rows = x2.shape[0]
    arrs, specs = weights
    row = lambda w: pl.BlockSpec((ROWS, w), lambda i: (i, 0))
    seq_row = lambda w: pl.BlockSpec((ROWS // SUBLANES, w), lambda i: (i, 0))
    return pl.pallas_call(
        functools.partial(_mixer_kernel, True),
        grid=(n_rows // ROWS,),
        in_specs=[row(D_MODEL), seq_row(6 * D_MODEL)] + specs + [seq_row(D_B), row(D_B)],
        out_specs=[row(D_MODEL), pl.BlockSpec((ROWS * TILE_ROWS, LANES), lambda i: (i, 0)),
                   pl.BlockSpec((LOGIT_ROWS, ROWS), lambda i: (0, i)),
                   row(D_B), row(D_B), row(D_B)],
        out_shape=[jax.ShapeDtypeStruct((n_rows, D_MODEL), F32),
                   jax.ShapeDtypeStruct((n_rows * TILE_ROWS, LANES), F32),
                   jax.ShapeDtypeStruct((LOGIT_ROWS, n_rows), F32),
                   jax.ShapeDtypeStruct((n_rows, D_B), F32),
                   jax.ShapeDtypeStruct((n_rows, D_B), F32),
                   jax.ShapeDtypeStruct((n_rows, D_B), F32)],
        compiler_params=pltpu.CompilerParams(
            dimension_semantics=("arbitrary",), vmem_limit_bytes=VMEM_LIMIT),
        name="mixer_sample",
    )(x2, mod_seq, *arrs, h0, cpad)


def _router_kernel(lg_ref, upper_ref, lower_ref, mi_ref, mf_ref, cnt_ref, tot_ref, run_ref):
    phase = pl.program_id(0)
    step = pl.program_id(1)
    l = lg_ref[...]
    cols = l.shape[1]
    row8 = lax.broadcasted_iota(jnp.int32, (SUBLANES, cols), 0)
    neg = -jnp.inf
    big = jnp.int32(SUBLANES)

    def first_max(vals):
        top = jnp.max(vals, axis=0, keepdims=True)
        return top, jnp.min(jnp.where(vals == top, row8, big), axis=0, keepdims=True)

    is_group = row8 < N_GROUPS
    lg = l[:SUBLANES]
    m, gsel = first_max(jnp.where(is_group, lg, neg))
    denom = jnp.sum(jnp.where(is_group, jnp.exp(lg - m), 0.0), axis=0, keepdims=True)
    pgsel = 1.0 / denom
    le = l[SUBLANES:2 * SUBLANES]
    for g in range(1, N_GROUPS):
        le = jnp.where(gsel == g, l[(g + 1) * SUBLANES:(g + 2) * SUBLANES], le)
    v1, i1 = first_max(le)
    v2, i2 = first_max(jnp.where(row8 == i1, neg, le))
    ex = jnp.exp(v2 - v1)
    wk1 = pgsel / (1.0 + ex)
    wk2 = pgsel * ex / (1.0 + ex)
    e1 = gsel * EXPERTS_PER_GROUP + i1
    e2 = gsel * EXPERTS_PER_GROUP + i2
    row_e = lax.broadcasted_iota(jnp.int32, (N_EXPERTS, cols), 0)
    oh1 = jnp.where(row_e == e1, 1.0, 0.0)
    oh2 = jnp.where(row_e == e2, 1.0, 0.0)
    oh = oh1 + oh2

    @pl.when(jnp.logical_and(phase == 0, step == 0))
    def _():
        tot_ref[...] = jnp.zeros_like(tot_ref)

    @pl.when(phase == 0)
    def _():
        tot_ref[...] = tot_ref[...] + jnp.sum(oh, axis=1, keepdims=True)

    @pl.when(jnp.logical_and(phase == 1, step == 0))
    def _():
        tot = tot_ref[...]
        hi = (tot * (1.0 / MXU_DIM)).astype(jnp.int32).astype(F32)
        lo = tot - hi * MXU_DIM
        run_ref[...] = (_dot(lower_ref[...], hi.astype(BF16)) * MXU_DIM
                        + _dot(lower_ref[...], lo.astype(BF16)))

    @pl.when(phase == 1)
    def _():
        blocks = cols // LANES
        stacked = jnp.concatenate([oh[:, b * LANES:(b + 1) * LANES] for b in range(blocks)], axis=0)
        local = _dot(stacked.astype(BF16), upper_ref[...])
        run = run_ref[...]
        p1, p2 = [], []
        for b in range(blocks):
            sl = slice(b * LANES, (b + 1) * LANES)
            before = local[b * N_EXPERTS:(b + 1) * N_EXPERTS] + run
            p1.append(jnp.sum(oh1[:, sl] * before, axis=0, keepdims=True))
            p2.append(jnp.sum(oh2[:, sl] * before, axis=0, keepdims=True))
            run = run + jnp.sum(oh[:, sl], axis=1, keepdims=True)
        run_ref[...] = run
        pos1 = jnp.concatenate(p1, axis=1).astype(jnp.int32)
        pos2 = jnp.concatenate(p2, axis=1).astype(jnp.int32)
        zero_i = jnp.zeros_like(row8)
        mi_ref[...] = jnp.where(row8 == 0, pos1, jnp.where(row8 == 1, pos2, jnp.where(
            row8 == 2, e1, jnp.where(row8 == 3, e2, zero_i))))
        mf_ref[...] = jnp.where(row8 == 0, wk1, jnp.where(row8 == 1, wk2, 0.0))
        cnt_ref[...] = tot_ref[...].astype(jnp.int32)


def _router_call(lg_t):
    n_tokens = lg_t.shape[1]
    tc = ROUTER_COLS
    r = lax.broadcasted_iota(jnp.int32, (LANES, LANES), 0)
    c = lax.broadcasted_iota(jnp.int32, (LANES, LANES), 1)
    upper = jnp.where(r < c, 1.0, 0.0).astype(BF16)
    lower = jnp.where(c < r, 1.0, 0.0)[:N_EXPERTS, :N_EXPERTS].astype(BF16)
    col = pl.BlockSpec((SUBLANES, tc), lambda p, i: (0, i * p))
    return pl.pallas_call(
        _router_kernel,
        grid=(2, n_tokens // tc),
        in_specs=[pl.BlockSpec((LOGIT_ROWS, tc), lambda p, i: (0, i)),
                  _const_spec((LANES, LANES)), _const_spec((N_EXPERTS, N_EXPERTS))],
        out_specs=[col, col, _const_spec((N_EXPERTS, LANES))],
        out_shape=[jax.ShapeDtypeStruct((SUBLANES, n_tokens), jnp.int32),
                   jax.ShapeDtypeStruct((SUBLANES, n_tokens), F32),
                   jax.ShapeDtypeStruct((N_EXPERTS, LANES), jnp.int32)],
        scratch_shapes=[pltpu.VMEM((N_EXPERTS, LANES), F32), pltpu.VMEM((N_EXPERTS, LANES), F32)],
        compiler_params=pltpu.CompilerParams(
            dimension_semantics=("arbitrary", "arbitrary"), vmem_limit_bytes=VMEM_LIMIT),
        name="router",
    )(lg_t, upper, lower)


def _ffn_plan(counts, n_pairs):
    ends = jnp.cumsum(counts)
    off = ends - counts
    ids = jnp.arange(N_EXPERTS, dtype=jnp.int32)

    n_tiles = n_pairs // FFN_ROWS
    n_items = n_tiles + N_EXPERTS
    first_tile = off // FFN_ROWS
    last_tile = (ends - 1) // FFN_ROWS
    items_e = jnp.where(counts > 0, last_tile - first_tile + 1, 0)
    items_end = jnp.cumsum(items_e)
    items_start = items_end - items_e
    n_work = items_end[-1]
    w = jnp.arange(n_items, dtype=jnp.int32)
    wv = jnp.minimum(w, n_work - 1)
    e_w = jnp.sum((items_end[None, :] <= wv[:, None]).astype(jnp.int32), axis=1)
    e_w = jnp.minimum(e_w, N_EXPERTS - 1)
    sel = e_w[:, None] == ids[None, :]
    pick = lambda a: jnp.sum(jnp.where(sel, a[None, :], 0), axis=1)
    t_w = pick(first_tile) + (wv - pick(items_start))
    lo_w = jnp.maximum(pick(off), t_w * FFN_ROWS) - t_w * FFN_ROWS
    hi_w = jnp.minimum(pick(ends), (t_w + 1) * FFN_ROWS) - t_w * FFN_ROWS
    valid = w < n_work
    lo_w = jnp.where(valid, lo_w, 0)
    hi_w = jnp.where(valid, hi_w, 0)
    used = counts > 0
    order_e = jnp.cumsum(used.astype(jnp.int32)) - 1
    later_used = jnp.logical_and(used[None, :], ids[None, :] > ids[:, None])
    next_e = jnp.min(jnp.where(later_used, ids[None, :], N_EXPERTS), axis=1)
    next_e = jnp.where(next_e == N_EXPERTS, -1, next_e)
    i32 = lambda a: a.astype(jnp.int32)
    return i32(t_w), i32(e_w), i32(lo_w), i32(hi_w), i32(pick(order_e)), i32(pick(next_e))


def _scatter_kernel(tiles_p, pos1_s, pos2_s, h2p_ref, h2s_ref, xs_hbm, sem):
    i = pl.program_id(0)
    base = i * DISPATCH_ROWS

    def scatter_rows(src_ref):
        def row_copy(r, pos_s):
            p = pos_s[base + r]
            return pltpu.make_async_copy(src_ref.at[_token_tile(r), :], xs_hbm.at[_token_tile(p), :],
                                         sem)

        def start(r, carry):
            row_copy(r, pos1_s).start(priority=0)
            row_copy(r, pos2_s).start(priority=1)
            return carry

        def wait(r, carry):
            row_copy(r, pos1_s).wait()
            row_copy(r, pos2_s).wait()
            return carry

        lax.fori_loop(0, DISPATCH_ROWS, start, 0, unroll=32)
        lax.fori_loop(0, DISPATCH_ROWS, wait, 0, unroll=8)

    @pl.when(i < tiles_p)
    def _():
        scatter_rows(h2p_ref)

    @pl.when(i >= tiles_p)
    def _():
        scatter_rows(h2s_ref)


def _scatter_call(pos1, pos2, h2_p, h2_s):
    tm = DISPATCH_ROWS
    tiles_p = h2_p.shape[0] // (tm * TILE_ROWS)
    tiles_s = h2_s.shape[0] // (tm * TILE_ROWS)
    n_pairs = TOP_K * (h2_p.shape[0] + h2_s.shape[0]) // TILE_ROWS
    blk = (tm * TILE_ROWS, LANES)
    return pl.pallas_call(
        functools.partial(_scatter_kernel, tiles_p),
        grid_spec=pltpu.PrefetchScalarGridSpec(
            num_scalar_prefetch=2,
            grid=(tiles_p + tiles_s,),
            in_specs=[pl.BlockSpec(blk, lambda i, *_: (jnp.minimum(i, tiles_p - 1), 0)),
                      pl.BlockSpec(blk, lambda i, *_: (jnp.maximum(i - tiles_p, 0), 0))],
            out_specs=pl.BlockSpec(memory_space=pl.ANY),
            scratch_shapes=[pltpu.SemaphoreType.DMA(())],
        ),
        out_shape=jax.ShapeDtypeStruct((n_pairs * TILE_ROWS, LANES), F32),
        compiler_params=pltpu.CompilerParams(
            dimension_semantics=("arbitrary",), vmem_limit_bytes=VMEM_LIMIT),
        name="dispatch_scatter",
    )(pos1, pos2, h2_p, h2_s)


def _ffn_kernel(n_items, wt_s, we_s, wlo_s, whi_s, word_s, wnext_s, x_hbm, w1_hbm, w3_hbm, w2_hbm,
                o_ref, xbuf, w1f_ref, w3f_ref, w2f_ref, w13b_ref, w2b_ref, xsems, sems):
    w = pl.program_id(0)
    e = we_s[w]
    e_prev = we_s[jnp.maximum(w - 1, 0)]
    slot = word_s[w] % 2
    tile_rows = FFN_ROWS * TILE_ROWS

    def x_copy(item):
        first = pl.multiple_of(wt_s[item] * tile_rows, tile_rows)
        return pltpu.make_async_copy(x_hbm.at[pl.ds(first, tile_rows), :], xbuf.at[item % X_SLOTS],
                                     xsems.at[item % X_SLOTS])

    @pl.when(w == 0)
    def _():
        for ahead in range(X_SLOTS - 1):
            x_copy(ahead).start()

    @pl.when(w + X_SLOTS - 1 < n_items)
    def _():
        x_copy(w + X_SLOTS - 1).start()

    def weight_copies(expert, buf):
        return [pltpu.make_async_copy(src.at[expert], dst.at[buf], sems.at[buf])
                for src, dst in ((w1_hbm, w1f_ref), (w3_hbm, w3f_ref), (w2_hbm, w2f_ref))]

    @pl.when(w == 0)
    def _():
        for cp in weight_copies(e, slot):
            cp.start()

    @pl.when(jnp.logical_or(w == 0, e != e_prev))
    def _():
        for cp in weight_copies(e, slot):
            cp.wait()
        w13b_ref[:, :D_EXPERT] = w1f_ref[slot].astype(BF16)
        w13b_ref[:, D_EXPERT:] = w3f_ref[slot].astype(BF16)
        w2b_ref[...] = w2f_ref[slot].astype(BF16)
        nxt = wnext_s[w]

        @pl.when(nxt >= 0)
        def _():
            for cp in weight_copies(nxt, 1 - slot):
                cp.start()

    lo = wlo_s[w]
    hi = whi_s[w]
    x_copy(w).wait()

    @pl.when(hi > lo)
    def _():
        x = _load_token_tiles(xbuf, (w % X_SLOTS,), FFN_ROWS)
        h13 = _dot(x.astype(BF16), w13b_ref[...])
        a = h13[:, :D_EXPERT]
        b = h13[:, D_EXPERT:]
        hdn = (a * jax.nn.sigmoid(a)) * b
        y = _dot(hdn.astype(BF16), w2b_ref[...])
        rows = lax.broadcasted_iota(jnp.int32, (FFN_ROWS, LANES), 0)
        mine = jnp.logical_and(rows >= lo, rows < hi)

        @pl.when(lo == 0)
        def _():
            _store_token_tiles(o_ref, y)

        @pl.when(lo > 0)
        def _():
            _store_token_tiles(o_ref, y, keep=mine)


def _ffn_call(plan, xs, w1, w3, w2):
    t_w = plan[0]
    n_pairs = xs.shape[0] // TILE_ROWS
    tm = FFN_ROWS
    any_spec = pl.BlockSpec(memory_space=pl.ANY)
    row = pl.BlockSpec((tm * TILE_ROWS, LANES), lambda i, wt, *_: (wt[i], 0))
    n_items = t_w.shape[0]
    assert n_items >= X_SLOTS
    return pl.pallas_call(
        functools.partial(_ffn_kernel, n_items),
        grid_spec=pltpu.PrefetchScalarGridSpec(
            num_scalar_prefetch=len(plan),
            grid=(n_items,),
            in_specs=[any_spec, any_spec, any_spec, any_spec],
            out_specs=row,
            scratch_shapes=[pltpu.VMEM((X_SLOTS, tm * TILE_ROWS, LANES), F32),
                            pltpu.VMEM((2, D_MODEL, D_EXPERT), F32),
                            pltpu.VMEM((2, D_MODEL, D_EXPERT), F32),
                            pltpu.VMEM((2, D_EXPERT, D_MODEL), F32),
                            pltpu.VMEM((D_MODEL, 2 * D_EXPERT), BF16),
                            pltpu.VMEM((D_EXPERT, D_MODEL), BF16),
                            pltpu.SemaphoreType.DMA((X_SLOTS,)),
                            pltpu.SemaphoreType.DMA((2,))],
        ),
        out_shape=jax.ShapeDtypeStruct((n_pairs * TILE_ROWS, LANES), F32),
        compiler_params=pltpu.CompilerParams(
            dimension_semantics=("arbitrary",), vmem_limit_bytes=VMEM_LIMIT),
        name="expert_ffn",
    )(*plan, xs, w1, w3, w2)


def _combine_kernel(three_d, row0, n_steps, pos1_s, pos2_s, x1_ref, wk_ref, mod_ref, g_ref, b_ref,
                    ys_hbm, o_ref, ybuf, sems):
    step = pl.program_id(0)
    slot = step % 2

    def gather(tile, tile_slot):
        base = row0 + tile * COMBINE_ROWS

        def row_copy(r, k, pos_s):
            p = pos_s[base + r]
            return pltpu.make_async_copy(ys_hbm.at[_token_tile(p), :],
                                         ybuf.at[tile_slot, k, _token_tile(r), :], sems.at[tile_slot])

        def start(r, carry):
            row_copy(r, 0, pos1_s).start(priority=0)
            row_copy(r, 1, pos2_s).start(priority=1)
            return carry

        def wait(r, carry):
            row_copy(r, 0, pos1_s).wait()
            row_copy(r, 1, pos2_s).wait()
            return carry

        return start, wait

    @pl.when(step == 0)
    def _():
        lax.fori_loop(0, COMBINE_ROWS, gather(step, slot)[0], 0, unroll=32)

    @pl.when(step + 1 < n_steps)
    def _():
        lax.fori_loop(0, COMBINE_ROWS, gather(step + 1, 1 - slot)[0], 0, unroll=32)

    lax.fori_loop(0, COMBINE_ROWS, gather(step, slot)[1], 0, unroll=8)

    wk = wk_ref[...]
    r = lax.broadcasted_iota(jnp.int32, (COMBINE_ROWS, COMBINE_ROWS), 0)
    c = lax.broadcasted_iota(jnp.int32, (COMBINE_ROWS, COMBINE_ROWS), 1)
    wk_col = [jnp.sum(jnp.where(r == c, wk[k:k + 1, :], 0.0), axis=1, keepdims=True)
              for k in range(TOP_K)]
    moe = (wk_col[0] * _load_token_tiles(ybuf, (slot, 0), COMBINE_ROWS)
           + wk_col[1] * _load_token_tiles(ybuf, (slot, 1), COMBINE_ROWS))
    x1 = x1_ref[...]
    if three_d:
        g2 = mod_ref[0, :, 5 * D_MODEL:]
        o_ref[0] = _layer_norm(ALPHA * x1 + (1.0 + g2) * moe, g_ref[...], b_ref[...])
    else:
        def seq_rows(v, b):
            return v[b * SUBLANES:(b + 1) * SUBLANES]

        resid = jnp.concatenate([
            ALPHA * seq_rows(x1, b) + (1.0 + mod_ref[b:b + 1, 5 * D_MODEL:]) * seq_rows(moe, b)
            for b in range(COMBINE_ROWS // SUBLANES)], axis=0)
        o_ref[...] = _layer_norm(resid, g_ref[...], b_ref[...])


def _combine_call(three_d, pos1, pos2, x1_all, mf_all, mod, ln2_g, ln2_b, ys, row0, batch, seq):
    tm = COMBINE_ROWS
    blk0 = row0 // tm
    if three_d:
        tiles = seq // tm
        grid = (batch * tiles,)
        mod_spec = pl.BlockSpec((1, 1, 6 * D_MODEL), lambda i, *_: (i // tiles, 0, 0))
        out_spec = pl.BlockSpec((1, tm, D_MODEL), lambda i, *_: (i // tiles, i % tiles, 0))
        out_shape = jax.ShapeDtypeStruct((batch, seq, D_MODEL), F32)
    else:
        grid = (batch * seq // tm,)
        mod_spec = pl.BlockSpec((tm // SUBLANES, 6 * D_MODEL), lambda i, *_: (i, 0))
        out_spec = pl.BlockSpec((tm, D_MODEL), lambda i, *_: (i, 0))
        out_shape = jax.ShapeDtypeStruct((batch * seq, D_MODEL), F32)
    return pl.pallas_call(
        functools.partial(_combine_kernel, three_d, row0, grid[0]),
        grid_spec=pltpu.PrefetchScalarGridSpec(
            num_scalar_prefetch=2,
            grid=grid,
            in_specs=[pl.BlockSpec((tm, D_MODEL), lambda i, *_: (i, 0)),
                      pl.BlockSpec((SUBLANES, tm), lambda i, *_: (0, blk0 + i)),
                      mod_spec,
                      pl.BlockSpec(ln2_g.shape, lambda i, *_: (0, 0)),
                      pl.BlockSpec(ln2_b.shape, lambda i, *_: (0, 0)),
                      pl.BlockSpec(memory_space=pl.ANY)],
            out_specs=out_spec,
            scratch_shapes=[pltpu.VMEM((2, TOP_K, tm * TILE_ROWS, LANES), F32),
                            pltpu.SemaphoreType.DMA((2,))],
        ),
        out_shape=out_shape,
        compiler_params=pltpu.CompilerParams(
            dimension_semantics=("arbitrary",), vmem_limit_bytes=VMEM_LIMIT),
        name="combine_prompt" if three_d else "combine_sample",
    )(pos1, pos2, x1_all, mf_all, mod, ln2_g, ln2_b, ys)


def _block_diag(w):
    heads, n, _ = w.shape
    eye = jnp.eye(heads, dtype=w.dtype)
    return (eye[:, None, :, None] * w[:, :, None, :]).reshape(heads * n, heads * n)


def kernel(x_prompt, x_sample, state_rglru_h, state_conv, c_prompt, c_sample, w_ada, b_ada, w_in, w_s, b_s, lnv_g, lnv_b, conv_w, conv_b, lru_wa, lru_ba, lru_wx, lru_bx, lru_lam, w_out, ln1_g, ln1_b, w_rg, b_rg, w_re, b_re, w1, w3, w2, ln2_g, ln2_b):
    batch, seq, _ = x_prompt.shape
    dec_batch, dec_seq, _ = x_sample.shape
    assert dec_seq == SUBLANES and seq % ROWS == 0 and (dec_batch * dec_seq) % ROWS == 0
    n_prompt = batch * seq
    n_sample = dec_batch * dec_seq
    n_tokens = n_prompt + n_sample
    n_pairs = TOP_K * n_tokens
    assert n_tokens % ROUTER_COLS == 0 and n_pairs % FFN_ROWS == 0
    l = 0

    mod_p, mod_s = _ada_call(c_prompt, c_sample, w_ada[l], b_ada[l][None])
    mod_p3 = mod_p[:, None, :]

    tri = jnp.tril(jnp.ones((CHUNK, CHUNK), dtype=bool))
    ws_tri = jnp.where(tri[None], w_s[l], 0.0)
    reps_p = MXU_DIM // CHUNK
    eye_p = jnp.eye(reps_p, dtype=F32)
    wsm_p = (eye_p[None, :, None, :, None] * ws_tri[:, None, :, None, :]).reshape(
        A_HEADS, MXU_DIM, MXU_DIM).astype(BF16)
    reps_s = MXU_DIM // dec_seq
    pos_in_seq = jnp.arange(MXU_DIM, dtype=jnp.int32) % dec_seq
    seq_of_row = jnp.arange(MXU_DIM, dtype=jnp.int32) // dec_seq
    spread = (pos_in_seq[:, None] == jnp.arange(dec_seq, dtype=jnp.int32)[None, :]).astype(F32)
    same_seq = seq_of_row[:, None] == seq_of_row[None, :]
    wsm_s = jnp.einsum("ri,hij,cj->hrc", spread, ws_tri[:, :dec_seq, :dec_seq], spread,
                       precision=lax.Precision.HIGHEST)
    wsm_s = jnp.where(same_seq[None], wsm_s, 0.0).astype(BF16)
    bs_pos = jnp.repeat(jnp.transpose(b_s[l]), A_HEAD_DIM, axis=1)
    bsf_p = jnp.tile(bs_pos, (reps_p, 1))
    bsf_s = jnp.tile(bs_pos[:dec_seq], (reps_s, 1))

    wa_bd = _block_diag(lru_wa[l])
    wx_bd = _block_diag(lru_wx[l])
    wg = jnp.stack([
        jnp.concatenate([wa_bd[q * MXU_DIM:(q + 1) * MXU_DIM, q * MXU_DIM:(q + 1) * MXU_DIM],
                         wx_bd[q * MXU_DIM:(q + 1) * MXU_DIM, q * MXU_DIM:(q + 1) * MXU_DIM]], axis=1)
        for q in range(D_B // MXU_DIM)]).astype(BF16)
    bg = jnp.concatenate([lru_ba[l].reshape(1, D_B), lru_bx[l].reshape(1, D_B)], axis=1)

    group_pad = jnp.zeros((SUBLANES - N_GROUPS, D_MODEL), F32)
    wr = jnp.concatenate([jnp.transpose(w_rg[l]), group_pad, jnp.transpose(w_re[l])], axis=0)
    wr_hi = wr.astype(BF16)
    wr_lo = jnp.concatenate([wr_hi, (wr - wr_hi.astype(F32)).astype(BF16)], axis=0)
    br = jnp.concatenate([b_rg[l], jnp.zeros((SUBLANES - N_GROUPS,), F32), b_re[l]])[:, None]

    common = dict(lnv_g=lnv_g[l][None], lnv_b=lnv_b[l][None], conv_w=conv_w[l], conv_b=conv_b[l][None],
                  wg=wg, bg=bg, lam=lru_lam[l][None], w_out=w_out[l].astype(BF16),
                  ln1_g=ln1_g[l][None], ln1_b=ln1_b[l][None], wr_hi=wr_hi, wr_lo=wr_lo, br=br)
    w_in_b = w_in[l].astype(BF16)
    weights_p = _mixer_weights(w_in_b, wsm_p, bsf_p, **common)
    weights_s = _mixer_weights(w_in_b, wsm_s, bsf_s, **common)

    x1_p, h2_p, lg_p, cst_p, hst_p = _mixer_prompt(x_prompt, mod_p3, weights_p)

    cpad = jnp.concatenate(
        [jnp.zeros((dec_batch, SUBLANES - (CONV_W - 1), D_B), F32), state_conv[l]], axis=1
    ).reshape(n_sample, D_B)
    x1_s, h2_s, lg_s, xr_s, v_s, hs_s = _mixer_sample(
        x_sample.reshape(n_sample, D_MODEL), mod_s, weights_s, state_rglru_h[l], cpad)

    meta_i, meta_f, counts = _router_call(jnp.concatenate([lg_p, lg_s], axis=1))
    pos1, pos2 = meta_i[0], meta_i[1]
    xs = _scatter_call(pos1, pos2, h2_p, h2_s)
    ys = _ffn_call(_ffn_plan(counts[:, 0], n_pairs), xs, w1[l], w3[l], w2[l])

    ln2g, ln2b = ln2_g[l][None], ln2_b[l][None]
    y_prompt = _combine_call(True, pos1, pos2, x1_p, meta_f, mod_p3, ln2g, ln2b, ys,
                             0, batch, seq)
    y_sample = _combine_call(False, pos1, pos2, x1_s, meta_f, mod_s, ln2g, ln2b, ys,
                             n_prompt, dec_batch, dec_seq).reshape(dec_batch, dec_seq, D_MODEL)

    new_h_prompt = hst_p[:, SUBLANES - 1, :][None]
    new_conv_prompt = cst_p[:, SUBLANES - (CONV_W - 1):, :][None]
    xr_s3 = xr_s.reshape(dec_batch, dec_seq, D_B)
    new_h_sample = hs_s.reshape(dec_batch, dec_seq, D_B)[:, dec_seq - 1, :][None]
    new_conv_sample = xr_s3[:, dec_seq - (CONV_W - 1):, :][None]
    new_chunk_v_sample = v_s.reshape(dec_batch, dec_seq, D_A)[None]
    return (y_prompt, y_sample, new_h_prompt, new_conv_prompt, new_h_sample, new_conv_sample,
            new_chunk_v_sample)
```

```python
import functools
import math

import jax
import jax.numpy as jnp
from jax import lax
from jax.experimental import pallas as pl
from jax.experimental.pallas import tpu as pltpu

D_MODEL = 1024
D_A = 512
A_HEADS = 4
A_HEAD_DIM = 128
CHUNK = 128
D_B = 512
B_HEADS = 8
B_HEAD_DIM = 64
CONV_W = 4
LRU_C = 8.0
D_IN = 2 * D_A + 2 * D_B
N_GROUPS = 4
EXPERTS_PER_GROUP = 8
N_EXPERTS = 32
TOP_K = 2
D_EXPERT = 512
LN_EPS = 1e-5
DEPTH = 1
ALPHA = (2.0 * DEPTH) ** 0.25

LANES = 128
SUBLANES = 8
MXU_DIM = 256
VMEM_LIMIT = 56 * 1024 * 1024

ROWS = 512
ROUTER_COLS = 1024
LOGIT_ROWS = SUBLANES + N_EXPERTS
DISPATCH_ROWS = 1024
D_SLOTS = 3
FFN_ROWS = 256
X_SLOTS = 3
COMBINE_ROWS = 256
BF16 = jnp.bfloat16
F32 = jnp.float32


def _dot(a, b):
    return jnp.dot(a, b, preferred_element_type=F32)


def _dot_nt(a, b):
    return lax.dot_general(a, b, (((1,), (1,)), ((), ())), preferred_element_type=F32)


TILE_ROWS = D_MODEL // LANES


def _token_tile(r):
    return pl.ds(pl.multiple_of(r * TILE_ROWS, TILE_ROWS), TILE_ROWS)


def _load_token_tiles(ref, lead, n_rows):
    parts = [ref[lead + (pl.ds(s, n_rows, stride=TILE_ROWS), slice(None))] for s in range(TILE_ROWS)]
    return jnp.concatenate(parts, axis=1)


def _store_token_tiles(ref, val, keep=None):
    n_rows = val.shape[0]
    for s in range(TILE_ROWS):
        rows = pl.ds(s, n_rows, stride=TILE_ROWS)
        new = val[:, s * LANES:(s + 1) * LANES]
        if keep is not None:
            new = jnp.where(keep, new, ref[rows, :])
        ref[rows, :] = new


def _layer_norm(x, g, b):
    mu = jnp.mean(x, axis=-1, keepdims=True)
    xc = x - mu
    var = jnp.mean(xc * xc, axis=-1, keepdims=True)
    return xc * lax.rsqrt(var + LN_EPS) * g + b


def _gelu_tanh(x):
    c = math.sqrt(2.0 / math.pi)
    return x * (0.5 * (1.0 + jnp.tanh(c * (x + 0.044715 * (x * x * x)))))


def _const_spec(shape):
    nd = len(shape)
    return pl.BlockSpec(shape, lambda *_: (0,) * nd)


def _ada_kernel(cp_ref, cs_ref, w_ref, b_ref, op_ref, os_ref):
    w = w_ref[...].astype(BF16)
    for c_ref, o_ref in ((cp_ref, op_ref), (cs_ref, os_ref)):
        c = c_ref[...]
        a = (c * jax.nn.sigmoid(c)).astype(BF16)
        o_ref[...] = _dot(a, w) + b_ref[...]


def _ada_call(c_prompt, c_sample, w_ada, b_ada):
    tn = 1536
    rows_p, rows_s = c_prompt.shape[0], c_sample.shape[0]
    return pl.pallas_call(
        _ada_kernel,
        grid=(6 * D_MODEL // tn,),
        in_specs=[
            pl.BlockSpec((rows_p, D_MODEL), lambda j: (0, 0)),
            pl.BlockSpec((rows_s, D_MODEL), lambda j: (0, 0)),
            pl.BlockSpec((D_MODEL, tn), lambda j: (0, j)),
            pl.BlockSpec((1, tn), lambda j: (0, j)),
        ],
        out_specs=[pl.BlockSpec((rows_p, tn), lambda j: (0, j)),
                   pl.BlockSpec((rows_s, tn), lambda j: (0, j))],
        out_shape=[jax.ShapeDtypeStruct((rows_p, 6 * D_MODEL), F32),
                   jax.ShapeDtypeStruct((rows_s, 6 * D_MODEL), F32)],
        compiler_params=pltpu.CompilerParams(
            dimension_semantics=("arbitrary",), vmem_limit_bytes=VMEM_LIMIT),
        name="ada_mod",
    )(c_prompt, c_sample, w_ada, b_ada)


def _mixer_kernel(sample, *refs):
    if sample:
        (x_ref, mod_ref, win_ref, wsm_ref, bsf_ref, lnvg_ref, lnvb_ref, convw_ref, convb_ref,
         wg_ref, bg_ref, lam_ref, wout_ref, ln1g_ref, ln1b_ref, wrh_ref, wrl_ref, br_ref,
         h0_ref, cpad_ref,
         x1_ref, h2_ref, lg_ref, xr_ref, v_ref, hs_ref) = refs
    else:
        (x_ref, mod_ref, win_ref, wsm_ref, bsf_ref, lnvg_ref, lnvb_ref, convw_ref, convb_ref,
         wg_ref, bg_ref, lam_ref, wout_ref, ln1g_ref, ln1b_ref, wrh_ref, wrl_ref, br_ref,
         x1_ref, h2_ref, lg_ref, cst_ref, hst_ref, prev_ref, hcar_ref) = refs

    rows = ROWS
    groups = rows // SUBLANES

    if sample:
        x = x_ref[...]

        def chunk(c):
            return lambda b: mod_ref[b:b + 1, c * D_MODEL:(c + 1) * D_MODEL]
    else:
        x = x_ref[0]

        def chunk(c):
            return lambda b: mod_ref[0, :, c * D_MODEL:(c + 1) * D_MODEL]

        @pl.when(pl.program_id(1) == 0)
        def _():
            prev_ref[...] = jnp.zeros_like(prev_ref)
            hcar_ref[...] = jnp.zeros_like(hcar_ref)

    def modulated(fn, *vals):
        if not sample:
            return fn(0, *vals)
        parts = [fn(b, *(v[b * SUBLANES:(b + 1) * SUBLANES] for v in vals)) for b in range(groups)]
        return jnp.concatenate(parts, axis=0)

    sh1, sc1, g1, sh2, sc2, g2 = [chunk(c) for c in range(6)]

    h = modulated(lambda b, xv: xv * (1.0 + sc1(b)) + sh1(b), x)
    proj = _dot(h.astype(BF16), win_ref[...])
    u = proj[:, :D_A]
    v = _layer_norm(proj[:, D_A:2 * D_A], lnvg_ref[...], lnvb_ref[...])
    xr = proj[:, 2 * D_A:2 * D_A + D_B]
    yr = proj[:, 2 * D_A + D_B:]

    vb = v.astype(BF16)
    halves = []
    for r in range(rows // MXU_DIM):
        heads = [
            _dot(wsm_ref[hd], vb[r * MXU_DIM:(r + 1) * MXU_DIM, hd * A_HEAD_DIM:(hd + 1) * A_HEAD_DIM])
            for hd in range(A_HEADS)
        ]
        halves.append(jnp.concatenate(heads, axis=1) + bsf_ref[...])
    mixed = halves[0] if len(halves) == 1 else jnp.concatenate(halves, axis=0)
    out_a = u * mixed

    xr3 = xr.reshape(groups, SUBLANES, D_B)
    t3 = lax.broadcasted_iota(jnp.int32, (1, SUBLANES, D_B), 1)
    cw = convw_ref[...]
    if sample:
        cpad3 = cpad_ref[...].reshape(groups, SUBLANES, D_B)
    else:
        prev8 = prev_ref[...]
    xc3 = convb_ref[...][None]
    for k in range(CONV_W):
        s = CONV_W - 1 - k
        if s == 0:
            shifted = xr3
        else:
            rot = pltpu.roll(xr3, s, 1)
            if sample:
                rot_prev = pltpu.roll(cpad3, s, 1)
            else:
                first = pltpu.roll(prev8, s, 0)[None]
                rot_prev = jnp.concatenate([first, rot[:-1]], axis=0) if groups > 1 else first
            shifted = jnp.where(t3 >= s, rot, rot_prev)
        xc3 = xc3 + shifted * cw[k:k + 1, :][None]
    xc = xc3.reshape(rows, D_B)

    xcb = xc.astype(BF16)
    gq = [_dot(xcb[:, q * MXU_DIM:(q + 1) * MXU_DIM], wg_ref[q]) for q in range(D_B // MXU_DIM)]
    bg = bg_ref[...]
    pre_a = jnp.concatenate([g[:, :MXU_DIM] for g in gq], axis=1) + bg[:, :D_B]
    pre_x = jnp.concatenate([g[:, MXU_DIM:] for g in gq], axis=1) + bg[:, D_B:]
    gate_a = jax.nn.sigmoid(pre_a)
    gate_x = jax.nn.sigmoid(pre_x)
    nl = -lam_ref[...]
    softplus = jnp.maximum(nl, 0.0) + jnp.log1p(jnp.exp(-jnp.abs(nl)))
    log_a = -LRU_C * gate_a * softplus
    a = jnp.exp(log_a)
    mult = jnp.sqrt(-jnp.tanh(log_a) * (1.0 + a * a))
    bx = mult * (gate_x * xc)

    a3 = a.reshape(groups, SUBLANES, D_B)
    b3 = bx.reshape(groups, SUBLANES, D_B)
    for d in (1, 2, 4):
        m = t3 >= d
        a_sh = jnp.where(m, pltpu.roll(a3, d, 1), 1.0)
        b_sh = jnp.where(m, pltpu.roll(b3, d, 1), 0.0)
        b3 = a3 * b_sh + b3
        a3 = a3 * a_sh
    if sample:
        hs3 = jnp.stack([a3[g] * h0_ref[g:g + 1, :] + b3[g] for g in range(groups)], axis=0)
    else:
        h_prev = hcar_ref[...][SUBLANES - 1:SUBLANES, :]
        hs_list = []
        for g in range(groups):
            hg = a3[g] * h_prev + b3[g]
            hs_list.append(hg)
            h_prev = hg[SUBLANES - 1:SUBLANES, :]
        hs3 = jnp.stack(hs_list, axis=0)
    hs = hs3.reshape(rows, D_B)

    out_b = hs * _gelu_tanh(yr)
    mix = _dot(out_a.astype(BF16), wout_ref[:D_A, :]) + _dot(out_b.astype(BF16), wout_ref[D_A:, :])
    y1 = modulated(lambda b, xv, mv: ALPHA * xv + (1.0 + g1(b)) * mv, x, mix)
    x1 = _layer_norm(y1, ln1g_ref[...], ln1b_ref[...])
    h2 = modulated(lambda b, xv: xv * (1.0 + sc2(b)) + sh2(b), x1)

    h2_hi = h2.astype(BF16)
    h2_lo = (h2 - h2_hi.astype(F32)).astype(BF16)
    both = _dot_nt(wrl_ref[...], h2_hi)
    logits = (both[:LOGIT_ROWS] + _dot_nt(wrh_ref[...], h2_lo) + both[LOGIT_ROWS:] + br_ref[...])

    x1_ref[...] = x1
    _store_token_tiles(h2_ref, h2)
    lg_ref[...] = logits
    if sample:
        xr_ref[...] = xr
        v_ref[...] = v
        hs_ref[...] = hs
    else:
        last_x = xr3[groups - 1]
        last_h = hs3[groups - 1]
        prev_ref[...] = last_x
        hcar_ref[...] = last_h
        cst_ref[0] = last_x
        hst_ref[0] = last_h


def _mixer_weights(w_in, wsm, bsf, lnv_g, lnv_b, conv_w, conv_b, wg, bg, lam, w_out, ln1_g, ln1_b,
                   wr_hi, wr_lo, br):
    arrs = [w_in, wsm, bsf, lnv_g, lnv_b, conv_w, conv_b, wg, bg, lam, w_out, ln1_g, ln1_b,
            wr_hi, wr_lo, br]
    return arrs, [_const_spec(a.shape) for a in arrs]


def _mixer_prompt(x, mod3, weights):
    batch, seq, _ = x.shape
    n_tokens = batch * seq
    tiles = seq // ROWS
    arrs, specs = weights
    row_spec = lambda w: pl.BlockSpec((ROWS, w), lambda b, j: (b * tiles + j, 0))
    st_spec = pl.BlockSpec((1, SUBLANES, D_B), lambda b, j: (b, 0, 0))
    return pl.pallas_call(
        functools.partial(_mixer_kernel, False),
        grid=(batch, tiles),
        in_specs=[pl.BlockSpec((1, ROWS, D_MODEL), lambda b, j: (b, j, 0)),
                  pl.BlockSpec((1, 1, 6 * D_MODEL), lambda b, j: (b, 0, 0))] + specs,
        out_specs=[row_spec(D_MODEL),
                   pl.BlockSpec((ROWS * TILE_ROWS, LANES), lambda b, j: (b * tiles + j, 0)),
                   pl.BlockSpec((LOGIT_ROWS, ROWS), lambda b, j: (0, b * tiles + j)), st_spec, st_spec],
        out_shape=[jax.ShapeDtypeStruct((n_tokens, D_MODEL), F32),
                   jax.ShapeDtypeStruct((n_tokens * TILE_ROWS, LANES), F32),
                   jax.ShapeDtypeStruct((LOGIT_ROWS, n_tokens), F32),
                   jax.ShapeDtypeStruct((batch, SUBLANES, D_B), F32),
                   jax.ShapeDtypeStruct((batch, SUBLANES, D_B), F32)],
        scratch_shapes=[pltpu.VMEM((SUBLANES, D_B), F32), pltpu.VMEM((SUBLANES, D_B), F32)],
        compiler_params=pltpu.CompilerParams(
            dimension_semantics=("arbitrary", "arbitrary"), vmem_limit_bytes=VMEM_LIMIT),
        name="mixer_prompt",
    )(x, mod3, *arrs)


def _mixer_sample(x2, mod_seq, weights, h0, cpad):
    n_rows = x2.shape[0]
    arrs, specs = weights
    row = lambda w: pl.BlockSpec((ROWS, w), lambda i: (i, 0))
    seq_row = lambda w: pl.BlockSpec((ROWS // SUBLANES, w), lambda i: (i, 0))
    return pl.pallas_call(
        functools.partial(_mixer_kernel, True),
        grid=(n_rows // ROWS,),
        in_specs=[row(D_MODEL), seq_row(6 * D_MODEL)] + specs + [seq_row(D_B), row(D_B)],
        out_specs=[row(D_MODEL), pl.BlockSpec((ROWS * TILE_ROWS, LANES), lambda i: (i, 0)),
                   pl.BlockSpec((LOGIT_ROWS, ROWS), lambda i: (0, i)),
                   row(D_B), row(D_B), row(D_B)],
        out_shape=[jax.ShapeDtypeStruct((n_rows, D_MODEL), F32),
                   jax.ShapeDtypeStruct((n_rows * TILE_ROWS, LANES), F32),
                   jax.ShapeDtypeStruct((LOGIT_ROWS, n_rows), F32),
                   jax.ShapeDtypeStruct((n_rows, D_B), F32),
                   jax.ShapeDtypeStruct((n_rows, D_B), F32),
                   jax.ShapeDtypeStruct((n_rows, D_B), F32)],
        compiler_params=pltpu.CompilerParams(
            dimension_semantics=("arbitrary",), vmem_limit_bytes=VMEM_LIMIT),
        name="mixer_sample",
    )(x2, mod_seq, *arrs, h0, cpad)


def _router_kernel(lg_ref, upper_ref, lower_ref, mi_ref, mf_ref, cnt_ref, tot_ref, run_ref):
    phase = pl.program_id(0)
    step = pl.program_id(1)
    l = lg_ref[...]
    cols = l.shape[1]
    row8 = lax.broadcasted_iota(jnp.int32, (SUBLANES, cols), 0)
    neg = -jnp.inf
    big = jnp.int32(SUBLANES)

    def first_max(vals):
        top = jnp.max(vals, axis=0, keepdims=True)
        return top, jnp.min(jnp.where(vals == top, row8, big), axis=0, keepdims=True)

    is_group = row8 < N_GROUPS
    lg = l[:SUBLANES]
    m, gsel = first_max(jnp.where(is_group, lg, neg))
    denom = jnp.sum(jnp.where(is_group, jnp.exp(lg - m), 0.0), axis=0, keepdims=True)
    pgsel = 1.0 / denom
    le = l[SUBLANES:2 * SUBLANES]
    for g in range(1, N_GROUPS):
        le = jnp.where(gsel == g, l[(g + 1) * SUBLANES:(g + 2) * SUBLANES], le)
    v1, i1 = first_max(le)
    v2, i2 = first_max(jnp.where(row8 == i1, neg, le))
    ex = jnp.exp(v2 - v1)
    wk1 = pgsel / (1.0 + ex)
    wk2 = pgsel * ex / (1.0 + ex)
    e1 = gsel * EXPERTS_PER_GROUP + i1
    e2 = gsel * EXPERTS_PER_GROUP + i2
    row_e = lax.broadcasted_iota(jnp.int32, (N_EXPERTS, cols), 0)
    oh1 = jnp.where(row_e == e1, 1.0, 0.0)
    oh2 = jnp.where(row_e == e2, 1.0, 0.0)
    oh = oh1 + oh2

    @pl.when(jnp.logical_and(phase == 0, step == 0))
    def _():
        tot_ref[...] = jnp.zeros_like(tot_ref)

    @pl.when(phase == 0)
    def _():
        tot_ref[...] = tot_ref[...] + jnp.sum(oh, axis=1, keepdims=True)

    @pl.when(jnp.logical_and(phase == 1, step == 0))
    def _():
        tot = tot_ref[...]
        hi = (tot * (1.0 / MXU_DIM)).astype(jnp.int32).astype(F32)
        lo = tot - hi * MXU_DIM
        run_ref[...] = (_dot(lower_ref[...], hi.astype(BF16)) * MXU_DIM
                        + _dot(lower_ref[...], lo.astype(BF16)))

    @pl.when(phase == 1)
    def _():
        blocks = cols // LANES
        stacked = jnp.concatenate([oh[:, b * LANES:(b + 1) * LANES] for b in range(blocks)], axis=0)
        local = _dot(stacked.astype(BF16), upper_ref[...])
        run = run_ref[...]
        p1, p2 = [], []
        for b in range(blocks):
            sl = slice(b * LANES, (b + 1) * LANES)
            before = local[b * N_EXPERTS:(b + 1) * N_EXPERTS] + run
            p1.append(jnp.sum(oh1[:, sl] * before, axis=0, keepdims=True))
            p2.append(jnp.sum(oh2[:, sl] * before, axis=0, keepdims=True))
            run = run + jnp.sum(oh[:, sl], axis=1, keepdims=True)
        run_ref[...] = run
        pos1 = jnp.concatenate(p1, axis=1).astype(jnp.int32)
        pos2 = jnp.concatenate(p2, axis=1).astype(jnp.int32)
        zero_i = jnp.zeros_like(row8)
        mi_ref[...] = jnp.where(row8 == 0, pos1, jnp.where(row8 == 1, pos2, jnp.where(
            row8 == 2, e1, jnp.where(row8 == 3, e2, zero_i))))
        mf_ref[...] = jnp.where(row8 == 0, wk1, jnp.where(row8 == 1, wk2, 0.0))
        cnt_ref[...] = tot_ref[...].astype(jnp.int32)


def _router_call(lg_t):
    n_tokens = lg_t.shape[1]
    tc = ROUTER_COLS
    r = lax.broadcasted_iota(jnp.int32, (LANES, LANES), 0)
    c = lax.broadcasted_iota(jnp.int32, (LANES, LANES), 1)
    upper = jnp.where(r < c, 1.0, 0.0).astype(BF16)
    lower = jnp.where(c < r, 1.0, 0.0)[:N_EXPERTS, :N_EXPERTS].astype(BF16)
    col = pl.BlockSpec((SUBLANES, tc), lambda p, i: (0, i * p))
    return pl.pallas_call(
        _router_kernel,
        grid=(2, n_tokens // tc),
        in_specs=[pl.BlockSpec((LOGIT_ROWS, tc), lambda p, i: (0, i)),
                  _const_spec((LANES, LANES)), _const_spec((N_EXPERTS, N_EXPERTS))],
        out_specs=[col, col, _const_spec((N_EXPERTS, LANES))],
        out_shape=[jax.ShapeDtypeStruct((SUBLANES, n_tokens), jnp.int32),
                   jax.ShapeDtypeStruct((SUBLANES, n_tokens), F32),
                   jax.ShapeDtypeStruct((N_EXPERTS, LANES), jnp.int32)],
        scratch_shapes=[pltpu.VMEM((N_EXPERTS, LANES), F32), pltpu.VMEM((N_EXPERTS, LANES), F32)],
        compiler_params=pltpu.CompilerParams(
            dimension_semantics=("arbitrary", "arbitrary"), vmem_limit_bytes=VMEM_LIMIT),
        name="router",
    )(lg_t, upper, lower)


def _ffn_plan(counts, n_pairs):
    ends = jnp.cumsum(counts)
    off = ends - counts
    ids = jnp.arange(N_EXPERTS, dtype=jnp.int32)

    n_tiles = n_pairs // FFN_ROWS
    n_items = n_tiles + N_EXPERTS
    first_tile = off // FFN_ROWS
    last_tile = (ends - 1) // FFN_ROWS
    items_e = jnp.where(counts > 0, last_tile - first_tile + 1, 0)
    items_end = jnp.cumsum(items_e)
    items_start = items_end - items_e
    n_work = items_end[-1]
    w = jnp.arange(n_items, dtype=jnp.int32)
    wv = jnp.minimum(w, n_work - 1)
    e_w = jnp.sum((items_end[None, :] <= wv[:, None]).astype(jnp.int32), axis=1)
    e_w = jnp.minimum(e_w, N_EXPERTS - 1)
    sel = e_w[:, None] == ids[None, :]
    pick = lambda a: jnp.sum(jnp.where(sel, a[None, :], 0), axis=1)
    t_w = pick(first_tile) + (wv - pick(items_start))
    lo_w = jnp.maximum(pick(off), t_w * FFN_ROWS) - t_w * FFN_ROWS
    hi_w = jnp.minimum(pick(ends), (t_w + 1) * FFN_ROWS) - t_w * FFN_ROWS
    valid = w < n_work
    lo_w = jnp.where(valid, lo_w, 0)
    hi_w = jnp.where(valid, hi_w, 0)
    used = counts > 0
    order_e = jnp.cumsum(used.astype(jnp.int32)) - 1
    later_used = jnp.logical_and(used[None, :], ids[None, :] > ids[:, None])
    next_e = jnp.min(jnp.where(later_used, ids[None, :], N_EXPERTS), axis=1)
    next_e = jnp.where(next_e == N_EXPERTS, -1, next_e)
    i32 = lambda a: a.astype(jnp.int32)
    return i32(t_w), i32(e_w), i32(lo_w), i32(hi_w), i32(pick(order_e)), i32(pick(next_e))


def _scatter_kernel(tiles_p, n_steps, pos1_s, pos2_s, h2p_hbm, h2s_hbm, xs_hbm, buf, lsems, ssems):
    i = pl.program_id(0)
    tile_rows = DISPATCH_ROWS * TILE_ROWS

    def tile_load(step, do):
        slot = step % D_SLOTS
        for src, first, mine in ((h2p_hbm, step, step < tiles_p),
                                 (h2s_hbm, step - tiles_p, step >= tiles_p)):
            @pl.when(mine)
            def _():
                start = pl.multiple_of(first * tile_rows, tile_rows)
                do(pltpu.make_async_copy(src.at[pl.ds(start, tile_rows), :], buf.at[slot],
                                         lsems.at[slot]))

    def scatter(step):
        slot = step % D_SLOTS
        base = step * DISPATCH_ROWS

        def row_copy(r, pos_s):
            p = pos_s[base + r]
            return pltpu.make_async_copy(buf.at[slot, _token_tile(r), :], xs_hbm.at[_token_tile(p), :],
                                         ssems.at[slot])

        def start(r, carry):
            row_copy(r, pos1_s).start(priority=0)
            row_copy(r, pos2_s).start(priority=1)
            return carry

        def wait(r, carry):
            row_copy(r, pos1_s).wait()
            row_copy(r, pos2_s).wait()
            return carry

        return start, wait

    @pl.when(i == 0)
    def _():
        for ahead in range(D_SLOTS - 1):
            tile_load(ahead, lambda cp: cp.start())

    tile_load(i, lambda cp: cp.wait())
    lax.fori_loop(0, DISPATCH_ROWS, scatter(i)[0], 0, unroll=32)

    @pl.when(i >= 1)
    def _():
        lax.fori_loop(0, DISPATCH_ROWS, scatter(i - 1)[1], 0, unroll=8)

    @pl.when(i + D_SLOTS - 1 < n_steps)
    def _():
        tile_load(i + D_SLOTS - 1, lambda cp: cp.start())

    @pl.when(i == n_steps - 1)
    def _():
        lax.fori_loop(0, DISPATCH_ROWS, scatter(i)[1], 0, unroll=8)


def _scatter_call(pos1, pos2, h2_p, h2_s):
    tm = DISPATCH_ROWS
    tiles_p = h2_p.shape[0] // (tm * TILE_ROWS)
    tiles_s = h2_s.shape[0] // (tm * TILE_ROWS)
    n_steps = tiles_p + tiles_s
    assert n_steps >= D_SLOTS
    n_pairs = TOP_K * (h2_p.shape[0] + h2_s.shape[0]) // TILE_ROWS
    any_spec = pl.BlockSpec(memory_space=pl.ANY)
    return pl.pallas_call(
        functools.partial(_scatter_kernel, tiles_p, n_steps),
        grid_spec=pltpu.PrefetchScalarGridSpec(
            num_scalar_prefetch=2,
            grid=(n_steps,),
            in_specs=[any_spec, any_spec],
            out_specs=pl.BlockSpec(memory_space=pl.ANY),
            scratch_shapes=[pltpu.VMEM((D_SLOTS, tm * TILE_ROWS, LANES), F32),
                            pltpu.SemaphoreType.DMA((D_SLOTS,)),
                            pltpu.SemaphoreType.DMA((D_SLOTS,))],
        ),
        out_shape=jax.ShapeDtypeStruct((n_pairs * TILE_ROWS, LANES), F32),
        compiler_params=pltpu.CompilerParams(
            dimension_semantics=("arbitrary",), vmem_limit_bytes=VMEM_LIMIT),
        name="dispatch_scatter",
    )(pos1, pos2, h2_p, h2_s)


def _ffn_kernel(n_items, wt_s, we_s, wlo_s, whi_s, word_s, wnext_s, x_hbm, w1_hbm, w3_hbm, w2_hbm,
                o_ref, xbuf, w1f_ref, w3f_ref, w2f_ref, w13b_ref, w2b_ref, xsems, sems):
    w = pl.program_id(0)
    e = we_s[w]
    e_prev = we_s[jnp.maximum(w - 1, 0)]
    slot = word_s[w] % 2
    tile_rows = FFN_ROWS * TILE_ROWS

    def x_copy(item):
        first = pl.multiple_of(wt_s[item] * tile_rows, tile_rows)
        return pltpu.make_async_copy(x_hbm.at[pl.ds(first, tile_rows), :], xbuf.at[item % X_SLOTS],
                                     xsems.at[item % X_SLOTS])

    @pl.when(w == 0)
    def _():
        for ahead in range(X_SLOTS - 1):
            x_copy(ahead).start()

    @pl.when(w + X_SLOTS - 1 < n_items)
    def _():
        x_copy(w + X_SLOTS - 1).start()

    def weight_copies(expert, buf):
        return [pltpu.make_async_copy(src.at[expert], dst.at[buf], sems.at[buf])
                for src, dst in ((w1_hbm, w1f_ref), (w3_hbm, w3f_ref), (w2_hbm, w2f_ref))]

    @pl.when(w == 0)
    def _():
        for cp in weight_copies(e, slot):
            cp.start()

    @pl.when(jnp.logical_or(w == 0, e != e_prev))
    def _():
        for cp in weight_copies(e, slot):
            cp.wait()
        w13b_ref[:, :D_EXPERT] = w1f_ref[slot].astype(BF16)
        w13b_ref[:, D_EXPERT:] = w3f_ref[slot].astype(BF16)
        w2b_ref[...] = w2f_ref[slot].astype(BF16)
        nxt = wnext_s[w]

        @pl.when(nxt >= 0)
        def _():
            for cp in weight_copies(nxt, 1 - slot):
                cp.start()

    lo = wlo_s[w]
    hi = whi_s[w]
    x_copy(w).wait()

    @pl.when(hi > lo)
    def _():
        x = _load_token_tiles(xbuf, (w % X_SLOTS,), FFN_ROWS)
        h13 = _dot(x.astype(BF16), w13b_ref[...])
        a = h13[:, :D_EXPERT]
        b = h13[:, D_EXPERT:]
        hdn = (a * jax.nn.sigmoid(a)) * b
        y = _dot(hdn.astype(BF16), w2b_ref[...])
        rows = lax.broadcasted_iota(jnp.int32, (FFN_ROWS, LANES), 0)
        mine = jnp.logical_and(rows >= lo, rows < hi)

        @pl.when(lo == 0)
        def _():
            _store_token_tiles(o_ref, y)

        @pl.when(lo > 0)
        def _():
            _store_token_tiles(o_ref, y, keep=mine)


def _ffn_call(plan, xs, w1, w3, w2):
    t_w = plan[0]
    n_pairs = xs.shape[0] // TILE_ROWS
    tm = FFN_ROWS
    any_spec = pl.BlockSpec(memory_space=pl.ANY)
    row = pl.BlockSpec((tm * TILE_ROWS, LANES), lambda i, wt, *_: (wt[i], 0))
    n_items = t_w.shape[0]
    assert n_items >= X_SLOTS
    return pl.pallas_call(
        functools.partial(_ffn_kernel, n_items),
        grid_spec=pltpu.PrefetchScalarGridSpec(
            num_scalar_prefetch=len(plan),
            grid=(n_items,),
            in_specs=[any_spec, any_spec, any_spec, any_spec],
            out_specs=row,
            scratch_shapes=[pltpu.VMEM((X_SLOTS, tm * TILE_ROWS, LANES), F32),
                            pltpu.VMEM((2, D_MODEL, D_EXPERT), F32),
                            pltpu.VMEM((2, D_MODEL, D_EXPERT), F32),
                            pltpu.VMEM((2, D_EXPERT, D_MODEL), F32),
                            pltpu.VMEM((D_MODEL, 2 * D_EXPERT), BF16),
                            pltpu.VMEM((D_EXPERT, D_MODEL), BF16),
                            pltpu.SemaphoreType.DMA((X_SLOTS,)),
                            pltpu.SemaphoreType.DMA((2,))],
        ),
        out_shape=jax.ShapeDtypeStruct((n_pairs * TILE_ROWS, LANES), F32),
        compiler_params=pltpu.CompilerParams(
            dimension_semantics=("arbitrary",), vmem_limit_bytes=VMEM_LIMIT),
        name="expert_ffn",
    )(*plan, xs, w1, w3, w2)


def _combine_kernel(three_d, row0, n_steps, pos1_s, pos2_s, x1_ref, wk_ref, mod_ref, g_ref, b_ref,
                    ys_hbm, o_ref, ybuf, sems):
    step = pl.program_id(0)
    slot = step % 2

    def gather(tile, tile_slot):
        base = row0 + tile * COMBINE_ROWS

        def row_copy(r, k, pos_s):
            p = pos_s[base + r]
            return pltpu.make_async_copy(ys_hbm.at[_token_tile(p), :],
                                         ybuf.at[tile_slot, k, _token_tile(r), :], sems.at[tile_slot])

        def start(r, carry):
            row_copy(r, 0, pos1_s).start(priority=0)
            row_copy(r, 1, pos2_s).start(priority=1)
            return carry

        def wait(r, carry):
            row_copy(r, 0, pos1_s).wait()
            row_copy(r, 1, pos2_s).wait()
            return carry

        return start, wait

    @pl.when(step == 0)
    def _():
        lax.fori_loop(0, COMBINE_ROWS, gather(step, slot)[0], 0, unroll=32)

    @pl.when(step + 1 < n_steps)
    def _():
        lax.fori_loop(0, COMBINE_ROWS, gather(step + 1, 1 - slot)[0], 0, unroll=32)

    lax.fori_loop(0, COMBINE_ROWS, gather(step, slot)[1], 0, unroll=8)

    wk = wk_ref[...]
    r = lax.broadcasted_iota(jnp.int32, (COMBINE_ROWS, COMBINE_ROWS), 0)
    c = lax.broadcasted_iota(jnp.int32, (COMBINE_ROWS, COMBINE_ROWS), 1)
    wk_col = [jnp.sum(jnp.where(r == c, wk[k:k + 1, :], 0.0), axis=1, keepdims=True)
              for k in range(TOP_K)]
    moe = (wk_col[0] * _load_token_tiles(ybuf, (slot, 0), COMBINE_ROWS)
           + wk_col[1] * _load_token_tiles(ybuf, (slot, 1), COMBINE_ROWS))
    x1 = x1_ref[...]
    if three_d:
        g2 = mod_ref[0, :, 5 * D_MODEL:]
        o_ref[0] = _layer_norm(ALPHA * x1 + (1.0 + g2) * moe, g_ref[...], b_ref[...])
    else:
        def seq_rows(v, b):
            return v[b * SUBLANES:(b + 1) * SUBLANES]

        resid = jnp.concatenate([
            ALPHA * seq_rows(x1, b) + (1.0 + mod_ref[b:b + 1, 5 * D_MODEL:]) * seq_rows(moe, b)
            for b in range(COMBINE_ROWS // SUBLANES)], axis=0)
        o_ref[...] = _layer_norm(resid, g_ref[...], b_ref[...])


def _combine_call(three_d, pos1, pos2, x1_all, mf_all, mod, ln2_g, ln2_b, ys, row0, batch, seq):
    tm = COMBINE_ROWS
    blk0 = row0 // tm
    if three_d:
        tiles = seq // tm
        grid = (batch * tiles,)
        mod_spec = pl.BlockSpec((1, 1, 6 * D_MODEL), lambda i, *_: (i // tiles, 0, 0))
        out_spec = pl.BlockSpec((1, tm, D_MODEL), lambda i, *_: (i // tiles, i % tiles, 0))
        out_shape = jax.ShapeDtypeStruct((batch, seq, D_MODEL), F32)
    else:
        grid = (batch * seq // tm,)
        mod_spec = pl.BlockSpec((tm // SUBLANES, 6 * D_MODEL), lambda i, *_: (i, 0))
        out_spec = pl.BlockSpec((tm, D_MODEL), lambda i, *_: (i, 0))
        out_shape = jax.ShapeDtypeStruct((batch * seq, D_MODEL), F32)
    return pl.pallas_call(
        functools.partial(_combine_kernel, three_d, row0, grid[0]),
        grid_spec=pltpu.PrefetchScalarGridSpec(
            num_scalar_prefetch=2,
            grid=grid,
            in_specs=[pl.BlockSpec((tm, D_MODEL), lambda i, *_: (i, 0)),
                      pl.BlockSpec((SUBLANES, tm), lambda i, *_: (0, blk0 + i)),
                      mod_spec,
                      pl.BlockSpec(ln2_g.shape, lambda i, *_: (0, 0)),
                      pl.BlockSpec(ln2_b.shape, lambda i, *_: (0, 0)),
                      pl.BlockSpec(memory_space=pl.ANY)],
            out_specs=out_spec,
            scratch_shapes=[pltpu.VMEM((2, TOP_K, tm * TILE_ROWS, LANES), F32),
                            pltpu.SemaphoreType.DMA((2,))],
        ),
        out_shape=out_shape,
        compiler_params=pltpu.CompilerParams(
            dimension_semantics=("arbitrary",), vmem_limit_bytes=VMEM_LIMIT),
        name="combine_prompt" if three_d else "combine_sample",
    )(pos1, pos2, x1_all, mf_all, mod, ln2_g, ln2_b, ys)


def _block_diag(w):
    heads, n, _ = w.shape
    eye = jnp.eye(heads, dtype=w.dtype)
    return (eye[:, None, :, None] * w[:, :, None, :]).reshape(heads * n, heads * n)


def kernel(x_prompt, x_sample, state_rglru_h, state_conv, c_prompt, c_sample, w_ada, b_ada, w_in, w_s, b_s, lnv_g, lnv_b, conv_w, conv_b, lru_wa, lru_ba, lru_wx, lru_bx, lru_lam, w_out, ln1_g, ln1_b, w_rg, b_rg, w_re, b_re, w1, w3, w2, ln2_g, ln2_b):
    batch, seq, _ = x_prompt.shape
    dec_batch, dec_seq, _ = x_sample.shape
    assert dec_seq == SUBLANES and seq % ROWS == 0 and (dec_batch * dec_seq) % ROWS == 0
    n_prompt = batch * seq
    n_sample = dec_batch * dec_seq
    n_tokens = n_prompt + n_sample
    n_pairs = TOP_K * n_tokens
    assert n_tokens % ROUTER_COLS == 0 and n_pairs % FFN_ROWS == 0
    l = 0

    mod_p, mod_s = _ada_call(c_prompt, c_sample, w_ada[l], b_ada[l][None])
    mod_p3 = mod_p[:, None, :]

    tri = jnp.tril(jnp.ones((CHUNK, CHUNK), dtype=bool))
    ws_tri = jnp.where(tri[None], w_s[l], 0.0)
    reps_p = MXU_DIM // CHUNK
    eye_p = jnp.eye(reps_p, dtype=F32)
    wsm_p = (eye_p[None, :, None, :, None] * ws_tri[:, None, :, None, :]).reshape(
        A_HEADS, MXU_DIM, MXU_DIM).astype(BF16)
    reps_s = MXU_DIM // dec_seq
    pos_in_seq = jnp.arange(MXU_DIM, dtype=jnp.int32) % dec_seq
    seq_of_row = jnp.arange(MXU_DIM, dtype=jnp.int32) // dec_seq
    spread = (pos_in_seq[:, None] == jnp.arange(dec_seq, dtype=jnp.int32)[None, :]).astype(F32)
    same_seq = seq_of_row[:, None] == seq_of_row[None, :]
    wsm_s = jnp.einsum("ri,hij,cj->hrc", spread, ws_tri[:, :dec_seq, :dec_seq], spread,
                       precision=lax.Precision.HIGHEST)
    wsm_s = jnp.where(same_seq[None], wsm_s, 0.0).astype(BF16)
    bs_pos = jnp.repeat(jnp.transpose(b_s[l]), A_HEAD_DIM, axis=1)
    bsf_p = jnp.tile(bs_pos, (reps_p, 1))
    bsf_s = jnp.tile(bs_pos[:dec_seq], (reps_s, 1))

    wa_bd = _block_diag(lru_wa[l])
    wx_bd = _block_diag(lru_wx[l])
    wg = jnp.stack([
        jnp.concatenate([wa_bd[q * MXU_DIM:(q + 1) * MXU_DIM, q * MXU_DIM:(q + 1) * MXU_DIM],
                         wx_bd[q * MXU_DIM:(q + 1) * MXU_DIM, q * MXU_DIM:(q + 1) * MXU_DIM]], axis=1)
        for q in range(D_B // MXU_DIM)]).astype(BF16)
    bg = jnp.concatenate([lru_ba[l].reshape(1, D_B), lru_bx[l].reshape(1, D_B)], axis=1)

    group_pad = jnp.zeros((SUBLANES - N_GROUPS, D_MODEL), F32)
    wr = jnp.concatenate([jnp.transpose(w_rg[l]), group_pad, jnp.transpose(w_re[l])], axis=0)
    wr_hi = wr.astype(BF16)
    wr_lo = jnp.concatenate([wr_hi, (wr - wr_hi.astype(F32)).astype(BF16)], axis=0)
    br = jnp.concatenate([b_rg[l], jnp.zeros((SUBLANES - N_GROUPS,), F32), b_re[l]])[:, None]

    common = dict(lnv_g=lnv_g[l][None], lnv_b=lnv_b[l][None], conv_w=conv_w[l], conv_b=conv_b[l][None],
                  wg=wg, bg=bg, lam=lru_lam[l][None], w_out=w_out[l].astype(BF16),
                  ln1_g=ln1_g[l][None], ln1_b=ln1_b[l][None], wr_hi=wr_hi, wr_lo=wr_lo, br=br)
    w_in_b = w_in[l].astype(BF16)
    weights_p = _mixer_weights(w_in_b, wsm_p, bsf_p, **common)
    weights_s = _mixer_weights(w_in_b, wsm_s, bsf_s, **common)

    x1_p, h2_p, lg_p, cst_p, hst_p = _mixer_prompt(x_prompt, mod_p3, weights_p)

    cpad = jnp.concatenate(
        [jnp.zeros((dec_batch, SUBLANES - (CONV_W - 1), D_B), F32), state_conv[l]], axis=1
    ).reshape(n_sample, D_B)
    x1_s, h2_s, lg_s, xr_s, v_s, hs_s = _mixer_sample(
        x_sample.reshape(n_sample, D_MODEL), mod_s, weights_s, state_rglru_h[l], cpad)

    meta_i, meta_f, counts = _router_call(jnp.concatenate([lg_p, lg_s], axis=1))
    pos1, pos2 = meta_i[0], meta_i[1]
    xs = _scatter_call(pos1, pos2, h2_p, h2_s)
    ys = _ffn_call(_ffn_plan(counts[:, 0], n_pairs), xs, w1[l], w3[l], w2[l])

    ln2g, ln2b = ln2_g[l][None], ln2_b[l][None]
    y_prompt = _combine_call(True, pos1, pos2, x1_p, meta_f, mod_p3, ln2g, ln2b, ys,
                             0, batch, seq)
    y_sample = _combine_call(False, pos1, pos2, x1_s, meta_f, mod_s, ln2g, ln2b, ys,
                             n_prompt, dec_batch, dec_seq).reshape(dec_batch, dec_seq, D_MODEL)

    new_h_prompt = hst_p[:, SUBLANES - 1, :][None]
    new_conv_prompt = cst_p[:, SUBLANES - (CONV_W - 1):, :][None]
    xr_s3 = xr_s.reshape(dec_batch, dec_seq, D_B)
    new_h_sample = hs_s.reshape(dec_batch, dec_seq, D_B)[:, dec_seq - 1, :][None]
    new_conv_sample = xr_s3[:, dec_seq - (CONV_W - 1):, :][None]
    new_chunk_v_sample = v_s.reshape(dec_batch, dec_seq, D_A)[None]
    return (y_prompt, y_sample, new_h_prompt, new_conv_prompt, new_h_sample, new_conv_sample,
            new_chunk_v_sample)
```

```python
import functools
import math

import jax
import jax.numpy as jnp
from jax import lax
from jax.experimental import pallas as pl
from jax.experimental.pallas import tpu as pltpu

D_MODEL = 1024
D_A = 512
A_HEADS = 4
A_HEAD_DIM = 128
CHUNK = 128
D_B = 512
B_HEADS = 8
B_HEAD_DIM = 64
CONV_W = 4
LRU_C = 8.0
D_IN = 2 * D_A + 2 * D_B
N_GROUPS = 4
EXPERTS_PER_GROUP = 8
N_EXPERTS = 32
TOP_K = 2
D_EXPERT = 512
LN_EPS = 1e-5
DEPTH = 1
ALPHA = (2.0 * DEPTH) ** 0.25

LANES = 128
SUBLANES = 8
MXU_DIM = 256
VMEM_LIMIT = 56 * 1024 * 1024

ROWS = 512
ROUTER_COLS = 1024
LOGIT_ROWS = SUBLANES + N_EXPERTS
DISPATCH_ROWS = 1024
D_SLOTS = 3
FFN_ROWS = 256
X_SLOTS = 3
COMBINE_ROWS = 256
BF16 = jnp.bfloat16
F32 = jnp.float32


def _dot(a, b):
    return jnp.dot(a, b, preferred_element_type=F32)


def _dot_nt(a, b):
    return lax.dot_general(a, b, (((1,), (1,)), ((), ())), preferred_element_type=F32)


TILE_ROWS = D_MODEL // LANES


def _token_tile(r):
    return pl.ds(pl.multiple_of(r * TILE_ROWS, TILE_ROWS), TILE_ROWS)


def _load_token_tiles(ref, lead, n_rows):
    parts = [ref[lead + (pl.ds(s, n_rows, stride=TILE_ROWS), slice(None))] for s in range(TILE_ROWS)]
    return jnp.concatenate(parts, axis=1)


def _store_token_tiles(ref, val, keep=None):
    n_rows = val.shape[0]
    for s in range(TILE_ROWS):
        rows = pl.ds(s, n_rows, stride=TILE_ROWS)
        new = val[:, s * LANES:(s + 1) * LANES]
        if keep is not None:
            new = jnp.where(keep, new, ref[rows, :])
        ref[rows, :] = new


def _layer_norm(x, g, b):
    mu = jnp.mean(x, axis=-1, keepdims=True)
    xc = x - mu
    var = jnp.mean(xc * xc, axis=-1, keepdims=True)
    return xc * lax.rsqrt(var + LN_EPS) * g + b


def _gelu_tanh(x):
    c = math.sqrt(2.0 / math.pi)
    return x * (0.5 * (1.0 + jnp.tanh(c * (x + 0.044715 * (x * x * x)))))


def _const_spec(shape):
    nd = len(shape)
    return pl.BlockSpec(shape, lambda *_: (0,) * nd)


def _ada_kernel(cp_ref, cs_ref, w_ref, b_ref, op_ref, os_ref):
    w = w_ref[...].astype(BF16)
    for c_ref, o_ref in ((cp_ref, op_ref), (cs_ref, os_ref)):
        c = c_ref[...]
        a = (c * jax.nn.sigmoid(c)).astype(BF16)
        o_ref[...] = _dot(a, w) + b_ref[...]


def _ada_call(c_prompt, c_sample, w_ada, b_ada):
    tn = 1536
    rows_p, rows_s = c_prompt.shape[0], c_sample.shape[0]
    return pl.pallas_call(
        _ada_kernel,
        grid=(6 * D_MODEL // tn,),
        in_specs=[
            pl.BlockSpec((rows_p, D_MODEL), lambda j: (0, 0)),
            pl.BlockSpec((rows_s, D_MODEL), lambda j: (0, 0)),
            pl.BlockSpec((D_MODEL, tn), lambda j: (0, j)),
            pl.BlockSpec((1, tn), lambda j: (0, j)),
        ],
        out_specs=[pl.BlockSpec((rows_p, tn), lambda j: (0, j)),
                   pl.BlockSpec((rows_s, tn), lambda j: (0, j))],
        out_shape=[jax.ShapeDtypeStruct((rows_p, 6 * D_MODEL), F32),
                   jax.ShapeDtypeStruct((rows_s, 6 * D_MODEL), F32)],
        compiler_params=pltpu.CompilerParams(
            dimension_semantics=("arbitrary",), vmem_limit_bytes=VMEM_LIMIT),
        name="ada_mod",
    )(c_prompt, c_sample, w_ada, b_ada)


def _mixer_kernel(sample, *refs):
    if sample:
        (x_ref, mod_ref, win_ref, wsm_ref, bsf_ref, lnvg_ref, lnvb_ref, convw_ref, convb_ref,
         wg_ref, bg_ref, lam_ref, wout_ref, ln1g_ref, ln1b_ref, wrh_ref, wrl_ref, br_ref,
         h0_ref, cpad_ref,
         x1_ref, h2_ref, lg_ref, xr_ref, v_ref, hs_ref) = refs
    else:
        (x_ref, mod_ref, win_ref, wsm_ref, bsf_ref, lnvg_ref, lnvb_ref, convw_ref, convb_ref,
         wg_ref, bg_ref, lam_ref, wout_ref, ln1g_ref, ln1b_ref, wrh_ref, wrl_ref, br_ref,
         x1_ref, h2_ref, lg_ref, cst_ref, hst_ref, prev_ref, hcar_ref) = refs

    rows = ROWS
    groups = rows // SUBLANES

    if sample:
        x = x_ref[...]

        def chunk(c):
            return lambda b: mod_ref[b:b + 1, c * D_MODEL:(c + 1) * D_MODEL]
    else:
        x = x_ref[0]

        def chunk(c):
            return lambda b: mod_ref[0, :, c * D_MODEL:(c + 1) * D_MODEL]

        @pl.when(pl.program_id(1) == 0)
        def _():
            prev_ref[...] = jnp.zeros_like(prev_ref)
            hcar_ref[...] = jnp.zeros_like(hcar_ref)

    def modulated(fn, *vals):
        if not sample:
            return fn(0, *vals)
        parts = [fn(b, *(v[b * SUBLANES:(b + 1) * SUBLANES] for v in vals)) for b in range(groups)]
        return jnp.concatenate(parts, axis=0)

    sh1, sc1, g1, sh2, sc2, g2 = [chunk(c) for c in range(6)]

    h = modulated(lambda b, xv: xv * (1.0 + sc1(b)) + sh1(b), x)
    proj = _dot(h.astype(BF16), win_ref[...])
    u = proj[:, :D_A]
    v = _layer_norm(proj[:, D_A:2 * D_A], lnvg_ref[...], lnvb_ref[...])
    xr = proj[:, 2 * D_A:2 * D_A + D_B]
    yr = proj[:, 2 * D_A + D_B:]

    vb = v.astype(BF16)
    halves = []
    for r in range(rows // MXU_DIM):
        heads = [
            _dot(wsm_ref[hd], vb[r * MXU_DIM:(r + 1) * MXU_DIM, hd * A_HEAD_DIM:(hd + 1) * A_HEAD_DIM])
            for hd in range(A_HEADS)
        ]
        halves.append(jnp.concatenate(heads, axis=1) + bsf_ref[...])
    mixed = halves[0] if len(halves) == 1 else jnp.concatenate(halves, axis=0)
    out_a = u * mixed

    xr3 = xr.reshape(groups, SUBLANES, D_B)
    t3 = lax.broadcasted_iota(jnp.int32, (1, SUBLANES, D_B), 1)
    cw = convw_ref[...]
    if sample:
        cpad3 = cpad_ref[...].reshape(groups, SUBLANES, D_B)
    else:
        prev8 = prev_ref[...]
    xc3 = convb_ref[...][None]
    for k in range(CONV_W):
        s = CONV_W - 1 - k
        if s == 0:
            shifted = xr3
        else:
            rot = pltpu.roll(xr3, s, 1)
            if sample:
                rot_prev = pltpu.roll(cpad3, s, 1)
            else:
                first = pltpu.roll(prev8, s, 0)[None]
                rot_prev = jnp.concatenate([first, rot[:-1]], axis=0) if groups > 1 else first
            shifted = jnp.where(t3 >= s, rot, rot_prev)
        xc3 = xc3 + shifted * cw[k:k + 1, :][None]
    xc = xc3.reshape(rows, D_B)

    xcb = xc.astype(BF16)
    gq = [_dot(xcb[:, q * MXU_DIM:(q + 1) * MXU_DIM], wg_ref[q]) for q in range(D_B // MXU_DIM)]
    bg = bg_ref[...]
    pre_a = jnp.concatenate([g[:, :MXU_DIM] for g in gq], axis=1) + bg[:, :D_B]
    pre_x = jnp.concatenate([g[:, MXU_DIM:] for g in gq], axis=1) + bg[:, D_B:]
    gate_a = jax.nn.sigmoid(pre_a)
    gate_x = jax.nn.sigmoid(pre_x)
    nl = -lam_ref[...]
    softplus = jnp.maximum(nl, 0.0) + jnp.log1p(jnp.exp(-jnp.abs(nl)))
    log_a = -LRU_C * gate_a * softplus
    a = jnp.exp(log_a)
    mult = jnp.sqrt(-jnp.tanh(log_a) * (1.0 + a * a))
    bx = mult * (gate_x * xc)

    a3 = a.reshape(groups, SUBLANES, D_B)
    b3 = bx.reshape(groups, SUBLANES, D_B)
    for d in (1, 2, 4):
        m = t3 >= d
        a_sh = jnp.where(m, pltpu.roll(a3, d, 1), 1.0)
        b_sh = jnp.where(m, pltpu.roll(b3, d, 1), 0.0)
        b3 = a3 * b_sh + b3
        a3 = a3 * a_sh
    if sample:
        hs3 = jnp.stack([a3[g] * h0_ref[g:g + 1, :] + b3[g] for g in range(groups)], axis=0)
    else:
        h_prev = hcar_ref[...][SUBLANES - 1:SUBLANES, :]
        hs_list = []
        for g in range(groups):
            hg = a3[g] * h_prev + b3[g]
            hs_list.append(hg)
            h_prev = hg[SUBLANES - 1:SUBLANES, :]
        hs3 = jnp.stack(hs_list, axis=0)
    hs = hs3.reshape(rows, D_B)

    out_b = hs * _gelu_tanh(yr)
    mix = _dot(out_a.astype(BF16), wout_ref[:D_A, :]) + _dot(out_b.astype(BF16), wout_ref[D_A:, :])
    y1 = modulated(lambda b, xv, mv: ALPHA * xv + (1.0 + g1(b)) * mv, x, mix)
    x1 = _layer_norm(y1, ln1g_ref[...], ln1b_ref[...])
    h2 = modulated(lambda b, xv: xv * (1.0 + sc2(b)) + sh2(b), x1)

    h2_hi = h2.astype(BF16)
    h2_lo = (h2 - h2_hi.astype(F32)).astype(BF16)
    both = _dot_nt(wrl_ref[...], h2_hi)
    logits = (both[:LOGIT_ROWS] + _dot_nt(wrh_ref[...], h2_lo) + both[LOGIT_ROWS:] + br_ref[...])

    x1_ref[...] = x1
    _store_token_tiles(h2_ref, h2)
    lg_ref[...] = logits
    if sample:
        xr_ref[...] = xr
        v_ref[...] = v
        hs_ref[...] = hs
    else:
        last_x = xr3[groups - 1]
        last_h = hs3[groups - 1]
        prev_ref[...] = last_x
        hcar_ref[...] = last_h
        cst_ref[0] = last_x
        hst_ref[0] = last_h


def _mixer_weights(w_in, wsm, bsf, lnv_g, lnv_b, conv_w, conv_b, wg, bg, lam, w_out, ln1_g, ln1_b,
                   wr_hi, wr_lo, br):
    arrs = [w_in, wsm, bsf, lnv_g, lnv_b, conv_w, conv_b, wg, bg, lam, w_out, ln1_g, ln1_b,
            wr_hi, wr_lo, br]
    return arrs, [_const_spec(a.shape) for a in arrs]


def _mixer_prompt(x, mod3, weights):
    batch, seq, _ = x.shape
    n_tokens = batch * seq
    tiles = seq // ROWS
    arrs, specs = weights
    row_spec = lambda w: pl.BlockSpec((ROWS, w), lambda b, j: (b * tiles + j, 0))
    st_spec = pl.BlockSpec((1, SUBLANES, D_B), lambda b, j: (b, 0, 0))
    return pl.pallas_call(
        functools.partial(_mixer_kernel, False),
        grid=(batch, tiles),
        in_specs=[pl.BlockSpec((1, ROWS, D_MODEL), lambda b, j: (b, j, 0)),
                  pl.BlockSpec((1, 1, 6 * D_MODEL), lambda b, j: (b, 0, 0))] + specs,
        out_specs=[row_spec(D_MODEL),
                   pl.BlockSpec((ROWS * TILE_ROWS, LANES), lambda b, j: (b * tiles + j, 0)),
                   pl.BlockSpec((LOGIT_ROWS, ROWS), lambda b, j: (0, b * tiles + j)), st_spec, st_spec],
        out_shape=[jax.ShapeDtypeStruct((n_tokens, D_MODEL), F32),
                   jax.ShapeDtypeStruct((n_tokens * TILE_ROWS, LANES), F32),
                   jax.ShapeDtypeStruct((LOGIT_ROWS, n_tokens), F32),
                   jax.ShapeDtypeStruct((batch, SUBLANES, D_B), F32),
                   jax.ShapeDtypeStruct((batch, SUBLANES, D_B), F32)],
        scratch_shapes=[pltpu.VMEM((SUBLANES, D_B), F32), pltpu.VMEM((SUBLANES, D_B), F32)],
        compiler_params=pltpu.CompilerParams(
            dimension_semantics=("arbitrary", "arbitrary"), vmem_limit_bytes=VMEM_LIMIT),
        name="mixer_prompt",
    )(x, mod3, *arrs)


def _mixer_sample(x2, mod_seq, weights, h0, cpad):
    n_rows = x2.shape[0]
    arrs, specs = weights
    row = lambda w: pl.BlockSpec((ROWS, w), lambda i: (i, 0))
    seq_row = lambda w: pl.BlockSpec((ROWS // SUBLANES, w), lambda i: (i, 0))
    return pl.pallas_call(
        functools.partial(_mixer_kernel, True),
        grid=(n_rows // ROWS,),
        in_specs=[row(D_MODEL), seq_row(6 * D_MODEL)] + specs + [seq_row(D_B), row(D_B)],
        out_specs=[row(D_MODEL), pl.BlockSpec((ROWS * TILE_ROWS, LANES), lambda i: (i, 0)),
                   pl.BlockSpec((LOGIT_ROWS, ROWS), lambda i: (0, i)),
                   row(D_B), row(D_B), row(D_B)],
        out_shape=[jax.ShapeDtypeStruct((n_rows, D_MODEL), F32),
                   jax.ShapeDtypeStruct((n_rows * TILE_ROWS, LANES), F32),
                   jax.ShapeDtypeStruct((LOGIT_ROWS, n_rows), F32),
                   jax.ShapeDtypeStruct((n_rows, D_B), F32),
                   jax.ShapeDtypeStruct((n_rows, D_B), F32),
                   jax.ShapeDtypeStruct((n_rows, D_B), F32)],
        compiler_params=pltpu.CompilerParams(
            dimension_semantics=("arbitrary",), vmem_limit_bytes=VMEM_LIMIT),
        name="mixer_sample",
    )(x2, mod_seq, *arrs, h0, cpad)


def _router_kernel(lg_ref, upper_ref, lower_ref, mi_ref, mf_ref, cnt_ref, tot_ref, run_ref):
    phase = pl.program_id(0)
    step = pl.program_id(1)
    l = lg_ref[...]
    cols = l.shape[1]
    row8 = lax.broadcasted_iota(jnp.int32, (SUBLANES, cols), 0)
    neg = -jnp.inf
    big = jnp.int32(SUBLANES)

    def first_max(vals):
        top = jnp.max(vals, axis=0, keepdims=True)
        return top, jnp.min(jnp.where(vals == top, row8, big), axis=0, keepdims=True)

    is_group = row8 < N_GROUPS
    lg = l[:SUBLANES]
    m, gsel = first_max(jnp.where(is_group, lg, neg))
    denom = jnp.sum(jnp.where(is_group, jnp.exp(lg - m), 0.0), axis=0, keepdims=True)
    pgsel = 1.0 / denom
    le = l[SUBLANES:2 * SUBLANES]
    for g in range(1, N_GROUPS):
        le = jnp.where(gsel == g, l[(g + 1) * SUBLANES:(g + 2) * SUBLANES], le)
    v1, i1 = first_max(le)
    v2, i2 = first_max(jnp.where(row8 == i1, neg, le))
    ex = jnp.exp(v2 - v1)
    wk1 = pgsel / (1.0 + ex)
    wk2 = pgsel * ex / (1.0 + ex)
    e1 = gsel * EXPERTS_PER_GROUP + i1
    e2 = gsel * EXPERTS_PER_GROUP + i2
    row_e = lax.broadcasted_iota(jnp.int32, (N_EXPERTS, cols), 0)
    oh1 = jnp.where(row_e == e1, 1.0, 0.0)
    oh2 = jnp.where(row_e == e2, 1.0, 0.0)
    oh = oh1 + oh2

    @pl.when(jnp.logical_and(phase == 0, step == 0))
    def _():
        tot_ref[...] = jnp.zeros_like(tot_ref)

    @pl.when(phase == 0)
    def _():
        tot_ref[...] = tot_ref[...] + jnp.sum(oh, axis=1, keepdims=True)

    @pl.when(jnp.logical_and(phase == 1, step == 0))
    def _():
        tot = tot_ref[...]
        tiles = ((tot + (FFN_ROWS - 1.0)) * (1.0 / FFN_ROWS)).astype(jnp.int32).astype(F32)
        run_ref[...] = _dot(lower_ref[...], tiles.astype(BF16)) * FFN_ROWS

    @pl.when(phase == 1)
    def _():
        blocks = cols // LANES
        stacked = jnp.concatenate([oh[:, b * LANES:(b + 1) * LANES] for b in range(blocks)], axis=0)
        local = _dot(stacked.astype(BF16), upper_ref[...])
        run = run_ref[...]
        p1, p2 = [], []
        for b in range(blocks):
            sl = slice(b * LANES, (b + 1) * LANES)
            before = local[b * N_EXPERTS:(b + 1) * N_EXPERTS] + run
            p1.append(jnp.sum(oh1[:, sl] * before, axis=0, keepdims=True))
            p2.append(jnp.sum(oh2[:, sl] * before, axis=0, keepdims=True))
            run = run + jnp.sum(oh[:, sl], axis=1, keepdims=True)
        run_ref[...] = run
        pos1 = jnp.concatenate(p1, axis=1).astype(jnp.int32)
        pos2 = jnp.concatenate(p2, axis=1).astype(jnp.int32)
        zero_i = jnp.zeros_like(row8)
        mi_ref[...] = jnp.where(row8 == 0, pos1, jnp.where(row8 == 1, pos2, jnp.where(
            row8 == 2, e1, jnp.where(row8 == 3, e2, zero_i))))
        mf_ref[...] = jnp.where(row8 == 0, wk1, jnp.where(row8 == 1, wk2, 0.0))
        cnt_ref[...] = tot_ref[...].astype(jnp.int32)


def _router_call(lg_t):
    n_tokens = lg_t.shape[1]
    tc = ROUTER_COLS
    r = lax.broadcasted_iota(jnp.int32, (LANES, LANES), 0)
    c = lax.broadcasted_iota(jnp.int32, (LANES, LANES), 1)
    upper = jnp.where(r < c, 1.0, 0.0).astype(BF16)
    lower = jnp.where(c < r, 1.0, 0.0)[:N_EXPERTS, :N_EXPERTS].astype(BF16)
    col = pl.BlockSpec((SUBLANES, tc), lambda p, i: (0, i * p))
    return pl.pallas_call(
        _router_kernel,
        grid=(2, n_tokens // tc),
        in_specs=[pl.BlockSpec((LOGIT_ROWS, tc), lambda p, i: (0, i)),
                  _const_spec((LANES, LANES)), _const_spec((N_EXPERTS, N_EXPERTS))],
        out_specs=[col, col, _const_spec((N_EXPERTS, LANES))],
        out_shape=[jax.ShapeDtypeStruct((SUBLANES, n_tokens), jnp.int32),
                   jax.ShapeDtypeStruct((SUBLANES, n_tokens), F32),
                   jax.ShapeDtypeStruct((N_EXPERTS, LANES), jnp.int32)],
        scratch_shapes=[pltpu.VMEM((N_EXPERTS, LANES), F32), pltpu.VMEM((N_EXPERTS, LANES), F32)],
        compiler_params=pltpu.CompilerParams(
            dimension_semantics=("arbitrary", "arbitrary"), vmem_limit_bytes=VMEM_LIMIT),
        name="router",
    )(lg_t, upper, lower)


def _ffn_plan(counts, n_pairs):
    ids = jnp.arange(N_EXPERTS, dtype=jnp.int32)
    tiles_e = (counts + FFN_ROWS - 1) // FFN_ROWS
    tile_end = jnp.cumsum(tiles_e)
    tile_start = tile_end - tiles_e
    n_work = tile_end[-1]
    n_items = n_pairs // FFN_ROWS + N_EXPERTS
    w = jnp.arange(n_items, dtype=jnp.int32)
    t_w = jnp.minimum(w, n_work - 1)
    e_w = jnp.sum((tile_end[None, :] <= t_w[:, None]).astype(jnp.int32), axis=1)
    e_w = jnp.minimum(e_w, N_EXPERTS - 1)
    sel = e_w[:, None] == ids[None, :]
    pick = lambda a: jnp.sum(jnp.where(sel, a[None, :], 0), axis=1)
    valid_w = (w < n_work).astype(jnp.int32)
    pad_start = tile_start * FFN_ROWS + counts
    pad_len = tiles_e * FFN_ROWS - counts
    used = counts > 0
    order_e = jnp.cumsum(used.astype(jnp.int32)) - 1
    later_used = jnp.logical_and(used[None, :], ids[None, :] > ids[:, None])
    next_e = jnp.min(jnp.where(later_used, ids[None, :], N_EXPERTS), axis=1)
    next_e = jnp.where(next_e == N_EXPERTS, -1, next_e)
    i32 = lambda a: a.astype(jnp.int32)
    plan = (i32(t_w), i32(e_w), valid_w, i32(pick(order_e)), i32(pick(next_e)))
    return plan, (i32(pad_start), i32(pad_len))


def _scatter_kernel(tiles_p, n_steps, pos1_s, pos2_s, pad0_s, padn_s, h2p_hbm, h2s_hbm, xs_hbm,
                    buf, zero_ref, lsems, ssems, zsem):
    i = pl.program_id(0)
    tile_rows = DISPATCH_ROWS * TILE_ROWS

    def tile_load(step, do):
        slot = step % D_SLOTS
        for src, first, mine in ((h2p_hbm, step, step < tiles_p),
                                 (h2s_hbm, step - tiles_p, step >= tiles_p)):
            @pl.when(mine)
            def _():
                start = pl.multiple_of(first * tile_rows, tile_rows)
                do(pltpu.make_async_copy(src.at[pl.ds(start, tile_rows), :], buf.at[slot],
                                         lsems.at[slot]))

    def scatter(step):
        slot = step % D_SLOTS
        base = step * DISPATCH_ROWS

        def row_copy(r, pos_s):
            p = pos_s[base + r]
            return pltpu.make_async_copy(buf.at[slot, _token_tile(r), :], xs_hbm.at[_token_tile(p), :],
                                         ssems.at[slot])

        def start(r, carry):
            row_copy(r, pos1_s).start(priority=0)
            row_copy(r, pos2_s).start(priority=1)
            return carry

        def wait(r, carry):
            row_copy(r, pos1_s).wait()
            row_copy(r, pos2_s).wait()
            return carry

        return start, wait

    @pl.when(i == 0)
    def _():
        for ahead in range(D_SLOTS - 1):
            tile_load(ahead, lambda cp: cp.start())

    tile_load(i, lambda cp: cp.wait())
    lax.fori_loop(0, DISPATCH_ROWS, scatter(i)[0], 0, unroll=32)

    @pl.when(i >= 1)
    def _():
        lax.fori_loop(0, DISPATCH_ROWS, scatter(i - 1)[1], 0, unroll=8)

    @pl.when(i + D_SLOTS - 1 < n_steps)
    def _():
        tile_load(i + D_SLOTS - 1, lambda cp: cp.start())

    @pl.when(i == n_steps - 1)
    def _():
        lax.fori_loop(0, DISPATCH_ROWS, scatter(i)[1], 0, unroll=8)

        zero_ref[...] = jnp.zeros_like(zero_ref)

        def pad_copy(expert, j):
            row = pad0_s[expert] + j
            return pltpu.make_async_copy(zero_ref, xs_hbm.at[_token_tile(row), :], zsem)

        def per_expert(action):
            def body(expert, carry):
                lax.fori_loop(0, padn_s[expert],
                              lambda j, c: (action(pad_copy(expert, j)), c)[1], 0)
                return carry
            lax.fori_loop(0, N_EXPERTS, body, 0)

        per_expert(lambda cp: cp.start())
        per_expert(lambda cp: cp.wait())


def _scatter_call(pos1, pos2, pads, h2_p, h2_s, n_sorted_rows):
    tm = DISPATCH_ROWS
    tiles_p = h2_p.shape[0] // (tm * TILE_ROWS)
    tiles_s = h2_s.shape[0] // (tm * TILE_ROWS)
    n_steps = tiles_p + tiles_s
    assert n_steps >= D_SLOTS
    any_spec = pl.BlockSpec(memory_space=pl.ANY)
    return pl.pallas_call(
        functools.partial(_scatter_kernel, tiles_p, n_steps),
        grid_spec=pltpu.PrefetchScalarGridSpec(
            num_scalar_prefetch=4,
            grid=(n_steps,),
            in_specs=[any_spec, any_spec],
            out_specs=pl.BlockSpec(memory_space=pl.ANY),
            scratch_shapes=[pltpu.VMEM((D_SLOTS, tm * TILE_ROWS, LANES), F32),
                            pltpu.VMEM((TILE_ROWS, LANES), F32),
                            pltpu.SemaphoreType.DMA((D_SLOTS,)),
                            pltpu.SemaphoreType.DMA((D_SLOTS,)),
                            pltpu.SemaphoreType.DMA(())],
        ),
        out_shape=jax.ShapeDtypeStruct((n_sorted_rows * TILE_ROWS, LANES), F32),
        compiler_params=pltpu.CompilerParams(
            dimension_semantics=("arbitrary",), vmem_limit_bytes=VMEM_LIMIT),
        name="dispatch_scatter",
    )(pos1, pos2, *pads, h2_p, h2_s)


def _ffn_kernel(n_items, wt_s, we_s, wvalid_s, word_s, wnext_s, x_hbm, w1_hbm, w3_hbm, w2_hbm,
                o_ref, xbuf, w1f_ref, w3f_ref, w2f_ref, w13b_ref, w2b_ref, xsems, sems):
    w = pl.program_id(0)
    e = we_s[w]
    e_prev = we_s[jnp.maximum(w - 1, 0)]
    slot = word_s[w] % 2
    tile_rows = FFN_ROWS * TILE_ROWS

    def x_copy(item):
        first = pl.multiple_of(wt_s[item] * tile_rows, tile_rows)
        return pltpu.make_async_copy(x_hbm.at[pl.ds(first, tile_rows), :], xbuf.at[item % X_SLOTS],
                                     xsems.at[item % X_SLOTS])

    @pl.when(w == 0)
    def _():
        for ahead in range(X_SLOTS - 1):
            x_copy(ahead).start()

    @pl.when(w + X_SLOTS - 1 < n_items)
    def _():
        x_copy(w + X_SLOTS - 1).start()

    def weight_copies(expert, buf):
        return [pltpu.make_async_copy(src.at[expert], dst.at[buf], sems.at[buf])
                for src, dst in ((w1_hbm, w1f_ref), (w3_hbm, w3f_ref), (w2_hbm, w2f_ref))]

    @pl.when(w == 0)
    def _():
        for cp in weight_copies(e, slot):
            cp.start()

    @pl.when(jnp.logical_or(w == 0, e != e_prev))
    def _():
        for cp in weight_copies(e, slot):
            cp.wait()
        w13b_ref[:, :D_EXPERT] = w1f_ref[slot].astype(BF16)
        w13b_ref[:, D_EXPERT:] = w3f_ref[slot].astype(BF16)
        w2b_ref[...] = w2f_ref[slot].astype(BF16)
        nxt = wnext_s[w]

        @pl.when(nxt >= 0)
        def _():
            for cp in weight_copies(nxt, 1 - slot):
                cp.start()

    x_copy(w).wait()

    @pl.when(wvalid_s[w] == 1)
    def _():
        x = _load_token_tiles(xbuf, (w % X_SLOTS,), FFN_ROWS)
        h13 = _dot(x.astype(BF16), w13b_ref[...])
        a = h13[:, :D_EXPERT]
        b = h13[:, D_EXPERT:]
        hdn = (a * jax.nn.sigmoid(a)) * b
        _store_token_tiles(o_ref, _dot(hdn.astype(BF16), w2b_ref[...]))


def _ffn_call(plan, xs, w1, w3, w2):
    t_w = plan[0]
    n_pairs = xs.shape[0] // TILE_ROWS
    tm = FFN_ROWS
    any_spec = pl.BlockSpec(memory_space=pl.ANY)
    row = pl.BlockSpec((tm * TILE_ROWS, LANES), lambda i, wt, *_: (wt[i], 0))
    n_items = t_w.shape[0]
    assert n_items >= X_SLOTS
    return pl.pallas_call(
        functools.partial(_ffn_kernel, n_items),
        grid_spec=pltpu.PrefetchScalarGridSpec(
            num_scalar_prefetch=len(plan),
            grid=(n_items,),
            in_specs=[any_spec, any_spec, any_spec, any_spec],
            out_specs=row,
            scratch_shapes=[pltpu.VMEM((X_SLOTS, tm * TILE_ROWS, LANES), F32),
                            pltpu.VMEM((2, D_MODEL, D_EXPERT), F32),
                            pltpu.VMEM((2, D_MODEL, D_EXPERT), F32),
                            pltpu.VMEM((2, D_EXPERT, D_MODEL), F32),
                            pltpu.VMEM((D_MODEL, 2 * D_EXPERT), BF16),
                            pltpu.VMEM((D_EXPERT, D_MODEL), BF16),
                            pltpu.SemaphoreType.DMA((X_SLOTS,)),
                            pltpu.SemaphoreType.DMA((2,))],
        ),
        out_shape=jax.ShapeDtypeStruct((n_pairs * TILE_ROWS, LANES), F32),
        compiler_params=pltpu.CompilerParams(
            dimension_semantics=("arbitrary",), vmem_limit_bytes=VMEM_LIMIT),
        name="expert_ffn",
    )(*plan, xs, w1, w3, w2)


def _combine_kernel(three_d, row0, n_steps, pos1_s, pos2_s, x1_ref, wk_ref, mod_ref, g_ref, b_ref,
                    ys_hbm, o_ref, ybuf, sems):
    step = pl.program_id(0)
    slot = step % 2

    def gather(tile, tile_slot):
        base = row0 + tile * COMBINE_ROWS

        def row_copy(r, k, pos_s):
            p = pos_s[base + r]
            return pltpu.make_async_copy(ys_hbm.at[_token_tile(p), :],
                                         ybuf.at[tile_slot, k, _token_tile(r), :], sems.at[tile_slot])

        def start(r, carry):
            row_copy(r, 0, pos1_s).start(priority=0)
            row_copy(r, 1, pos2_s).start(priority=1)
            return carry

        def wait(r, carry):
            row_copy(r, 0, pos1_s).wait()
            row_copy(r, 1, pos2_s).wait()
            return carry

        return start, wait

    @pl.when(step == 0)
    def _():
        lax.fori_loop(0, COMBINE_ROWS, gather(step, slot)[0], 0, unroll=32)

    @pl.when(step + 1 < n_steps)
    def _():
        lax.fori_loop(0, COMBINE_ROWS, gather(step + 1, 1 - slot)[0], 0, unroll=32)

    lax.fori_loop(0, COMBINE_ROWS, gather(step, slot)[1], 0, unroll=8)

    wk = wk_ref[...]
    r = lax.broadcasted_iota(jnp.int32, (COMBINE_ROWS, COMBINE_ROWS), 0)
    c = lax.broadcasted_iota(jnp.int32, (COMBINE_ROWS, COMBINE_ROWS), 1)
    wk_col = [jnp.sum(jnp.where(r == c, wk[k:k + 1, :], 0.0), axis=1, keepdims=True)
              for k in range(TOP_K)]
    moe = (wk_col[0] * _load_token_tiles(ybuf, (slot, 0), COMBINE_ROWS)
           + wk_col[1] * _load_token_tiles(ybuf, (slot, 1), COMBINE_ROWS))
    x1 = x1_ref[...]
    if three_d:
        g2 = mod_ref[0, :, 5 * D_MODEL:]
        o_ref[0] = _layer_norm(ALPHA * x1 + (1.0 + g2) * moe, g_ref[...], b_ref[...])
    else:
        def seq_rows(v, b):
            return v[b * SUBLANES:(b + 1) * SUBLANES]

        resid = jnp.concatenate([
            ALPHA * seq_rows(x1, b) + (1.0 + mod_ref[b:b + 1, 5 * D_MODEL:]) * seq_rows(moe, b)
            for b in range(COMBINE_ROWS // SUBLANES)], axis=0)
        o_ref[...] = _layer_norm(resid, g_ref[...], b_ref[...])


def _combine_call(three_d, pos1, pos2, x1_all, mf_all, mod, ln2_g, ln2_b, ys, row0, batch, seq):
    tm = COMBINE_ROWS
    blk0 = row0 // tm
    if three_d:
        tiles = seq // tm
        grid = (batch * tiles,)
        mod_spec = pl.BlockSpec((1, 1, 6 * D_MODEL), lambda i, *_: (i // tiles, 0, 0))
        out_spec = pl.BlockSpec((1, tm, D_MODEL), lambda i, *_: (i // tiles, i % tiles, 0))
        out_shape = jax.ShapeDtypeStruct((batch, seq, D_MODEL), F32)
    else:
        grid = (batch * seq // tm,)
        mod_spec = pl.BlockSpec((tm // SUBLANES, 6 * D_MODEL), lambda i, *_: (i, 0))
        out_spec = pl.BlockSpec((tm, D_MODEL), lambda i, *_: (i, 0))
        out_shape = jax.ShapeDtypeStruct((batch * seq, D_MODEL), F32)
    return pl.pallas_call(
        functools.partial(_combine_kernel, three_d, row0, grid[0]),
        grid_spec=pltpu.PrefetchScalarGridSpec(
            num_scalar_prefetch=2,
            grid=grid,
            in_specs=[pl.BlockSpec((tm, D_MODEL), lambda i, *_: (i, 0)),
                      pl.BlockSpec((SUBLANES, tm), lambda i, *_: (0, blk0 + i)),
                      mod_spec,
                      pl.BlockSpec(ln2_g.shape, lambda i, *_: (0, 0)),
                      pl.BlockSpec(ln2_b.shape, lambda i, *_: (0, 0)),
                      pl.BlockSpec(memory_space=pl.ANY)],
            out_specs=out_spec,
            scratch_shapes=[pltpu.VMEM((2, TOP_K, tm * TILE_ROWS, LANES), F32),
                            pltpu.SemaphoreType.DMA((2,))],
        ),
        out_shape=out_shape,
        compiler_params=pltpu.CompilerParams(
            dimension_semantics=("arbitrary",), vmem_limit_bytes=VMEM_LIMIT),
        name="combine_prompt" if three_d else "combine_sample",
    )(pos1, pos2, x1_all, mf_all, mod, ln2_g, ln2_b, ys)


def _block_diag(w):
    heads, n, _ = w.shape
    eye = jnp.eye(heads, dtype=w.dtype)
    return (eye[:, None, :, None] * w[:, :, None, :]).reshape(heads * n, heads * n)


def kernel(x_prompt, x_sample, state_rglru_h, state_conv, c_prompt, c_sample, w_ada, b_ada, w_in, w_s, b_s, lnv_g, lnv_b, conv_w, conv_b, lru_wa, lru_ba, lru_wx, lru_bx, lru_lam, w_out, ln1_g, ln1_b, w_rg, b_rg, w_re, b_re, w1, w3, w2, ln2_g, ln2_b):
    batch, seq, _ = x_prompt.shape
    dec_batch, dec_seq, _ = x_sample.shape
    assert dec_seq == SUBLANES and seq % ROWS == 0 and (dec_batch * dec_seq) % ROWS == 0
    n_prompt = batch * seq
    n_sample = dec_batch * dec_seq
    n_tokens = n_prompt + n_sample
    n_pairs = TOP_K * n_tokens
    assert n_tokens % ROUTER_COLS == 0 and n_pairs % FFN_ROWS == 0
    l = 0

    mod_p, mod_s = _ada_call(c_prompt, c_sample, w_ada[l], b_ada[l][None])
    mod_p3 = mod_p[:, None, :]

    tri = jnp.tril(jnp.ones((CHUNK, CHUNK), dtype=bool))
    ws_tri = jnp.where(tri[None], w_s[l], 0.0)
    reps_p = MXU_DIM // CHUNK
    eye_p = jnp.eye(reps_p, dtype=F32)
    wsm_p = (eye_p[None, :, None, :, None] * ws_tri[:, None, :, None, :]).reshape(
        A_HEADS, MXU_DIM, MXU_DIM).astype(BF16)
    reps_s = MXU_DIM // dec_seq
    pos_in_seq = jnp.arange(MXU_DIM, dtype=jnp.int32) % dec_seq
    seq_of_row = jnp.arange(MXU_DIM, dtype=jnp.int32) // dec_seq
    spread = (pos_in_seq[:, None] == jnp.arange(dec_seq, dtype=jnp.int32)[None, :]).astype(F32)
    same_seq = seq_of_row[:, None] == seq_of_row[None, :]
    wsm_s = jnp.einsum("ri,hij,cj->hrc", spread, ws_tri[:, :dec_seq, :dec_seq], spread,
                       precision=lax.Precision.HIGHEST)
    wsm_s = jnp.where(same_seq[None], wsm_s, 0.0).astype(BF16)
    bs_pos = jnp.repeat(jnp.transpose(b_s[l]), A_HEAD_DIM, axis=1)
    bsf_p = jnp.tile(bs_pos, (reps_p, 1))
    bsf_s = jnp.tile(bs_pos[:dec_seq], (reps_s, 1))

    wa_bd = _block_diag(lru_wa[l])
    wx_bd = _block_diag(lru_wx[l])
    wg = jnp.stack([
        jnp.concatenate([wa_bd[q * MXU_DIM:(q + 1) * MXU_DIM, q * MXU_DIM:(q + 1) * MXU_DIM],
                         wx_bd[q * MXU_DIM:(q + 1) * MXU_DIM, q * MXU_DIM:(q + 1) * MXU_DIM]], axis=1)
        for q in range(D_B // MXU_DIM)]).astype(BF16)
    bg = jnp.concatenate([lru_ba[l].reshape(1, D_B), lru_bx[l].reshape(1, D_B)], axis=1)

    group_pad = jnp.zeros((SUBLANES - N_GROUPS, D_MODEL), F32)
    wr = jnp.concatenate([jnp.transpose(w_rg[l]), group_pad, jnp.transpose(w_re[l])], axis=0)
    wr_hi = wr.astype(BF16)
    wr_lo = jnp.concatenate([wr_hi, (wr - wr_hi.astype(F32)).astype(BF16)], axis=0)
    br = jnp.concatenate([b_rg[l], jnp.zeros((SUBLANES - N_GROUPS,), F32), b_re[l]])[:, None]

    common = dict(lnv_g=lnv_g[l][None], lnv_b=lnv_b[l][None], conv_w=conv_w[l], conv_b=conv_b[l][None],
                  wg=wg, bg=bg, lam=lru_lam[l][None], w_out=w_out[l].astype(BF16),
                  ln1_g=ln1_g[l][None], ln1_b=ln1_b[l][None], wr_hi=wr_hi, wr_lo=wr_lo, br=br)
    w_in_b = w_in[l].astype(BF16)
    weights_p = _mixer_weights(w_in_b, wsm_p, bsf_p, **common)
    weights_s = _mixer_weights(w_in_b, wsm_s, bsf_s, **common)

    x1_p, h2_p, lg_p, cst_p, hst_p = _mixer_prompt(x_prompt, mod_p3, weights_p)

    cpad = jnp.concatenate(
        [jnp.zeros((dec_batch, SUBLANES - (CONV_W - 1), D_B), F32), state_conv[l]], axis=1
    ).reshape(n_sample, D_B)
    x1_s, h2_s, lg_s, xr_s, v_s, hs_s = _mixer_sample(
        x_sample.reshape(n_sample, D_MODEL), mod_s, weights_s, state_rglru_h[l], cpad)

    meta_i, meta_f, counts = _router_call(jnp.concatenate([lg_p, lg_s], axis=1))
    pos1, pos2 = meta_i[0], meta_i[1]
    plan, pads = _ffn_plan(counts[:, 0], n_pairs)
    xs = _scatter_call(pos1, pos2, pads, h2_p, h2_s, plan[0].shape[0] * FFN_ROWS)
    ys = _ffn_call(plan, xs, w1[l], w3[l], w2[l])

    ln2g, ln2b = ln2_g[l][None], ln2_b[l][None]
    y_prompt = _combine_call(True, pos1, pos2, x1_p, meta_f, mod_p3, ln2g, ln2b, ys,
                             0, batch, seq)
    y_sample = _combine_call(False, pos1, pos2, x1_s, meta_f, mod_s, ln2g, ln2b, ys,
                             n_prompt, dec_batch, dec_seq).reshape(dec_batch, dec_seq, D_MODEL)

    new_h_prompt = hst_p[:, SUBLANES - 1, :][None]
    new_conv_prompt = cst_p[:, SUBLANES - (CONV_W - 1):, :][None]
    xr_s3 = xr_s.reshape(dec_batch, dec_seq, D_B)
    new_h_sample = hs_s.reshape(dec_batch, dec_seq, D_B)[:, dec_seq - 1, :][None]
    new_conv_sample = xr_s3[:, dec_seq - (CONV_W - 1):, :][None]
    new_chunk_v_sample = v_s.reshape(dec_batch, dec_seq, D_A)[None]
    return (y_prompt, y_sample, new_h_prompt, new_conv_prompt, new_h_sample, new_conv_sample,
            new_chunk_v_sample)
```

```python
import functools
import math

import jax
import jax.numpy as jnp
from jax import lax
from jax.experimental import pallas as pl
from jax.experimental.pallas import tpu as pltpu

D_MODEL = 1024
D_A = 512
A_HEADS = 4
A_HEAD_DIM = 128
CHUNK = 128
D_B = 512
B_HEADS = 8
B_HEAD_DIM = 64
CONV_W = 4
LRU_C = 8.0
D_IN = 2 * D_A + 2 * D_B
N_GROUPS = 4
EXPERTS_PER_GROUP = 8
N_EXPERTS = 32
TOP_K = 2
D_EXPERT = 512
LN_EPS = 1e-5
DEPTH = 1
ALPHA = (2.0 * DEPTH) ** 0.25

LANES = 128
SUBLANES = 8
MXU_DIM = 256
VMEM_LIMIT = 56 * 1024 * 1024

ROWS = 512
ROUTER_COLS = 1024
LOGIT_ROWS = SUBLANES + N_EXPERTS
DISPATCH_ROWS = 1024
D_SLOTS = 3
FFN_ROWS = 256
X_SLOTS = 3
COMBINE_ROWS = 256
BF16 = jnp.bfloat16
F32 = jnp.float32


def _dot(a, b):
    return jnp.dot(a, b, preferred_element_type=F32)


def _dot_nt(a, b):
    return lax.dot_general(a, b, (((1,), (1,)), ((), ())), preferred_element_type=F32)


TILE_ROWS = D_MODEL // LANES


def _token_tile(r):
    return pl.ds(pl.multiple_of(r * TILE_ROWS, TILE_ROWS), TILE_ROWS)


def _load_token_tiles(ref, lead, n_rows):
    parts = [ref[lead + (pl.ds(s, n_rows, stride=TILE_ROWS), slice(None))] for s in range(TILE_ROWS)]
    return jnp.concatenate(parts, axis=1)


def _store_token_tiles(ref, val, keep=None):
    n_rows = val.shape[0]
    for s in range(TILE_ROWS):
        rows = pl.ds(s, n_rows, stride=TILE_ROWS)
        new = val[:, s * LANES:(s + 1) * LANES]
        if keep is not None:
            new = jnp.where(keep, new, ref[rows, :])
        ref[rows, :] = new


def _layer_norm(x, g, b):
    mu = jnp.mean(x, axis=-1, keepdims=True)
    xc = x - mu
    var = jnp.mean(xc * xc, axis=-1, keepdims=True)
    return xc * lax.rsqrt(var + LN_EPS) * g + b


def _gelu_tanh(x):
    c = math.sqrt(2.0 / math.pi)
    return x * (0.5 * (1.0 + jnp.tanh(c * (x + 0.044715 * (x * x * x)))))


def _const_spec(shape):
    nd = len(shape)
    return pl.BlockSpec(shape, lambda *_: (0,) * nd)


def _ada_kernel(cp_ref, cs_ref, w_ref, b_ref, op_ref, os_ref):
    w = w_ref[...].astype(BF16)
    for c_ref, o_ref in ((cp_ref, op_ref), (cs_ref, os_ref)):
        c = c_ref[...]
        a = (c * jax.nn.sigmoid(c)).astype(BF16)
        o_ref[...] = _dot(a, w) + b_ref[...]


def _ada_call(c_prompt, c_sample, w_ada, b_ada):
    tn = 1536
    rows_p, rows_s = c_prompt.shape[0], c_sample.shape[0]
    return pl.pallas_call(
        _ada_kernel,
        grid=(6 * D_MODEL // tn,),
        in_specs=[
            pl.BlockSpec((rows_p, D_MODEL), lambda j: (0, 0)),
            pl.BlockSpec((rows_s, D_MODEL), lambda j: (0, 0)),
            pl.BlockSpec((D_MODEL, tn), lambda j: (0, j)),
            pl.BlockSpec((1, tn), lambda j: (0, j)),
        ],
        out_specs=[pl.BlockSpec((rows_p, tn), lambda j: (0, j)),
                   pl.BlockSpec((rows_s, tn), lambda j: (0, j))],
        out_shape=[jax.ShapeDtypeStruct((rows_p, 6 * D_MODEL), F32),
                   jax.ShapeDtypeStruct((rows_s, 6 * D_MODEL), F32)],
        compiler_params=pltpu.CompilerParams(
            dimension_semantics=("arbitrary",), vmem_limit_bytes=VMEM_LIMIT),
        name="ada_mod",
    )(c_prompt, c_sample, w_ada, b_ada)


def _mixer_kernel(sample, *refs):
    if sample:
        (x_ref, mod_ref, win_ref, wsm_ref, bsf_ref, lnvg_ref, lnvb_ref, convw_ref, convb_ref,
         wg_ref, bg_ref, lam_ref, wout_ref, ln1g_ref, ln1b_ref, wrh_ref, wrl_ref, br_ref,
         h0_ref, cpad_ref,
         x1_ref, h2_ref, lg_ref, xr_ref, v_ref, hs_ref) = refs
    else:
        (x_ref, mod_ref, win_ref, wsm_ref, bsf_ref, lnvg_ref, lnvb_ref, convw_ref, convb_ref,
         wg_ref, bg_ref, lam_ref, wout_ref, ln1g_ref, ln1b_ref, wrh_ref, wrl_ref, br_ref,
         x1_ref, h2_ref, lg_ref, cst_ref, hst_ref, prev_ref, hcar_ref) = refs

    rows = ROWS
    groups = rows // SUBLANES

    if sample:
        x = x_ref[...]

        def chunk(c):
            return lambda b: mod_ref[b:b + 1, c * D_MODEL:(c + 1) * D_MODEL]
    else:
        x = x_ref[0]

        def chunk(c):
            return lambda b: mod_ref[0, :, c * D_MODEL:(c + 1) * D_MODEL]

        @pl.when(pl.program_id(1) == 0)
        def _():
            prev_ref[...] = jnp.zeros_like(prev_ref)
            hcar_ref[...] = jnp.zeros_like(hcar_ref)

    def modulated(fn, *vals):
        if not sample:
            return fn(0, *vals)
        parts = [fn(b, *(v[b * SUBLANES:(b + 1) * SUBLANES] for v in vals)) for b in range(groups)]
        return jnp.concatenate(parts, axis=0)

    sh1, sc1, g1, sh2, sc2, g2 = [chunk(c) for c in range(6)]

    h = modulated(lambda b, xv: xv * (1.0 + sc1(b)) + sh1(b), x)
    proj = _dot(h.astype(BF16), win_ref[...])
    u = proj[:, :D_A]
    v = _layer_norm(proj[:, D_A:2 * D_A], lnvg_ref[...], lnvb_ref[...])
    xr = proj[:, 2 * D_A:2 * D_A + D_B]
    yr = proj[:, 2 * D_A + D_B:]

    vb = v.astype(BF16)
    halves = []
    for r in range(rows // MXU_DIM):
        heads = [
            _dot(wsm_ref[hd], vb[r * MXU_DIM:(r + 1) * MXU_DIM, hd * A_HEAD_DIM:(hd + 1) * A_HEAD_DIM])
            for hd in range(A_HEADS)
        ]
        halves.append(jnp.concatenate(heads, axis=1) + bsf_ref[...])
    mixed = halves[0] if len(halves) == 1 else jnp.concatenate(halves, axis=0)
    out_a = u * mixed

    xr3 = xr.reshape(groups, SUBLANES, D_B)
    t3 = lax.broadcasted_iota(jnp.int32, (1, SUBLANES, D_B), 1)
    cw = convw_ref[...]
    if sample:
        cpad3 = cpad_ref[...].reshape(groups, SUBLANES, D_B)
    else:
        prev8 = prev_ref[...]
    xc3 = convb_ref[...][None]
    for k in range(CONV_W):
        s = CONV_W - 1 - k
        if s == 0:
            shifted = xr3
        else:
            rot = pltpu.roll(xr3, s, 1)
            if sample:
                rot_prev = pltpu.roll(cpad3, s, 1)
            else:
                first = pltpu.roll(prev8, s, 0)[None]
                rot_prev = jnp.concatenate([first, rot[:-1]], axis=0) if groups > 1 else first
            shifted = jnp.where(t3 >= s, rot, rot_prev)
        xc3 = xc3 + shifted * cw[k:k + 1, :][None]
    xc = xc3.reshape(rows, D_B)

    xcb = xc.astype(BF16)
    gq = [_dot(xcb[:, q * MXU_DIM:(q + 1) * MXU_DIM], wg_ref[q]) for q in range(D_B // MXU_DIM)]
    bg = bg_ref[...]
    pre_a = jnp.concatenate([g[:, :MXU_DIM] for g in gq], axis=1) + bg[:, :D_B]
    pre_x = jnp.concatenate([g[:, MXU_DIM:] for g in gq], axis=1) + bg[:, D_B:]
    gate_a = jax.nn.sigmoid(pre_a)
    gate_x = jax.nn.sigmoid(pre_x)
    nl = -lam_ref[...]
    softplus = jnp.maximum(nl, 0.0) + jnp.log1p(jnp.exp(-jnp.abs(nl)))
    log_a = -LRU_C * gate_a * softplus
    a = jnp.exp(log_a)
    mult = jnp.sqrt(-jnp.tanh(log_a) * (1.0 + a * a))
    bx = mult * (gate_x * xc)

    a3 = a.reshape(groups, SUBLANES, D_B)
    b3 = bx.reshape(groups, SUBLANES, D_B)
    for d in (1, 2, 4):
        m = t3 >= d
        a_sh = jnp.where(m, pltpu.roll(a3, d, 1), 1.0)
        b_sh = jnp.where(m, pltpu.roll(b3, d, 1), 0.0)
        b3 = a3 * b_sh + b3
        a3 = a3 * a_sh
    if sample:
        hs3 = jnp.stack([a3[g] * h0_ref[g:g + 1, :] + b3[g] for g in range(groups)], axis=0)
    else:
        h_prev = hcar_ref[...][SUBLANES - 1:SUBLANES, :]
        hs_list = []
        for g in range(groups):
            hg = a3[g] * h_prev + b3[g]
            hs_list.append(hg)
            h_prev = hg[SUBLANES - 1:SUBLANES, :]
        hs3 = jnp.stack(hs_list, axis=0)
    hs = hs3.reshape(rows, D_B)

    out_b = hs * _gelu_tanh(yr)
    mix = _dot(out_a.astype(BF16), wout_ref[:D_A, :]) + _dot(out_b.astype(BF16), wout_ref[D_A:, :])
    y1 = modulated(lambda b, xv, mv: ALPHA * xv + (1.0 + g1(b)) * mv, x, mix)
    x1 = _layer_norm(y1, ln1g_ref[...], ln1b_ref[...])
    h2 = modulated(lambda b, xv: xv * (1.0 + sc2(b)) + sh2(b), x1)

    h2_hi = h2.astype(BF16)
    h2_lo = (h2 - h2_hi.astype(F32)).astype(BF16)
    both = _dot_nt(wrl_ref[...], h2_hi)
    logits = (both[:LOGIT_ROWS] + _dot_nt(wrh_ref[...], h2_lo) + both[LOGIT_ROWS:] + br_ref[...])

    x1_ref[...] = x1
    _store_token_tiles(h2_ref, h2)
    lg_ref[...] = logits
    if sample:
        xr_ref[...] = xr
        v_ref[...] = v
        hs_ref[...] = hs
    else:
        last_x = xr3[groups - 1]
        last_h = hs3[groups - 1]
        prev_ref[...] = last_x
        hcar_ref[...] = last_h
        cst_ref[0] = last_x
        hst_ref[0] = last_h


def _mixer_weights(w_in, wsm, bsf, lnv_g, lnv_b, conv_w, conv_b, wg, bg, lam, w_out, ln1_g, ln1_b,
                   wr_hi, wr_lo, br):
    arrs = [w_in, wsm, bsf, lnv_g, lnv_b, conv_w, conv_b, wg, bg, lam, w_out, ln1_g, ln1_b,
            wr_hi, wr_lo, br]
    return arrs, [_const_spec(a.shape) for a in arrs]


def _mixer_prompt(x, mod3, weights):
    batch, seq, _ = x.shape
    n_tokens = batch * seq
    tiles = seq // ROWS
    arrs, specs = weights
    row_spec = lambda w: pl.BlockSpec((ROWS, w), lambda b, j: (b * tiles + j, 0))
    st_spec = pl.BlockSpec((1, SUBLANES, D_B), lambda b, j: (b, 0, 0))
    return pl.pallas_call(
        functools.partial(_mixer_kernel, False),
        grid=(batch, tiles),
        in_specs=[pl.BlockSpec((1, ROWS, D_MODEL), lambda b, j: (b, j, 0)),
                  pl.BlockSpec((1, 1, 6 * D_MODEL), lambda b, j: (b, 0, 0))] + specs,
        out_specs=[row_spec(D_MODEL),
                   pl.BlockSpec((ROWS * TILE_ROWS, LANES), lambda b, j: (b * tiles + j, 0)),
                   pl.BlockSpec((LOGIT_ROWS, ROWS), lambda b, j: (0, b * tiles + j)), st_spec, st_spec],
        out_shape=[jax.ShapeDtypeStruct((n_tokens, D_MODEL), F32),
                   jax.ShapeDtypeStruct((n_tokens * TILE_ROWS, LANES), F32),
                   jax.ShapeDtypeStruct((LOGIT_ROWS, n_tokens), F32),
                   jax.ShapeDtypeStruct((batch, SUBLANES, D_B), F32),
                   jax.ShapeDtypeStruct((batch, SUBLANES, D_B), F32)],
        scratch_shapes=[pltpu.VMEM((SUBLANES, D_B), F32), pltpu.VMEM((SUBLANES, D_B), F32)],
        compiler_params=pltpu.CompilerParams(
            dimension_semantics=("arbitrary", "arbitrary"), vmem_limit_bytes=VMEM_LIMIT),
        name="mixer_prompt",
    )(x, mod3, *arrs)


def _mixer_sample(x2, mod_seq, weights, h0, cpad):
    n_rows = x2.shape[0]
    arrs, specs = weights
    row = lambda w: pl.BlockSpec((ROWS, w), lambda i: (i, 0))
    seq_row = lambda w: pl.BlockSpec((ROWS // SUBLANES, w), lambda i: (i, 0))
    return pl.pallas_call(
        functools.partial(_mixer_kernel, True),
        grid=(n_rows // ROWS,),
        in_specs=[row(D_MODEL), seq_row(6 * D_MODEL)] + specs + [seq_row(D_B), row(D_B)],
        out_specs=[row(D_MODEL), pl.BlockSpec((ROWS * TILE_ROWS, LANES), lambda i: (i, 0)),
                   pl.BlockSpec((LOGIT_ROWS, ROWS), lambda i: (0, i)),
                   row(D_B), row(D_B), row(D_B)],
        out_shape=[jax.ShapeDtypeStruct((n_rows, D_MODEL), F32),
                   jax.ShapeDtypeStruct((n_rows * TILE_ROWS, LANES), F32),
                   jax.ShapeDtypeStruct((LOGIT_ROWS, n_rows), F32),
                   jax.ShapeDtypeStruct((n_rows, D_B), F32),
                   jax.ShapeDtypeStruct((n_rows, D_B), F32),
                   jax.ShapeDtypeStruct((n_rows, D_B), F32)],
        compiler_params=pltpu.CompilerParams(
            dimension_semantics=("arbitrary",), vmem_limit_bytes=VMEM_LIMIT),
        name="mixer_sample",
    )(x2, mod_seq, *arrs, h0, cpad)


def _router_kernel(lg_ref, upper_ref, lower_ref, mi_ref, mf_ref, cnt_ref, tot_ref, run_ref):
    phase = pl.program_id(0)
    step = pl.program_id(1)
    l = lg_ref[...]
    cols = l.shape[1]
    row8 = lax.broadcasted_iota(jnp.int32, (SUBLANES, cols), 0)
    neg = -jnp.inf
    big = jnp.int32(SUBLANES)

    def first_max(vals):
        top = jnp.max(vals, axis=0, keepdims=True)
        return top, jnp.min(jnp.where(vals == top, row8, big), axis=0, keepdims=True)

    is_group = row8 < N_GROUPS
    lg = l[:SUBLANES]
    m, gsel = first_max(jnp.where(is_group, lg, neg))
    denom = jnp.sum(jnp.where(is_group, jnp.exp(lg - m), 0.0), axis=0, keepdims=True)
    pgsel = 1.0 / denom
    le = l[SUBLANES:2 * SUBLANES]
    for g in range(1, N_GROUPS):
        le = jnp.where(gsel == g, l[(g + 1) * SUBLANES:(g + 2) * SUBLANES], le)
    v1, i1 = first_max(le)
    v2, i2 = first_max(jnp.where(row8 == i1, neg, le))
    ex = jnp.exp(v2 - v1)
    wk1 = pgsel / (1.0 + ex)
    wk2 = pgsel * ex / (1.0 + ex)
    e1 = gsel * EXPERTS_PER_GROUP + i1
    e2 = gsel * EXPERTS_PER_GROUP + i2
    row_e = lax.broadcasted_iota(jnp.int32, (N_EXPERTS, cols), 0)
    oh1 = jnp.where(row_e == e1, 1.0, 0.0)
    oh2 = jnp.where(row_e == e2, 1.0, 0.0)
    oh = oh1 + oh2

    @pl.when(jnp.logical_and(phase == 0, step == 0))
    def _():
        tot_ref[...] = jnp.zeros_like(tot_ref)

    @pl.when(phase == 0)
    def _():
        tot_ref[...] = tot_ref[...] + jnp.sum(oh, axis=1, keepdims=True)

    @pl.when(jnp.logical_and(phase == 1, step == 0))
    def _():
        tot = tot_ref[...]
        tiles = ((tot + (FFN_ROWS - 1.0)) * (1.0 / FFN_ROWS)).astype(jnp.int32).astype(F32)
        run_ref[...] = _dot(lower_ref[...], tiles.astype(BF16)) * FFN_ROWS

    @pl.when(phase == 1)
    def _():
        blocks = cols // LANES
        stacked = jnp.concatenate([oh[:, b * LANES:(b + 1) * LANES] for b in range(blocks)], axis=0)
        local = _dot(stacked.astype(BF16), upper_ref[...])
        run = run_ref[...]
        p1, p2 = [], []
        for b in range(blocks):
            sl = slice(b * LANES, (b + 1) * LANES)
            before = local[b * N_EXPERTS:(b + 1) * N_EXPERTS] + run
            p1.append(jnp.sum(oh1[:, sl] * before, axis=0, keepdims=True))
            p2.append(jnp.sum(oh2[:, sl] * before, axis=0, keepdims=True))
            run = run + jnp.sum(oh[:, sl], axis=1, keepdims=True)
        run_ref[...] = run
        pos1 = jnp.concatenate(p1, axis=1).astype(jnp.int32)
        pos2 = jnp.concatenate(p2, axis=1).astype(jnp.int32)
        zero_i = jnp.zeros_like(row8)
        mi_ref[...] = jnp.where(row8 == 0, pos1, jnp.where(row8 == 1, pos2, jnp.where(
            row8 == 2, e1, jnp.where(row8 == 3, e2, zero_i))))
        mf_ref[...] = jnp.where(row8 == 0, wk1, jnp.where(row8 == 1, wk2, 0.0))
        cnt_ref[...] = tot_ref[...].astype(jnp.int32)


def _router_call(lg_t):
    n_tokens = lg_t.shape[1]
    tc = ROUTER_COLS
    r = lax.broadcasted_iota(jnp.int32, (LANES, LANES), 0)
    c = lax.broadcasted_iota(jnp.int32, (LANES, LANES), 1)
    upper = jnp.where(r < c, 1.0, 0.0).astype(BF16)
    lower = jnp.where(c < r, 1.0, 0.0)[:N_EXPERTS, :N_EXPERTS].astype(BF16)
    col = pl.BlockSpec((SUBLANES, tc), lambda p, i: (0, i * p))
    return pl.pallas_call(
        _router_kernel,
        grid=(2, n_tokens // tc),
        in_specs=[pl.BlockSpec((LOGIT_ROWS, tc), lambda p, i: (0, i)),
                  _const_spec((LANES, LANES)), _const_spec((N_EXPERTS, N_EXPERTS))],
        out_specs=[col, col, _const_spec((N_EXPERTS, LANES))],
        out_shape=[jax.ShapeDtypeStruct((SUBLANES, n_tokens), jnp.int32),
                   jax.ShapeDtypeStruct((SUBLANES, n_tokens), F32),
                   jax.ShapeDtypeStruct((N_EXPERTS, LANES), jnp.int32)],
        scratch_shapes=[pltpu.VMEM((N_EXPERTS, LANES), F32), pltpu.VMEM((N_EXPERTS, LANES), F32)],
        compiler_params=pltpu.CompilerParams(
            dimension_semantics=("arbitrary", "arbitrary"), vmem_limit_bytes=VMEM_LIMIT),
        name="router",
    )(lg_t, upper, lower)


def _ffn_plan(counts, n_pairs):
    ids = jnp.arange(N_EXPERTS, dtype=jnp.int32)
    tiles_e = (counts + FFN_ROWS - 1) // FFN_ROWS
    tile_end = jnp.cumsum(tiles_e)
    tile_start = tile_end - tiles_e
    n_work = tile_end[-1]
    n_items = n_pairs // FFN_ROWS + N_EXPERTS
    t_w = jnp.arange(n_items, dtype=jnp.int32)
    w = t_w
    e_w = jnp.sum((tile_end[None, :] <= jnp.minimum(w, n_work - 1)[:, None]).astype(jnp.int32), axis=1)
    e_w = jnp.minimum(e_w, N_EXPERTS - 1)
    sel = e_w[:, None] == ids[None, :]
    pick = lambda a: jnp.sum(jnp.where(sel, a[None, :], 0), axis=1)
    valid_w = (w < n_work).astype(jnp.int32)
    pad_start = tile_start * FFN_ROWS + counts
    pad_len = tiles_e * FFN_ROWS - counts
    used = counts > 0
    order_e = jnp.cumsum(used.astype(jnp.int32)) - 1
    later_used = jnp.logical_and(used[None, :], ids[None, :] > ids[:, None])
    next_e = jnp.min(jnp.where(later_used, ids[None, :], N_EXPERTS), axis=1)
    next_e = jnp.where(next_e == N_EXPERTS, -1, next_e)
    i32 = lambda a: a.astype(jnp.int32)
    plan = (i32(t_w), i32(e_w), valid_w, i32(pick(order_e)), i32(pick(next_e)))
    return plan, (i32(pad_start), i32(pad_len), i32(n_work)[None])


def _scatter_kernel(tiles_p, n_steps, n_ffn_tiles, pos1_s, pos2_s, pad0_s, padn_s, ntile_s,
                    h2p_hbm, h2s_hbm, xs_hbm, buf, zero_ref, lsems, ssems, zsem):
    i = pl.program_id(0)
    tile_rows = DISPATCH_ROWS * TILE_ROWS

    def tile_load(step, do):
        slot = step % D_SLOTS
        for src, first, mine in ((h2p_hbm, step, step < tiles_p),
                                 (h2s_hbm, step - tiles_p, step >= tiles_p)):
            @pl.when(mine)
            def _():
                start = pl.multiple_of(first * tile_rows, tile_rows)
                do(pltpu.make_async_copy(src.at[pl.ds(start, tile_rows), :], buf.at[slot],
                                         lsems.at[slot]))

    def scatter(step):
        slot = step % D_SLOTS
        base = step * DISPATCH_ROWS

        def row_copy(r, pos_s):
            p = pos_s[base + r]
            return pltpu.make_async_copy(buf.at[slot, _token_tile(r), :], xs_hbm.at[_token_tile(p), :],
                                         ssems.at[slot])

        def start(r, carry):
            row_copy(r, pos1_s).start(priority=0)
            row_copy(r, pos2_s).start(priority=1)
            return carry

        def wait(r, carry):
            row_copy(r, pos1_s).wait()
            row_copy(r, pos2_s).wait()
            return carry

        return start, wait

    @pl.when(i == 0)
    def _():
        for ahead in range(D_SLOTS - 1):
            tile_load(ahead, lambda cp: cp.start())

    tile_load(i, lambda cp: cp.wait())
    lax.fori_loop(0, DISPATCH_ROWS, scatter(i)[0], 0, unroll=32)

    @pl.when(i >= 1)
    def _():
        lax.fori_loop(0, DISPATCH_ROWS, scatter(i - 1)[1], 0, unroll=8)

    @pl.when(i + D_SLOTS - 1 < n_steps)
    def _():
        tile_load(i + D_SLOTS - 1, lambda cp: cp.start())

    @pl.when(i == n_steps - 1)
    def _():
        lax.fori_loop(0, DISPATCH_ROWS, scatter(i)[1], 0, unroll=8)

        zero_ref[...] = jnp.zeros_like(zero_ref)

        def zero_copy(first_row, n_rows):
            first = pl.multiple_of(first_row * TILE_ROWS, TILE_ROWS)
            return pltpu.make_async_copy(zero_ref.at[pl.ds(0, n_rows * TILE_ROWS), :],
                                         xs_hbm.at[pl.ds(first, n_rows * TILE_ROWS), :], zsem)

        def fill(action):
            def pad_rows(expert, carry):
                first, left = pad0_s[expert], padn_s[expert]
                run = FFN_ROWS // 2
                while run >= 1:
                    take = left & run

                    @pl.when(take != 0)
                    def _(first=first, run=run):
                        action(zero_copy(first, run))

                    first = first + take
                    run //= 2
                return carry

            def tail_tile(tile, carry):
                action(zero_copy(tile * FFN_ROWS, FFN_ROWS))
                return carry

            lax.fori_loop(0, N_EXPERTS, pad_rows, 0)
            lax.fori_loop(ntile_s[0], n_ffn_tiles, tail_tile, 0)

        fill(lambda cp: cp.start())
        fill(lambda cp: cp.wait())


def _scatter_call(pos1, pos2, pads, h2_p, h2_s, n_sorted_rows):
    tm = DISPATCH_ROWS
    tiles_p = h2_p.shape[0] // (tm * TILE_ROWS)
    tiles_s = h2_s.shape[0] // (tm * TILE_ROWS)
    n_steps = tiles_p + tiles_s
    assert n_steps >= D_SLOTS
    any_spec = pl.BlockSpec(memory_space=pl.ANY)
    return pl.pallas_call(
        functools.partial(_scatter_kernel, tiles_p, n_steps, n_sorted_rows // FFN_ROWS),
        grid_spec=pltpu.PrefetchScalarGridSpec(
            num_scalar_prefetch=5,
            grid=(n_steps,),
            in_specs=[any_spec, any_spec],
            out_specs=pl.BlockSpec(memory_space=pl.ANY),
            scratch_shapes=[pltpu.VMEM((D_SLOTS, tm * TILE_ROWS, LANES), F32),
                            pltpu.VMEM((FFN_ROWS * TILE_ROWS, LANES), F32),
                            pltpu.SemaphoreType.DMA((D_SLOTS,)),
                            pltpu.SemaphoreType.DMA((D_SLOTS,)),
                            pltpu.SemaphoreType.DMA(())],
        ),
        out_shape=jax.ShapeDtypeStruct((n_sorted_rows * TILE_ROWS, LANES), F32),
        compiler_params=pltpu.CompilerParams(
            dimension_semantics=("arbitrary",), vmem_limit_bytes=VMEM_LIMIT),
        name="dispatch_scatter",
    )(pos1, pos2, *pads, h2_p, h2_s)


def _ffn_kernel(n_items, wt_s, we_s, wvalid_s, word_s, wnext_s, x_hbm, w1_hbm, w3_hbm, w2_hbm,
                o_ref, xbuf, w1f_ref, w3f_ref, w2f_ref, w13b_ref, w2b_ref, xsems, sems):
    w = pl.program_id(0)
    e = we_s[w]
    e_prev = we_s[jnp.maximum(w - 1, 0)]
    slot = word_s[w] % 2
    tile_rows = FFN_ROWS * TILE_ROWS

    def x_copy(item):
        first = pl.multiple_of(wt_s[item] * tile_rows, tile_rows)
        return pltpu.make_async_copy(x_hbm.at[pl.ds(first, tile_rows), :], xbuf.at[item % X_SLOTS],
                                     xsems.at[item % X_SLOTS])

    @pl.when(w == 0)
    def _():
        for ahead in range(X_SLOTS - 1):
            x_copy(ahead).start()

    @pl.when(w + X_SLOTS - 1 < n_items)
    def _():
        x_copy(w + X_SLOTS - 1).start()

    def weight_copies(expert, buf):
        return [pltpu.make_async_copy(src.at[expert], dst.at[buf], sems.at[buf])
                for src, dst in ((w1_hbm, w1f_ref), (w3_hbm, w3f_ref), (w2_hbm, w2f_ref))]

    @pl.when(w == 0)
    def _():
        for cp in weight_copies(e, slot):
            cp.start()

    @pl.when(jnp.logical_or(w == 0, e != e_prev))
    def _():
        for cp in weight_copies(e, slot):
            cp.wait()
        w13b_ref[:, :D_EXPERT] = w1f_ref[slot].astype(BF16)
        w13b_ref[:, D_EXPERT:] = w3f_ref[slot].astype(BF16)
        w2b_ref[...] = w2f_ref[slot].astype(BF16)
        nxt = wnext_s[w]

        @pl.when(nxt >= 0)
        def _():
            for cp in weight_copies(nxt, 1 - slot):
                cp.start()

    x_copy(w).wait()

    @pl.when(wvalid_s[w] == 1)
    def _():
        x = _load_token_tiles(xbuf, (w % X_SLOTS,), FFN_ROWS)
        h13 = _dot(x.astype(BF16), w13b_ref[...])
        a = h13[:, :D_EXPERT]
        b = h13[:, D_EXPERT:]
        hdn = (a * jax.nn.sigmoid(a)) * b
        _store_token_tiles(o_ref, _dot(hdn.astype(BF16), w2b_ref[...]))

    @pl.when(wvalid_s[w] == 0)
    def _():
        o_ref[...] = jnp.zeros_like(o_ref)


def _ffn_call(plan, xs, w1, w3, w2):
    t_w = plan[0]
    n_pairs = xs.shape[0] // TILE_ROWS
    tm = FFN_ROWS
    any_spec = pl.BlockSpec(memory_space=pl.ANY)
    row = pl.BlockSpec((tm * TILE_ROWS, LANES), lambda i, wt, *_: (wt[i], 0))
    n_items = t_w.shape[0]
    assert n_items >= X_SLOTS
    return pl.pallas_call(
        functools.partial(_ffn_kernel, n_items),
        grid_spec=pltpu.PrefetchScalarGridSpec(
            num_scalar_prefetch=len(plan),
            grid=(n_items,),
            in_specs=[any_spec, any_spec, any_spec, any_spec],
            out_specs=row,
            scratch_shapes=[pltpu.VMEM((X_SLOTS, tm * TILE_ROWS, LANES), F32),
                            pltpu.VMEM((2, D_MODEL, D_EXPERT), F32),
                            pltpu.VMEM((2, D_MODEL, D_EXPERT), F32),
                            pltpu.VMEM((2, D_EXPERT, D_MODEL), F32),
                            pltpu.VMEM((D_MODEL, 2 * D_EXPERT), BF16),
                            pltpu.VMEM((D_EXPERT, D_MODEL), BF16),
                            pltpu.SemaphoreType.DMA((X_SLOTS,)),
                            pltpu.SemaphoreType.DMA((2,))],
        ),
        out_shape=jax.ShapeDtypeStruct((n_pairs * TILE_ROWS, LANES), F32),
        compiler_params=pltpu.CompilerParams(
            dimension_semantics=("arbitrary",), vmem_limit_bytes=VMEM_LIMIT),
        name="expert_ffn",
    )(*plan, xs, w1, w3, w2)


def _combine_kernel(three_d, row0, n_steps, pos1_s, pos2_s, x1_ref, wk_ref, mod_ref, g_ref, b_ref,
                    ys_hbm, o_ref, ybuf, sems):
    step = pl.program_id(0)
    slot = step % 2

    def gather(tile, tile_slot):
        base = row0 + tile * COMBINE_ROWS

        def row_copy(r, k, pos_s):
            p = pos_s[base + r]
            return pltpu.make_async_copy(ys_hbm.at[_token_tile(p), :],
                                         ybuf.at[tile_slot, k, _token_tile(r), :], sems.at[tile_slot])

        def start(r, carry):
            row_copy(r, 0, pos1_s).start(priority=0)
            row_copy(r, 1, pos2_s).start(priority=1)
            return carry

        def wait(r, carry):
            row_copy(r, 0, pos1_s).wait()
            row_copy(r, 1, pos2_s).wait()
            return carry

        return start, wait

    @pl.when(step == 0)
    def _():
        lax.fori_loop(0, COMBINE_ROWS, gather(step, slot)[0], 0, unroll=32)

    @pl.when(step + 1 < n_steps)
    def _():
        lax.fori_loop(0, COMBINE_ROWS, gather(step + 1, 1 - slot)[0], 0, unroll=32)

    lax.fori_loop(0, COMBINE_ROWS, gather(step, slot)[1], 0, unroll=8)

    wk = wk_ref[...]
    r = lax.broadcasted_iota(jnp.int32, (COMBINE_ROWS, COMBINE_ROWS), 0)
    c = lax.broadcasted_iota(jnp.int32, (COMBINE_ROWS, COMBINE_ROWS), 1)
    wk_col = [jnp.sum(jnp.where(r == c, wk[k:k + 1, :], 0.0), axis=1, keepdims=True)
              for k in range(TOP_K)]
    moe = (wk_col[0] * _load_token_tiles(ybuf, (slot, 0), COMBINE_ROWS)
           + wk_col[1] * _load_token_tiles(ybuf, (slot, 1), COMBINE_ROWS))
    x1 = x1_ref[...]
    if three_d:
        g2 = mod_ref[0, :, 5 * D_MODEL:]
        o_ref[0] = _layer_norm(ALPHA * x1 + (1.0 + g2) * moe, g_ref[...], b_ref[...])
    else:
        def seq_rows(v, b):
            return v[b * SUBLANES:(b + 1) * SUBLANES]

        resid = jnp.concatenate([
            ALPHA * seq_rows(x1, b) + (1.0 + mod_ref[b:b + 1, 5 * D_MODEL:]) * seq_rows(moe, b)
            for b in range(COMBINE_ROWS // SUBLANES)], axis=0)
        o_ref[...] = _layer_norm(resid, g_ref[...], b_ref[...])


def _combine_call(three_d, pos1, pos2, x1_all, mf_all, mod, ln2_g, ln2_b, ys, row0, batch, seq):
    tm = COMBINE_ROWS
    blk0 = row0 // tm
    if three_d:
        tiles = seq // tm
        grid = (batch * tiles,)
        mod_spec = pl.BlockSpec((1, 1, 6 * D_MODEL), lambda i, *_: (i // tiles, 0, 0))
        out_spec = pl.BlockSpec((1, tm, D_MODEL), lambda i, *_: (i // tiles, i % tiles, 0))
        out_shape = jax.ShapeDtypeStruct((batch, seq, D_MODEL), F32)
    else:
        grid = (batch * seq // tm,)
        mod_spec = pl.BlockSpec((tm // SUBLANES, 6 * D_MODEL), lambda i, *_: (i, 0))
        out_spec = pl.BlockSpec((tm, D_MODEL), lambda i, *_: (i, 0))
        out_shape = jax.ShapeDtypeStruct((batch * seq, D_MODEL), F32)
    return pl.pallas_call(
        functools.partial(_combine_kernel, three_d, row0, grid[0]),
        grid_spec=pltpu.PrefetchScalarGridSpec(
            num_scalar_prefetch=2,
            grid=grid,
            in_specs=[pl.BlockSpec((tm, D_MODEL), lambda i, *_: (i, 0)),
                      pl.BlockSpec((SUBLANES, tm), lambda i, *_: (0, blk0 + i)),
                      mod_spec,
                      pl.BlockSpec(ln2_g.shape, lambda i, *_: (0, 0)),
                      pl.BlockSpec(ln2_b.shape, lambda i, *_: (0, 0)),
                      pl.BlockSpec(memory_space=pl.ANY)],
            out_specs=out_spec,
            scratch_shapes=[pltpu.VMEM((2, TOP_K, tm * TILE_ROWS, LANES), F32),
                            pltpu.SemaphoreType.DMA((2,))],
        ),
        out_shape=out_shape,
        compiler_params=pltpu.CompilerParams(
            dimension_semantics=("arbitrary",), vmem_limit_bytes=VMEM_LIMIT),
        name="combine_prompt" if three_d else "combine_sample",
    )(pos1, pos2, x1_all, mf_all, mod, ln2_g, ln2_b, ys)


def _block_diag(w):
    heads, n, _ = w.shape
    eye = jnp.eye(heads, dtype=w.dtype)
    return (eye[:, None, :, None] * w[:, :, None, :]).reshape(heads * n, heads * n)


def kernel(x_prompt, x_sample, state_rglru_h, state_conv, c_prompt, c_sample, w_ada, b_ada, w_in, w_s, b_s, lnv_g, lnv_b, conv_w, conv_b, lru_wa, lru_ba, lru_wx, lru_bx, lru_lam, w_out, ln1_g, ln1_b, w_rg, b_rg, w_re, b_re, w1, w3, w2, ln2_g, ln2_b):
    batch, seq, _ = x_prompt.shape
    dec_batch, dec_seq, _ = x_sample.shape
    assert dec_seq == SUBLANES and seq % ROWS == 0 and (dec_batch * dec_seq) % ROWS == 0
    n_prompt = batch * seq
    n_sample = dec_batch * dec_seq
    n_tokens = n_prompt + n_sample
    n_pairs = TOP_K * n_tokens
    assert n_tokens % ROUTER_COLS == 0 and n_pairs % FFN_ROWS == 0
    l = 0

    mod_p, mod_s = _ada_call(c_prompt, c_sample, w_ada[l], b_ada[l][None])
    mod_p3 = mod_p[:, None, :]

    tri = jnp.tril(jnp.ones((CHUNK, CHUNK), dtype=bool))
    ws_tri = jnp.where(tri[None], w_s[l], 0.0)
    reps_p = MXU_DIM // CHUNK
    eye_p = jnp.eye(reps_p, dtype=F32)
    wsm_p = (eye_p[None, :, None, :, None] * ws_tri[:, None, :, None, :]).reshape(
        A_HEADS, MXU_DIM, MXU_DIM).astype(BF16)
    reps_s = MXU_DIM // dec_seq
    pos_in_seq = jnp.arange(MXU_DIM, dtype=jnp.int32) % dec_seq
    seq_of_row = jnp.arange(MXU_DIM, dtype=jnp.int32) // dec_seq
    spread = (pos_in_seq[:, None] == jnp.arange(dec_seq, dtype=jnp.int32)[None, :]).astype(F32)
    same_seq = seq_of_row[:, None] == seq_of_row[None, :]
    wsm_s = jnp.einsum("ri,hij,cj->hrc", spread, ws_tri[:, :dec_seq, :dec_seq], spread,
                       precision=lax.Precision.HIGHEST)
    wsm_s = jnp.where(same_seq[None], wsm_s, 0.0).astype(BF16)
    bs_pos = jnp.repeat(jnp.transpose(b_s[l]), A_HEAD_DIM, axis=1)
    bsf_p = jnp.tile(bs_pos, (reps_p, 1))
    bsf_s = jnp.tile(bs_pos[:dec_seq], (reps_s, 1))

    wa_bd = _block_diag(lru_wa[l])
    wx_bd = _block_diag(lru_wx[l])
    wg = jnp.stack([
        jnp.concatenate([wa_bd[q * MXU_DIM:(q + 1) * MXU_DIM, q * MXU_DIM:(q + 1) * MXU_DIM],
                         wx_bd[q * MXU_DIM:(q + 1) * MXU_DIM, q * MXU_DIM:(q + 1) * MXU_DIM]], axis=1)
        for q in range(D_B // MXU_DIM)]).astype(BF16)
    bg = jnp.concatenate([lru_ba[l].reshape(1, D_B), lru_bx[l].reshape(1, D_B)], axis=1)

    group_pad = jnp.zeros((SUBLANES - N_GROUPS, D_MODEL), F32)
    wr = jnp.concatenate([jnp.transpose(w_rg[l]), group_pad, jnp.transpose(w_re[l])], axis=0)
    wr_hi = wr.astype(BF16)
    wr_lo = jnp.concatenate([wr_hi, (wr - wr_hi.astype(F32)).astype(BF16)], axis=0)
    br = jnp.concatenate([b_rg[l], jnp.zeros((SUBLANES - N_GROUPS,), F32), b_re[l]])[:, None]

    common = dict(lnv_g=lnv_g[l][None], lnv_b=lnv_b[l][None], conv_w=conv_w[l], conv_b=conv_b[l][None],
                  wg=wg, bg=bg, lam=lru_lam[l][None], w_out=w_out[l].astype(BF16),
                  ln1_g=ln1_g[l][None], ln1_b=ln1_b[l][None], wr_hi=wr_hi, wr_lo=wr_lo, br=br)
    w_in_b = w_in[l].astype(BF16)
    weights_p = _mixer_weights(w_in_b, wsm_p, bsf_p, **common)
    weights_s = _mixer_weights(w_in_b, wsm_s, bsf_s, **common)

    x1_p, h2_p, lg_p, cst_p, hst_p = _mixer_prompt(x_prompt, mod_p3, weights_p)

    cpad = jnp.concatenate(
        [jnp.zeros((dec_batch, SUBLANES - (CONV_W - 1), D_B), F32), state_conv[l]], axis=1
    ).reshape(n_sample, D_B)
    x1_s, h2_s, lg_s, xr_s, v_s, hs_s = _mixer_sample(
        x_sample.reshape(n_sample, D_MODEL), mod_s, weights_s, state_rglru_h[l], cpad)

    meta_i, meta_f, counts = _router_call(jnp.concatenate([lg_p, lg_s], axis=1))
    pos1, pos2 = meta_i[0], meta_i[1]
    plan, pads = _ffn_plan(counts[:, 0], n_pairs)
    xs = _scatter_call(pos1, pos2, pads, h2_p, h2_s, plan[0].shape[0] * FFN_ROWS)
    ys = _ffn_call(plan, xs, w1[l], w3[l], w2[l])

    ln2g, ln2b = ln2_g[l][None], ln2_b[l][None]
    y_prompt = _combine_call(True, pos1, pos2, x1_p, meta_f, mod_p3, ln2g, ln2b, ys,
                             0, batch, seq)
    y_sample = _combine_call(False, pos1, pos2, x1_s, meta_f, mod_s, ln2g, ln2b, ys,
                             n_prompt, dec_batch, dec_seq).reshape(dec_batch, dec_seq, D_MODEL)

    new_h_prompt = hst_p[:, SUBLANES - 1, :][None]
    new_conv_prompt = cst_p[:, SUBLANES - (CONV_W - 1):, :][None]
    xr_s3 = xr_s.reshape(dec_batch, dec_seq, D_B)
    new_h_sample = hs_s.reshape(dec_batch, dec_seq, D_B)[:, dec_seq - 1, :][None]
    new_conv_sample = xr_s3[:, dec_seq - (CONV_W - 1):, :][None]
    new_chunk_v_sample = v_s.reshape(dec_batch, dec_seq, D_A)[None]
    return (y_prompt, y_sample, new_h_prompt, new_conv_prompt, new_h_sample, new_conv_sample,
            new_chunk_v_sample)
```

```python
import functools
import math

import jax
import jax.numpy as jnp
from jax import lax
from jax.experimental import pallas as pl
from jax.experimental.pallas import tpu as pltpu

D_MODEL = 1024
D_A = 512
A_HEADS = 4
A_HEAD_DIM = 128
CHUNK = 128
D_B = 512
B_HEADS = 8
B_HEAD_DIM = 64
CONV_W = 4
LRU_C = 8.0
D_IN = 2 * D_A + 2 * D_B
N_GROUPS = 4
EXPERTS_PER_GROUP = 8
N_EXPERTS = 32
TOP_K = 2
D_EXPERT = 512
LN_EPS = 1e-5
DEPTH = 1
ALPHA = (2.0 * DEPTH) ** 0.25

LANES = 128
SUBLANES = 8
MXU_DIM = 256
VMEM_LIMIT = 56 * 1024 * 1024

ROWS = 512
ROUTER_COLS = 1024
LOGIT_ROWS = SUBLANES + N_EXPERTS
DISPATCH_ROWS = 1024
D_SLOTS = 3
FFN_ROWS = 256
FFN_GROUP = 2
X_SLOTS = 3
COMBINE_ROWS = 256
BF16 = jnp.bfloat16
F32 = jnp.float32


def _dot(a, b):
    return jnp.dot(a, b, preferred_element_type=F32)


def _dot_nt(a, b):
    return lax.dot_general(a, b, (((1,), (1,)), ((), ())), preferred_element_type=F32)


TILE_ROWS = D_MODEL // LANES


def _token_tile(r):
    return pl.ds(pl.multiple_of(r * TILE_ROWS, TILE_ROWS), TILE_ROWS)


def _load_token_tiles(ref, lead, n_rows, first_token=0):
    base = first_token * TILE_ROWS
    parts = [ref[lead + (pl.ds(base + s, n_rows, stride=TILE_ROWS), slice(None))]
             for s in range(TILE_ROWS)]
    return jnp.concatenate(parts, axis=1)


def _store_token_tiles(ref, val, first_token=0):
    n_rows = val.shape[0]
    base = first_token * TILE_ROWS
    for s in range(TILE_ROWS):
        ref[pl.ds(base + s, n_rows, stride=TILE_ROWS), :] = val[:, s * LANES:(s + 1) * LANES]


def _layer_norm(x, g, b):
    mu = jnp.mean(x, axis=-1, keepdims=True)
    xc = x - mu
    var = jnp.mean(xc * xc, axis=-1, keepdims=True)
    return xc * lax.rsqrt(var + LN_EPS) * g + b


def _gelu_tanh(x):
    c = math.sqrt(2.0 / math.pi)
    return x * (0.5 * (1.0 + jnp.tanh(c * (x + 0.044715 * (x * x * x)))))


def _const_spec(shape):
    nd = len(shape)
    return pl.BlockSpec(shape, lambda *_: (0,) * nd)


def _ada_kernel(cp_ref, cs_ref, w_ref, b_ref, op_ref, os_ref):
    w = w_ref[...].astype(BF16)
    for c_ref, o_ref in ((cp_ref, op_ref), (cs_ref, os_ref)):
        c = c_ref[...]
        a = (c * jax.nn.sigmoid(c)).astype(BF16)
        o_ref[...] = _dot(a, w) + b_ref[...]


def _ada_call(c_prompt, c_sample, w_ada, b_ada):
    tn = 1536
    rows_p, rows_s = c_prompt.shape[0], c_sample.shape[0]
    return pl.pallas_call(
        _ada_kernel,
        grid=(6 * D_MODEL // tn,),
        in_specs=[
            pl.BlockSpec((rows_p, D_MODEL), lambda j: (0, 0)),
            pl.BlockSpec((rows_s, D_MODEL), lambda j: (0, 0)),
            pl.BlockSpec((D_MODEL, tn), lambda j: (0, j)),
            pl.BlockSpec((1, tn), lambda j: (0, j)),
        ],
        out_specs=[pl.BlockSpec((rows_p, tn), lambda j: (0, j)),
                   pl.BlockSpec((rows_s, tn), lambda j: (0, j))],
        out_shape=[jax.ShapeDtypeStruct((rows_p, 6 * D_MODEL), F32),
                   jax.ShapeDtypeStruct((rows_s, 6 * D_MODEL), F32)],
        compiler_params=pltpu.CompilerParams(
            dimension_semantics=("arbitrary",), vmem_limit_bytes=VMEM_LIMIT),
        name="ada_mod",
    )(c_prompt, c_sample, w_ada, b_ada)


def _mixer_kernel(sample, *refs):
    if sample:
        (x_ref, mod_ref, win_ref, wsm_ref, bsf_ref, lnvg_ref, lnvb_ref, convw_ref, convb_ref,
         wg_ref, bg_ref, lam_ref, wout_ref, ln1g_ref, ln1b_ref, wrh_ref, wrl_ref, br_ref,
         h0_ref, cpad_ref,
         x1_ref, h2_ref, lg_ref, xr_ref, v_ref, hs_ref) = refs
    else:
        (x_ref, mod_ref, win_ref, wsm_ref, bsf_ref, lnvg_ref, lnvb_ref, convw_ref, convb_ref,
         wg_ref, bg_ref, lam_ref, wout_ref, ln1g_ref, ln1b_ref, wrh_ref, wrl_ref, br_ref,
         x1_ref, h2_ref, lg_ref, cst_ref, hst_ref, prev_ref, hcar_ref) = refs

    rows = ROWS
    groups = rows // SUBLANES

    if sample:
        x = x_ref[...]

        def chunk(c):
            return lambda b: mod_ref[b:b + 1, c * D_MODEL:(c + 1) * D_MODEL]
    else:
        x = x_ref[0]

        def chunk(c):
            return lambda b: mod_ref[0, :, c * D_MODEL:(c + 1) * D_MODEL]

        @pl.when(pl.program_id(1) == 0)
        def _():
            prev_ref[...] = jnp.zeros_like(prev_ref)
            hcar_ref[...] = jnp.zeros_like(hcar_ref)

    def modulated(fn, *vals):
        if not sample:
            return fn(0, *vals)
        parts = [fn(b, *(v[b * SUBLANES:(b + 1) * SUBLANES] for v in vals)) for b in range(groups)]
        return jnp.concatenate(parts, axis=0)

    sh1, sc1, g1, sh2, sc2, g2 = [chunk(c) for c in range(6)]

    h = modulated(lambda b, xv: xv * (1.0 + sc1(b)) + sh1(b), x)
    proj = _dot(h.astype(BF16), win_ref[...])
    u = proj[:, :D_A]
    v = _layer_norm(proj[:, D_A:2 * D_A], lnvg_ref[...], lnvb_ref[...])
    xr = proj[:, 2 * D_A:2 * D_A + D_B]
    yr = proj[:, 2 * D_A + D_B:]

    vb = v.astype(BF16)
    halves = []
    for r in range(rows // MXU_DIM):
        heads = [
            _dot(wsm_ref[hd], vb[r * MXU_DIM:(r + 1) * MXU_DIM, hd * A_HEAD_DIM:(hd + 1) * A_HEAD_DIM])
            for hd in range(A_HEADS)
        ]
        halves.append(jnp.concatenate(heads, axis=1) + bsf_ref[...])
    mixed = halves[0] if len(halves) == 1 else jnp.concatenate(halves, axis=0)
    out_a = u * mixed

    xr3 = xr.reshape(groups, SUBLANES, D_B)
    t3 = lax.broadcasted_iota(jnp.int32, (1, SUBLANES, D_B), 1)
    cw = convw_ref[...]
    if sample:
        cpad3 = cpad_ref[...].reshape(groups, SUBLANES, D_B)
    else:
        prev8 = prev_ref[...]
    xc3 = convb_ref[...][None]
    for k in range(CONV_W):
        s = CONV_W - 1 - k
        if s == 0:
            shifted = xr3
        else:
            rot = pltpu.roll(xr3, s, 1)
            if sample:
                rot_prev = pltpu.roll(cpad3, s, 1)
            else:
                first = pltpu.roll(prev8, s, 0)[None]
                rot_prev = jnp.concatenate([first, rot[:-1]], axis=0) if groups > 1 else first
            shifted = jnp.where(t3 >= s, rot, rot_prev)
        xc3 = xc3 + shifted * cw[k:k + 1, :][None]
    xc = xc3.reshape(rows, D_B)

    xcb = xc.astype(BF16)
    gq = [_dot(xcb[:, q * MXU_DIM:(q + 1) * MXU_DIM], wg_ref[q]) for q in range(D_B // MXU_DIM)]
    bg = bg_ref[...]
    pre_a = jnp.concatenate([g[:, :MXU_DIM] for g in gq], axis=1) + bg[:, :D_B]
    pre_x = jnp.concatenate([g[:, MXU_DIM:] for g in gq], axis=1) + bg[:, D_B:]
    gate_a = jax.nn.sigmoid(pre_a)
    gate_x = jax.nn.sigmoid(pre_x)
    nl = -lam_ref[...]
    softplus = jnp.maximum(nl, 0.0) + jnp.log1p(jnp.exp(-jnp.abs(nl)))
    log_a = -LRU_C * gate_a * softplus
    a = jnp.exp(log_a)
    mult = jnp.sqrt(-jnp.tanh(log_a) * (1.0 + a * a))
    bx = mult * (gate_x * xc)

    a3 = a.reshape(groups, SUBLANES, D_B)
    b3 = bx.reshape(groups, SUBLANES, D_B)
    for d in (1, 2, 4):
        m = t3 >= d
        a_sh = jnp.where(m, pltpu.roll(a3, d, 1), 1.0)
        b_sh = jnp.where(m, pltpu.roll(b3, d, 1), 0.0)
        b3 = a3 * b_sh + b3
        a3 = a3 * a_sh
    if sample:
        hs3 = jnp.stack([a3[g] * h0_ref[g:g + 1, :] + b3[g] for g in range(groups)], axis=0)
    else:
        h_prev = hcar_ref[...][SUBLANES - 1:SUBLANES, :]
        hs_list = []
        for g in range(groups):
            hg = a3[g] * h_prev + b3[g]
            hs_list.append(hg)
            h_prev = hg[SUBLANES - 1:SUBLANES, :]
        hs3 = jnp.stack(hs_list, axis=0)
    hs = hs3.reshape(rows, D_B)

    out_b = hs * _gelu_tanh(yr)
    mix = _dot(out_a.astype(BF16), wout_ref[:D_A, :]) + _dot(out_b.astype(BF16), wout_ref[D_A:, :])
    y1 = modulated(lambda b, xv, mv: ALPHA * xv + (1.0 + g1(b)) * mv, x, mix)
    x1 = _layer_norm(y1, ln1g_ref[...], ln1b_ref[...])
    h2 = modulated(lambda b, xv: xv * (1.0 + sc2(b)) + sh2(b), x1)

    h2_hi = h2.astype(BF16)
    h2_lo = (h2 - h2_hi.astype(F32)).astype(BF16)
    both = _dot_nt(wrl_ref[...], h2_hi)
    logits = (both[:LOGIT_ROWS] + _dot_nt(wrh_ref[...], h2_lo) + both[LOGIT_ROWS:] + br_ref[...])

    x1_ref[...] = x1
    _store_token_tiles(h2_ref, h2)
    lg_ref[...] = logits
    if sample:
        xr_ref[...] = xr
        v_ref[...] = v
        hs_ref[...] = hs
    else:
        last_x = xr3[groups - 1]
        last_h = hs3[groups - 1]
        prev_ref[...] = last_x
        hcar_ref[...] = last_h
        cst_ref[0] = last_x
        hst_ref[0] = last_h


def _mixer_weights(w_in, wsm, bsf, lnv_g, lnv_b, conv_w, conv_b, wg, bg, lam, w_out, ln1_g, ln1_b,
                   wr_hi, wr_lo, br):
    arrs = [w_in, wsm, bsf, lnv_g, lnv_b, conv_w, conv_b, wg, bg, lam, w_out, ln1_g, ln1_b,
            wr_hi, wr_lo, br]
    return arrs, [_const_spec(a.shape) for a in arrs]


def _mixer_prompt(x, mod3, weights):
    batch, seq, _ = x.shape
    n_tokens = batch * seq
    tiles = seq // ROWS
    arrs, specs = weights
    row_spec = lambda w: pl.BlockSpec((ROWS, w), lambda b, j: (b * tiles + j, 0))
    st_spec = pl.BlockSpec((1, SUBLANES, D_B), lambda b, j: (b, 0, 0))
    return pl.pallas_call(
        functools.partial(_mixer_kernel, False),
        grid=(batch, tiles),
        in_specs=[pl.BlockSpec((1, ROWS, D_MODEL), lambda b, j: (b, j, 0)),
                  pl.BlockSpec((1, 1, 6 * D_MODEL), lambda b, j: (b, 0, 0))] + specs,
        out_specs=[row_spec(D_MODEL),
                   pl.BlockSpec((ROWS * TILE_ROWS, LANES), lambda b, j: (b * tiles + j, 0)),
                   pl.BlockSpec((LOGIT_ROWS, ROWS), lambda b, j: (0, b * tiles + j)), st_spec, st_spec],
        out_shape=[jax.ShapeDtypeStruct((n_tokens, D_MODEL), F32),
                   jax.ShapeDtypeStruct((n_tokens * TILE_ROWS, LANES), F32),
                   jax.ShapeDtypeStruct((LOGIT_ROWS, n_tokens), F32),
                   jax.ShapeDtypeStruct((batch, SUBLANES, D_B), F32),
                   jax.ShapeDtypeStruct((batch, SUBLANES, D_B), F32)],
        scratch_shapes=[pltpu.VMEM((SUBLANES, D_B), F32), pltpu.VMEM((SUBLANES, D_B), F32)],
        compiler_params=pltpu.CompilerParams(
            dimension_semantics=("arbitrary", "arbitrary"), vmem_limit_bytes=VMEM_LIMIT),
        name="mixer_prompt",
    )(x, mod3, *arrs)


def _mixer_sample(x2, mod_seq, weights, h0, cpad):
    n_rows = x2.shape[0]
    arrs, specs = weights
    row = lambda w: pl.BlockSpec((ROWS, w), lambda i: (i, 0))
    seq_row = lambda w: pl.BlockSpec((ROWS // SUBLANES, w), lambda i: (i, 0))
    return pl.pallas_call(
        functools.partial(_mixer_kernel, True),
        grid=(n_rows // ROWS,),
        in_specs=[row(D_MODEL), seq_row(6 * D_MODEL)] + specs + [seq_row(D_B), row(D_B)],
        out_specs=[row(D_MODEL), pl.BlockSpec((ROWS * TILE_ROWS, LANES), lambda i: (i, 0)),
                   pl.BlockSpec((LOGIT_ROWS, ROWS), lambda i: (0, i)),
                   row(D_B), row(D_B), row(D_B)],
        out_shape=[jax.ShapeDtypeStruct((n_rows, D_MODEL), F32),
                   jax.ShapeDtypeStruct((n_rows * TILE_ROWS, LANES), F32),
                   jax.ShapeDtypeStruct((LOGIT_ROWS, n_rows), F32),
                   jax.ShapeDtypeStruct((n_rows, D_B), F32),
                   jax.ShapeDtypeStruct((n_rows, D_B), F32),
                   jax.ShapeDtypeStruct((n_rows, D_B), F32)],
        compiler_params=pltpu.CompilerParams(
            dimension_semantics=("arbitrary",), vmem_limit_bytes=VMEM_LIMIT),
        name="mixer_sample",
    )(x2, mod_seq, *arrs, h0, cpad)


def _router_kernel(lg_ref, upper_ref, lower_ref, mi_ref, mf_ref, cnt_ref, tot_ref, run_ref):
    phase = pl.program_id(0)
    step = pl.program_id(1)
    l = lg_ref[...]
    cols = l.shape[1]
    row8 = lax.broadcasted_iota(jnp.int32, (SUBLANES, cols), 0)
    neg = -jnp.inf
    big = jnp.int32(SUBLANES)

    def first_max(vals):
        top = jnp.max(vals, axis=0, keepdims=True)
        return top, jnp.min(jnp.where(vals == top, row8, big), axis=0, keepdims=True)

    is_group = row8 < N_GROUPS
    lg = l[:SUBLANES]
    m, gsel = first_max(jnp.where(is_group, lg, neg))
    denom = jnp.sum(jnp.where(is_group, jnp.exp(lg - m), 0.0), axis=0, keepdims=True)
    pgsel = 1.0 / denom
    le = l[SUBLANES:2 * SUBLANES]
    for g in range(1, N_GROUPS):
        le = jnp.where(gsel == g, l[(g + 1) * SUBLANES:(g + 2) * SUBLANES], le)
    v1, i1 = first_max(le)
    v2, i2 = first_max(jnp.where(row8 == i1, neg, le))
    ex = jnp.exp(v2 - v1)
    wk1 = pgsel / (1.0 + ex)
    wk2 = pgsel * ex / (1.0 + ex)
    e1 = gsel * EXPERTS_PER_GROUP + i1
    e2 = gsel * EXPERTS_PER_GROUP + i2
    row_e = lax.broadcasted_iota(jnp.int32, (N_EXPERTS, cols), 0)
    oh1 = jnp.where(row_e == e1, 1.0, 0.0)
    oh2 = jnp.where(row_e == e2, 1.0, 0.0)
    oh = oh1 + oh2

    @pl.when(jnp.logical_and(phase == 0, step == 0))
    def _():
        tot_ref[...] = jnp.zeros_like(tot_ref)

    @pl.when(phase == 0)
    def _():
        tot_ref[...] = tot_ref[...] + jnp.sum(oh, axis=1, keepdims=True)

    @pl.when(jnp.logical_and(phase == 1, step == 0))
    def _():
        tot = tot_ref[...]
        tiles = ((tot + (FFN_ROWS - 1.0)) * (1.0 / FFN_ROWS)).astype(jnp.int32).astype(F32)
        run_ref[...] = _dot(lower_ref[...], tiles.astype(BF16)) * FFN_ROWS

    @pl.when(phase == 1)
    def _():
        blocks = cols // LANES
        stacked = jnp.concatenate([oh[:, b * LANES:(b + 1) * LANES] for b in range(blocks)], axis=0)
        local = _dot(stacked.astype(BF16), upper_ref[...])
        run = run_ref[...]
        p1, p2 = [], []
        for b in range(blocks):
            sl = slice(b * LANES, (b + 1) * LANES)
            before = local[b * N_EXPERTS:(b + 1) * N_EXPERTS] + run
            p1.append(jnp.sum(oh1[:, sl] * before, axis=0, keepdims=True))
            p2.append(jnp.sum(oh2[:, sl] * before, axis=0, keepdims=True))
            run = run + jnp.sum(oh[:, sl], axis=1, keepdims=True)
        run_ref[...] = run
        pos1 = jnp.concatenate(p1, axis=1).astype(jnp.int32)
        pos2 = jnp.concatenate(p2, axis=1).astype(jnp.int32)
        zero_i = jnp.zeros_like(row8)
        mi_ref[...] = jnp.where(row8 == 0, pos1, jnp.where(row8 == 1, pos2, jnp.where(
            row8 == 2, e1, jnp.where(row8 == 3, e2, zero_i))))
        mf_ref[...] = jnp.where(row8 == 0, wk1, jnp.where(row8 == 1, wk2, 0.0))
        cnt_ref[...] = tot_ref[...].astype(jnp.int32)


def _router_call(lg_t):
    n_tokens = lg_t.shape[1]
    tc = ROUTER_COLS
    r = lax.broadcasted_iota(jnp.int32, (LANES, LANES), 0)
    c = lax.broadcasted_iota(jnp.int32, (LANES, LANES), 1)
    upper = jnp.where(r < c, 1.0, 0.0).astype(BF16)
    lower = jnp.where(c < r, 1.0, 0.0)[:N_EXPERTS, :N_EXPERTS].astype(BF16)
    col = pl.BlockSpec((SUBLANES, tc), lambda p, i: (0, i * p))
    return pl.pallas_call(
        _router_kernel,
        grid=(2, n_tokens // tc),
        in_specs=[pl.BlockSpec((LOGIT_ROWS, tc), lambda p, i: (0, i)),
                  _const_spec((LANES, LANES)), _const_spec((N_EXPERTS, N_EXPERTS))],
        out_specs=[col, col, _const_spec((N_EXPERTS, LANES))],
        out_shape=[jax.ShapeDtypeStruct((SUBLANES, n_tokens), jnp.int32),
                   jax.ShapeDtypeStruct((SUBLANES, n_tokens), F32),
                   jax.ShapeDtypeStruct((N_EXPERTS, LANES), jnp.int32)],
        scratch_shapes=[pltpu.VMEM((N_EXPERTS, LANES), F32), pltpu.VMEM((N_EXPERTS, LANES), F32)],
        compiler_params=pltpu.CompilerParams(
            dimension_semantics=("arbitrary", "arbitrary"), vmem_limit_bytes=VMEM_LIMIT),
        name="router",
    )(lg_t, upper, lower)


def _ffn_plan(counts, n_pairs):
    ids = jnp.arange(N_EXPERTS, dtype=jnp.int32)
    tiles_e = (counts + FFN_ROWS - 1) // FFN_ROWS
    tile_end = jnp.cumsum(tiles_e)
    tile_start = tile_end - tiles_e
    n_work = tile_end[-1]
    n_items = n_pairs // FFN_ROWS + N_EXPERTS
    t_w = jnp.arange(n_items, dtype=jnp.int32)
    w = t_w
    e_w = jnp.sum((tile_end[None, :] <= jnp.minimum(w, n_work - 1)[:, None]).astype(jnp.int32), axis=1)
    e_w = jnp.minimum(e_w, N_EXPERTS - 1)
    sel = e_w[:, None] == ids[None, :]
    pick = lambda a: jnp.sum(jnp.where(sel, a[None, :], 0), axis=1)
    valid_w = (w < n_work).astype(jnp.int32)
    pad_start = tile_start * FFN_ROWS + counts
    pad_len = tiles_e * FFN_ROWS - counts
    used = counts > 0
    order_e = jnp.cumsum(used.astype(jnp.int32)) - 1
    later_used = jnp.logical_and(used[None, :], ids[None, :] > ids[:, None])
    next_e = jnp.min(jnp.where(later_used, ids[None, :], N_EXPERTS), axis=1)
    next_e = jnp.where(next_e == N_EXPERTS, -1, next_e)
    i32 = lambda a: a.astype(jnp.int32)
    plan = (i32(e_w), valid_w, i32(pick(order_e)), i32(pick(next_e)))
    return plan, (i32(pad_start), i32(pad_len), i32(n_work)[None])


def _scatter_kernel(tiles_p, n_steps, n_ffn_tiles, pos1_s, pos2_s, pad0_s, padn_s, ntile_s,
                    h2p_hbm, h2s_hbm, xs_hbm, buf, zero_ref, lsems, ssems, zsem):
    i = pl.program_id(0)
    tile_rows = DISPATCH_ROWS * TILE_ROWS

    def tile_load(step, do):
        slot = step % D_SLOTS
        for src, first, mine in ((h2p_hbm, step, step < tiles_p),
                                 (h2s_hbm, step - tiles_p, step >= tiles_p)):
            @pl.when(mine)
            def _():
                start = pl.multiple_of(first * tile_rows, tile_rows)
                do(pltpu.make_async_copy(src.at[pl.ds(start, tile_rows), :], buf.at[slot],
                                         lsems.at[slot]))

    def scatter(step):
        slot = step % D_SLOTS
        base = step * DISPATCH_ROWS

        def row_copy(r, pos_s):
            p = pos_s[base + r]
            return pltpu.make_async_copy(buf.at[slot, _token_tile(r), :], xs_hbm.at[_token_tile(p), :],
                                         ssems.at[slot])

        def start(r, carry):
            row_copy(r, pos1_s).start(priority=0)
            row_copy(r, pos2_s).start(priority=1)
            return carry

        def wait(r, carry):
            row_copy(r, pos1_s).wait()
            row_copy(r, pos2_s).wait()
            return carry

        return start, wait

    @pl.when(i == 0)
    def _():
        for ahead in range(D_SLOTS - 1):
            tile_load(ahead, lambda cp: cp.start())

    tile_load(i, lambda cp: cp.wait())
    lax.fori_loop(0, DISPATCH_ROWS, scatter(i)[0], 0, unroll=32)

    @pl.when(i >= 1)
    def _():
        lax.fori_loop(0, DISPATCH_ROWS, scatter(i - 1)[1], 0, unroll=8)

    @pl.when(i + D_SLOTS - 1 < n_steps)
    def _():
        tile_load(i + D_SLOTS - 1, lambda cp: cp.start())

    @pl.when(i == n_steps - 1)
    def _():
        lax.fori_loop(0, DISPATCH_ROWS, scatter(i)[1], 0, unroll=8)

        zero_ref[...] = jnp.zeros_like(zero_ref)

        def zero_copy(first_row, n_rows):
            first = pl.multiple_of(first_row * TILE_ROWS, TILE_ROWS)
            return pltpu.make_async_copy(zero_ref.at[pl.ds(0, n_rows * TILE_ROWS), :],
                                         xs_hbm.at[pl.ds(first, n_rows * TILE_ROWS), :], zsem)

        def fill(action):
            def pad_rows(expert, carry):
                first, left = pad0_s[expert], padn_s[expert]
                run = FFN_ROWS // 2
                while run >= 1:
                    take = left & run

                    @pl.when(take != 0)
                    def _(first=first, run=run):
                        action(zero_copy(first, run))

                    first = first + take
                    run //= 2
                return carry

            def tail_tile(tile, carry):
                action(zero_copy(tile * FFN_ROWS, FFN_ROWS))
                return carry

            lax.fori_loop(0, N_EXPERTS, pad_rows, 0)
            lax.fori_loop(ntile_s[0], n_ffn_tiles, tail_tile, 0)

        fill(lambda cp: cp.start())
        fill(lambda cp: cp.wait())


def _scatter_call(pos1, pos2, pads, h2_p, h2_s, n_sorted_rows):
    tm = DISPATCH_ROWS
    tiles_p = h2_p.shape[0] // (tm * TILE_ROWS)
    tiles_s = h2_s.shape[0] // (tm * TILE_ROWS)
    n_steps = tiles_p + tiles_s
    assert n_steps >= D_SLOTS
    any_spec = pl.BlockSpec(memory_space=pl.ANY)
    return pl.pallas_call(
        functools.partial(_scatter_kernel, tiles_p, n_steps, n_sorted_rows // FFN_ROWS),
        grid_spec=pltpu.PrefetchScalarGridSpec(
            num_scalar_prefetch=5,
            grid=(n_steps,),
            in_specs=[any_spec, any_spec],
            out_specs=pl.BlockSpec(memory_space=pl.ANY),
            scratch_shapes=[pltpu.VMEM((D_SLOTS, tm * TILE_ROWS, LANES), F32),
                            pltpu.VMEM((FFN_ROWS * TILE_ROWS, LANES), F32),
                            pltpu.SemaphoreType.DMA((D_SLOTS,)),
                            pltpu.SemaphoreType.DMA((D_SLOTS,)),
                            pltpu.SemaphoreType.DMA(())],
        ),
        out_shape=jax.ShapeDtypeStruct((n_sorted_rows * TILE_ROWS, LANES), F32),
        compiler_params=pltpu.CompilerParams(
            dimension_semantics=("arbitrary",), vmem_limit_bytes=VMEM_LIMIT),
        name="dispatch_scatter",
    )(pos1, pos2, *pads, h2_p, h2_s)


def _ffn_kernel(n_steps, we_s, wvalid_s, word_s, wnext_s, x_hbm, w1_hbm, w3_hbm, w2_hbm,
                o_ref, xbuf, w1f_ref, w3f_ref, w2f_ref, w13b_ref, w2b_ref, xsems, sems):
    step = pl.program_id(0)
    step_rows = FFN_GROUP * FFN_ROWS * TILE_ROWS

    def x_copy(s):
        first = pl.multiple_of(s * step_rows, step_rows)
        return pltpu.make_async_copy(x_hbm.at[pl.ds(first, step_rows), :], xbuf.at[s % X_SLOTS],
                                     xsems.at[s % X_SLOTS])

    @pl.when(step == 0)
    def _():
        for ahead in range(X_SLOTS - 1):
            x_copy(ahead).start()

    @pl.when(step + X_SLOTS - 1 < n_steps)
    def _():
        x_copy(step + X_SLOTS - 1).start()

    def weight_copies(expert, buf):
        return [pltpu.make_async_copy(src.at[expert], dst.at[buf], sems.at[buf])
                for src, dst in ((w1_hbm, w1f_ref), (w3_hbm, w3f_ref), (w2_hbm, w2f_ref))]

    x_copy(step).wait()

    for sub in range(FFN_GROUP):
        w = step * FFN_GROUP + sub
        e = we_s[w]
        e_prev = we_s[jnp.maximum(w - 1, 0)]
        slot = word_s[w] % 2

        @pl.when(w == 0)
        def _():
            for cp in weight_copies(e, slot):
                cp.start()

        @pl.when(jnp.logical_or(w == 0, e != e_prev))
        def _():
            for cp in weight_copies(e, slot):
                cp.wait()
            w13b_ref[:, :D_EXPERT] = w1f_ref[slot].astype(BF16)
            w13b_ref[:, D_EXPERT:] = w3f_ref[slot].astype(BF16)
            w2b_ref[...] = w2f_ref[slot].astype(BF16)
            nxt = wnext_s[w]

            @pl.when(nxt >= 0)
            def _():
                for cp in weight_copies(nxt, 1 - slot):
                    cp.start()

        @pl.when(wvalid_s[w] == 1)
        def _():
            x = _load_token_tiles(xbuf, (step % X_SLOTS,), FFN_ROWS, sub * FFN_ROWS)
            h13 = _dot(x.astype(BF16), w13b_ref[...])
            a = h13[:, :D_EXPERT]
            b = h13[:, D_EXPERT:]
            hdn = (a * jax.nn.sigmoid(a)) * b
            _store_token_tiles(o_ref, _dot(hdn.astype(BF16), w2b_ref[...]), sub * FFN_ROWS)

        @pl.when(wvalid_s[w] == 0)
        def _():
            rows = pl.ds(sub * FFN_ROWS * TILE_ROWS, FFN_ROWS * TILE_ROWS)
            o_ref[rows, :] = jnp.zeros((FFN_ROWS * TILE_ROWS, LANES), F32)


def _ffn_call(plan, xs, w1, w3, w2):
    n_pairs = xs.shape[0] // TILE_ROWS
    tm = FFN_GROUP * FFN_ROWS
    any_spec = pl.BlockSpec(memory_space=pl.ANY)
    row = pl.BlockSpec((tm * TILE_ROWS, LANES), lambda i, *_: (i, 0))
    n_items = plan[0].shape[0]
    assert n_items % FFN_GROUP == 0 and n_items // FFN_GROUP >= X_SLOTS
    n_steps = n_items // FFN_GROUP
    return pl.pallas_call(
        functools.partial(_ffn_kernel, n_steps),
        grid_spec=pltpu.PrefetchScalarGridSpec(
            num_scalar_prefetch=len(plan),
            grid=(n_steps,),
            in_specs=[any_spec, any_spec, any_spec, any_spec],
            out_specs=row,
            scratch_shapes=[pltpu.VMEM((X_SLOTS, tm * TILE_ROWS, LANES), F32),
                            pltpu.VMEM((2, D_MODEL, D_EXPERT), F32),
                            pltpu.VMEM((2, D_MODEL, D_EXPERT), F32),
                            pltpu.VMEM((2, D_EXPERT, D_MODEL), F32),
                            pltpu.VMEM((D_MODEL, 2 * D_EXPERT), BF16),
                            pltpu.VMEM((D_EXPERT, D_MODEL), BF16),
                            pltpu.SemaphoreType.DMA((X_SLOTS,)),
                            pltpu.SemaphoreType.DMA((2,))],
        ),
        out_shape=jax.ShapeDtypeStruct((n_pairs * TILE_ROWS, LANES), F32),
        compiler_params=pltpu.CompilerParams(
            dimension_semantics=("arbitrary",), vmem_limit_bytes=VMEM_LIMIT),
        name="expert_ffn",
    )(*plan, xs, w1, w3, w2)


def _combine_kernel(three_d, row0, n_steps, pos1_s, pos2_s, x1_ref, wk_ref, mod_ref, g_ref, b_ref,
                    ys_hbm, o_ref, ybuf, sems):
    step = pl.program_id(0)
    slot = step % 2

    def gather(tile, tile_slot):
        base = row0 + tile * COMBINE_ROWS

        def row_copy(r, k, pos_s):
            p = pos_s[base + r]
            return pltpu.make_async_copy(ys_hbm.at[_token_tile(p), :],
                                         ybuf.at[tile_slot, k, _token_tile(r), :], sems.at[tile_slot])

        def start(r, carry):
            row_copy(r, 0, pos1_s).start(priority=0)
            row_copy(r, 1, pos2_s).start(priority=1)
            return carry

        def wait(r, carry):
            row_copy(r, 0, pos1_s).wait()
            row_copy(r, 1, pos2_s).wait()
            return carry

        return start, wait

    @pl.when(step == 0)
    def _():
        lax.fori_loop(0, COMBINE_ROWS, gather(step, slot)[0], 0, unroll=32)

    @pl.when(step + 1 < n_steps)
    def _():
        lax.fori_loop(0, COMBINE_ROWS, gather(step + 1, 1 - slot)[0], 0, unroll=32)

    lax.fori_loop(0, COMBINE_ROWS, gather(step, slot)[1], 0, unroll=8)

    wk = wk_ref[...]
    r = lax.broadcasted_iota(jnp.int32, (COMBINE_ROWS, COMBINE_ROWS), 0)
    c = lax.broadcasted_iota(jnp.int32, (COMBINE_ROWS, COMBINE_ROWS), 1)
    wk_col = [jnp.sum(jnp.where(r == c, wk[k:k + 1, :], 0.0), axis=1, keepdims=True)
              for k in range(TOP_K)]
    moe = (wk_col[0] * _load_token_tiles(ybuf, (slot, 0), COMBINE_ROWS)
           + wk_col[1] * _load_token_tiles(ybuf, (slot, 1), COMBINE_ROWS))
    x1 = x1_ref[...]
    if three_d:
        g2 = mod_ref[0, :, 5 * D_MODEL:]
        o_ref[0] = _layer_norm(ALPHA * x1 + (1.0 + g2) * moe, g_ref[...], b_ref[...])
    else:
        def seq_rows(v, b):
            return v[b * SUBLANES:(b + 1) * SUBLANES]

        resid = jnp.concatenate([
            ALPHA * seq_rows(x1, b) + (1.0 + mod_ref[b:b + 1, 5 * D_MODEL:]) * seq_rows(moe, b)
            for b in range(COMBINE_ROWS // SUBLANES)], axis=0)
        o_ref[...] = _layer_norm(resid, g_ref[...], b_ref[...])


def _combine_call(three_d, pos1, pos2, x1_all, mf_all, mod, ln2_g, ln2_b, ys, row0, batch, seq):
    tm = COMBINE_ROWS
    blk0 = row0 // tm
    if three_d:
        tiles = seq // tm
        grid = (batch * tiles,)
        mod_spec = pl.BlockSpec((1, 1, 6 * D_MODEL), lambda i, *_: (i // tiles, 0, 0))
        out_spec = pl.BlockSpec((1, tm, D_MODEL), lambda i, *_: (i // tiles, i % tiles, 0))
        out_shape = jax.ShapeDtypeStruct((batch, seq, D_MODEL), F32)
    else:
        grid = (batch * seq // tm,)
        mod_spec = pl.BlockSpec((tm // SUBLANES, 6 * D_MODEL), lambda i, *_: (i, 0))
        out_spec = pl.BlockSpec((tm, D_MODEL), lambda i, *_: (i, 0))
        out_shape = jax.ShapeDtypeStruct((batch * seq, D_MODEL), F32)
    return pl.pallas_call(
        functools.partial(_combine_kernel, three_d, row0, grid[0]),
        grid_spec=pltpu.PrefetchScalarGridSpec(
            num_scalar_prefetch=2,
            grid=grid,
            in_specs=[pl.BlockSpec((tm, D_MODEL), lambda i, *_: (i, 0)),
                      pl.BlockSpec((SUBLANES, tm), lambda i, *_: (0, blk0 + i)),
                      mod_spec,
                      pl.BlockSpec(ln2_g.shape, lambda i, *_: (0, 0)),
                      pl.BlockSpec(ln2_b.shape, lambda i, *_: (0, 0)),
                      pl.BlockSpec(memory_space=pl.ANY)],
            out_specs=out_spec,
            scratch_shapes=[pltpu.VMEM((2, TOP_K, tm * TILE_ROWS, LANES), F32),
                            pltpu.SemaphoreType.DMA((2,))],
        ),
        out_shape=out_shape,
        compiler_params=pltpu.CompilerParams(
            dimension_semantics=("arbitrary",), vmem_limit_bytes=VMEM_LIMIT),
        name="combine_prompt" if three_d else "combine_sample",
    )(pos1, pos2, x1_all, mf_all, mod, ln2_g, ln2_b, ys)


def _block_diag(w):
    heads, n, _ = w.shape
    eye = jnp.eye(heads, dtype=w.dtype)
    return (eye[:, None, :, None] * w[:, :, None, :]).reshape(heads * n, heads * n)


def kernel(x_prompt, x_sample, state_rglru_h, state_conv, c_prompt, c_sample, w_ada, b_ada, w_in, w_s, b_s, lnv_g, lnv_b, conv_w, conv_b, lru_wa, lru_ba, lru_wx, lru_bx, lru_lam, w_out, ln1_g, ln1_b, w_rg, b_rg, w_re, b_re, w1, w3, w2, ln2_g, ln2_b):
    batch, seq, _ = x_prompt.shape
    dec_batch, dec_seq, _ = x_sample.shape
    assert dec_seq == SUBLANES and seq % ROWS == 0 and (dec_batch * dec_seq) % ROWS == 0
    n_prompt = batch * seq
    n_sample = dec_batch * dec_seq
    n_tokens = n_prompt + n_sample
    n_pairs = TOP_K * n_tokens
    assert n_tokens % ROUTER_COLS == 0 and n_pairs % FFN_ROWS == 0
    l = 0

    mod_p, mod_s = _ada_call(c_prompt, c_sample, w_ada[l], b_ada[l][None])
    mod_p3 = mod_p[:, None, :]

    tri = jnp.tril(jnp.ones((CHUNK, CHUNK), dtype=bool))
    ws_tri = jnp.where(tri[None], w_s[l], 0.0)
    reps_p = MXU_DIM // CHUNK
    eye_p = jnp.eye(reps_p, dtype=F32)
    wsm_p = (eye_p[None, :, None, :, None] * ws_tri[:, None, :, None, :]).reshape(
        A_HEADS, MXU_DIM, MXU_DIM).astype(BF16)
    reps_s = MXU_DIM // dec_seq
    pos_in_seq = jnp.arange(MXU_DIM, dtype=jnp.int32) % dec_seq
    seq_of_row = jnp.arange(MXU_DIM, dtype=jnp.int32) // dec_seq
    spread = (pos_in_seq[:, None] == jnp.arange(dec_seq, dtype=jnp.int32)[None, :]).astype(F32)
    same_seq = seq_of_row[:, None] == seq_of_row[None, :]
    wsm_s = jnp.einsum("ri,hij,cj->hrc", spread, ws_tri[:, :dec_seq, :dec_seq], spread,
                       precision=lax.Precision.HIGHEST)
    wsm_s = jnp.where(same_seq[None], wsm_s, 0.0).astype(BF16)
    bs_pos = jnp.repeat(jnp.transpose(b_s[l]), A_HEAD_DIM, axis=1)
    bsf_p = jnp.tile(bs_pos, (reps_p, 1))
    bsf_s = jnp.tile(bs_pos[:dec_seq], (reps_s, 1))

    wa_bd = _block_diag(lru_wa[l])
    wx_bd = _block_diag(lru_wx[l])
    wg = jnp.stack([
        jnp.concatenate([wa_bd[q * MXU_DIM:(q + 1) * MXU_DIM, q * MXU_DIM:(q + 1) * MXU_DIM],
                         wx_bd[q * MXU_DIM:(q + 1) * MXU_DIM, q * MXU_DIM:(q + 1) * MXU_DIM]], axis=1)
        for q in range(D_B // MXU_DIM)]).astype(BF16)
    bg = jnp.concatenate([lru_ba[l].reshape(1, D_B), lru_bx[l].reshape(1, D_B)], axis=1)

    group_pad = jnp.zeros((SUBLANES - N_GROUPS, D_MODEL), F32)
    wr = jnp.concatenate([jnp.transpose(w_rg[l]), group_pad, jnp.transpose(w_re[l])], axis=0)
    wr_hi = wr.astype(BF16)
    wr_lo = jnp.concatenate([wr_hi, (wr - wr_hi.astype(F32)).astype(BF16)], axis=0)
    br = jnp.concatenate([b_rg[l], jnp.zeros((SUBLANES - N_GROUPS,), F32), b_re[l]])[:, None]

    common = dict(lnv_g=lnv_g[l][None], lnv_b=lnv_b[l][None], conv_w=conv_w[l], conv_b=conv_b[l][None],
                  wg=wg, bg=bg, lam=lru_lam[l][None], w_out=w_out[l].astype(BF16),
                  ln1_g=ln1_g[l][None], ln1_b=ln1_b[l][None], wr_hi=wr_hi, wr_lo=wr_lo, br=br)
    w_in_b = w_in[l].astype(BF16)
    weights_p = _mixer_weights(w_in_b, wsm_p, bsf_p, **common)
    weights_s = _mixer_weights(w_in_b, wsm_s, bsf_s, **common)

    x1_p, h2_p, lg_p, cst_p, hst_p = _mixer_prompt(x_prompt, mod_p3, weights_p)

    cpad = jnp.concatenate(
        [jnp.zeros((dec_batch, SUBLANES - (CONV_W - 1), D_B), F32), state_conv[l]], axis=1
    ).reshape(n_sample, D_B)
    x1_s, h2_s, lg_s, xr_s, v_s, hs_s = _mixer_sample(
        x_sample.reshape(n_sample, D_MODEL), mod_s, weights_s, state_rglru_h[l], cpad)

    meta_i, meta_f, counts = _router_call(jnp.concatenate([lg_p, lg_s], axis=1))
    pos1, pos2 = meta_i[0], meta_i[1]
    plan, pads = _ffn_plan(counts[:, 0], n_pairs)
    xs = _scatter_call(pos1, pos2, pads, h2_p, h2_s, plan[0].shape[0] * FFN_ROWS)
    ys = _ffn_call(plan, xs, w1[l], w3[l], w2[l])

    ln2g, ln2b = ln2_g[l][None], ln2_b[l][None]
    y_prompt = _combine_call(True, pos1, pos2, x1_p, meta_f, mod_p3, ln2g, ln2b, ys,
                             0, batch, seq)
    y_sample = _combine_call(False, pos1, pos2, x1_s, meta_f, mod_s, ln2g, ln2b, ys,
                             n_prompt, dec_batch, dec_seq).reshape(dec_batch, dec_seq, D_MODEL)

    new_h_prompt = hst_p[:, SUBLANES - 1, :][None]
    new_conv_prompt = cst_p[:, SUBLANES - (CONV_W - 1):, :][None]
    xr_s3 = xr_s.reshape(dec_batch, dec_seq, D_B)
    new_h_sample = hs_s.reshape(dec_batch, dec_seq, D_B)[:, dec_seq - 1, :][None]
    new_conv_sample = xr_s3[:, dec_seq - (CONV_W - 1):, :][None]
    new_chunk_v_sample = v_s.reshape(dec_batch, dec_seq, D_A)[None]
    return (y_prompt, y_sample, new_h_prompt, new_conv_prompt, new_h_sample, new_conv_sample,
            new_chunk_v_sample)
```

```python
import functools
import math

import jax
import jax.numpy as jnp
from jax import lax
from jax.experimental import pallas as pl
from jax.experimental.pallas import tpu as pltpu

D_MODEL = 1024
D_A = 512
A_HEADS = 4
A_HEAD_DIM = 128
CHUNK = 128
D_B = 512
B_HEADS = 8
B_HEAD_DIM = 64
CONV_W = 4
LRU_C = 8.0
D_IN = 2 * D_A + 2 * D_B
N_GROUPS = 4
EXPERTS_PER_GROUP = 8
N_EXPERTS = 32
TOP_K = 2
D_EXPERT = 512
LN_EPS = 1e-5
DEPTH = 1
ALPHA = (2.0 * DEPTH) ** 0.25

LANES = 128
SUBLANES = 8
MXU_DIM = 256
VMEM_LIMIT = 56 * 1024 * 1024

ROWS = 512
ROUTER_COLS = 1024
LOGIT_ROWS = SUBLANES + N_EXPERTS
DISPATCH_ROWS = 1024
D_SLOTS = 3
FFN_ROWS = 256
FFN_GROUP = 4
X_SLOTS = 3
COMBINE_ROWS = 256
BF16 = jnp.bfloat16
F32 = jnp.float32


def _dot(a, b):
    return jnp.dot(a, b, preferred_element_type=F32)


def _dot_nt(a, b):
    return lax.dot_general(a, b, (((1,), (1,)), ((), ())), preferred_element_type=F32)


TILE_ROWS = D_MODEL // LANES


def _token_tile(r):
    return pl.ds(pl.multiple_of(r * TILE_ROWS, TILE_ROWS), TILE_ROWS)


def _load_token_tiles(ref, lead, n_rows, first_token=0):
    base = first_token * TILE_ROWS
    parts = [ref[lead + (pl.ds(base + s, n_rows, stride=TILE_ROWS), slice(None))]
             for s in range(TILE_ROWS)]
    return jnp.concatenate(parts, axis=1)


def _store_token_tiles(ref, val, first_token=0):
    n_rows = val.shape[0]
    base = first_token * TILE_ROWS
    for s in range(TILE_ROWS):
        ref[pl.ds(base + s, n_rows, stride=TILE_ROWS), :] = val[:, s * LANES:(s + 1) * LANES]


def _layer_norm(x, g, b):
    mu = jnp.mean(x, axis=-1, keepdims=True)
    xc = x - mu
    var = jnp.mean(xc * xc, axis=-1, keepdims=True)
    return xc * lax.rsqrt(var + LN_EPS) * g + b


def _gelu_tanh(x):
    c = math.sqrt(2.0 / math.pi)
    return x * (0.5 * (1.0 + jnp.tanh(c * (x + 0.044715 * (x * x * x)))))


def _const_spec(shape):
    nd = len(shape)
    return pl.BlockSpec(shape, lambda *_: (0,) * nd)


def _ada_kernel(cp_ref, cs_ref, w_ref, b_ref, op_ref, os_ref):
    w = w_ref[...].astype(BF16)
    for c_ref, o_ref in ((cp_ref, op_ref), (cs_ref, os_ref)):
        c = c_ref[...]
        a = (c * jax.nn.sigmoid(c)).astype(BF16)
        o_ref[...] = _dot(a, w) + b_ref[...]


def _ada_call(c_prompt, c_sample, w_ada, b_ada):
    tn = 1536
    rows_p, rows_s = c_prompt.shape[0], c_sample.shape[0]
    return pl.pallas_call(
        _ada_kernel,
        grid=(6 * D_MODEL // tn,),
        in_specs=[
            pl.BlockSpec((rows_p, D_MODEL), lambda j: (0, 0)),
            pl.BlockSpec((rows_s, D_MODEL), lambda j: (0, 0)),
            pl.BlockSpec((D_MODEL, tn), lambda j: (0, j)),
            pl.BlockSpec((1, tn), lambda j: (0, j)),
        ],
        out_specs=[pl.BlockSpec((rows_p, tn), lambda j: (0, j)),
                   pl.BlockSpec((rows_s, tn), lambda j: (0, j))],
        out_shape=[jax.ShapeDtypeStruct((rows_p, 6 * D_MODEL), F32),
                   jax.ShapeDtypeStruct((rows_s, 6 * D_MODEL), F32)],
        compiler_params=pltpu.CompilerParams(
            dimension_semantics=("arbitrary",), vmem_limit_bytes=VMEM_LIMIT),
        name="ada_mod",
    )(c_prompt, c_sample, w_ada, b_ada)


def _mixer_kernel(sample, *refs):
    if sample:
        (x_ref, mod_ref, win_ref, wsm_ref, bsf_ref, lnvg_ref, lnvb_ref, convw_ref, convb_ref,
         wg_ref, bg_ref, lam_ref, wout_ref, ln1g_ref, ln1b_ref, wrh_ref, wrl_ref, br_ref,
         h0_ref, cpad_ref,
         x1_ref, h2_ref, lg_ref, xr_ref, v_ref, hs_ref) = refs
    else:
        (x_ref, mod_ref, win_ref, wsm_ref, bsf_ref, lnvg_ref, lnvb_ref, convw_ref, convb_ref,
         wg_ref, bg_ref, lam_ref, wout_ref, ln1g_ref, ln1b_ref, wrh_ref, wrl_ref, br_ref,
         x1_ref, h2_ref, lg_ref, cst_ref, hst_ref, prev_ref, hcar_ref) = refs

    rows = ROWS
    groups = rows // SUBLANES

    if sample:
        x = x_ref[...]

        def chunk(c):
            return lambda b: mod_ref[b:b + 1, c * D_MODEL:(c + 1) * D_MODEL]
    else:
        x = x_ref[0]

        def chunk(c):
            return lambda b: mod_ref[0, :, c * D_MODEL:(c + 1) * D_MODEL]

        @pl.when(pl.program_id(1) == 0)
        def _():
            prev_ref[...] = jnp.zeros_like(prev_ref)
            hcar_ref[...] = jnp.zeros_like(hcar_ref)

    def modulated(fn, *vals):
        if not sample:
            return fn(0, *vals)
        parts = [fn(b, *(v[b * SUBLANES:(b + 1) * SUBLANES] for v in vals)) for b in range(groups)]
        return jnp.concatenate(parts, axis=0)

    sh1, sc1, g1, sh2, sc2, g2 = [chunk(c) for c in range(6)]

    h = modulated(lambda b, xv: xv * (1.0 + sc1(b)) + sh1(b), x)
    proj = _dot(h.astype(BF16), win_ref[...])
    u = proj[:, :D_A]
    v = _layer_norm(proj[:, D_A:2 * D_A], lnvg_ref[...], lnvb_ref[...])
    xr = proj[:, 2 * D_A:2 * D_A + D_B]
    yr = proj[:, 2 * D_A + D_B:]

    vb = v.astype(BF16)
    halves = []
    for r in range(rows // MXU_DIM):
        heads = [
            _dot(wsm_ref[hd], vb[r * MXU_DIM:(r + 1) * MXU_DIM, hd * A_HEAD_DIM:(hd + 1) * A_HEAD_DIM])
            for hd in range(A_HEADS)
        ]
        halves.append(jnp.concatenate(heads, axis=1) + bsf_ref[...])
    mixed = halves[0] if len(halves) == 1 else jnp.concatenate(halves, axis=0)
    out_a = u * mixed

    xr3 = xr.reshape(groups, SUBLANES, D_B)
    t3 = lax.broadcasted_iota(jnp.int32, (1, SUBLANES, D_B), 1)
    cw = convw_ref[...]
    if sample:
        cpad3 = cpad_ref[...].reshape(groups, SUBLANES, D_B)
    else:
        prev8 = prev_ref[...]
    xc3 = convb_ref[...][None]
    for k in range(CONV_W):
        s = CONV_W - 1 - k
        if s == 0:
            shifted = xr3
        else:
            rot = pltpu.roll(xr3, s, 1)
            if sample:
                rot_prev = pltpu.roll(cpad3, s, 1)
            else:
                first = pltpu.roll(prev8, s, 0)[None]
                rot_prev = jnp.concatenate([first, rot[:-1]], axis=0) if groups > 1 else first
            shifted = jnp.where(t3 >= s, rot, rot_prev)
        xc3 = xc3 + shifted * cw[k:k + 1, :][None]
    xc = xc3.reshape(rows, D_B)

    xcb = xc.astype(BF16)
    gq = [_dot(xcb[:, q * MXU_DIM:(q + 1) * MXU_DIM], wg_ref[q]) for q in range(D_B // MXU_DIM)]
    bg = bg_ref[...]
    pre_a = jnp.concatenate([g[:, :MXU_DIM] for g in gq], axis=1) + bg[:, :D_B]
    pre_x = jnp.concatenate([g[:, MXU_DIM:] for g in gq], axis=1) + bg[:, D_B:]
    gate_a = jax.nn.sigmoid(pre_a)
    gate_x = jax.nn.sigmoid(pre_x)
    nl = -lam_ref[...]
    softplus = jnp.maximum(nl, 0.0) + jnp.log1p(jnp.exp(-jnp.abs(nl)))
    log_a = -LRU_C * gate_a * softplus
    a = jnp.exp(log_a)
    mult = jnp.sqrt(-jnp.tanh(log_a) * (1.0 + a * a))
    bx = mult * (gate_x * xc)

    a3 = a.reshape(groups, SUBLANES, D_B)
    b3 = bx.reshape(groups, SUBLANES, D_B)
    for d in (1, 2, 4):
        m = t3 >= d
        a_sh = jnp.where(m, pltpu.roll(a3, d, 1), 1.0)
        b_sh = jnp.where(m, pltpu.roll(b3, d, 1), 0.0)
        b3 = a3 * b_sh + b3
        a3 = a3 * a_sh
    if sample:
        hs3 = jnp.stack([a3[g] * h0_ref[g:g + 1, :] + b3[g] for g in range(groups)], axis=0)
    else:
        h_prev = hcar_ref[...][SUBLANES - 1:SUBLANES, :]
        hs_list = []
        for g in range(groups):
            hg = a3[g] * h_prev + b3[g]
            hs_list.append(hg)
            h_prev = hg[SUBLANES - 1:SUBLANES, :]
        hs3 = jnp.stack(hs_list, axis=0)
    hs = hs3.reshape(rows, D_B)

    out_b = hs * _gelu_tanh(yr)
    mix = _dot(out_a.astype(BF16), wout_ref[:D_A, :]) + _dot(out_b.astype(BF16), wout_ref[D_A:, :])
    y1 = modulated(lambda b, xv, mv: ALPHA * xv + (1.0 + g1(b)) * mv, x, mix)
    x1 = _layer_norm(y1, ln1g_ref[...], ln1b_ref[...])
    h2 = modulated(lambda b, xv: xv * (1.0 + sc2(b)) + sh2(b), x1)

    h2_hi = h2.astype(BF16)
    h2_lo = (h2 - h2_hi.astype(F32)).astype(BF16)
    both = _dot_nt(wrl_ref[...], h2_hi)
    logits = (both[:LOGIT_ROWS] + _dot_nt(wrh_ref[...], h2_lo) + both[LOGIT_ROWS:] + br_ref[...])

    x1_ref[...] = x1
    _store_token_tiles(h2_ref, h2)
    lg_ref[...] = logits
    if sample:
        xr_ref[...] = xr
        v_ref[...] = v
        hs_ref[...] = hs
    else:
        last_x = xr3[groups - 1]
        last_h = hs3[groups - 1]
        prev_ref[...] = last_x
        hcar_ref[...] = last_h
        cst_ref[0] = last_x
        hst_ref[0] = last_h


def _mixer_weights(w_in, wsm, bsf, lnv_g, lnv_b, conv_w, conv_b, wg, bg, lam, w_out, ln1_g, ln1_b,
                   wr_hi, wr_lo, br):
    arrs = [w_in, wsm, bsf, lnv_g, lnv_b, conv_w, conv_b, wg, bg, lam, w_out, ln1_g, ln1_b,
            wr_hi, wr_lo, br]
    return arrs, [_const_spec(a.shape) for a in arrs]


def _mixer_prompt(x, mod3, weights):
    batch, seq, _ = x.shape
    n_tokens = batch * seq
    tiles = seq // ROWS
    arrs, specs = weights
    row_spec = lambda w: pl.BlockSpec((ROWS, w), lambda b, j: (b * tiles + j, 0))
    st_spec = pl.BlockSpec((1, SUBLANES, D_B), lambda b, j: (b, 0, 0))
    return pl.pallas_call(
        functools.partial(_mixer_kernel, False),
        grid=(batch, tiles),
        in_specs=[pl.BlockSpec((1, ROWS, D_MODEL), lambda b, j: (b, j, 0)),
                  pl.BlockSpec((1, 1, 6 * D_MODEL), lambda b, j: (b, 0, 0))] + specs,
        out_specs=[row_spec(D_MODEL),
                   pl.BlockSpec((ROWS * TILE_ROWS, LANES), lambda b, j: (b * tiles + j, 0)),
                   pl.BlockSpec((LOGIT_ROWS, ROWS), lambda b, j: (0, b * tiles + j)), st_spec, st_spec],
        out_shape=[jax.ShapeDtypeStruct((n_tokens, D_MODEL), F32),
                   jax.ShapeDtypeStruct((n_tokens * TILE_ROWS, LANES), F32),
                   jax.ShapeDtypeStruct((LOGIT_ROWS, n_tokens), F32),
                   jax.ShapeDtypeStruct((batch, SUBLANES, D_B), F32),
                   jax.ShapeDtypeStruct((batch, SUBLANES, D_B), F32)],
        scratch_shapes=[pltpu.VMEM((SUBLANES, D_B), F32), pltpu.VMEM((SUBLANES, D_B), F32)],
        compiler_params=pltpu.CompilerParams(
            dimension_semantics=("arbitrary", "arbitrary"), vmem_limit_bytes=VMEM_LIMIT),
        name="mixer_prompt",
    )(x, mod3, *arrs)


def _mixer_sample(x2, mod_seq, weights, h0, cpad):
    n_rows = x2.shape[0]
    arrs, specs = weights
    row = lambda w: pl.BlockSpec((ROWS, w), lambda i: (i, 0))
    seq_row = lambda w: pl.BlockSpec((ROWS // SUBLANES, w), lambda i: (i, 0))
    return pl.pallas_call(
        functools.partial(_mixer_kernel, True),
        grid=(n_rows // ROWS,),
        in_specs=[row(D_MODEL), seq_row(6 * D_MODEL)] + specs + [seq_row(D_B), row(D_B)],
        out_specs=[row(D_MODEL), pl.BlockSpec((ROWS * TILE_ROWS, LANES), lambda i: (i, 0)),
                   pl.BlockSpec((LOGIT_ROWS, ROWS), lambda i: (0, i)),
                   row(D_B), row(D_B), row(D_B)],
        out_shape=[jax.ShapeDtypeStruct((n_rows, D_MODEL), F32),
                   jax.ShapeDtypeStruct((n_rows * TILE_ROWS, LANES), F32),
                   jax.ShapeDtypeStruct((LOGIT_ROWS, n_rows), F32),
                   jax.ShapeDtypeStruct((n_rows, D_B), F32),
                   jax.ShapeDtypeStruct((n_rows, D_B), F32),
                   jax.ShapeDtypeStruct((n_rows, D_B), F32)],
        compiler_params=pltpu.CompilerParams(
            dimension_semantics=("arbitrary",), vmem_limit_bytes=VMEM_LIMIT),
        name="mixer_sample",
    )(x2, mod_seq, *arrs, h0, cpad)


def _router_kernel(lg_ref, upper_ref, lower_ref, mi_ref, mf_ref, cnt_ref, tot_ref, run_ref):
    phase = pl.program_id(0)
    step = pl.program_id(1)
    l = lg_ref[...]
    cols = l.shape[1]
    row8 = lax.broadcasted_iota(jnp.int32, (SUBLANES, cols), 0)
    neg = -jnp.inf
    big = jnp.int32(SUBLANES)

    def first_max(vals):
        top = jnp.max(vals, axis=0, keepdims=True)
        return top, jnp.min(jnp.where(vals == top, row8, big), axis=0, keepdims=True)

    is_group = row8 < N_GROUPS
    lg = l[:SUBLANES]
    m, gsel = first_max(jnp.where(is_group, lg, neg))
    denom = jnp.sum(jnp.where(is_group, jnp.exp(lg - m), 0.0), axis=0, keepdims=True)
    pgsel = 1.0 / denom
    le = l[SUBLANES:2 * SUBLANES]
    for g in range(1, N_GROUPS):
        le = jnp.where(gsel == g, l[(g + 1) * SUBLANES:(g + 2) * SUBLANES], le)
    v1, i1 = first_max(le)
    v2, i2 = first_max(jnp.where(row8 == i1, neg, le))
    ex = jnp.exp(v2 - v1)
    wk1 = pgsel / (1.0 + ex)
    wk2 = pgsel * ex / (1.0 + ex)
    e1 = gsel * EXPERTS_PER_GROUP + i1
    e2 = gsel * EXPERTS_PER_GROUP + i2
    row_e = lax.broadcasted_iota(jnp.int32, (N_EXPERTS, cols), 0)
    oh1 = jnp.where(row_e == e1, 1.0, 0.0)
    oh2 = jnp.where(row_e == e2, 1.0, 0.0)
    oh = oh1 + oh2

    @pl.when(jnp.logical_and(phase == 0, step == 0))
    def _():
        tot_ref[...] = jnp.zeros_like(tot_ref)

    @pl.when(phase == 0)
    def _():
        tot_ref[...] = tot_ref[...] + jnp.sum(oh, axis=1, keepdims=True)

    @pl.when(jnp.logical_and(phase == 1, step == 0))
    def _():
        tot = tot_ref[...]
        tiles = ((tot + (FFN_ROWS - 1.0)) * (1.0 / FFN_ROWS)).astype(jnp.int32).astype(F32)
        run_ref[...] = _dot(lower_ref[...], tiles.astype(BF16)) * FFN_ROWS

    @pl.when(phase == 1)
    def _():
        blocks = cols // LANES
        stacked = jnp.concatenate([oh[:, b * LANES:(b + 1) * LANES] for b in range(blocks)], axis=0)
        local = _dot(stacked.astype(BF16), upper_ref[...])
        run = run_ref[...]
        p1, p2 = [], []
        for b in range(blocks):
            sl = slice(b * LANES, (b + 1) * LANES)
            before = local[b * N_EXPERTS:(b + 1) * N_EXPERTS] + run
            p1.append(jnp.sum(oh1[:, sl] * before, axis=0, keepdims=True))
            p2.append(jnp.sum(oh2[:, sl] * before, axis=0, keepdims=True))
            run = run + jnp.sum(oh[:, sl], axis=1, keepdims=True)
        run_ref[...] = run
        pos1 = jnp.concatenate(p1, axis=1).astype(jnp.int32)
        pos2 = jnp.concatenate(p2, axis=1).astype(jnp.int32)
        zero_i = jnp.zeros_like(row8)
        mi_ref[...] = jnp.where(row8 == 0, pos1, jnp.where(row8 == 1, pos2, jnp.where(
            row8 == 2, e1, jnp.where(row8 == 3, e2, zero_i))))
        mf_ref[...] = jnp.where(row8 == 0, wk1, jnp.where(row8 == 1, wk2, 0.0))
        cnt_ref[...] = tot_ref[...].astype(jnp.int32)


def _router_call(lg_t):
    n_tokens = lg_t.shape[1]
    tc = ROUTER_COLS
    r = lax.broadcasted_iota(jnp.int32, (LANES, LANES), 0)
    c = lax.broadcasted_iota(jnp.int32, (LANES, LANES), 1)
    upper = jnp.where(r < c, 1.0, 0.0).astype(BF16)
    lower = jnp.where(c < r, 1.0, 0.0)[:N_EXPERTS, :N_EXPERTS].astype(BF16)
    col = pl.BlockSpec((SUBLANES, tc), lambda p, i: (0, i * p))
    return pl.pallas_call(
        _router_kernel,
        grid=(2, n_tokens // tc),
        in_specs=[pl.BlockSpec((LOGIT_ROWS, tc), lambda p, i: (0, i)),
                  _const_spec((LANES, LANES)), _const_spec((N_EXPERTS, N_EXPERTS))],
        out_specs=[col, col, _const_spec((N_EXPERTS, LANES))],
        out_shape=[jax.ShapeDtypeStruct((SUBLANES, n_tokens), jnp.int32),
                   jax.ShapeDtypeStruct((SUBLANES, n_tokens), F32),
                   jax.ShapeDtypeStruct((N_EXPERTS, LANES), jnp.int32)],
        scratch_shapes=[pltpu.VMEM((N_EXPERTS, LANES), F32), pltpu.VMEM((N_EXPERTS, LANES), F32)],
        compiler_params=pltpu.CompilerParams(
            dimension_semantics=("arbitrary", "arbitrary"), vmem_limit_bytes=VMEM_LIMIT),
        name="router",
    )(lg_t, upper, lower)


def _ffn_plan(counts, n_pairs):
    ids = jnp.arange(N_EXPERTS, dtype=jnp.int32)
    tiles_e = (counts + FFN_ROWS - 1) // FFN_ROWS
    tile_end = jnp.cumsum(tiles_e)
    tile_start = tile_end - tiles_e
    n_work = tile_end[-1]
    n_items = n_pairs // FFN_ROWS + N_EXPERTS
    t_w = jnp.arange(n_items, dtype=jnp.int32)
    w = t_w
    e_w = jnp.sum((tile_end[None, :] <= jnp.minimum(w, n_work - 1)[:, None]).astype(jnp.int32), axis=1)
    e_w = jnp.minimum(e_w, N_EXPERTS - 1)
    sel = e_w[:, None] == ids[None, :]
    pick = lambda a: jnp.sum(jnp.where(sel, a[None, :], 0), axis=1)
    valid_w = (w < n_work).astype(jnp.int32)
    pad_start = tile_start * FFN_ROWS + counts
    pad_len = tiles_e * FFN_ROWS - counts
    used = counts > 0
    order_e = jnp.cumsum(used.astype(jnp.int32)) - 1
    later_used = jnp.logical_and(used[None, :], ids[None, :] > ids[:, None])
    next_e = jnp.min(jnp.where(later_used, ids[None, :], N_EXPERTS), axis=1)
    next_e = jnp.where(next_e == N_EXPERTS, -1, next_e)
    i32 = lambda a: a.astype(jnp.int32)
    plan = (i32(e_w), valid_w, i32(pick(order_e)), i32(pick(next_e)))
    return plan, (i32(pad_start), i32(pad_len), i32(n_work)[None])


def _scatter_kernel(tiles_p, n_steps, n_ffn_tiles, pos1_s, pos2_s, pad0_s, padn_s, ntile_s,
                    h2p_hbm, h2s_hbm, xs_hbm, buf, zero_ref, lsems, ssems, zsem):
    i = pl.program_id(0)
    tile_rows = DISPATCH_ROWS * TILE_ROWS

    def tile_load(step, do):
        slot = step % D_SLOTS
        for src, first, mine in ((h2p_hbm, step, step < tiles_p),
                                 (h2s_hbm, step - tiles_p, step >= tiles_p)):
            @pl.when(mine)
            def _():
                start = pl.multiple_of(first * tile_rows, tile_rows)
                do(pltpu.make_async_copy(src.at[pl.ds(start, tile_rows), :], buf.at[slot],
                                         lsems.at[slot]))

    def scatter(step):
        slot = step % D_SLOTS
        base = step * DISPATCH_ROWS

        def row_copy(r, pos_s):
            p = pos_s[base + r]
            return pltpu.make_async_copy(buf.at[slot, _token_tile(r), :], xs_hbm.at[_token_tile(p), :],
                                         ssems.at[slot])

        def start(r, carry):
            row_copy(r, pos1_s).start(priority=0)
            row_copy(r, pos2_s).start(priority=1)
            return carry

        def wait(r, carry):
            row_copy(r, pos1_s).wait()
            row_copy(r, pos2_s).wait()
            return carry

        return start, wait

    @pl.when(i == 0)
    def _():
        for ahead in range(D_SLOTS - 1):
            tile_load(ahead, lambda cp: cp.start())

    tile_load(i, lambda cp: cp.wait())
    lax.fori_loop(0, DISPATCH_ROWS, scatter(i)[0], 0, unroll=32)

    @pl.when(i >= 1)
    def _():
        lax.fori_loop(0, DISPATCH_ROWS, scatter(i - 1)[1], 0, unroll=8)

    @pl.when(i + D_SLOTS - 1 < n_steps)
    def _():
        tile_load(i + D_SLOTS - 1, lambda cp: cp.start())

    @pl.when(i == n_steps - 1)
    def _():
        lax.fori_loop(0, DISPATCH_ROWS, scatter(i)[1], 0, unroll=8)

        zero_ref[...] = jnp.zeros_like(zero_ref)

        def zero_copy(first_row, n_rows):
            first = pl.multiple_of(first_row * TILE_ROWS, TILE_ROWS)
            return pltpu.make_async_copy(zero_ref.at[pl.ds(0, n_rows * TILE_ROWS), :],
                                         xs_hbm.at[pl.ds(first, n_rows * TILE_ROWS), :], zsem)

        def fill(action):
            def pad_rows(expert, carry):
                first, left = pad0_s[expert], padn_s[expert]
                run = FFN_ROWS // 2
                while run >= 1:
                    take = left & run

                    @pl.when(take != 0)
                    def _(first=first, run=run):
                        action(zero_copy(first, run))

                    first = first + take
                    run //= 2
                return carry

            def tail_tile(tile, carry):
                action(zero_copy(tile * FFN_ROWS, FFN_ROWS))
                return carry

            lax.fori_loop(0, N_EXPERTS, pad_rows, 0)
            lax.fori_loop(ntile_s[0], n_ffn_tiles, tail_tile, 0)

        fill(lambda cp: cp.start())
        fill(lambda cp: cp.wait())


def _scatter_call(pos1, pos2, pads, h2_p, h2_s, n_sorted_rows):
    tm = DISPATCH_ROWS
    tiles_p = h2_p.shape[0] // (tm * TILE_ROWS)
    tiles_s = h2_s.shape[0] // (tm * TILE_ROWS)
    n_steps = tiles_p + tiles_s
    assert n_steps >= D_SLOTS
    any_spec = pl.BlockSpec(memory_space=pl.ANY)
    return pl.pallas_call(
        functools.partial(_scatter_kernel, tiles_p, n_steps, n_sorted_rows // FFN_ROWS),
        grid_spec=pltpu.PrefetchScalarGridSpec(
            num_scalar_prefetch=5,
            grid=(n_steps,),
            in_specs=[any_spec, any_spec],
            out_specs=pl.BlockSpec(memory_space=pl.ANY),
            scratch_shapes=[pltpu.VMEM((D_SLOTS, tm * TILE_ROWS, LANES), F32),
                            pltpu.VMEM((FFN_ROWS * TILE_ROWS, LANES), F32),
                            pltpu.SemaphoreType.DMA((D_SLOTS,)),
                            pltpu.SemaphoreType.DMA((D_SLOTS,)),
                            pltpu.SemaphoreType.DMA(())],
        ),
        out_shape=jax.ShapeDtypeStruct((n_sorted_rows * TILE_ROWS, LANES), F32),
        compiler_params=pltpu.CompilerParams(
            dimension_semantics=("arbitrary",), vmem_limit_bytes=VMEM_LIMIT),
        name="dispatch_scatter",
    )(pos1, pos2, *pads, h2_p, h2_s)


def _ffn_kernel(n_steps, we_s, wvalid_s, word_s, wnext_s, x_hbm, w1_hbm, w3_hbm, w2_hbm,
                o_ref, xbuf, w1f_ref, w3f_ref, w2f_ref, w13b_ref, w2b_ref, xsems, sems):
    step = pl.program_id(0)
    step_rows = FFN_GROUP * FFN_ROWS * TILE_ROWS

    def x_copy(s):
        first = pl.multiple_of(s * step_rows, step_rows)
        return pltpu.make_async_copy(x_hbm.at[pl.ds(first, step_rows), :], xbuf.at[s % X_SLOTS],
                                     xsems.at[s % X_SLOTS])

    @pl.when(step == 0)
    def _():
        for ahead in range(X_SLOTS - 1):
            x_copy(ahead).start()

    @pl.when(step + X_SLOTS - 1 < n_steps)
    def _():
        x_copy(step + X_SLOTS - 1).start()

    def weight_copies(expert, buf):
        return [pltpu.make_async_copy(src.at[expert], dst.at[buf], sems.at[buf])
                for src, dst in ((w1_hbm, w1f_ref), (w3_hbm, w3f_ref), (w2_hbm, w2f_ref))]

    x_copy(step).wait()

    for sub in range(FFN_GROUP):
        w = step * FFN_GROUP + sub
        e = we_s[w]
        e_prev = we_s[jnp.maximum(w - 1, 0)]
        slot = word_s[w] % 2

        @pl.when(w == 0)
        def _():
            for cp in weight_copies(e, slot):
                cp.start()

        @pl.when(jnp.logical_or(w == 0, e != e_prev))
        def _():
            for cp in weight_copies(e, slot):
                cp.wait()
            w13b_ref[:, :D_EXPERT] = w1f_ref[slot].astype(BF16)
            w13b_ref[:, D_EXPERT:] = w3f_ref[slot].astype(BF16)
            w2b_ref[...] = w2f_ref[slot].astype(BF16)
            nxt = wnext_s[w]

            @pl.when(nxt >= 0)
            def _():
                for cp in weight_copies(nxt, 1 - slot):
                    cp.start()

        @pl.when(wvalid_s[w] == 1)
        def _():
            x = _load_token_tiles(xbuf, (step % X_SLOTS,), FFN_ROWS, sub * FFN_ROWS)
            h13 = _dot(x.astype(BF16), w13b_ref[...])
            a = h13[:, :D_EXPERT]
            b = h13[:, D_EXPERT:]
            hdn = (a * jax.nn.sigmoid(a)) * b
            _store_token_tiles(o_ref, _dot(hdn.astype(BF16), w2b_ref[...]), sub * FFN_ROWS)

        @pl.when(wvalid_s[w] == 0)
        def _():
            rows = pl.ds(sub * FFN_ROWS * TILE_ROWS, FFN_ROWS * TILE_ROWS)
            o_ref[rows, :] = jnp.zeros((FFN_ROWS * TILE_ROWS, LANES), F32)


def _ffn_call(plan, xs, w1, w3, w2):
    n_pairs = xs.shape[0] // TILE_ROWS
    tm = FFN_GROUP * FFN_ROWS
    any_spec = pl.BlockSpec(memory_space=pl.ANY)
    row = pl.BlockSpec((tm * TILE_ROWS, LANES), lambda i, *_: (i, 0))
    n_items = plan[0].shape[0]
    assert n_items % FFN_GROUP == 0 and n_items // FFN_GROUP >= X_SLOTS
    n_steps = n_items // FFN_GROUP
    return pl.pallas_call(
        functools.partial(_ffn_kernel, n_steps),
        grid_spec=pltpu.PrefetchScalarGridSpec(
            num_scalar_prefetch=len(plan),
            grid=(n_steps,),
            in_specs=[any_spec, any_spec, any_spec, any_spec],
            out_specs=row,
            scratch_shapes=[pltpu.VMEM((X_SLOTS, tm * TILE_ROWS, LANES), F32),
                            pltpu.VMEM((2, D_MODEL, D_EXPERT), F32),
                            pltpu.VMEM((2, D_MODEL, D_EXPERT), F32),
                            pltpu.VMEM((2, D_EXPERT, D_MODEL), F32),
                            pltpu.VMEM((D_MODEL, 2 * D_EXPERT), BF16),
                            pltpu.VMEM((D_EXPERT, D_MODEL), BF16),
                            pltpu.SemaphoreType.DMA((X_SLOTS,)),
                            pltpu.SemaphoreType.DMA((2,))],
        ),
        out_shape=jax.ShapeDtypeStruct((n_pairs * TILE_ROWS, LANES), F32),
        compiler_params=pltpu.CompilerParams(
            dimension_semantics=("arbitrary",), vmem_limit_bytes=VMEM_LIMIT),
        name="expert_ffn",
    )(*plan, xs, w1, w3, w2)


def _combine_kernel(three_d, row0, n_steps, pos1_s, pos2_s, x1_ref, wk_ref, mod_ref, g_ref, b_ref,
                    ys_hbm, o_ref, ybuf, sems):
    step = pl.program_id(0)
    slot = step % 2

    def gather(tile, tile_slot):
        base = row0 + tile * COMBINE_ROWS

        def row_copy(r, k, pos_s):
            p = pos_s[base + r]
            return pltpu.make_async_copy(ys_hbm.at[_token_tile(p), :],
                                         ybuf.at[tile_slot, k, _token_tile(r), :], sems.at[tile_slot])

        def start(r, carry):
            row_copy(r, 0, pos1_s).start(priority=0)
            row_copy(r, 1, pos2_s).start(priority=1)
            return carry

        def wait(r, carry):
            row_copy(r, 0, pos1_s).wait()
            row_copy(r, 1, pos2_s).wait()
            return carry

        return start, wait

    @pl.when(step == 0)
    def _():
        lax.fori_loop(0, COMBINE_ROWS, gather(step, slot)[0], 0, unroll=32)

    @pl.when(step + 1 < n_steps)
    def _():
        lax.fori_loop(0, COMBINE_ROWS, gather(step + 1, 1 - slot)[0], 0, unroll=32)

    lax.fori_loop(0, COMBINE_ROWS, gather(step, slot)[1], 0, unroll=8)

    wk = wk_ref[...]
    r = lax.broadcasted_iota(jnp.int32, (COMBINE_ROWS, COMBINE_ROWS), 0)
    c = lax.broadcasted_iota(jnp.int32, (COMBINE_ROWS, COMBINE_ROWS), 1)
    wk_col = [jnp.sum(jnp.where(r == c, wk[k:k + 1, :], 0.0), axis=1, keepdims=True)
              for k in range(TOP_K)]
    moe = (wk_col[0] * _load_token_tiles(ybuf, (slot, 0), COMBINE_ROWS)
           + wk_col[1] * _load_token_tiles(ybuf, (slot, 1), COMBINE_ROWS))
    x1 = x1_ref[...]
    if three_d:
        g2 = mod_ref[0, :, 5 * D_MODEL:]
        o_ref[0] = _layer_norm(ALPHA * x1 + (1.0 + g2) * moe, g_ref[...], b_ref[...])
    else:
        def seq_rows(v, b):
            return v[b * SUBLANES:(b + 1) * SUBLANES]

        resid = jnp.concatenate([
            ALPHA * seq_rows(x1, b) + (1.0 + mod_ref[b:b + 1, 5 * D_MODEL:]) * seq_rows(moe, b)
            for b in range(COMBINE_ROWS // SUBLANES)], axis=0)
        o_ref[...] = _layer_norm(resid, g_ref[...], b_ref[...])


def _combine_call(three_d, pos1, pos2, x1_all, mf_all, mod, ln2_g, ln2_b, ys, row0, batch, seq):
    tm = COMBINE_ROWS
    blk0 = row0 // tm
    if three_d:
        tiles = seq // tm
        grid = (batch * tiles,)
        mod_spec = pl.BlockSpec((1, 1, 6 * D_MODEL), lambda i, *_: (i // tiles, 0, 0))
        out_spec = pl.BlockSpec((1, tm, D_MODEL), lambda i, *_: (i // tiles, i % tiles, 0))
        out_shape = jax.ShapeDtypeStruct((batch, seq, D_MODEL), F32)
    else:
        grid = (batch * seq // tm,)
        mod_spec = pl.BlockSpec((tm // SUBLANES, 6 * D_MODEL), lambda i, *_: (i, 0))
        out_spec = pl.BlockSpec((tm, D_MODEL), lambda i, *_: (i, 0))
        out_shape = jax.ShapeDtypeStruct((batch * seq, D_MODEL), F32)
    return pl.pallas_call(
        functools.partial(_combine_kernel, three_d, row0, grid[0]),
        grid_spec=pltpu.PrefetchScalarGridSpec(
            num_scalar_prefetch=2,
            grid=grid,
            in_specs=[pl.BlockSpec((tm, D_MODEL), lambda i, *_: (i, 0)),
                      pl.BlockSpec((SUBLANES, tm), lambda i, *_: (0, blk0 + i)),
                      mod_spec,
                      pl.BlockSpec(ln2_g.shape, lambda i, *_: (0, 0)),
                      pl.BlockSpec(ln2_b.shape, lambda i, *_: (0, 0)),
                      pl.BlockSpec(memory_space=pl.ANY)],
            out_specs=out_spec,
            scratch_shapes=[pltpu.VMEM((2, TOP_K, tm * TILE_ROWS, LANES), F32),
                            pltpu.SemaphoreType.DMA((2,))],
        ),
        out_shape=out_shape,
        compiler_params=pltpu.CompilerParams(
            dimension_semantics=("arbitrary",), vmem_limit_bytes=VMEM_LIMIT),
        name="combine_prompt" if three_d else "combine_sample",
    )(pos1, pos2, x1_all, mf_all, mod, ln2_g, ln2_b, ys)


def _block_diag(w):
    heads, n, _ = w.shape
    eye = jnp.eye(heads, dtype=w.dtype)
    return (eye[:, None, :, None] * w[:, :, None, :]).reshape(heads * n, heads * n)


def kernel(x_prompt, x_sample, state_rglru_h, state_conv, c_prompt, c_sample, w_ada, b_ada, w_in, w_s, b_s, lnv_g, lnv_b, conv_w, conv_b, lru_wa, lru_ba, lru_wx, lru_bx, lru_lam, w_out, ln1_g, ln1_b, w_rg, b_rg, w_re, b_re, w1, w3, w2, ln2_g, ln2_b):
    batch, seq, _ = x_prompt.shape
    dec_batch, dec_seq, _ = x_sample.shape
    assert dec_seq == SUBLANES and seq % ROWS == 0 and (dec_batch * dec_seq) % ROWS == 0
    n_prompt = batch * seq
    n_sample = dec_batch * dec_seq
    n_tokens = n_prompt + n_sample
    n_pairs = TOP_K * n_tokens
    assert n_tokens % ROUTER_COLS == 0 and n_pairs % FFN_ROWS == 0
    l = 0

    mod_p, mod_s = _ada_call(c_prompt, c_sample, w_ada[l], b_ada[l][None])
    mod_p3 = mod_p[:, None, :]

    tri = jnp.tril(jnp.ones((CHUNK, CHUNK), dtype=bool))
    ws_tri = jnp.where(tri[None], w_s[l], 0.0)
    reps_p = MXU_DIM // CHUNK
    eye_p = jnp.eye(reps_p, dtype=F32)
    wsm_p = (eye_p[None, :, None, :, None] * ws_tri[:, None, :, None, :]).reshape(
        A_HEADS, MXU_DIM, MXU_DIM).astype(BF16)
    reps_s = MXU_DIM // dec_seq
    pos_in_seq = jnp.arange(MXU_DIM, dtype=jnp.int32) % dec_seq
    seq_of_row = jnp.arange(MXU_DIM, dtype=jnp.int32) // dec_seq
    spread = (pos_in_seq[:, None] == jnp.arange(dec_seq, dtype=jnp.int32)[None, :]).astype(F32)
    same_seq = seq_of_row[:, None] == seq_of_row[None, :]
    wsm_s = jnp.einsum("ri,hij,cj->hrc", spread, ws_tri[:, :dec_seq, :dec_seq], spread,
                       precision=lax.Precision.HIGHEST)
    wsm_s = jnp.where(same_seq[None], wsm_s, 0.0).astype(BF16)
    bs_pos = jnp.repeat(jnp.transpose(b_s[l]), A_HEAD_DIM, axis=1)
    bsf_p = jnp.tile(bs_pos, (reps_p, 1))
    bsf_s = jnp.tile(bs_pos[:dec_seq], (reps_s, 1))

    wa_bd = _block_diag(lru_wa[l])
    wx_bd = _block_diag(lru_wx[l])
    wg = jnp.stack([
        jnp.concatenate([wa_bd[q * MXU_DIM:(q + 1) * MXU_DIM, q * MXU_DIM:(q + 1) * MXU_DIM],
                         wx_bd[q * MXU_DIM:(q + 1) * MXU_DIM, q * MXU_DIM:(q + 1) * MXU_DIM]], axis=1)
        for q in range(D_B // MXU_DIM)]).astype(BF16)
    bg = jnp.concatenate([lru_ba[l].reshape(1, D_B), lru_bx[l].reshape(1, D_B)], axis=1)

    group_pad = jnp.zeros((SUBLANES - N_GROUPS, D_MODEL), F32)
    wr = jnp.concatenate([jnp.transpose(w_rg[l]), group_pad, jnp.transpose(w_re[l])], axis=0)
    wr_hi = wr.astype(BF16)
    wr_lo = jnp.concatenate([wr_hi, (wr - wr_hi.astype(F32)).astype(BF16)], axis=0)
    br = jnp.concatenate([b_rg[l], jnp.zeros((SUBLANES - N_GROUPS,), F32), b_re[l]])[:, None]

    common = dict(lnv_g=lnv_g[l][None], lnv_b=lnv_b[l][None], conv_w=conv_w[l], conv_b=conv_b[l][None],
                  wg=wg, bg=bg, lam=lru_lam[l][None], w_out=w_out[l].astype(BF16),
                  ln1_g=ln1_g[l][None], ln1_b=ln1_b[l][None], wr_hi=wr_hi, wr_lo=wr_lo, br=br)
    w_in_b = w_in[l].astype(BF16)
    weights_p = _mixer_weights(w_in_b, wsm_p, bsf_p, **common)
    weights_s = _mixer_weights(w_in_b, wsm_s, bsf_s, **common)

    x1_p, h2_p, lg_p, cst_p, hst_p = _mixer_prompt(x_prompt, mod_p3, weights_p)

    cpad = jnp.concatenate(
        [jnp.zeros((dec_batch, SUBLANES - (CONV_W - 1), D_B), F32), state_conv[l]], axis=1
    ).reshape(n_sample, D_B)
    x1_s, h2_s, lg_s, xr_s, v_s, hs_s = _mixer_sample(
        x_sample.reshape(n_sample, D_MODEL), mod_s, weights_s, state_rglru_h[l], cpad)

    meta_i, meta_f, counts = _router_call(jnp.concatenate([lg_p, lg_s], axis=1))
    pos1, pos2 = meta_i[0], meta_i[1]
    plan, pads = _ffn_plan(counts[:, 0], n_pairs)
    xs = _scatter_call(pos1, pos2, pads, h2_p, h2_s, plan[0].shape[0] * FFN_ROWS)
    ys = _ffn_call(plan, xs, w1[l], w3[l], w2[l])

    ln2g, ln2b = ln2_g[l][None], ln2_b[l][None]
    y_prompt = _combine_call(True, pos1, pos2, x1_p, meta_f, mod_p3, ln2g, ln2b, ys,
                             0, batch, seq)
    y_sample = _combine_call(False, pos1, pos2, x1_s, meta_f, mod_s, ln2g, ln2b, ys,
                             n_prompt, dec_batch, dec_seq).reshape(dec_batch, dec_seq, D_MODEL)

    new_h_prompt = hst_p[:, SUBLANES - 1, :][None]
    new_conv_prompt = cst_p[:, SUBLANES - (CONV_W - 1):, :][None]
    xr_s3 = xr_s.reshape(dec_batch, dec_seq, D_B)
    new_h_sample = hs_s.reshape(dec_batch, dec_seq, D_B)[:, dec_seq - 1, :][None]
    new_conv_sample = xr_s3[:, dec_seq - (CONV_W - 1):, :][None]
    new_chunk_v_sample = v_s.reshape(dec_batch, dec_seq, D_A)[None]
    return (y_prompt, y_sample, new_h_prompt, new_conv_prompt, new_h_sample, new_conv_sample,
            new_chunk_v_sample)
```

```python
import functools
import math

import jax
import jax.numpy as jnp
from jax import lax
from jax.experimental import pallas as pl
from jax.experimental.pallas import tpu as pltpu

D_MODEL = 1024
D_A = 512
A_HEADS = 4
A_HEAD_DIM = 128
CHUNK = 128
D_B = 512
B_HEADS = 8
B_HEAD_DIM = 64
CONV_W = 4
LRU_C = 8.0
D_IN = 2 * D_A + 2 * D_B
N_GROUPS = 4
EXPERTS_PER_GROUP = 8
N_EXPERTS = 32
TOP_K = 2
D_EXPERT = 512
LN_EPS = 1e-5
DEPTH = 1
ALPHA = (2.0 * DEPTH) ** 0.25

LANES = 128
SUBLANES = 8
MXU_DIM = 256
VMEM_LIMIT = 56 * 1024 * 1024

ROWS = 512
ROUTER_COLS = 1024
LOGIT_ROWS = SUBLANES + N_EXPERTS
DISPATCH_ROWS = 1024
D_SLOTS = 3
FFN_ROWS = 256
FFN_GROUP = 2
X_SLOTS = 3
COMBINE_ROWS = 256
BF16 = jnp.bfloat16
F32 = jnp.float32


def _dot(a, b):
    return jnp.dot(a, b, preferred_element_type=F32)


def _dot_nt(a, b):
    return lax.dot_general(a, b, (((1,), (1,)), ((), ())), preferred_element_type=F32)


TILE_ROWS = D_MODEL // LANES


def _token_tile(r):
    return pl.ds(pl.multiple_of(r * TILE_ROWS, TILE_ROWS), TILE_ROWS)


def _load_token_tiles(ref, lead, n_rows, first_token=0):
    base = first_token * TILE_ROWS
    parts = [ref[lead + (pl.ds(base + s, n_rows, stride=TILE_ROWS), slice(None))]
             for s in range(TILE_ROWS)]
    return jnp.concatenate(parts, axis=1)


def _store_token_tiles(ref, val, first_token=0):
    n_rows = val.shape[0]
    base = first_token * TILE_ROWS
    for s in range(TILE_ROWS):
        ref[pl.ds(base + s, n_rows, stride=TILE_ROWS), :] = val[:, s * LANES:(s + 1) * LANES]


def _layer_norm(x, g, b):
    mu = jnp.mean(x, axis=-1, keepdims=True)
    xc = x - mu
    var = jnp.mean(xc * xc, axis=-1, keepdims=True)
    return xc * lax.rsqrt(var + LN_EPS) * g + b


def _gelu_tanh(x):
    c = math.sqrt(2.0 / math.pi)
    return x * (0.5 * (1.0 + jnp.tanh(c * (x + 0.044715 * (x * x * x)))))


def _const_spec(shape):
    nd = len(shape)
    return pl.BlockSpec(shape, lambda *_: (0,) * nd)


def _ada_kernel(cp_ref, cs_ref, w_ref, b_ref, op_ref, os_ref):
    w = w_ref[...].astype(BF16)
    for c_ref, o_ref in ((cp_ref, op_ref), (cs_ref, os_ref)):
        c = c_ref[...]
        a = (c * jax.nn.sigmoid(c)).astype(BF16)
        o_ref[...] = _dot(a, w) + b_ref[...]


def _ada_call(c_prompt, c_sample, w_ada, b_ada):
    tn = 1536
    rows_p, rows_s = c_prompt.shape[0], c_sample.shape[0]
    return pl.pallas_call(
        _ada_kernel,
        grid=(6 * D_MODEL // tn,),
        in_specs=[
            pl.BlockSpec((rows_p, D_MODEL), lambda j: (0, 0)),
            pl.BlockSpec((rows_s, D_MODEL), lambda j: (0, 0)),
            pl.BlockSpec((D_MODEL, tn), lambda j: (0, j)),
            pl.BlockSpec((1, tn), lambda j: (0, j)),
        ],
        out_specs=[pl.BlockSpec((rows_p, tn), lambda j: (0, j)),
                   pl.BlockSpec((rows_s, tn), lambda j: (0, j))],
        out_shape=[jax.ShapeDtypeStruct((rows_p, 6 * D_MODEL), F32),
                   jax.ShapeDtypeStruct((rows_s, 6 * D_MODEL), F32)],
        compiler_params=pltpu.CompilerParams(
            dimension_semantics=("arbitrary",), vmem_limit_bytes=VMEM_LIMIT),
        name="ada_mod",
    )(c_prompt, c_sample, w_ada, b_ada)


def _mixer_kernel(sample, *refs):
    if sample:
        (x_ref, mod_ref, win_ref, wsm_ref, bsf_ref, lnvg_ref, lnvb_ref, convw_ref, convb_ref,
         wg_ref, bg_ref, lam_ref, wout_ref, ln1g_ref, ln1b_ref, wrh_ref, wrl_ref, br_ref,
         h0_ref, cpad_ref,
         x1_ref, h2_ref, lg_ref, xr_ref, v_ref, hs_ref) = refs
    else:
        (x_ref, mod_ref, win_ref, wsm_ref, bsf_ref, lnvg_ref, lnvb_ref, convw_ref, convb_ref,
         wg_ref, bg_ref, lam_ref, wout_ref, ln1g_ref, ln1b_ref, wrh_ref, wrl_ref, br_ref,
         x1_ref, h2_ref, lg_ref, cst_ref, hst_ref, prev_ref, hcar_ref) = refs

    rows = ROWS
    groups = rows // SUBLANES

    if sample:
        x = x_ref[...]

        def chunk(c):
            return lambda b: mod_ref[b:b + 1, c * D_MODEL:(c + 1) * D_MODEL]
    else:
        x = x_ref[0]

        def chunk(c):
            return lambda b: mod_ref[0, :, c * D_MODEL:(c + 1) * D_MODEL]

        @pl.when(pl.program_id(1) == 0)
        def _():
            prev_ref[...] = jnp.zeros_like(prev_ref)
            hcar_ref[...] = jnp.zeros_like(hcar_ref)

    def modulated(fn, *vals):
        if not sample:
            return fn(0, *vals)
        parts = [fn(b, *(v[b * SUBLANES:(b + 1) * SUBLANES] for v in vals)) for b in range(groups)]
        return jnp.concatenate(parts, axis=0)

    sh1, sc1, g1, sh2, sc2, g2 = [chunk(c) for c in range(6)]

    h = modulated(lambda b, xv: xv * (1.0 + sc1(b)) + sh1(b), x)
    proj = _dot(h.astype(BF16), win_ref[...])
    u = proj[:, :D_A]
    v = _layer_norm(proj[:, D_A:2 * D_A], lnvg_ref[...], lnvb_ref[...])
    xr = proj[:, 2 * D_A:2 * D_A + D_B]
    yr = proj[:, 2 * D_A + D_B:]

    vb = v.astype(BF16)
    halves = []
    for r in range(rows // MXU_DIM):
        heads = [
            _dot(wsm_ref[hd], vb[r * MXU_DIM:(r + 1) * MXU_DIM, hd * A_HEAD_DIM:(hd + 1) * A_HEAD_DIM])
            for hd in range(A_HEADS)
        ]
        halves.append(jnp.concatenate(heads, axis=1) + bsf_ref[...])
    mixed = halves[0] if len(halves) == 1 else jnp.concatenate(halves, axis=0)
    out_a = u * mixed

    xr3 = xr.reshape(groups, SUBLANES, D_B)
    t3 = lax.broadcasted_iota(jnp.int32, (1, SUBLANES, D_B), 1)
    cw = convw_ref[...]
    if sample:
        cpad3 = cpad_ref[...].reshape(groups, SUBLANES, D_B)
    else:
        prev8 = prev_ref[...]
    xc3 = convb_ref[...][None]
    for k in range(CONV_W):
        s = CONV_W - 1 - k
        if s == 0:
            shifted = xr3
        else:
            rot = pltpu.roll(xr3, s, 1)
            if sample:
                rot_prev = pltpu.roll(cpad3, s, 1)
            else:
                first = pltpu.roll(prev8, s, 0)[None]
                rot_prev = jnp.concatenate([first, rot[:-1]], axis=0) if groups > 1 else first
            shifted = jnp.where(t3 >= s, rot, rot_prev)
        xc3 = xc3 + shifted * cw[k:k + 1, :][None]
    xc = xc3.reshape(rows, D_B)

    xcb = xc.astype(BF16)
    gq = [_dot(xcb[:, q * MXU_DIM:(q + 1) * MXU_DIM], wg_ref[q]) for q in range(D_B // MXU_DIM)]
    bg = bg_ref[...]
    pre_a = jnp.concatenate([g[:, :MXU_DIM] for g in gq], axis=1) + bg[:, :D_B]
    pre_x = jnp.concatenate([g[:, MXU_DIM:] for g in gq], axis=1) + bg[:, D_B:]
    gate_a = jax.nn.sigmoid(pre_a)
    gate_x = jax.nn.sigmoid(pre_x)
    nl = -lam_ref[...]
    softplus = jnp.maximum(nl, 0.0) + jnp.log1p(jnp.exp(-jnp.abs(nl)))
    log_a = -LRU_C * gate_a * softplus
    a = jnp.exp(log_a)
    mult = jnp.sqrt(-jnp.tanh(log_a) * (1.0 + a * a))
    bx = mult * (gate_x * xc)

    a3 = a.reshape(groups, SUBLANES, D_B)
    b3 = bx.reshape(groups, SUBLANES, D_B)
    for d in (1, 2, 4):
        m = t3 >= d
        a_sh = jnp.where(m, pltpu.roll(a3, d, 1), 1.0)
        b_sh = jnp.where(m, pltpu.roll(b3, d, 1), 0.0)
        b3 = a3 * b_sh + b3
        a3 = a3 * a_sh
    if sample:
        hs3 = jnp.stack([a3[g] * h0_ref[g:g + 1, :] + b3[g] for g in range(groups)], axis=0)
    else:
        h_prev = hcar_ref[...][SUBLANES - 1:SUBLANES, :]
        hs_list = []
        for g in range(groups):
            hg = a3[g] * h_prev + b3[g]
            hs_list.append(hg)
            h_prev = hg[SUBLANES - 1:SUBLANES, :]
        hs3 = jnp.stack(hs_list, axis=0)
    hs = hs3.reshape(rows, D_B)

    out_b = hs * _gelu_tanh(yr)
    mix = _dot(out_a.astype(BF16), wout_ref[:D_A, :]) + _dot(out_b.astype(BF16), wout_ref[D_A:, :])
    y1 = modulated(lambda b, xv, mv: ALPHA * xv + (1.0 + g1(b)) * mv, x, mix)
    x1 = _layer_norm(y1, ln1g_ref[...], ln1b_ref[...])
    h2 = modulated(lambda b, xv: xv * (1.0 + sc2(b)) + sh2(b), x1)

    h2_hi = h2.astype(BF16)
    h2_lo = (h2 - h2_hi.astype(F32)).astype(BF16)
    both = _dot_nt(wrl_ref[...], h2_hi)
    logits = (both[:LOGIT_ROWS] + _dot_nt(wrh_ref[...], h2_lo) + both[LOGIT_ROWS:] + br_ref[...])

    x1_ref[...] = x1
    _store_token_tiles(h2_ref, h2)
    lg_ref[...] = logits
    if sample:
        xr_ref[...] = xr
        v_ref[...] = v
        hs_ref[...] = hs
    else:
        last_x = xr3[groups - 1]
        last_h = hs3[groups - 1]
        prev_ref[...] = last_x
        hcar_ref[...] = last_h
        cst_ref[0] = last_x
        hst_ref[0] = last_h


def _mixer_weights(w_in, wsm, bsf, lnv_g, lnv_b, conv_w, conv_b, wg, bg, lam, w_out, ln1_g, ln1_b,
                   wr_hi, wr_lo, br):
    arrs = [w_in, wsm, bsf, lnv_g, lnv_b, conv_w, conv_b, wg, bg, lam, w_out, ln1_g, ln1_b,
            wr_hi, wr_lo, br]
    return arrs, [_const_spec(a.shape) for a in arrs]


def _mixer_prompt(x, mod3, weights):
    batch, seq, _ = x.shape
    n_tokens = batch * seq
    tiles = seq // ROWS
    arrs, specs = weights
    row_spec = lambda w: pl.BlockSpec((ROWS, w), lambda b, j: (b * tiles + j, 0))
    st_spec = pl.BlockSpec((1, SUBLANES, D_B), lambda b, j: (b, 0, 0))
    return pl.pallas_call(
        functools.partial(_mixer_kernel, False),
        grid=(batch, tiles),
        in_specs=[pl.BlockSpec((1, ROWS, D_MODEL), lambda b, j: (b, j, 0)),
                  pl.BlockSpec((1, 1, 6 * D_MODEL), lambda b, j: (b, 0, 0))] + specs,
        out_specs=[row_spec(D_MODEL),
                   pl.BlockSpec((ROWS * TILE_ROWS, LANES), lambda b, j: (b * tiles + j, 0)),
                   pl.BlockSpec((LOGIT_ROWS, ROWS), lambda b, j: (0, b * tiles + j)), st_spec, st_spec],
        out_shape=[jax.ShapeDtypeStruct((n_tokens, D_MODEL), F32),
                   jax.ShapeDtypeStruct((n_tokens * TILE_ROWS, LANES), F32),
                   jax.ShapeDtypeStruct((LOGIT_ROWS, n_tokens), F32),
                   jax.ShapeDtypeStruct((batch, SUBLANES, D_B), F32),
                   jax.ShapeDtypeStruct((batch, SUBLANES, D_B), F32)],
        scratch_shapes=[pltpu.VMEM((SUBLANES, D_B), F32), pltpu.VMEM((SUBLANES, D_B), F32)],
        compiler_params=pltpu.CompilerParams(
            dimension_semantics=("arbitrary", "arbitrary"), vmem_limit_bytes=VMEM_LIMIT),
        name="mixer_prompt",
    )(x, mod3, *arrs)


def _mixer_sample(x2, mod_seq, weights, h0, cpad):
    n_rows = x2.shape[0]
    arrs, specs = weights
    row = lambda w: pl.BlockSpec((ROWS, w), lambda i: (i, 0))
    seq_row = lambda w: pl.BlockSpec((ROWS // SUBLANES, w), lambda i: (i, 0))
    return pl.pallas_call(
        functools.partial(_mixer_kernel, True),
        grid=(n_rows // ROWS,),
        in_specs=[row(D_MODEL), seq_row(6 * D_MODEL)] + specs + [seq_row(D_B), row(D_B)],
        out_specs=[row(D_MODEL), pl.BlockSpec((ROWS * TILE_ROWS, LANES), lambda i: (i, 0)),
                   pl.BlockSpec((LOGIT_ROWS, ROWS), lambda i: (0, i)),
                   row(D_B), row(D_B), row(D_B)],
        out_shape=[jax.ShapeDtypeStruct((n_rows, D_MODEL), F32),
                   jax.ShapeDtypeStruct((n_rows * TILE_ROWS, LANES), F32),
                   jax.ShapeDtypeStruct((LOGIT_ROWS, n_rows), F32),
                   jax.ShapeDtypeStruct((n_rows, D_B), F32),
                   jax.ShapeDtypeStruct((n_rows, D_B), F32),
                   jax.ShapeDtypeStruct((n_rows, D_B), F32)],
        compiler_params=pltpu.CompilerParams(
            dimension_semantics=("arbitrary",), vmem_limit_bytes=VMEM_LIMIT),
        name="mixer_sample",
    )(x2, mod_seq, *arrs, h0, cpad)


def _router_kernel(lg_ref, upper_ref, lower_ref, mi_ref, mf_ref, cnt_ref, tot_ref, run_ref):
    phase = pl.program_id(0)
    step = pl.program_id(1)
    l = lg_ref[...]
    cols = l.shape[1]
    row8 = lax.broadcasted_iota(jnp.int32, (SUBLANES, cols), 0)
    neg = -jnp.inf
    big = jnp.int32(SUBLANES)

    def first_max(vals):
        top = jnp.max(vals, axis=0, keepdims=True)
        return top, jnp.min(jnp.where(vals == top, row8, big), axis=0, keepdims=True)

    is_group = row8 < N_GROUPS
    lg = l[:SUBLANES]
    m, gsel = first_max(jnp.where(is_group, lg, neg))
    denom = jnp.sum(jnp.where(is_group, jnp.exp(lg - m), 0.0), axis=0, keepdims=True)
    pgsel = 1.0 / denom
    le = l[SUBLANES:2 * SUBLANES]
    for g in range(1, N_GROUPS):
        le = jnp.where(gsel == g, l[(g + 1) * SUBLANES:(g + 2) * SUBLANES], le)
    v1, i1 = first_max(le)
    v2, i2 = first_max(jnp.where(row8 == i1, neg, le))
    ex = jnp.exp(v2 - v1)
    wk1 = pgsel / (1.0 + ex)
    wk2 = pgsel * ex / (1.0 + ex)
    e1 = gsel * EXPERTS_PER_GROUP + i1
    e2 = gsel * EXPERTS_PER_GROUP + i2
    row_e = lax.broadcasted_iota(jnp.int32, (N_EXPERTS, cols), 0)
    oh1 = jnp.where(row_e == e1, 1.0, 0.0)
    oh2 = jnp.where(row_e == e2, 1.0, 0.0)
    oh = oh1 + oh2

    @pl.when(jnp.logical_and(phase == 0, step == 0))
    def _():
        tot_ref[...] = jnp.zeros_like(tot_ref)

    @pl.when(phase == 0)
    def _():
        tot_ref[...] = tot_ref[...] + jnp.sum(oh, axis=1, keepdims=True)

    @pl.when(jnp.logical_and(phase == 1, step == 0))
    def _():
        tot = tot_ref[...]
        tiles = ((tot + (FFN_ROWS - 1.0)) * (1.0 / FFN_ROWS)).astype(jnp.int32).astype(F32)
        run_ref[...] = _dot(lower_ref[...], tiles.astype(BF16)) * FFN_ROWS

    @pl.when(phase == 1)
    def _():
        blocks = cols // LANES
        stacked = jnp.concatenate([oh[:, b * LANES:(b + 1) * LANES] for b in range(blocks)], axis=0)
        local = _dot(stacked.astype(BF16), upper_ref[...])
        run = run_ref[...]
        p1, p2 = [], []
        for b in range(blocks):
            sl = slice(b * LANES, (b + 1) * LANES)
            before = local[b * N_EXPERTS:(b + 1) * N_EXPERTS] + run
            p1.append(jnp.sum(oh1[:, sl] * before, axis=0, keepdims=True))
            p2.append(jnp.sum(oh2[:, sl] * before, axis=0, keepdims=True))
            run = run + jnp.sum(oh[:, sl], axis=1, keepdims=True)
        run_ref[...] = run
        pos1 = jnp.concatenate(p1, axis=1).astype(jnp.int32)
        pos2 = jnp.concatenate(p2, axis=1).astype(jnp.int32)
        zero_i = jnp.zeros_like(row8)
        mi_ref[...] = jnp.where(row8 == 0, pos1, jnp.where(row8 == 1, pos2, jnp.where(
            row8 == 2, e1, jnp.where(row8 == 3, e2, zero_i))))
        mf_ref[...] = jnp.where(row8 == 0, wk1, jnp.where(row8 == 1, wk2, 0.0))
        cnt_ref[...] = tot_ref[...].astype(jnp.int32)


def _router_call(lg_t):
    n_tokens = lg_t.shape[1]
    tc = ROUTER_COLS
    r = lax.broadcasted_iota(jnp.int32, (LANES, LANES), 0)
    c = lax.broadcasted_iota(jnp.int32, (LANES, LANES), 1)
    upper = jnp.where(r < c, 1.0, 0.0).astype(BF16)
    lower = jnp.where(c < r, 1.0, 0.0)[:N_EXPERTS, :N_EXPERTS].astype(BF16)
    col = pl.BlockSpec((SUBLANES, tc), lambda p, i: (0, i * p))
    return pl.pallas_call(
        _router_kernel,
        grid=(2, n_tokens // tc),
        in_specs=[pl.BlockSpec((LOGIT_ROWS, tc), lambda p, i: (0, i)),
                  _const_spec((LANES, LANES)), _const_spec((N_EXPERTS, N_EXPERTS))],
        out_specs=[col, col, _const_spec((N_EXPERTS, LANES))],
        out_shape=[jax.ShapeDtypeStruct((SUBLANES, n_tokens), jnp.int32),
                   jax.ShapeDtypeStruct((SUBLANES, n_tokens), F32),
                   jax.ShapeDtypeStruct((N_EXPERTS, LANES), jnp.int32)],
        scratch_shapes=[pltpu.VMEM((N_EXPERTS, LANES), F32), pltpu.VMEM((N_EXPERTS, LANES), F32)],
        compiler_params=pltpu.CompilerParams(
            dimension_semantics=("arbitrary", "arbitrary"), vmem_limit_bytes=VMEM_LIMIT),
        name="router",
    )(lg_t, upper, lower)


def _ffn_plan(counts, n_pairs):
    ids = jnp.arange(N_EXPERTS, dtype=jnp.int32)
    tiles_e = (counts + FFN_ROWS - 1) // FFN_ROWS
    tile_end = jnp.cumsum(tiles_e)
    tile_start = tile_end - tiles_e
    n_work = tile_end[-1]
    n_items = n_pairs // FFN_ROWS + N_EXPERTS
    t_w = jnp.arange(n_items, dtype=jnp.int32)
    w = t_w
    e_w = jnp.sum((tile_end[None, :] <= jnp.minimum(w, n_work - 1)[:, None]).astype(jnp.int32), axis=1)
    e_w = jnp.minimum(e_w, N_EXPERTS - 1)
    sel = e_w[:, None] == ids[None, :]
    pick = lambda a: jnp.sum(jnp.where(sel, a[None, :], 0), axis=1)
    valid_w = (w < n_work).astype(jnp.int32)
    pad_start = tile_start * FFN_ROWS + counts
    pad_len = tiles_e * FFN_ROWS - counts
    used = counts > 0
    order_e = jnp.cumsum(used.astype(jnp.int32)) - 1
    later_used = jnp.logical_and(used[None, :], ids[None, :] > ids[:, None])
    next_e = jnp.min(jnp.where(later_used, ids[None, :], N_EXPERTS), axis=1)
    next_e = jnp.where(next_e == N_EXPERTS, -1, next_e)
    i32 = lambda a: a.astype(jnp.int32)
    plan = (i32(e_w), valid_w, i32(pick(order_e)), i32(pick(next_e)))
    return plan, (i32(pad_start), i32(pad_len), i32(n_work)[None])


def _scatter_kernel(tiles_p, n_steps, n_ffn_tiles, pos1_s, pos2_s, pad0_s, padn_s, ntile_s,
                    h2p_hbm, h2s_hbm, xs_hbm, buf, zero_ref, lsems, ssems, zsem):
    i = pl.program_id(0)
    tile_rows = DISPATCH_ROWS * TILE_ROWS

    def tile_load(step, do):
        slot = step % D_SLOTS
        for src, first, mine in ((h2p_hbm, step, step < tiles_p),
                                 (h2s_hbm, step - tiles_p, step >= tiles_p)):
            @pl.when(mine)
            def _():
                start = pl.multiple_of(first * tile_rows, tile_rows)
                do(pltpu.make_async_copy(src.at[pl.ds(start, tile_rows), :], buf.at[slot],
                                         lsems.at[slot]))

    def scatter(step):
        slot = step % D_SLOTS
        base = step * DISPATCH_ROWS

        def row_copy(r, pos_s):
            p = pos_s[base + r]
            return pltpu.make_async_copy(buf.at[slot, _token_tile(r), :], xs_hbm.at[_token_tile(p), :],
                                         ssems.at[slot])

        def start(r, carry):
            row_copy(r, pos1_s).start(priority=0)
            row_copy(r, pos2_s).start(priority=1)
            return carry

        def wait(r, carry):
            row_copy(r, pos1_s).wait()
            row_copy(r, pos2_s).wait()
            return carry

        return start, wait

    @pl.when(i == 0)
    def _():
        for ahead in range(D_SLOTS - 1):
            tile_load(ahead, lambda cp: cp.start())

    tile_load(i, lambda cp: cp.wait())
    lax.fori_loop(0, DISPATCH_ROWS, scatter(i)[0], 0, unroll=32)

    @pl.when(i >= 1)
    def _():
        lax.fori_loop(0, DISPATCH_ROWS, scatter(i - 1)[1], 0, unroll=8)

    @pl.when(i + D_SLOTS - 1 < n_steps)
    def _():
        tile_load(i + D_SLOTS - 1, lambda cp: cp.start())

    @pl.when(i == n_steps - 1)
    def _():
        lax.fori_loop(0, DISPATCH_ROWS, scatter(i)[1], 0, unroll=8)

        zero_ref[...] = jnp.zeros_like(zero_ref)

        def zero_copy(first_row, n_rows):
            first = pl.multiple_of(first_row * TILE_ROWS, TILE_ROWS)
            return pltpu.make_async_copy(zero_ref.at[pl.ds(0, n_rows * TILE_ROWS), :],
                                         xs_hbm.at[pl.ds(first, n_rows * TILE_ROWS), :], zsem)

        def fill(action):
            def pad_rows(expert, carry):
                first, left = pad0_s[expert], padn_s[expert]
                run = FFN_ROWS // 2
                while run >= 1:
                    take = left & run

                    @pl.when(take != 0)
                    def _(first=first, run=run):
                        action(zero_copy(first, run))

                    first = first + take
                    run //= 2
                return carry

            def tail_tile(tile, carry):
                action(zero_copy(tile * FFN_ROWS, FFN_ROWS))
                return carry

            lax.fori_loop(0, N_EXPERTS, pad_rows, 0)
            lax.fori_loop(ntile_s[0], n_ffn_tiles, tail_tile, 0)

        fill(lambda cp: cp.start())
        fill(lambda cp: cp.wait())


def _scatter_call(pos1, pos2, pads, h2_p, h2_s, n_sorted_rows):
    tm = DISPATCH_ROWS
    tiles_p = h2_p.shape[0] // (tm * TILE_ROWS)
    tiles_s = h2_s.shape[0] // (tm * TILE_ROWS)
    n_steps = tiles_p + tiles_s
    assert n_steps >= D_SLOTS
    any_spec = pl.BlockSpec(memory_space=pl.ANY)
    return pl.pallas_call(
        functools.partial(_scatter_kernel, tiles_p, n_steps, n_sorted_rows // FFN_ROWS),
        grid_spec=pltpu.PrefetchScalarGridSpec(
            num_scalar_prefetch=5,
            grid=(n_steps,),
            in_specs=[any_spec, any_spec],
            out_specs=pl.BlockSpec(memory_space=pl.ANY),
            scratch_shapes=[pltpu.VMEM((D_SLOTS, tm * TILE_ROWS, LANES), F32),
                            pltpu.VMEM((FFN_ROWS * TILE_ROWS, LANES), F32),
                            pltpu.SemaphoreType.DMA((D_SLOTS,)),
                            pltpu.SemaphoreType.DMA((D_SLOTS,)),
                            pltpu.SemaphoreType.DMA(())],
        ),
        out_shape=jax.ShapeDtypeStruct((n_sorted_rows * TILE_ROWS, LANES), F32),
        compiler_params=pltpu.CompilerParams(
            dimension_semantics=("arbitrary",), vmem_limit_bytes=VMEM_LIMIT),
        name="dispatch_scatter",
    )(pos1, pos2, *pads, h2_p, h2_s)


def _ffn_kernel(n_steps, we_s, wvalid_s, word_s, wnext_s, x_hbm, w1_hbm, w3_hbm, w2_hbm,
                o_ref, xbuf, w1f_ref, w3f_ref, w2f_ref, w13b_ref, w2b_ref, xsems, sems):
    step = pl.program_id(0)
    step_rows = FFN_GROUP * FFN_ROWS * TILE_ROWS

    def x_copy(s):
        first = pl.multiple_of(s * step_rows, step_rows)
        return pltpu.make_async_copy(x_hbm.at[pl.ds(first, step_rows), :], xbuf.at[s % X_SLOTS],
                                     xsems.at[s % X_SLOTS])

    def has_rows(s):
        return wvalid_s[s * FFN_GROUP] == 1

    @pl.when(step == 0)
    def _():
        for ahead in range(X_SLOTS - 1):
            @pl.when(has_rows(ahead))
            def _(ahead=ahead):
                x_copy(ahead).start()

    nxt_step = jnp.minimum(step + X_SLOTS - 1, n_steps - 1)

    @pl.when(jnp.logical_and(step + X_SLOTS - 1 < n_steps, has_rows(nxt_step)))
    def _():
        x_copy(step + X_SLOTS - 1).start()

    def weight_copies(expert, buf):
        return [pltpu.make_async_copy(src.at[expert], dst.at[buf], sems.at[buf])
                for src, dst in ((w1_hbm, w1f_ref), (w3_hbm, w3f_ref), (w2_hbm, w2f_ref))]

    @pl.when(has_rows(step))
    def _():
        x_copy(step).wait()

    for sub in range(FFN_GROUP):
        w = step * FFN_GROUP + sub
        e = we_s[w]
        e_prev = we_s[jnp.maximum(w - 1, 0)]
        slot = word_s[w] % 2

        @pl.when(w == 0)
        def _():
            for cp in weight_copies(e, slot):
                cp.start()

        @pl.when(jnp.logical_or(w == 0, e != e_prev))
        def _():
            for cp in weight_copies(e, slot):
                cp.wait()
            w13b_ref[:, :D_EXPERT] = w1f_ref[slot].astype(BF16)
            w13b_ref[:, D_EXPERT:] = w3f_ref[slot].astype(BF16)
            w2b_ref[...] = w2f_ref[slot].astype(BF16)
            nxt = wnext_s[w]

            @pl.when(nxt >= 0)
            def _():
                for cp in weight_copies(nxt, 1 - slot):
                    cp.start()

        @pl.when(wvalid_s[w] == 1)
        def _():
            x = _load_token_tiles(xbuf, (step % X_SLOTS,), FFN_ROWS, sub * FFN_ROWS)
            h13 = _dot(x.astype(BF16), w13b_ref[...])
            a = h13[:, :D_EXPERT]
            b = h13[:, D_EXPERT:]
            hdn = (a * jax.nn.sigmoid(a)) * b
            _store_token_tiles(o_ref, _dot(hdn.astype(BF16), w2b_ref[...]), sub * FFN_ROWS)

        @pl.when(wvalid_s[w] == 0)
        def _():
            rows = pl.ds(sub * FFN_ROWS * TILE_ROWS, FFN_ROWS * TILE_ROWS)
            o_ref[rows, :] = jnp.zeros((FFN_ROWS * TILE_ROWS, LANES), F32)


def _ffn_call(plan, xs, w1, w3, w2):
    n_pairs = xs.shape[0] // TILE_ROWS
    tm = FFN_GROUP * FFN_ROWS
    any_spec = pl.BlockSpec(memory_space=pl.ANY)
    row = pl.BlockSpec((tm * TILE_ROWS, LANES), lambda i, *_: (i, 0))
    n_items = plan[0].shape[0]
    assert n_items % FFN_GROUP == 0 and n_items // FFN_GROUP >= X_SLOTS
    n_steps = n_items // FFN_GROUP
    return pl.pallas_call(
        functools.partial(_ffn_kernel, n_steps),
        grid_spec=pltpu.PrefetchScalarGridSpec(
            num_scalar_prefetch=len(plan),
            grid=(n_steps,),
            in_specs=[any_spec, any_spec, any_spec, any_spec],
            out_specs=row,
            scratch_shapes=[pltpu.VMEM((X_SLOTS, tm * TILE_ROWS, LANES), F32),
                            pltpu.VMEM((2, D_MODEL, D_EXPERT), F32),
                            pltpu.VMEM((2, D_MODEL, D_EXPERT), F32),
                            pltpu.VMEM((2, D_EXPERT, D_MODEL), F32),
                            pltpu.VMEM((D_MODEL, 2 * D_EXPERT), BF16),
                            pltpu.VMEM((D_EXPERT, D_MODEL), BF16),
                            pltpu.SemaphoreType.DMA((X_SLOTS,)),
                            pltpu.SemaphoreType.DMA((2,))],
        ),
        out_shape=jax.ShapeDtypeStruct((n_pairs * TILE_ROWS, LANES), F32),
        compiler_params=pltpu.CompilerParams(
            dimension_semantics=("arbitrary",), vmem_limit_bytes=VMEM_LIMIT),
        name="expert_ffn",
    )(*plan, xs, w1, w3, w2)


def _combine_kernel(three_d, row0, n_steps, pos1_s, pos2_s, x1_ref, wk_ref, mod_ref, g_ref, b_ref,
                    ys_hbm, o_ref, ybuf, sems):
    step = pl.program_id(0)
    slot = step % 2

    def gather(tile, tile_slot):
        base = row0 + tile * COMBINE_ROWS

        def row_copy(r, k, pos_s):
            p = pos_s[base + r]
            return pltpu.make_async_copy(ys_hbm.at[_token_tile(p), :],
                                         ybuf.at[tile_slot, k, _token_tile(r), :], sems.at[tile_slot])

        def start(r, carry):
            row_copy(r, 0, pos1_s).start(priority=0)
            row_copy(r, 1, pos2_s).start(priority=1)
            return carry

        def wait(r, carry):
            row_copy(r, 0, pos1_s).wait()
            row_copy(r, 1, pos2_s).wait()
            return carry

        return start, wait

    @pl.when(step == 0)
    def _():
        lax.fori_loop(0, COMBINE_ROWS, gather(step, slot)[0], 0, unroll=32)

    @pl.when(step + 1 < n_steps)
    def _():
        lax.fori_loop(0, COMBINE_ROWS, gather(step + 1, 1 - slot)[0], 0, unroll=32)

    lax.fori_loop(0, COMBINE_ROWS, gather(step, slot)[1], 0, unroll=8)

    wk = wk_ref[...]
    r = lax.broadcasted_iota(jnp.int32, (COMBINE_ROWS, COMBINE_ROWS), 0)
    c = lax.broadcasted_iota(jnp.int32, (COMBINE_ROWS, COMBINE_ROWS), 1)
    wk_col = [jnp.sum(jnp.where(r == c, wk[k:k + 1, :], 0.0), axis=1, keepdims=True)
              for k in range(TOP_K)]
    moe = (wk_col[0] * _load_token_tiles(ybuf, (slot, 0), COMBINE_ROWS)
           + wk_col[1] * _load_token_tiles(ybuf, (slot, 1), COMBINE_ROWS))
    x1 = x1_ref[...]
    if three_d:
        g2 = mod_ref[0, :, 5 * D_MODEL:]
        o_ref[0] = _layer_norm(ALPHA * x1 + (1.0 + g2) * moe, g_ref[...], b_ref[...])
    else:
        def seq_rows(v, b):
            return v[b * SUBLANES:(b + 1) * SUBLANES]

        resid = jnp.concatenate([
            ALPHA * seq_rows(x1, b) + (1.0 + mod_ref[b:b + 1, 5 * D_MODEL:]) * seq_rows(moe, b)
            for b in range(COMBINE_ROWS // SUBLANES)], axis=0)
        o_ref[...] = _layer_norm(resid, g_ref[...], b_ref[...])


def _combine_call(three_d, pos1, pos2, x1_all, mf_all, mod, ln2_g, ln2_b, ys, row0, batch, seq):
    tm = COMBINE_ROWS
    blk0 = row0 // tm
    if three_d:
        tiles = seq // tm
        grid = (batch * tiles,)
        mod_spec = pl.BlockSpec((1, 1, 6 * D_MODEL), lambda i, *_: (i // tiles, 0, 0))
        out_spec = pl.BlockSpec((1, tm, D_MODEL), lambda i, *_: (i // tiles, i % tiles, 0))
        out_shape = jax.ShapeDtypeStruct((batch, seq, D_MODEL), F32)
    else:
        grid = (batch * seq // tm,)
        mod_spec = pl.BlockSpec((tm // SUBLANES, 6 * D_MODEL), lambda i, *_: (i, 0))
        out_spec = pl.BlockSpec((tm, D_MODEL), lambda i, *_: (i, 0))
        out_shape = jax.ShapeDtypeStruct((batch * seq, D_MODEL), F32)
    return pl.pallas_call(
        functools.partial(_combine_kernel, three_d, row0, grid[0]),
        grid_spec=pltpu.PrefetchScalarGridSpec(
            num_scalar_prefetch=2,
            grid=grid,
            in_specs=[pl.BlockSpec((tm, D_MODEL), lambda i, *_: (i, 0)),
                      pl.BlockSpec((SUBLANES, tm), lambda i, *_: (0, blk0 + i)),
                      mod_spec,
                      pl.BlockSpec(ln2_g.shape, lambda i, *_: (0, 0)),
                      pl.BlockSpec(ln2_b.shape, lambda i, *_: (0, 0)),
                      pl.BlockSpec(memory_space=pl.ANY)],
            out_specs=out_spec,
            scratch_shapes=[pltpu.VMEM((2, TOP_K, tm * TILE_ROWS, LANES), F32),
                            pltpu.SemaphoreType.DMA((2,))],
        ),
        out_shape=out_shape,
        compiler_params=pltpu.CompilerParams(
            dimension_semantics=("arbitrary",), vmem_limit_bytes=VMEM_LIMIT),
        name="combine_prompt" if three_d else "combine_sample",
    )(pos1, pos2, x1_all, mf_all, mod, ln2_g, ln2_b, ys)


def _block_diag(w):
    heads, n, _ = w.shape
    eye = jnp.eye(heads, dtype=w.dtype)
    return (eye[:, None, :, None] * w[:, :, None, :]).reshape(heads * n, heads * n)


def kernel(x_prompt, x_sample, state_rglru_h, state_conv, c_prompt, c_sample, w_ada, b_ada, w_in, w_s, b_s, lnv_g, lnv_b, conv_w, conv_b, lru_wa, lru_ba, lru_wx, lru_bx, lru_lam, w_out, ln1_g, ln1_b, w_rg, b_rg, w_re, b_re, w1, w3, w2, ln2_g, ln2_b):
    batch, seq, _ = x_prompt.shape
    dec_batch, dec_seq, _ = x_sample.shape
    assert dec_seq == SUBLANES and seq % ROWS == 0 and (dec_batch * dec_seq) % ROWS == 0
    n_prompt = batch * seq
    n_sample = dec_batch * dec_seq
    n_tokens = n_prompt + n_sample
    n_pairs = TOP_K * n_tokens
    assert n_tokens % ROUTER_COLS == 0 and n_pairs % FFN_ROWS == 0
    l = 0

    mod_p, mod_s = _ada_call(c_prompt, c_sample, w_ada[l], b_ada[l][None])
    mod_p3 = mod_p[:, None, :]

    tri = jnp.tril(jnp.ones((CHUNK, CHUNK), dtype=bool))
    ws_tri = jnp.where(tri[None], w_s[l], 0.0)
    reps_p = MXU_DIM // CHUNK
    eye_p = jnp.eye(reps_p, dtype=F32)
    wsm_p = (eye_p[None, :, None, :, None] * ws_tri[:, None, :, None, :]).reshape(
        A_HEADS, MXU_DIM, MXU_DIM).astype(BF16)
    reps_s = MXU_DIM // dec_seq
    pos_in_seq = jnp.arange(MXU_DIM, dtype=jnp.int32) % dec_seq
    seq_of_row = jnp.arange(MXU_DIM, dtype=jnp.int32) // dec_seq
    spread = (pos_in_seq[:, None] == jnp.arange(dec_seq, dtype=jnp.int32)[None, :]).astype(F32)
    same_seq = seq_of_row[:, None] == seq_of_row[None, :]
    wsm_s = jnp.einsum("ri,hij,cj->hrc", spread, ws_tri[:, :dec_seq, :dec_seq], spread,
                       precision=lax.Precision.HIGHEST)
    wsm_s = jnp.where(same_seq[None], wsm_s, 0.0).astype(BF16)
    bs_pos = jnp.repeat(jnp.transpose(b_s[l]), A_HEAD_DIM, axis=1)
    bsf_p = jnp.tile(bs_pos, (reps_p, 1))
    bsf_s = jnp.tile(bs_pos[:dec_seq], (reps_s, 1))

    wa_bd = _block_diag(lru_wa[l])
    wx_bd = _block_diag(lru_wx[l])
    wg = jnp.stack([
        jnp.concatenate([wa_bd[q * MXU_DIM:(q + 1) * MXU_DIM, q * MXU_DIM:(q + 1) * MXU_DIM],
                         wx_bd[q * MXU_DIM:(q + 1) * MXU_DIM, q * MXU_DIM:(q + 1) * MXU_DIM]], axis=1)
        for q in range(D_B // MXU_DIM)]).astype(BF16)
    bg = jnp.concatenate([lru_ba[l].reshape(1, D_B), lru_bx[l].reshape(1, D_B)], axis=1)

    group_pad = jnp.zeros((SUBLANES - N_GROUPS, D_MODEL), F32)
    wr = jnp.concatenate([jnp.transpose(w_rg[l]), group_pad, jnp.transpose(w_re[l])], axis=0)
    wr_hi = wr.astype(BF16)
    wr_lo = jnp.concatenate([wr_hi, (wr - wr_hi.astype(F32)).astype(BF16)], axis=0)
    br = jnp.concatenate([b_rg[l], jnp.zeros((SUBLANES - N_GROUPS,), F32), b_re[l]])[:, None]

    common = dict(lnv_g=lnv_g[l][None], lnv_b=lnv_b[l][None], conv_w=conv_w[l], conv_b=conv_b[l][None],
                  wg=wg, bg=bg, lam=lru_lam[l][None], w_out=w_out[l].astype(BF16),
                  ln1_g=ln1_g[l][None], ln1_b=ln1_b[l][None], wr_hi=wr_hi, wr_lo=wr_lo, br=br)
    w_in_b = w_in[l].astype(BF16)
    weights_p = _mixer_weights(w_in_b, wsm_p, bsf_p, **common)
    weights_s = _mixer_weights(w_in_b, wsm_s, bsf_s, **common)

    x1_p, h2_p, lg_p, cst_p, hst_p = _mixer_prompt(x_prompt, mod_p3, weights_p)

    cpad = jnp.concatenate(
        [jnp.zeros((dec_batch, SUBLANES - (CONV_W - 1), D_B), F32), state_conv[l]], axis=1
    ).reshape(n_sample, D_B)
    x1_s, h2_s, lg_s, xr_s, v_s, hs_s = _mixer_sample(
        x_sample.reshape(n_sample, D_MODEL), mod_s, weights_s, state_rglru_h[l], cpad)

    meta_i, meta_f, counts = _router_call(jnp.concatenate([lg_p, lg_s], axis=1))
    pos1, pos2 = meta_i[0], meta_i[1]
    plan, pads = _ffn_plan(counts[:, 0], n_pairs)
    xs = _scatter_call(pos1, pos2, pads, h2_p, h2_s, plan[0].shape[0] * FFN_ROWS)
    ys = _ffn_call(plan, xs, w1[l], w3[l], w2[l])

    ln2g, ln2b = ln2_g[l][None], ln2_b[l][None]
    y_prompt = _combine_call(True, pos1, pos2, x1_p, meta_f, mod_p3, ln2g, ln2b, ys,
                             0, batch, seq)
    y_sample = _combine_call(False, pos1, pos2, x1_s, meta_f, mod_s, ln2g, ln2b, ys,
                             n_prompt, dec_batch, dec_seq).reshape(dec_batch, dec_seq, D_MODEL)

    new_h_prompt = hst_p[:, SUBLANES - 1, :][None]
    new_conv_prompt = cst_p[:, SUBLANES - (CONV_W - 1):, :][None]
    xr_s3 = xr_s.reshape(dec_batch, dec_seq, D_B)
    new_h_sample = hs_s.reshape(dec_batch, dec_seq, D_B)[:, dec_seq - 1, :][None]
    new_conv_sample = xr_s3[:, dec_seq - (CONV_W - 1):, :][None]
    new_chunk_v_sample = v_s.reshape(dec_batch, dec_seq, D_A)[None]
    return (y_prompt, y_sample, new_h_prompt, new_conv_prompt, new_h_sample, new_conv_sample,
            new_chunk_v_sample)
```
